```python
import jax, jax.numpy as jnp
from jax import lax
import numpy as np

D_MODEL = 1024
BATCH = 16
SEQ = 2048
DEPTH = 1

MIX_WIDTH = D_MODEL
RET_HEADS = 4
RET_DK = MIX_WIDTH // (2 * RET_HEADS)
RET_DV = MIX_WIDTH // (2 * RET_HEADS)
RET_WIDTH = RET_HEADS * RET_DV
RET_CHUNK = 128
DIFF_HEADS = 4
DIFF_DQK = MIX_WIDTH // (4 * DIFF_HEADS)
DIFF_DV = 2 * DIFF_DQK
DIFF_WIDTH = DIFF_HEADS * DIFF_DV
Q_BLOCK = 128
IN_SPLITS = [RET_HEADS * RET_DK, RET_HEADS * RET_DK, RET_WIDTH, RET_WIDTH,
             DIFF_HEADS * 2 * DIFF_DQK, DIFF_HEADS * 2 * DIFF_DQK, DIFF_WIDTH]
IN_COLS = sum(IN_SPLITS)
IN_OFFSETS = [int(o) for o in np.cumsum(IN_SPLITS)[:-1]]

PEER_HEADS = 8
PEER_TOPK = 16
N_KEYS = 128
N_EXPERTS = N_KEYS * N_KEYS
PEER_DQ = 256
PEER_TOKEN_BLOCK = 128

RMS_EPS = 1e-6
NEG_INF = -1e30

kernel_name = "hybrid_retnet_diffattn_peer"


def rmsnorm(x, g=None):
    xf = x.astype(jnp.float32)
    y = xf * lax.rsqrt(jnp.mean(xf * xf, axis=-1, keepdims=True) + RMS_EPS)
    if g is not None:
        y = y * g.astype(jnp.float32)
    return y.astype(x.dtype)


def lambda_init(layer_idx):
    return 0.8 - 0.6 * float(np.exp(-0.3 * layer_idx))


def retention(q, k, v):
    B, T, H, dk = q.shape
    dv = v.shape[-1]
    C = RET_CHUNK
    N = T // C
    log_gamma = jnp.log1p(-jnp.exp2(-5.0 - jnp.arange(H, dtype=jnp.float32)))

    def chunks(a):
        return a.astype(jnp.float32).reshape(B, N, C, H, a.shape[-1]).transpose(1, 0, 3, 2, 4)

    qc, kc, vc = chunks(q), chunks(k) * (dk ** -0.5), chunks(v)
    idx = jnp.arange(C, dtype=jnp.float32)
    rel = idx[:, None] - idx[None, :]
    intra = jnp.where(rel >= 0, jnp.exp(log_gamma[:, None, None] * jnp.maximum(rel, 0.0)), 0.0)
    cross = jnp.exp(log_gamma[:, None] * (idx + 1.0))
    kdec = jnp.exp(log_gamma[:, None] * (C - 1.0 - idx))
    cdec = jnp.exp(log_gamma * C)

    def step(S, inp):
        qi, ki, vi = inp
        o = jnp.einsum('bhij,bhjv->bhiv', jnp.einsum('bhid,bhjd->bhij', qi, ki) * intra[None], vi)
        o = o + jnp.einsum('bhid,bhdv->bhiv', qi, S) * cross[None, :, :, None]
        S = S * cdec[None, :, None, None] + jnp.einsum('bhjd,bhjv->bhdv', ki * kdec[None, :, :, None], vi)
        return S, o

    S0 = jnp.zeros((B, H, dk, dv), jnp.float32)
    _, o = lax.scan(step, S0, (qc, kc, vc))
    return o.transpose(1, 0, 3, 2, 4).reshape(B, T, H, dv)


def alibi_slopes(n_heads):
    return jnp.exp2(-8.0 * (jnp.arange(n_heads, dtype=jnp.float32) + 1.0) / n_heads)


def diff_attention(q, k, v, lam):
    B, T, H, _, d = q.shape
    dv = v.shape[-1]
    C = Q_BLOCK
    N = T // C
    qb = (q.astype(jnp.float32) * (d ** -0.5)).reshape(B, N, C, H, 2, d).transpose(1, 0, 3, 4, 2, 5)
    kt = k.astype(jnp.float32).transpose(0, 2, 3, 1, 4)
    vt = v.astype(jnp.float32).transpose(0, 2, 1, 3)
    slopes = alibi_slopes(H)
    kpos = jnp.arange(T)

    def block(args):
        qi, start = args
        qpos = start + jnp.arange(C)
        dist = (qpos[:, None] - kpos[None, :]).astype(jnp.float32)
        bias = jnp.where(dist >= 0, -slopes[:, None, None] * dist, NEG_INF)
        s = jnp.einsum('bhpid,bhpjd->bhpij', qi, kt) + bias[None, :, None]
        p = jax.nn.softmax(s, axis=-1)
        a = p[:, :, 0] - lam * p[:, :, 1]
        return jnp.einsum('bhij,bhjv->bhiv', a, vt)

    o = lax.map(block, (qb, jnp.arange(N) * C))
    return o.transpose(1, 0, 3, 2, 4).reshape(B, T, H, dv)


def peer(xn, wq, subkeys, u, v):
    B, T, D = xn.shape
    xt = xn.reshape(-1, PEER_TOKEN_BLOCK, D)
    K = PEER_TOPK

    def block(xb):
        t = xb.shape[0]
        q = (xb @ wq).reshape(t, PEER_HEADS, 2, PEER_DQ // 2)
        s = jnp.einsum('thpd,hpkd->thpk', q, subkeys).astype(jnp.float32)
        s_top, i_top = lax.top_k(s, K)
        cand = (s_top[:, :, 0, :, None] + s_top[:, :, 1, None, :]).reshape(t, PEER_HEADS, K * K)
        c_top, c_idx = lax.top_k(cand, K)
        i1 = jnp.take_along_axis(i_top[:, :, 0], c_idx // K, axis=-1)
        i2 = jnp.take_along_axis(i_top[:, :, 1], c_idx % K, axis=-1)
        e = i1 * N_KEYS + i2
        g = jax.nn.softmax(c_top, axis=-1)
        ue = jnp.take(u, e, axis=0)
        ve = jnp.take(v, e, axis=0)
        act = jax.nn.gelu(jnp.einsum('thkd,td->thk', ue, xb).astype(jnp.float32), approximate=False)
        return jnp.einsum('thk,thkd->td', (g * act).astype(xb.dtype), ve)

    return lax.map(block, xt).reshape(B, T, D)


def setup_inputs(seed: int = 0) -> dict:
    key = jax.random.key(seed)
    ks = jax.random.split(key, 13)
    f = jnp.float32
    x = jax.random.normal(ks[0], (BATCH, SEQ, D_MODEL), f)
    norm_mix_g = 1.0 + 0.02 * jax.random.normal(ks[1], (DEPTH, D_MODEL), f)
    w_in = jax.random.normal(ks[2], (DEPTH, D_MODEL, IN_COLS), f) * D_MODEL ** -0.5
    diff_lambda = 0.1 * jax.random.normal(ks[3], (DEPTH, 4, DIFF_DQK), f)
    diff_subln_g = 1.0 + 0.02 * jax.random.normal(ks[4], (DEPTH, DIFF_DV), f)
    w_out = jax.random.normal(ks[5], (DEPTH, MIX_WIDTH, D_MODEL), f) * MIX_WIDTH ** -0.5
    norm_ffn_g = 1.0 + 0.02 * jax.random.normal(ks[6], (DEPTH, D_MODEL), f)
    peer_wq = jax.random.normal(ks[7], (DEPTH, D_MODEL, PEER_HEADS * PEER_DQ), f) * D_MODEL ** -0.5
    peer_subkeys = jax.random.normal(ks[8], (DEPTH, PEER_HEADS, 2, N_KEYS, PEER_DQ // 2), f) * (PEER_DQ // 2) ** -0.5
    peer_u = jax.random.normal(ks[9], (DEPTH, N_EXPERTS, D_MODEL), f) * D_MODEL ** -0.5
    peer_v = jax.random.normal(ks[10], (DEPTH, N_EXPERTS, D_MODEL), f) * PEER_TOPK ** -0.5
    final_norm_g = 1.0 + 0.02 * jax.random.normal(ks[11], (D_MODEL,), f)
    return {"x": x, "norm_mix_g": norm_mix_g, "w_in": w_in, "diff_lambda": diff_lambda,
            "diff_subln_g": diff_subln_g, "w_out": w_out, "norm_ffn_g": norm_ffn_g,
            "peer_wq": peer_wq, "peer_subkeys": peer_subkeys, "peer_u": peer_u,
            "peer_v": peer_v, "final_norm_g": final_norm_g}


def reference(x, norm_mix_g, w_in, diff_lambda, diff_subln_g, w_out, norm_ffn_g,
              peer_wq, peer_subkeys, peer_u, peer_v, final_norm_g):
    B, T, D = x.shape
    for l in range(DEPTH):
        h = rmsnorm(x, norm_mix_g[l])
        proj = h @ w_in[l]
        rq, rk, rv, rg, dq, dk, dv = jnp.split(proj, IN_OFFSETS, axis=-1)
        ry = retention(rq.reshape(B, T, RET_HEADS, RET_DK), rk.reshape(B, T, RET_HEADS, RET_DK),
                       rv.reshape(B, T, RET_HEADS, RET_DV))
        ry = rmsnorm(ry).reshape(B, T, RET_WIDTH)
        ret_out = (jax.nn.silu(rg.astype(jnp.float32)) * ry).astype(x.dtype)
        li = lambda_init(l)
        lp = diff_lambda[l].astype(jnp.float32)
        lam = jnp.exp(jnp.sum(lp[0] * lp[1])) - jnp.exp(jnp.sum(lp[2] * lp[3])) + li
        dy = diff_attention(dq.reshape(B, T, DIFF_HEADS, 2, DIFF_DQK), dk.reshape(B, T, DIFF_HEADS, 2, DIFF_DQK),
                            dv.reshape(B, T, DIFF_HEADS, DIFF_DV), lam)
        diff_out = (rmsnorm(dy, diff_subln_g[l]) * (1.0 - li)).reshape(B, T, DIFF_WIDTH).astype(x.dtype)
        x = x + jnp.concatenate([ret_out, diff_out], axis=-1) @ w_out[l]
        x = x + peer(rmsnorm(x, norm_ffn_g[l]), peer_wq[l], peer_subkeys[l], peer_u[l], peer_v[l])
    return rmsnorm(x, final_norm_g)
```

```python
import functools
import math

import jax
import jax.numpy as jnp
import numpy as np
from jax import lax
from jax.experimental import pallas as pl
from jax.experimental.pallas import tpu as pltpu

F32 = jnp.float32
BF16 = jnp.bfloat16

RMS_EPS = 1e-6
NEG_INF = -1e30

RET_HEADS = 4
RET_CHUNK = 128
DIFF_HEADS = 4
HEAD_W = 128
PEER_HEADS = 8
PEER_TOPK = 16
N_KEYS = 128

VMEM_LIMIT = 56 * 1024 * 1024

_NT = (((1,), (1,)), ((), ()))


def _rms(x, eps=RMS_EPS):
    return x * lax.rsqrt(jnp.mean(x * x, axis=-1, keepdims=True) + eps)


def _gelu_exact(x):
    return 0.5 * x * (1.0 + lax.erf(x * (2.0 ** -0.5)))


def _inproj_kernel(x_ref, g_ref, w_ref, o_ref, *, col_chunk):
    h = (_rms(x_ref[...]) * g_ref[...]).astype(BF16)
    for j in range(o_ref.shape[1] // col_chunk):
        sl = slice(j * col_chunk, (j + 1) * col_chunk)
        o_ref[:, sl] = jnp.dot(h, w_ref[:, sl], preferred_element_type=F32).astype(o_ref.dtype)


def _in_proj(x2, g, w_bf16, tm=512):
    n, d = x2.shape
    cols = w_bf16.shape[1]
    return pl.pallas_call(
        functools.partial(_inproj_kernel, col_chunk=512),
        out_shape=jax.ShapeDtypeStruct((n, cols), BF16),
        grid=(n // tm,),
        in_specs=[pl.BlockSpec((tm, d), lambda i: (i, 0)),
                  pl.BlockSpec((1, d), lambda i: (0, 0)),
                  pl.BlockSpec((d, cols), lambda i: (0, 0))],
        out_specs=pl.BlockSpec((tm, cols), lambda i: (i, 0)),
        compiler_params=pltpu.CompilerParams(dimension_semantics=("parallel",),
                                             vmem_limit_bytes=VMEM_LIMIT),
        name="in_proj",
    )(x2, g, w_bf16)


def _ret_kernel(q_ref, k_ref, v_ref, g_ref, intra_ref, cross_ref, kdec_ref, cdec_ref, o_ref):
    c = RET_CHUNK
    n_chunks = q_ref.shape[1] // c
    intra = intra_ref[0]
    cross = cross_ref[0]
    kdec = kdec_ref[0]
    cdec = cdec_ref[0]

    def body(n, s):
        sl = pl.ds(pl.multiple_of(n * c, c), c)
        qi = q_ref[0, sl, :]
        ki = k_ref[0, sl, :]
        vi = v_ref[0, sl, :]
        a = lax.dot_general(qi, ki, _NT, preferred_element_type=F32) * intra
        o = jnp.dot(a.astype(BF16), vi, preferred_element_type=F32)
        o = o + jnp.dot(qi, s.astype(BF16), preferred_element_type=F32) * cross
        kd_t = (ki.astype(F32) * kdec).T.astype(BF16)
        s = s * cdec + jnp.dot(kd_t, vi, preferred_element_type=F32)
        gate = g_ref[0, sl, :].astype(F32)
        o_ref[0, sl, :] = (gate * jax.nn.sigmoid(gate) * _rms(o)).astype(o_ref.dtype)
        return s

    lax.fori_loop(0, n_chunks, body, jnp.zeros((HEAD_W, HEAD_W), F32))


def _retention(proj3, intra, cross, kdec, cdec):
    b, t, _ = proj3.shape
    h = RET_HEADS

    def col(off):
        return pl.BlockSpec((1, t, HEAD_W), lambda bi, hi: (bi, 0, off + hi))

    def per_head(shape):
        return pl.BlockSpec((1,) + shape, lambda bi, hi: (hi, 0, 0))

    return pl.pallas_call(
        _ret_kernel,
        out_shape=jax.ShapeDtypeStruct((b, t, h * HEAD_W), BF16),
        grid=(b, h),
        in_specs=[col(0), col(h), col(2 * h), col(3 * h),
                  per_head((RET_CHUNK, RET_CHUNK)), per_head((RET_CHUNK, 1)),
                  per_head((RET_CHUNK, 1)), per_head((1, 1))],
        out_specs=pl.BlockSpec((1, t, HEAD_W), lambda bi, hi: (bi, 0, hi)),
        compiler_params=pltpu.CompilerParams(dimension_semantics=("parallel", "parallel"),
                                             vmem_limit_bytes=VMEM_LIMIT),
        name="retention",
    )(proj3, proj3, proj3, proj3, intra, cross, kdec, cdec)


def _diff_kernel(slope_ref, lam_ref, q_ref, k_ref, v_ref, gsub_ref, o_ref, *, tq, tk, out_scale):
    hi = pl.program_id(1)
    qi = pl.program_id(2)
    half = HEAD_W // 2
    slope = slope_ref[hi]

    lp = lam_ref[...]
    lam = (jnp.exp(jnp.sum(lp[0:1] * lp[1:2], axis=1, keepdims=True))
           - jnp.exp(jnp.sum(lp[2:3] * lp[3:4], axis=1, keepdims=True))
           + (1.0 - out_scale))

    q = q_ref[0]
    lane = lax.broadcasted_iota(jnp.int32, q.shape, 1)
    zero = jnp.zeros_like(q)
    qs = jnp.concatenate([jnp.where(lane < half, q, zero),
                          jnp.where(lane >= half, q, zero)], axis=0)

    row = lax.broadcasted_iota(jnp.int32, (2 * tq, tk), 0)
    row = jnp.where(row >= tq, row - tq, row)
    col = lax.broadcasted_iota(jnp.int32, (2 * tq, tk), 1)
    rel = (row - col).astype(F32)

    def body(j, carry):
        m, l, acc = carry
        ksl = pl.ds(pl.multiple_of(j * tk, tk), tk)
        k = k_ref[0, ksl, :]
        v = v_ref[0, ksl, :]
        s = lax.dot_general(qs, k, _NT, preferred_element_type=F32) * (half ** -0.5)
        dist = rel + (qi * tq - j * tk).astype(F32)
        s = s + jnp.where(dist >= 0, -slope * dist, NEG_INF)
        m_new = jnp.maximum(m, jnp.max(s, axis=1, keepdims=True))
        alpha = jnp.exp(m - m_new)
        p = jnp.exp(s - m_new)
        l = alpha * l + jnp.sum(p, axis=1, keepdims=True)
        acc = alpha * acc + jnp.dot(p.astype(BF16), v, preferred_element_type=F32)
        return m_new, l, acc

    n_kv = (qi * tq + tq - 1) // tk + 1
    m0 = jnp.full((2 * tq, 1), NEG_INF, F32)
    l0 = jnp.zeros((2 * tq, 1), F32)
    a0 = jnp.zeros((2 * tq, HEAD_W), F32)
    _, l, acc = lax.fori_loop(0, n_kv, body, (m0, l0, a0))
    o = acc / l
    dy = o[:tq] - lam * o[tq:]
    o_ref[0] = (_rms(dy) * gsub_ref[...] * out_scale).astype(o_ref.dtype)


def _diff_attention(proj3, slopes, diff_lambda_l, gsub, out_scale, tq=128, tk=128):
    b, t, _ = proj3.shape
    h = DIFF_HEADS
    base = 4 * RET_HEADS
    kern = functools.partial(_diff_kernel, tq=tq, tk=tk, out_scale=out_scale)
    return pl.pallas_call(
        kern,
        out_shape=jax.ShapeDtypeStruct((b, t, h * HEAD_W), BF16),
        grid=(b, h, t // tq),
        in_specs=[pl.BlockSpec(memory_space=pltpu.SMEM),
                  pl.BlockSpec(diff_lambda_l.shape, lambda bi, hi, qi: (0, 0)),
                  pl.BlockSpec((1, tq, HEAD_W), lambda bi, hi, qi: (bi, qi, base + hi)),
                  pl.BlockSpec((1, t, HEAD_W), lambda bi, hi, qi: (bi, 0, base + h + hi)),
                  pl.BlockSpec((1, t, HEAD_W), lambda bi, hi, qi: (bi, 0, base + 2 * h + hi)),
                  pl.BlockSpec((1, HEAD_W), lambda bi, hi, qi: (0, 0))],
        out_specs=pl.BlockSpec((1, tq, HEAD_W), lambda bi, hi, qi: (bi, qi, hi)),
        compiler_params=pltpu.CompilerParams(
            dimension_semantics=("parallel", "parallel", "arbitrary"),
            vmem_limit_bytes=VMEM_LIMIT),
        name="diff_attention",
    )(slopes, diff_lambda_l, proj3, proj3, proj3, gsub)


def _outproj_kernel(x_ref, ret_ref, dif_ref, wo_ref, g_ref, wq_ref, x1_ref, xn_ref, q_ref, *, col_chunk):
    half = ret_ref.shape[1]
    mix = jnp.dot(ret_ref[...], wo_ref[:half, :], preferred_element_type=F32)
    mix = mix + jnp.dot(dif_ref[...], wo_ref[half:, :], preferred_element_type=F32)
    x1 = x_ref[...] + mix
    x1_ref[...] = x1
    xn = (_rms(x1) * g_ref[...]).astype(BF16)
    xn_ref[...] = xn
    for j in range(q_ref.shape[1] // col_chunk):
        sl = slice(j * col_chunk, (j + 1) * col_chunk)
        q_ref[:, sl] = jnp.dot(xn, wq_ref[:, sl], preferred_element_type=F32).astype(q_ref.dtype)


def _out_proj(x2, ret2, dif2, wo_bf16, g, wq_bf16, tm=512):
    n, d = x2.shape
    half = ret2.shape[1]
    qc = wq_bf16.shape[1]
    row = lambda w: pl.BlockSpec((tm, w), lambda i: (i, 0))
    full = lambda a: pl.BlockSpec(a.shape, lambda i: (0, 0))
    return pl.pallas_call(
        functools.partial(_outproj_kernel, col_chunk=512),
        out_shape=(jax.ShapeDtypeStruct((n, d), F32),
                   jax.ShapeDtypeStruct((n, d), BF16),
                   jax.ShapeDtypeStruct((n, qc), BF16)),
        grid=(n // tm,),
        in_specs=[row(d), row(half), row(half), full(wo_bf16), full(g), full(wq_bf16)],
        out_specs=(row(d), row(d), row(qc)),
        compiler_params=pltpu.CompilerParams(dimension_semantics=("parallel",),
                                             vmem_limit_bytes=VMEM_LIMIT),
        name="out_proj",
    )(x2, ret2, dif2, wo_bf16, g, wq_bf16)


def _topk_kernel(q_ref, sk_ref, e_ref, g_ref):
    tb = q_ref.shape[0]
    k = PEER_TOPK
    kk = k * k
    neg = -jnp.inf

    key_f = lax.broadcasted_iota(jnp.int32, (tb, N_KEYS), 1).astype(F32)
    cl = lax.broadcasted_iota(jnp.int32, (tb, kk), 1)
    cl_f = cl.astype(F32)
    cl_a = cl // k
    cl_b = cl % k
    ol = lax.broadcasted_iota(jnp.int32, (tb, PEER_HEADS * k), 1)

    def top_keys(s, sel):
        vals = jnp.zeros((tb, kk), F32)
        idxs = jnp.zeros((tb, kk), F32)
        for r in range(k):
            m = jnp.max(s, axis=1, keepdims=True)
            i = jnp.min(jnp.where(s == m, key_f, float(N_KEYS)), axis=1, keepdims=True)
            s = jnp.where(key_f == i, neg, s)
            vals = jnp.where(sel == r, m, vals)
            idxs = jnp.where(sel == r, i, idxs)
        return vals, idxs

    e_out = jnp.zeros((tb, PEER_HEADS * k), F32)
    g_out = jnp.zeros((tb, PEER_HEADS * k), F32)
    for h in range(PEER_HEADS):
        s1 = lax.dot_general(q_ref[:, (2 * h) * N_KEYS:(2 * h + 1) * N_KEYS], sk_ref[h, 0], _NT,
                             preferred_element_type=F32)
        s2 = lax.dot_general(q_ref[:, (2 * h + 1) * N_KEYS:(2 * h + 2) * N_KEYS], sk_ref[h, 1], _NT,
                             preferred_element_type=F32)
        v1, i1 = top_keys(s1, cl_a)
        v2, i2 = top_keys(s2, cl_b)
        cand = v1 + v2
        eid = i1 * float(N_KEYS) + i2
        denom = jnp.zeros((tb, 1), F32)
        m0 = None
        for r in range(k):
            m = jnp.max(cand, axis=1, keepdims=True)
            pos = jnp.min(jnp.where(cand == m, cl_f, float(kk)), axis=1, keepdims=True)
            hit = cl_f == pos
            e = jnp.sum(jnp.where(hit, eid, 0.0), axis=1, keepdims=True)
            cand = jnp.where(hit, neg, cand)
            if r == 0:
                m0 = m
            p = jnp.exp(m - m0)
            denom = denom + p
            here = ol == (h * k + r)
            e_out = jnp.where(here, e, e_out)
            g_out = jnp.where(here, p, g_out)
        g_out = jnp.where(ol // k == h, g_out / denom, g_out)
    e_ref[...] = e_out.astype(jnp.int32)
    g_ref[...] = g_out


def _peer_topk(q2, sk_bf16, tb=128):
    n, qc = q2.shape
    width = PEER_HEADS * PEER_TOPK
    return pl.pallas_call(
        _topk_kernel,
        out_shape=(jax.ShapeDtypeStruct((n, width), jnp.int32),
                   jax.ShapeDtypeStruct((n, width), F32)),
        grid=(n // tb,),
        in_specs=[pl.BlockSpec((tb, qc), lambda i: (i, 0)),
                  pl.BlockSpec(sk_bf16.shape, lambda i: (0, 0, 0, 0))],
        out_specs=(pl.BlockSpec((tb, width), lambda i: (i, 0)),
                   pl.BlockSpec((tb, width), lambda i: (i, 0))),
        compiler_params=pltpu.CompilerParams(dimension_semantics=("parallel",),
                                             vmem_limit_bytes=VMEM_LIMIT),
        name="peer_topk",
    )(q2, sk_bf16)


SUBLANES = 8


def _mix_kernel(e_ref, g_ref, xn_ref, x1_ref, gfin_ref, u_hbm, v_hbm, o_ref, ubuf, vbuf, sem, *, final_norm):
    tb = g_ref.shape[0]
    n_sel = g_ref.shape[1]

    def row_copies(t, k, slot):
        e = e_ref[t, k]
        cu = pltpu.make_async_copy(u_hbm.at[pl.ds(e, 1), :], ubuf.at[slot, pl.ds(k, 1), :], sem.at[0, slot])
        cv = pltpu.make_async_copy(v_hbm.at[pl.ds(e, 1), :], vbuf.at[slot, pl.ds(k, 1), :], sem.at[1, slot])
        return cu, cv

    def issue(t, slot):
        def body(k, c):
            cu, cv = row_copies(t, k, slot)
            cu.start()
            cv.start()
            return c
        lax.fori_loop(0, n_sel, body, 0, unroll=8)

    def wait(slot):
        pltpu.make_async_copy(u_hbm.at[pl.ds(0, n_sel), :], ubuf.at[slot], sem.at[0, slot]).wait()
        pltpu.make_async_copy(v_hbm.at[pl.ds(0, n_sel), :], vbuf.at[slot], sem.at[1, slot]).wait()

    rid = lax.broadcasted_iota(jnp.int32, (SUBLANES, n_sel), 0)

    issue(0, 0)

    def group(gi, c):
        base = pl.multiple_of(gi * SUBLANES, SUBLANES)
        xrows = xn_ref[pl.ds(base, SUBLANES), :]
        grows = g_ref[pl.ds(base, SUBLANES), :]
        acc = x1_ref[pl.ds(base, SUBLANES), :]
        for j in range(SUBLANES):
            t = base + j
            slot = lax.rem(t, 2)

            @pl.when(t + 1 < tb)
            def _():
                issue(t + 1, 1 - slot)

            wait(slot)
            s = lax.dot_general(xrows, ubuf[slot].astype(BF16), _NT, preferred_element_type=F32)
            w = jnp.where(rid == j, grows * _gelu_exact(s), 0.0)
            acc = acc + jnp.dot(w.astype(BF16), vbuf[slot].astype(BF16), preferred_element_type=F32)
        o_ref[pl.ds(base, SUBLANES), :] = acc
        return c

    lax.fori_loop(0, tb // SUBLANES, group, 0)
    if final_norm:
        o_ref[...] = _rms(o_ref[...]) * gfin_ref[...]


def _peer_mix(eids, gates, xn, x1, gfin, u, v, final_norm, tb=64):
    n, d = x1.shape
    n_sel = eids.shape[1]
    row = lambda w: pl.BlockSpec((tb, w), lambda i: (i, 0))
    return pl.pallas_call(
        functools.partial(_mix_kernel, final_norm=final_norm),
        out_shape=jax.ShapeDtypeStruct((n, d), F32),
        grid=(n // tb,),
        in_specs=[pl.BlockSpec((tb, n_sel), lambda i: (i, 0), memory_space=pltpu.SMEM),
                  row(n_sel), row(d), row(d),
                  pl.BlockSpec((1, d), lambda i: (0, 0)),
                  pl.BlockSpec(memory_space=pl.ANY),
                  pl.BlockSpec(memory_space=pl.ANY)],
        out_specs=row(d),
        scratch_shapes=[pltpu.VMEM((2, n_sel, d), u.dtype),
                        pltpu.VMEM((2, n_sel, d), v.dtype),
                        pltpu.SemaphoreType.DMA((2, 2))],
        compiler_params=pltpu.CompilerParams(dimension_semantics=("arbitrary",),
                                             vmem_limit_bytes=VMEM_LIMIT),
        name="peer_mix",
    )(eids, gates, xn, x1, gfin, u, v)


def _retention_tables(dk):
    c = RET_CHUNK
    h = RET_HEADS
    log_gamma = jnp.log1p(-jnp.exp2(-5.0 - jnp.arange(h, dtype=F32)))
    idx = jnp.arange(c, dtype=F32)
    rel = idx[:, None] - idx[None, :]
    intra = jnp.where(rel >= 0, jnp.exp(log_gamma[:, None, None] * jnp.maximum(rel, 0.0)), 0.0)
    cross = jnp.exp(log_gamma[:, None] * (idx + 1.0))
    kdec = jnp.exp(log_gamma[:, None] * (c - 1.0 - idx))
    cdec = jnp.exp(log_gamma * c)
    scale = dk ** -0.5
    return intra * scale, cross[:, :, None], (kdec * scale)[:, :, None], cdec[:, None, None]


def kernel(x, norm_mix_g, w_in, diff_lambda, diff_subln_g, w_out, norm_ffn_g, peer_wq, peer_subkeys,
           peer_u, peer_v, final_norm_g):
    b, t, d = x.shape
    depth = w_in.shape[0]
    n = b * t
    x2 = x.reshape(n, d)
    intra, cross, kdec, cdec = _retention_tables(HEAD_W)
    slopes = jnp.exp2(-8.0 * (jnp.arange(DIFF_HEADS, dtype=F32) + 1.0) / DIFF_HEADS)
    for l in range(depth):
        li = 0.8 - 0.6 * math.exp(-0.3 * l)
        proj = _in_proj(x2, norm_mix_g[l][None, :], w_in[l].astype(BF16))
        proj3 = proj.reshape(b, t, proj.shape[1])
        ret = _retention(proj3, intra, cross, kdec, cdec)
        dif = _diff_attention(proj3, slopes, diff_lambda[l], diff_subln_g[l][None, :], 1.0 - li)
        half = ret.shape[2]
        x1, xn, q = _out_proj(x2, ret.reshape(n, half), dif.reshape(n, half), w_out[l].astype(BF16),
                              norm_ffn_g[l][None, :], peer_wq[l].astype(BF16))
        eids, gates = _peer_topk(q, peer_subkeys[l].astype(BF16))
        x2 = _peer_mix(eids, gates, xn, x1, final_norm_g[None, :], peer_u[l], peer_v[l],
                       final_norm=(l == depth - 1))
    return x2.reshape(b, t, d)
```

```python
import functools
import math

import jax
import jax.numpy as jnp
from jax import lax
from jax.experimental import pallas as pl
from jax.experimental.pallas import tpu as pltpu

F32 = jnp.float32
BF16 = jnp.bfloat16

RMS_EPS = 1e-6
NEG_INF = -1e30

RET_HEADS = 4
RET_CHUNK = 128
DIFF_HEADS = 4
HEAD_W = 128
PEER_HEADS = 8
PEER_TOPK = 16
N_KEYS = 128
N_SEL = PEER_HEADS * PEER_TOPK

SUBLANES = 8
VMEM_LIMIT = 56 * 1024 * 1024

_NT = (((1,), (1,)), ((), ()))


def _rms(x, eps=RMS_EPS):
    return x * lax.rsqrt(jnp.mean(x * x, axis=-1, keepdims=True) + eps)


def _gelu_exact(x):
    return 0.5 * x * (1.0 + lax.erf(x * (2.0 ** -0.5)))


def _inproj_kernel(x_ref, g_ref, w_ref, o_ref, *, col_chunk):
    h = (_rms(x_ref[...]) * g_ref[...]).astype(BF16)
    for j in range(o_ref.shape[1] // col_chunk):
        sl = slice(j * col_chunk, (j + 1) * col_chunk)
        o_ref[:, sl] = jnp.dot(h, w_ref[:, sl], preferred_element_type=F32).astype(o_ref.dtype)


def _in_proj(x2, g, w_bf16, tm=512):
    n, d = x2.shape
    cols = w_bf16.shape[1]
    return pl.pallas_call(
        functools.partial(_inproj_kernel, col_chunk=512),
        out_shape=jax.ShapeDtypeStruct((n, cols), BF16),
        grid=(n // tm,),
        in_specs=[pl.BlockSpec((tm, d), lambda i: (i, 0)),
                  pl.BlockSpec((1, d), lambda i: (0, 0)),
                  pl.BlockSpec((d, cols), lambda i: (0, 0))],
        out_specs=pl.BlockSpec((tm, cols), lambda i: (i, 0)),
        compiler_params=pltpu.CompilerParams(dimension_semantics=("parallel",),
                                             vmem_limit_bytes=VMEM_LIMIT),
        name="in_proj",
    )(x2, g, w_bf16)


def _ret_kernel(q_ref, k_ref, v_ref, g_ref, intra_ref, cross_ref, kdec_ref, cdec_ref, o_ref):
    c = RET_CHUNK
    n_chunks = q_ref.shape[1] // c
    intra = intra_ref[0]
    cross = cross_ref[0]
    kdec = kdec_ref[0]
    cdec = cdec_ref[0]

    def body(n, s):
        sl = pl.ds(pl.multiple_of(n * c, c), c)
        qi = q_ref[0, sl, :]
        ki = k_ref[0, sl, :]
        vi = v_ref[0, sl, :]
        a = lax.dot_general(qi, ki, _NT, preferred_element_type=F32) * intra
        o = jnp.dot(a.astype(BF16), vi, preferred_element_type=F32)
        o = o + jnp.dot(qi, s.astype(BF16), preferred_element_type=F32) * cross
        kd_t = (ki.astype(F32) * kdec).T.astype(BF16)
        s = s * cdec + jnp.dot(kd_t, vi, preferred_element_type=F32)
        gate = g_ref[0, sl, :].astype(F32)
        o_ref[0, sl, :] = (gate * jax.nn.sigmoid(gate) * _rms(o)).astype(o_ref.dtype)
        return s

    lax.fori_loop(0, n_chunks, body, jnp.zeros((HEAD_W, HEAD_W), F32))


def _retention(proj3, intra, cross, kdec, cdec):
    b, t, _ = proj3.shape
    h = RET_HEADS

    def col(off):
        return pl.BlockSpec((1, t, HEAD_W), lambda bi, hi: (bi, 0, off + hi))

    def per_head(shape):
        return pl.BlockSpec((1,) + shape, lambda bi, hi: (hi, 0, 0))

    return pl.pallas_call(
        _ret_kernel,
        out_shape=jax.ShapeDtypeStruct((b, t, h * HEAD_W), BF16),
        grid=(b, h),
        in_specs=[col(0), col(h), col(2 * h), col(3 * h),
                  per_head((RET_CHUNK, RET_CHUNK)), per_head((RET_CHUNK, 1)),
                  per_head((RET_CHUNK, 1)), per_head((1, 1))],
        out_specs=pl.BlockSpec((1, t, HEAD_W), lambda bi, hi: (bi, 0, hi)),
        compiler_params=pltpu.CompilerParams(dimension_semantics=("parallel", "parallel"),
                                             vmem_limit_bytes=VMEM_LIMIT),
        name="retention",
    )(proj3, proj3, proj3, proj3, intra, cross, kdec, cdec)


def _diff_kernel(slope_ref, lam_ref, q_ref, k_ref, v_ref, gsub_ref, o_ref, *, tq, tk, out_scale):
    hi = pl.program_id(1)
    qi = pl.program_id(2)
    half = HEAD_W // 2
    slope = slope_ref[hi]

    lp = lam_ref[...]
    lam = (jnp.exp(jnp.sum(lp[0:1] * lp[1:2], axis=1, keepdims=True))
           - jnp.exp(jnp.sum(lp[2:3] * lp[3:4], axis=1, keepdims=True))
           + (1.0 - out_scale))

    q = q_ref[0]
    lane = lax.broadcasted_iota(jnp.int32, q.shape, 1)
    zero = jnp.zeros_like(q)
    qs = jnp.concatenate([jnp.where(lane < half, q, zero),
                          jnp.where(lane >= half, q, zero)], axis=0)

    row = lax.broadcasted_iota(jnp.int32, (2 * tq, tk), 0)
    row = jnp.where(row >= tq, row - tq, row)
    col = lax.broadcasted_iota(jnp.int32, (2 * tq, tk), 1)
    rel = (row - col).astype(F32)

    def body(j, carry):
        m, l, acc = carry
        ksl = pl.ds(pl.multiple_of(j * tk, tk), tk)
        k = k_ref[0, ksl, :]
        v = v_ref[0, ksl, :]
        s = lax.dot_general(qs, k, _NT, preferred_element_type=F32) * (half ** -0.5)
        dist = rel + (qi * tq - j * tk).astype(F32)
        s = s + jnp.where(dist >= 0, -slope * dist, NEG_INF)
        m_new = jnp.maximum(m, jnp.max(s, axis=1, keepdims=True))
        alpha = jnp.exp(m - m_new)
        p = jnp.exp(s - m_new)
        l = alpha * l + jnp.sum(p, axis=1, keepdims=True)
        acc = alpha * acc + jnp.dot(p.astype(BF16), v, preferred_element_type=F32)
        return m_new, l, acc

    n_kv = (qi * tq + tq - 1) // tk + 1
    m0 = jnp.full((2 * tq, 1), NEG_INF, F32)
    l0 = jnp.zeros((2 * tq, 1), F32)
    a0 = jnp.zeros((2 * tq, HEAD_W), F32)
    _, l, acc = lax.fori_loop(0, n_kv, body, (m0, l0, a0))
    o = acc / l
    dy = o[:tq] - lam * o[tq:]
    o_ref[0] = (_rms(dy) * gsub_ref[...] * out_scale).astype(o_ref.dtype)


def _diff_attention(proj3, slopes, diff_lambda_l, gsub, out_scale, tq=128, tk=128):
    b, t, _ = proj3.shape
    h = DIFF_HEADS
    base = 4 * RET_HEADS
    kern = functools.partial(_diff_kernel, tq=tq, tk=tk, out_scale=out_scale)
    return pl.pallas_call(
        kern,
        out_shape=jax.ShapeDtypeStruct((b, t, h * HEAD_W), BF16),
        grid=(b, h, t // tq),
        in_specs=[pl.BlockSpec(memory_space=pltpu.SMEM),
                  pl.BlockSpec(diff_lambda_l.shape, lambda bi, hi, qi: (0, 0)),
                  pl.BlockSpec((1, tq, HEAD_W), lambda bi, hi, qi: (bi, qi, base + hi)),
                  pl.BlockSpec((1, t, HEAD_W), lambda bi, hi, qi: (bi, 0, base + h + hi)),
                  pl.BlockSpec((1, t, HEAD_W), lambda bi, hi, qi: (bi, 0, base + 2 * h + hi)),
                  pl.BlockSpec((1, HEAD_W), lambda bi, hi, qi: (0, 0))],
        out_specs=pl.BlockSpec((1, tq, HEAD_W), lambda bi, hi, qi: (bi, qi, hi)),
        compiler_params=pltpu.CompilerParams(
            dimension_semantics=("parallel", "parallel", "arbitrary"),
            vmem_limit_bytes=VMEM_LIMIT),
        name="diff_attention",
    )(slopes, diff_lambda_l, proj3, proj3, proj3, gsub)


def _outproj_kernel(x_ref, ret_ref, dif_ref, wo_ref, g_ref, wq_ref, x1_ref, xn_ref, q_ref, *, col_chunk):
    half = ret_ref.shape[1]
    mix = jnp.dot(ret_ref[...], wo_ref[:half, :], preferred_element_type=F32)
    mix = mix + jnp.dot(dif_ref[...], wo_ref[half:, :], preferred_element_type=F32)
    x1 = x_ref[...] + mix
    x1_ref[...] = x1
    xn = (_rms(x1) * g_ref[...]).astype(BF16)
    xn_ref[...] = xn
    for j in range(q_ref.shape[1] // col_chunk):
        sl = slice(j * col_chunk, (j + 1) * col_chunk)
        q_ref[:, sl] = jnp.dot(xn, wq_ref[:, sl], preferred_element_type=F32).astype(q_ref.dtype)


def _out_proj(x2, ret2, dif2, wo_bf16, g, wq_bf16, tm=512):
    n, d = x2.shape
    half = ret2.shape[1]
    qc = wq_bf16.shape[1]
    row = lambda w: pl.BlockSpec((tm, w), lambda i: (i, 0))
    full = lambda a: pl.BlockSpec(a.shape, lambda i: (0, 0))
    return pl.pallas_call(
        functools.partial(_outproj_kernel, col_chunk=512),
        out_shape=(jax.ShapeDtypeStruct((n, d), F32),
                   jax.ShapeDtypeStruct((n, d), BF16),
                   jax.ShapeDtypeStruct((n, qc), BF16)),
        grid=(n // tm,),
        in_specs=[row(d), row(half), row(half), full(wo_bf16), full(g), full(wq_bf16)],
        out_specs=(row(d), row(d), row(qc)),
        compiler_params=pltpu.CompilerParams(dimension_semantics=("parallel",),
                                             vmem_limit_bytes=VMEM_LIMIT),
        name="out_proj",
    )(x2, ret2, dif2, wo_bf16, g, wq_bf16)


def _spread_exact(v, sel_bf16):
    hi = v.astype(BF16)
    r1 = v - hi.astype(F32)
    mid = r1.astype(BF16)
    lo = (r1 - mid.astype(F32)).astype(BF16)
    d = lambda a: jnp.dot(a, sel_bf16, preferred_element_type=F32)
    return (d(hi) + d(mid)) + d(lo)


def _topk_kernel(q_ref, sk_ref, e_ref, g2_ref):
    tb = q_ref.shape[0]
    k = PEER_TOPK
    kk = k * k
    n_sets = 2 * PEER_HEADS
    neg = -jnp.inf

    s = jnp.concatenate(
        [lax.dot_general(q_ref[:, j * N_KEYS:(j + 1) * N_KEYS], sk_ref[j // 2, j % 2], _NT,
                         preferred_element_type=F32) for j in range(n_sets)], axis=0)
    lane = lax.broadcasted_iota(jnp.int32, s.shape, 1)
    key_f = lane.astype(F32)
    v16 = jnp.zeros(s.shape, F32)
    i16 = jnp.zeros(s.shape, F32)
    for r in range(k):
        m = jnp.max(s, axis=1, keepdims=True)
        i = jnp.min(jnp.where(s == m, key_f, float(N_KEYS)), axis=1, keepdims=True)
        s = jnp.where(key_f == i, neg, s)
        v16 = jnp.where(lane == r, m, v16)
        i16 = jnp.where(lane == r, i, i16)

    sr = lax.broadcasted_iota(jnp.int32, (N_KEYS, kk), 0)
    sc = lax.broadcasted_iota(jnp.int32, (N_KEYS, kk), 1)
    sel_a = jnp.where(sc // k == sr, 1.0, 0.0).astype(BF16)
    sel_b = jnp.where(sc % k == sr, 1.0, 0.0).astype(BF16)
    cands, eids = [], []
    for h in range(PEER_HEADS):
        lo, mid, hi = 2 * h * tb, (2 * h + 1) * tb, (2 * h + 2) * tb
        cands.append(_spread_exact(v16[lo:mid], sel_a) + _spread_exact(v16[mid:hi], sel_b))
        eids.append(jnp.dot(i16[lo:mid].astype(BF16), sel_a, preferred_element_type=F32) * float(N_KEYS)
                    + jnp.dot(i16[mid:hi].astype(BF16), sel_b, preferred_element_type=F32))
    cand = jnp.concatenate(cands, axis=0)
    eid = jnp.concatenate(eids, axis=0)

    cl_f = lax.broadcasted_iota(jnp.int32, cand.shape, 1).astype(F32)
    top_c, top_e = [], []
    for r in range(k):
        m = jnp.max(cand, axis=1, keepdims=True)
        pos = jnp.min(jnp.where(cand == m, cl_f, float(kk)), axis=1, keepdims=True)
        hit = cl_f == pos
        top_e.append(jnp.sum(jnp.where(hit, eid, 0.0), axis=1, keepdims=True))
        cand = jnp.where(hit, neg, cand)
        top_c.append(m)
    ps = [jnp.exp(c - top_c[0]) for c in top_c]
    denom = ps[0]
    for p in ps[1:]:
        denom = denom + p

    ol = lax.broadcasted_iota(jnp.int32, (tb, N_SEL), 1)
    ol2 = lax.broadcasted_iota(jnp.int32, (tb, 2 * N_SEL), 1)
    e_out = jnp.zeros((tb, N_SEL), F32)
    g2 = jnp.zeros((tb, 2 * N_SEL), F32)
    for h in range(PEER_HEADS):
        rows = slice(h * tb, (h + 1) * tb)
        for r in range(k):
            c = h * k + r
            e_out = jnp.where(ol == c, top_e[r][rows], e_out)
            g2 = jnp.where(ol2 == 2 * c + 1, (ps[r] / denom)[rows], g2)
    e_ref[...] = e_out.astype(jnp.int32)
    g2_ref[...] = g2


def _peer_topk(q2, sk_bf16, tb=128):
    n, qc = q2.shape
    return pl.pallas_call(
        _topk_kernel,
        out_shape=(jax.ShapeDtypeStruct((n, N_SEL), jnp.int32),
                   jax.ShapeDtypeStruct((n, 2 * N_SEL), F32)),
        grid=(n // tb,),
        in_specs=[pl.BlockSpec((tb, qc), lambda i: (i, 0)),
                  pl.BlockSpec(sk_bf16.shape, lambda i: (0, 0, 0, 0))],
        out_specs=(pl.BlockSpec((tb, N_SEL), lambda i: (i, 0)),
                   pl.BlockSpec((tb, 2 * N_SEL), lambda i: (i, 0))),
        compiler_params=pltpu.CompilerParams(dimension_semantics=("parallel",),
                                             vmem_limit_bytes=VMEM_LIMIT),
        name="peer_topk",
    )(q2, sk_bf16)


def _pack_tables(u, v):
    ub = lax.bitcast_convert_type(u.astype(BF16), jnp.uint16).astype(jnp.uint32)
    vb = lax.bitcast_convert_type(v.astype(BF16), jnp.uint16).astype(jnp.uint32)
    e, d = u.shape
    return (ub | (vb << 16)).reshape(e, d // HEAD_W, 1, HEAD_W)


def _mix_kernel(efirst_ref, enext_ref, g2_ref, xn_ref, x1_ref, gfin_ref, tab_hbm, o_ref, buf, sem, *, final_norm):
    step = pl.program_id(0)
    tb = xn_ref.shape[0]
    n_chunks = buf.shape[1]

    def issue(e_ref, t, slot):
        for k in range(N_SEL):
            e = e_ref[t, k]
            pltpu.make_async_copy(tab_hbm.at[e], buf.at[slot, :, pl.ds(k, 1), :],
                                  sem.at[slot]).start(priority=k % 2)

    def wait(slot):
        pltpu.make_async_copy(buf.at[slot], buf.at[slot], sem.at[slot]).wait()

    @pl.when(step == 0)
    def _():
        issue(efirst_ref, 0, 0)

    rid = lax.broadcasted_iota(jnp.int32, (SUBLANES, 2 * N_SEL), 0)

    def group(gi, c):
        base = pl.multiple_of(gi * SUBLANES, SUBLANES)
        xrows = xn_ref[pl.ds(base, SUBLANES), :]
        grows = g2_ref[pl.ds(base, SUBLANES), :]
        acc = x1_ref[pl.ds(base, SUBLANES), :]
        for j in range(SUBLANES):
            slot = j % 2
            issue(enext_ref, base + j, 1 - slot)
            wait(slot)
            uv = jnp.concatenate([pltpu.bitcast(buf[slot, c], BF16) for c in range(n_chunks)], axis=1)
            s2 = lax.dot_general(xrows, uv, _NT, preferred_element_type=F32)
            act = pltpu.roll(_gelu_exact(s2), 1, axis=1)
            w2 = jnp.where(rid == j, grows * act, 0.0)
            acc = acc + jnp.dot(w2.astype(BF16), uv, preferred_element_type=F32)
        o_ref[pl.ds(base, SUBLANES), :] = acc
        return c

    lax.fori_loop(0, tb // SUBLANES, group, 0)

    @pl.when(step == pl.num_programs(0) - 1)
    def _():
        wait(0)

    if final_norm:
        o_ref[...] = _rms(o_ref[...]) * gfin_ref[...]


def _peer_mix(eids, gates2, xn, x1, gfin, table, final_norm, tb=64):
    n, d = x1.shape
    assert tb % (2 * SUBLANES) == 0
    e_next = jnp.roll(eids, -1, axis=0)
    row = lambda w: pl.BlockSpec((tb, w), lambda i: (i, 0))
    return pl.pallas_call(
        functools.partial(_mix_kernel, final_norm=final_norm),
        out_shape=jax.ShapeDtypeStruct((n, d), F32),
        grid=(n // tb,),
        in_specs=[pl.BlockSpec((SUBLANES, N_SEL), lambda i: (0, 0), memory_space=pltpu.SMEM),
                  pl.BlockSpec((tb, N_SEL), lambda i: (i, 0), memory_space=pltpu.SMEM),
                  row(2 * N_SEL), row(d), row(d),
                  pl.BlockSpec((1, d), lambda i: (0, 0)),
                  pl.BlockSpec(memory_space=pl.ANY)],
        out_specs=row(d),
        scratch_shapes=[pltpu.VMEM((2, d // HEAD_W, N_SEL, HEAD_W), jnp.uint32),
                        pltpu.SemaphoreType.DMA((2,))],
        compiler_params=pltpu.CompilerParams(dimension_semantics=("arbitrary",),
                                             vmem_limit_bytes=VMEM_LIMIT),
        name="peer_mix",
    )(eids, e_next, gates2, xn, x1, gfin, table)


def _retention_tables(dk):
    c = RET_CHUNK
    h = RET_HEADS
    log_gamma = jnp.log1p(-jnp.exp2(-5.0 - jnp.arange(h, dtype=F32)))
    idx = jnp.arange(c, dtype=F32)
    rel = idx[:, None] - idx[None, :]
    intra = jnp.where(rel >= 0, jnp.exp(log_gamma[:, None, None] * jnp.maximum(rel, 0.0)), 0.0)
    cross = jnp.exp(log_gamma[:, None] * (idx + 1.0))
    kdec = jnp.exp(log_gamma[:, None] * (c - 1.0 - idx))
    cdec = jnp.exp(log_gamma * c)
    scale = dk ** -0.5
    return intra * scale, cross[:, :, None], (kdec * scale)[:, :, None], cdec[:, None, None]


def kernel(x, norm_mix_g, w_in, diff_lambda, diff_subln_g, w_out, norm_ffn_g, peer_wq, peer_subkeys,
           peer_u, peer_v, final_norm_g):
    b, t, d = x.shape
    depth = w_in.shape[0]
    n = b * t
    x2 = x.reshape(n, d)
    intra, cross, kdec, cdec = _retention_tables(HEAD_W)
    slopes = jnp.exp2(-8.0 * (jnp.arange(DIFF_HEADS, dtype=F32) + 1.0) / DIFF_HEADS)
    for l in range(depth):
        li = 0.8 - 0.6 * math.exp(-0.3 * l)
        proj = _in_proj(x2, norm_mix_g[l][None, :], w_in[l].astype(BF16))
        proj3 = proj.reshape(b, t, proj.shape[1])
        ret = _retention(proj3, intra, cross, kdec, cdec)
        dif = _diff_attention(proj3, slopes, diff_lambda[l], diff_subln_g[l][None, :], 1.0 - li)
        half = ret.shape[2]
        x1, xn, q = _out_proj(x2, ret.reshape(n, half), dif.reshape(n, half), w_out[l].astype(BF16),
                              norm_ffn_g[l][None, :], peer_wq[l].astype(BF16))
        eids, gates2 = _peer_topk(q, peer_subkeys[l].astype(BF16))
        x2 = _peer_mix(eids, gates2, xn, x1, final_norm_g[None, :], _pack_tables(peer_u[l], peer_v[l]),
                       final_norm=(l == depth - 1))
    return x2.reshape(b, t, d)
```

```python
import functools
import math

import jax
import jax.numpy as jnp
from jax import lax
from jax.experimental import pallas as pl
from jax.experimental.pallas import tpu as pltpu

F32 = jnp.float32
BF16 = jnp.bfloat16

RMS_EPS = 1e-6
NEG_INF = -1e30

RET_HEADS = 4
RET_CHUNK = 128
DIFF_HEADS = 4
HEAD_W = 128
PEER_HEADS = 8
PEER_TOPK = 16
N_KEYS = 128
N_SEL = PEER_HEADS * PEER_TOPK

SUBLANES = 8
VMEM_LIMIT = 56 * 1024 * 1024

_NT = (((1,), (1,)), ((), ()))


def _rms(x, eps=RMS_EPS):
    return x * lax.rsqrt(jnp.mean(x * x, axis=-1, keepdims=True) + eps)


def _gelu_exact(x):
    return 0.5 * x * (1.0 + lax.erf(x * (2.0 ** -0.5)))


def _inproj_kernel(x_ref, g_ref, w_ref, o_ref, *, col_chunk):
    h = (_rms(x_ref[...]) * g_ref[...]).astype(BF16)
    for j in range(o_ref.shape[1] // col_chunk):
        sl = slice(j * col_chunk, (j + 1) * col_chunk)
        o_ref[:, sl] = jnp.dot(h, w_ref[:, sl], preferred_element_type=F32).astype(o_ref.dtype)


def _in_proj(x2, g, w_bf16, tm=512):
    n, d = x2.shape
    cols = w_bf16.shape[1]
    return pl.pallas_call(
        functools.partial(_inproj_kernel, col_chunk=512),
        out_shape=jax.ShapeDtypeStruct((n, cols), BF16),
        grid=(n // tm,),
        in_specs=[pl.BlockSpec((tm, d), lambda i: (i, 0)),
                  pl.BlockSpec((1, d), lambda i: (0, 0)),
                  pl.BlockSpec((d, cols), lambda i: (0, 0))],
        out_specs=pl.BlockSpec((tm, cols), lambda i: (i, 0)),
        compiler_params=pltpu.CompilerParams(dimension_semantics=("parallel",),
                                             vmem_limit_bytes=VMEM_LIMIT),
        name="in_proj",
    )(x2, g, w_bf16)


def _ret_kernel(q_ref, k_ref, v_ref, g_ref, intra_ref, cross_ref, kdec_ref, cdec_ref, o_ref):
    c = RET_CHUNK
    n_chunks = q_ref.shape[1] // c
    intra = intra_ref[0]
    cross = cross_ref[0]
    kdec = kdec_ref[0]
    cdec = cdec_ref[0]

    def body(n, s):
        sl = pl.ds(pl.multiple_of(n * c, c), c)
        qi = q_ref[0, sl, :]
        ki = k_ref[0, sl, :]
        vi = v_ref[0, sl, :]
        a = lax.dot_general(qi, ki, _NT, preferred_element_type=F32) * intra
        o = jnp.dot(a.astype(BF16), vi, preferred_element_type=F32)
        o = o + jnp.dot(qi, s.astype(BF16), preferred_element_type=F32) * cross
        kd_t = (ki.astype(F32) * kdec).T.astype(BF16)
        s = s * cdec + jnp.dot(kd_t, vi, preferred_element_type=F32)
        gate = g_ref[0, sl, :].astype(F32)
        o_ref[0, sl, :] = (gate * jax.nn.sigmoid(gate) * _rms(o)).astype(o_ref.dtype)
        return s

    lax.fori_loop(0, n_chunks, body, jnp.zeros((HEAD_W, HEAD_W), F32))


def _retention(proj3, intra, cross, kdec, cdec):
    b, t, _ = proj3.shape
    h = RET_HEADS

    def col(off):
        return pl.BlockSpec((1, t, HEAD_W), lambda bi, hi: (bi, 0, off + hi))

    def per_head(shape):
        return pl.BlockSpec((1,) + shape, lambda bi, hi: (hi, 0, 0))

    return pl.pallas_call(
        _ret_kernel,
        out_shape=jax.ShapeDtypeStruct((b, t, h * HEAD_W), BF16),
        grid=(b, h),
        in_specs=[col(0), col(h), col(2 * h), col(3 * h),
                  per_head((RET_CHUNK, RET_CHUNK)), per_head((RET_CHUNK, 1)),
                  per_head((RET_CHUNK, 1)), per_head((1, 1))],
        out_specs=pl.BlockSpec((1, t, HEAD_W), lambda bi, hi: (bi, 0, hi)),
        compiler_params=pltpu.CompilerParams(dimension_semantics=("parallel", "parallel"),
                                             vmem_limit_bytes=VMEM_LIMIT),
        name="retention",
    )(proj3, proj3, proj3, proj3, intra, cross, kdec, cdec)


def _diff_kernel(slope_ref, lam_ref, q_ref, k_ref, v_ref, gsub_ref, o_ref, *, tq, tk, out_scale):
    hi = pl.program_id(1)
    qi = pl.program_id(2)
    half = HEAD_W // 2
    slope = slope_ref[hi]

    lp = lam_ref[...]
    lam = (jnp.exp(jnp.sum(lp[0:1] * lp[1:2], axis=1, keepdims=True))
           - jnp.exp(jnp.sum(lp[2:3] * lp[3:4], axis=1, keepdims=True))
           + (1.0 - out_scale))

    q = q_ref[0]
    lane = lax.broadcasted_iota(jnp.int32, q.shape, 1)
    zero = jnp.zeros_like(q)
    qs = jnp.concatenate([jnp.where(lane < half, q, zero),
                          jnp.where(lane >= half, q, zero)], axis=0)

    row = lax.broadcasted_iota(jnp.int32, (2 * tq, tk), 0)
    row = jnp.where(row >= tq, row - tq, row)
    col = lax.broadcasted_iota(jnp.int32, (2 * tq, tk), 1)
    rel = (row - col).astype(F32)

    def body(j, carry):
        m, l, acc = carry
        ksl = pl.ds(pl.multiple_of(j * tk, tk), tk)
        k = k_ref[0, ksl, :]
        v = v_ref[0, ksl, :]
        s = lax.dot_general(qs, k, _NT, preferred_element_type=F32) * (half ** -0.5)
        dist = rel + (qi * tq - j * tk).astype(F32)
        s = s + jnp.where(dist >= 0, -slope * dist, NEG_INF)
        m_new = jnp.maximum(m, jnp.max(s, axis=1, keepdims=True))
        alpha = jnp.exp(m - m_new)
        p = jnp.exp(s - m_new)
        l = alpha * l + jnp.sum(p, axis=1, keepdims=True)
        acc = alpha * acc + jnp.dot(p.astype(BF16), v, preferred_element_type=F32)
        return m_new, l, acc

    n_kv = (qi * tq + tq - 1) // tk + 1
    m0 = jnp.full((2 * tq, 1), NEG_INF, F32)
    l0 = jnp.zeros((2 * tq, 1), F32)
    a0 = jnp.zeros((2 * tq, HEAD_W), F32)
    _, l, acc = lax.fori_loop(0, n_kv, body, (m0, l0, a0))
    o = acc / l
    dy = o[:tq] - lam * o[tq:]
    o_ref[0] = (_rms(dy) * gsub_ref[...] * out_scale).astype(o_ref.dtype)


def _diff_attention(proj3, slopes, diff_lambda_l, gsub, out_scale, tq=128, tk=128):
    b, t, _ = proj3.shape
    h = DIFF_HEADS
    base = 4 * RET_HEADS
    kern = functools.partial(_diff_kernel, tq=tq, tk=tk, out_scale=out_scale)
    return pl.pallas_call(
        kern,
        out_shape=jax.ShapeDtypeStruct((b, t, h * HEAD_W), BF16),
        grid=(b, h, t // tq),
        in_specs=[pl.BlockSpec(memory_space=pltpu.SMEM),
                  pl.BlockSpec(diff_lambda_l.shape, lambda bi, hi, qi: (0, 0)),
                  pl.BlockSpec((1, tq, HEAD_W), lambda bi, hi, qi: (bi, qi, base + hi)),
                  pl.BlockSpec((1, t, HEAD_W), lambda bi, hi, qi: (bi, 0, base + h + hi)),
                  pl.BlockSpec((1, t, HEAD_W), lambda bi, hi, qi: (bi, 0, base + 2 * h + hi)),
                  pl.BlockSpec((1, HEAD_W), lambda bi, hi, qi: (0, 0))],
        out_specs=pl.BlockSpec((1, tq, HEAD_W), lambda bi, hi, qi: (bi, qi, hi)),
        compiler_params=pltpu.CompilerParams(
            dimension_semantics=("parallel", "parallel", "arbitrary"),
            vmem_limit_bytes=VMEM_LIMIT),
        name="diff_attention",
    )(slopes, diff_lambda_l, proj3, proj3, proj3, gsub)


def _outproj_kernel(x_ref, ret_ref, dif_ref, wo_ref, g_ref, wq_ref, x1_ref, xn_ref, q_ref, *, col_chunk):
    half = ret_ref.shape[1]
    mix = jnp.dot(ret_ref[...], wo_ref[:half, :], preferred_element_type=F32)
    mix = mix + jnp.dot(dif_ref[...], wo_ref[half:, :], preferred_element_type=F32)
    x1 = x_ref[...] + mix
    x1_ref[...] = x1
    xn = (_rms(x1) * g_ref[...]).astype(BF16)
    xn_ref[...] = xn
    for j in range(q_ref.shape[1] // col_chunk):
        sl = slice(j * col_chunk, (j + 1) * col_chunk)
        q_ref[:, sl] = jnp.dot(xn, wq_ref[:, sl], preferred_element_type=F32).astype(q_ref.dtype)


def _out_proj(x2, ret2, dif2, wo_bf16, g, wq_bf16, tm=512):
    n, d = x2.shape
    half = ret2.shape[1]
    qc = wq_bf16.shape[1]
    row = lambda w: pl.BlockSpec((tm, w), lambda i: (i, 0))
    full = lambda a: pl.BlockSpec(a.shape, lambda i: (0, 0))
    return pl.pallas_call(
        functools.partial(_outproj_kernel, col_chunk=512),
        out_shape=(jax.ShapeDtypeStruct((n, d), F32),
                   jax.ShapeDtypeStruct((n, d), BF16),
                   jax.ShapeDtypeStruct((n, qc), BF16)),
        grid=(n // tm,),
        in_specs=[row(d), row(half), row(half), full(wo_bf16), full(g), full(wq_bf16)],
        out_specs=(row(d), row(d), row(qc)),
        compiler_params=pltpu.CompilerParams(dimension_semantics=("parallel",),
                                             vmem_limit_bytes=VMEM_LIMIT),
        name="out_proj",
    )(x2, ret2, dif2, wo_bf16, g, wq_bf16)


def _spread_exact(v, sel_bf16):
    hi = v.astype(BF16)
    r1 = v - hi.astype(F32)
    mid = r1.astype(BF16)
    lo = (r1 - mid.astype(F32)).astype(BF16)
    d = lambda a: jnp.dot(a, sel_bf16, preferred_element_type=F32)
    return (d(hi) + d(mid)) + d(lo)


def _topk_kernel(q_ref, sk_ref, e_ref, g2_ref):
    tb = q_ref.shape[0]
    k = PEER_TOPK
    kk = k * k
    n_sets = 2 * PEER_HEADS
    neg = -jnp.inf

    s = jnp.concatenate(
        [lax.dot_general(q_ref[:, j * N_KEYS:(j + 1) * N_KEYS], sk_ref[j // 2, j % 2], _NT,
                         preferred_element_type=F32) for j in range(n_sets)], axis=0)
    lane = lax.broadcasted_iota(jnp.int32, s.shape, 1)
    key_f = lane.astype(F32)
    v16 = jnp.zeros(s.shape, F32)
    i16 = jnp.zeros(s.shape, F32)
    for r in range(k):
        m = jnp.max(s, axis=1, keepdims=True)
        i = jnp.min(jnp.where(s == m, key_f, float(N_KEYS)), axis=1, keepdims=True)
        s = jnp.where(key_f == i, neg, s)
        v16 = jnp.where(lane == r, m, v16)
        i16 = jnp.where(lane == r, i, i16)

    sr = lax.broadcasted_iota(jnp.int32, (N_KEYS, kk), 0)
    sc = lax.broadcasted_iota(jnp.int32, (N_KEYS, kk), 1)
    sel_a = jnp.where(sc // k == sr, 1.0, 0.0).astype(BF16)
    sel_b = jnp.where(sc % k == sr, 1.0, 0.0).astype(BF16)
    cands, eids = [], []
    for h in range(PEER_HEADS):
        lo, mid, hi = 2 * h * tb, (2 * h + 1) * tb, (2 * h + 2) * tb
        cands.append(_spread_exact(v16[lo:mid], sel_a) + _spread_exact(v16[mid:hi], sel_b))
        eids.append(jnp.dot(i16[lo:mid].astype(BF16), sel_a, preferred_element_type=F32) * float(N_KEYS)
                    + jnp.dot(i16[mid:hi].astype(BF16), sel_b, preferred_element_type=F32))
    cand = jnp.concatenate(cands, axis=0)
    eid = jnp.concatenate(eids, axis=0)

    cl_f = lax.broadcasted_iota(jnp.int32, cand.shape, 1).astype(F32)
    top_c, top_e = [], []
    for r in range(k):
        m = jnp.max(cand, axis=1, keepdims=True)
        pos = jnp.min(jnp.where(cand == m, cl_f, float(kk)), axis=1, keepdims=True)
        hit = cl_f == pos
        top_e.append(jnp.sum(jnp.where(hit, eid, 0.0), axis=1, keepdims=True))
        cand = jnp.where(hit, neg, cand)
        top_c.append(m)
    ps = [jnp.exp(c - top_c[0]) for c in top_c]
    denom = ps[0]
    for p in ps[1:]:
        denom = denom + p

    ol = lax.broadcasted_iota(jnp.int32, (tb, N_SEL), 1)
    ol2 = lax.broadcasted_iota(jnp.int32, (tb, 2 * N_SEL), 1)
    e_out = jnp.zeros((tb, N_SEL), F32)
    g2 = jnp.zeros((tb, 2 * N_SEL), F32)
    for h in range(PEER_HEADS):
        rows = slice(h * tb, (h + 1) * tb)
        for r in range(k):
            c = h * k + r
            e_out = jnp.where(ol == c, top_e[r][rows], e_out)
            g2 = jnp.where(ol2 == 2 * c + 1, (ps[r] / denom)[rows], g2)
    e_ref[...] = e_out.astype(jnp.int32)
    g2_ref[...] = g2


def _peer_topk(q2, sk_bf16, tb=128):
    n, qc = q2.shape
    return pl.pallas_call(
        _topk_kernel,
        out_shape=(jax.ShapeDtypeStruct((n, N_SEL), jnp.int32),
                   jax.ShapeDtypeStruct((n, 2 * N_SEL), F32)),
        grid=(n // tb,),
        in_specs=[pl.BlockSpec((tb, qc), lambda i: (i, 0)),
                  pl.BlockSpec(sk_bf16.shape, lambda i: (0, 0, 0, 0))],
        out_specs=(pl.BlockSpec((tb, N_SEL), lambda i: (i, 0)),
                   pl.BlockSpec((tb, 2 * N_SEL), lambda i: (i, 0))),
        compiler_params=pltpu.CompilerParams(dimension_semantics=("parallel",),
                                             vmem_limit_bytes=VMEM_LIMIT),
        name="peer_topk",
    )(q2, sk_bf16)


def _pack_tables(u, v):
    ub = lax.bitcast_convert_type(u.astype(BF16), jnp.uint16).astype(jnp.uint32)
    vb = lax.bitcast_convert_type(v.astype(BF16), jnp.uint16).astype(jnp.uint32)
    e, d = u.shape
    return (ub | (vb << 16)).reshape(e, d // HEAD_W, HEAD_W)


def _mix_kernel(efirst_ref, enext_ref, g2_ref, xn_ref, x1_ref, gfin_ref, tab_hbm, o_ref, buf, sem, *, final_norm):
    step = pl.program_id(0)
    tb = xn_ref.shape[0]
    n_chunks = tab_hbm.shape[1]

    def issue(e_ref, t, slot):
        for k in range(N_SEL):
            e = e_ref[t, k]
            pltpu.make_async_copy(tab_hbm.at[e], buf.at[slot, pl.ds(k * n_chunks, n_chunks), :],
                                  sem.at[slot]).start(priority=k % 2)

    def wait(slot):
        pltpu.make_async_copy(buf.at[slot], buf.at[slot], sem.at[slot]).wait()

    @pl.when(step == 0)
    def _():
        issue(efirst_ref, 0, 0)

    rid = lax.broadcasted_iota(jnp.int32, (SUBLANES, 2 * N_SEL), 0)

    def group(gi, c):
        base = pl.multiple_of(gi * SUBLANES, SUBLANES)
        xrows = xn_ref[pl.ds(base, SUBLANES), :]
        grows = g2_ref[pl.ds(base, SUBLANES), :]
        acc = x1_ref[pl.ds(base, SUBLANES), :]
        for j in range(SUBLANES):
            slot = j % 2
            issue(enext_ref, base + j, 1 - slot)
            wait(slot)
            uv = jnp.concatenate(
                [pltpu.bitcast(buf[slot, pl.ds(c, N_SEL, stride=n_chunks), :], BF16) for c in range(n_chunks)],
                axis=1)
            s2 = lax.dot_general(xrows, uv, _NT, preferred_element_type=F32)
            act = pltpu.roll(_gelu_exact(s2), 1, axis=1)
            w2 = jnp.where(rid == j, grows * act, 0.0)
            acc = acc + jnp.dot(w2.astype(BF16), uv, preferred_element_type=F32)
        o_ref[pl.ds(base, SUBLANES), :] = acc
        return c

    lax.fori_loop(0, tb // SUBLANES, group, 0)

    @pl.when(step == pl.num_programs(0) - 1)
    def _():
        wait(0)

    if final_norm:
        o_ref[...] = _rms(o_ref[...]) * gfin_ref[...]


def _peer_mix(eids, gates2, xn, x1, gfin, table, final_norm, tb=64):
    n, d = x1.shape
    assert tb % (2 * SUBLANES) == 0
    e_next = jnp.roll(eids, -1, axis=0)
    row = lambda w: pl.BlockSpec((tb, w), lambda i: (i, 0))
    return pl.pallas_call(
        functools.partial(_mix_kernel, final_norm=final_norm),
        out_shape=jax.ShapeDtypeStruct((n, d), F32),
        grid=(n // tb,),
        in_specs=[pl.BlockSpec((SUBLANES, N_SEL), lambda i: (0, 0), memory_space=pltpu.SMEM),
                  pl.BlockSpec((tb, N_SEL), lambda i: (i, 0), memory_space=pltpu.SMEM),
                  row(2 * N_SEL), row(d), row(d),
                  pl.BlockSpec((1, d), lambda i: (0, 0)),
                  pl.BlockSpec(memory_space=pl.ANY)],
        out_specs=row(d),
        scratch_shapes=[pltpu.VMEM((2, N_SEL * (d // HEAD_W), HEAD_W), jnp.uint32),
                        pltpu.SemaphoreType.DMA((2,))],
        compiler_params=pltpu.CompilerParams(dimension_semantics=("arbitrary",),
                                             vmem_limit_bytes=VMEM_LIMIT),
        name="peer_mix",
    )(eids, e_next, gates2, xn, x1, gfin, table)


def _retention_tables(dk):
    c = RET_CHUNK
    h = RET_HEADS
    log_gamma = jnp.log1p(-jnp.exp2(-5.0 - jnp.arange(h, dtype=F32)))
    idx = jnp.arange(c, dtype=F32)
    rel = idx[:, None] - idx[None, :]
    intra = jnp.where(rel >= 0, jnp.exp(log_gamma[:, None, None] * jnp.maximum(rel, 0.0)), 0.0)
    cross = jnp.exp(log_gamma[:, None] * (idx + 1.0))
    kdec = jnp.exp(log_gamma[:, None] * (c - 1.0 - idx))
    cdec = jnp.exp(log_gamma * c)
    scale = dk ** -0.5
    return intra * scale, cross[:, :, None], (kdec * scale)[:, :, None], cdec[:, None, None]


def kernel(x, norm_mix_g, w_in, diff_lambda, diff_subln_g, w_out, norm_ffn_g, peer_wq, peer_subkeys,
           peer_u, peer_v, final_norm_g):
    b, t, d = x.shape
    depth = w_in.shape[0]
    n = b * t
    x2 = x.reshape(n, d)
    intra, cross, kdec, cdec = _retention_tables(HEAD_W)
    slopes = jnp.exp2(-8.0 * (jnp.arange(DIFF_HEADS, dtype=F32) + 1.0) / DIFF_HEADS)
    for l in range(depth):
        li = 0.8 - 0.6 * math.exp(-0.3 * l)
        proj = _in_proj(x2, norm_mix_g[l][None, :], w_in[l].astype(BF16))
        proj3 = proj.reshape(b, t, proj.shape[1])
        ret = _retention(proj3, intra, cross, kdec, cdec)
        dif = _diff_attention(proj3, slopes, diff_lambda[l], diff_subln_g[l][None, :], 1.0 - li)
        half = ret.shape[2]
        x1, xn, q = _out_proj(x2, ret.reshape(n, half), dif.reshape(n, half), w_out[l].astype(BF16),
                              norm_ffn_g[l][None, :], peer_wq[l].astype(BF16))
        eids, gates2 = _peer_topk(q, peer_subkeys[l].astype(BF16))
        x2 = _peer_mix(eids, gates2, xn, x1, final_norm_g[None, :], _pack_tables(peer_u[l], peer_v[l]),
                       final_norm=(l == depth - 1))
    return x2.reshape(b, t, d)
```

```python
import functools
import math

import jax
import jax.numpy as jnp
from jax import lax
from jax.experimental import pallas as pl
from jax.experimental.pallas import tpu as pltpu
from jax.experimental.pallas import tpu_sc as plsc

F32 = jnp.float32
BF16 = jnp.bfloat16

RMS_EPS = 1e-6
NEG_INF = -1e30

RET_HEADS = 4
RET_CHUNK = 128
DIFF_HEADS = 4
HEAD_W = 128
PEER_HEADS = 8
PEER_TOPK = 16
N_KEYS = 128
N_SEL = PEER_HEADS * PEER_TOPK

SUBLANES = 8
VMEM_LIMIT = 56 * 1024 * 1024

_NT = (((1,), (1,)), ((), ()))


def _rms(x, eps=RMS_EPS):
    return x * lax.rsqrt(jnp.mean(x * x, axis=-1, keepdims=True) + eps)


def _gelu_exact(x):
    return 0.5 * x * (1.0 + lax.erf(x * (2.0 ** -0.5)))


def _inproj_kernel(x_ref, g_ref, w_ref, o_ref, *, col_chunk):
    h = (_rms(x_ref[...]) * g_ref[...]).astype(BF16)
    for j in range(o_ref.shape[1] // col_chunk):
        sl = slice(j * col_chunk, (j + 1) * col_chunk)
        o_ref[:, sl] = jnp.dot(h, w_ref[:, sl], preferred_element_type=F32).astype(o_ref.dtype)


def _in_proj(x2, g, w_bf16, tm=512):
    n, d = x2.shape
    cols = w_bf16.shape[1]
    return pl.pallas_call(
        functools.partial(_inproj_kernel, col_chunk=512),
        out_shape=jax.ShapeDtypeStruct((n, cols), BF16),
        grid=(n // tm,),
        in_specs=[pl.BlockSpec((tm, d), lambda i: (i, 0)),
                  pl.BlockSpec((1, d), lambda i: (0, 0)),
                  pl.BlockSpec((d, cols), lambda i: (0, 0))],
        out_specs=pl.BlockSpec((tm, cols), lambda i: (i, 0)),
        compiler_params=pltpu.CompilerParams(dimension_semantics=("parallel",),
                                             vmem_limit_bytes=VMEM_LIMIT),
        name="in_proj",
    )(x2, g, w_bf16)


def _ret_kernel(q_ref, k_ref, v_ref, g_ref, intra_ref, cross_ref, kdec_ref, cdec_ref, o_ref):
    c = RET_CHUNK
    n_chunks = q_ref.shape[1] // c
    intra = intra_ref[0]
    cross = cross_ref[0]
    kdec = kdec_ref[0]
    cdec = cdec_ref[0]

    def body(n, s):
        sl = pl.ds(pl.multiple_of(n * c, c), c)
        qi = q_ref[0, sl, :]
        ki = k_ref[0, sl, :]
        vi = v_ref[0, sl, :]
        a = lax.dot_general(qi, ki, _NT, preferred_element_type=F32) * intra
        o = jnp.dot(a.astype(BF16), vi, preferred_element_type=F32)
        o = o + jnp.dot(qi, s.astype(BF16), preferred_element_type=F32) * cross
        kd_t = (ki.astype(F32) * kdec).T.astype(BF16)
        s = s * cdec + jnp.dot(kd_t, vi, preferred_element_type=F32)
        gate = g_ref[0, sl, :].astype(F32)
        o_ref[0, sl, :] = (gate * jax.nn.sigmoid(gate) * _rms(o)).astype(o_ref.dtype)
        return s

    lax.fori_loop(0, n_chunks, body, jnp.zeros((HEAD_W, HEAD_W), F32))


def _retention(proj3, intra, cross, kdec, cdec):
    b, t, _ = proj3.shape
    h = RET_HEADS

    def col(off):
        return pl.BlockSpec((1, t, HEAD_W), lambda bi, hi: (bi, 0, off + hi))

    def per_head(shape):
        return pl.BlockSpec((1,) + shape, lambda bi, hi: (hi, 0, 0))

    return pl.pallas_call(
        _ret_kernel,
        out_shape=jax.ShapeDtypeStruct((b, t, h * HEAD_W), BF16),
        grid=(b, h),
        in_specs=[col(0), col(h), col(2 * h), col(3 * h),
                  per_head((RET_CHUNK, RET_CHUNK)), per_head((RET_CHUNK, 1)),
                  per_head((RET_CHUNK, 1)), per_head((1, 1))],
        out_specs=pl.BlockSpec((1, t, HEAD_W), lambda bi, hi: (bi, 0, hi)),
        compiler_params=pltpu.CompilerParams(dimension_semantics=("parallel", "parallel"),
                                             vmem_limit_bytes=VMEM_LIMIT),
        name="retention",
    )(proj3, proj3, proj3, proj3, intra, cross, kdec, cdec)


def _diff_kernel(slope_ref, lam_ref, q_ref, k_ref, v_ref, gsub_ref, o_ref, *, tq, tk, out_scale):
    hi = pl.program_id(1)
    qi = pl.program_id(2)
    half = HEAD_W // 2
    slope = slope_ref[hi]

    lp = lam_ref[...]
    lam = (jnp.exp(jnp.sum(lp[0:1] * lp[1:2], axis=1, keepdims=True))
           - jnp.exp(jnp.sum(lp[2:3] * lp[3:4], axis=1, keepdims=True))
           + (1.0 - out_scale))

    q = q_ref[0]
    lane = lax.broadcasted_iota(jnp.int32, q.shape, 1)
    zero = jnp.zeros_like(q)
    qs = jnp.concatenate([jnp.where(lane < half, q, zero),
                          jnp.where(lane >= half, q, zero)], axis=0)

    row = lax.broadcasted_iota(jnp.int32, (2 * tq, tk), 0)
    row = jnp.where(row >= tq, row - tq, row)
    col = lax.broadcasted_iota(jnp.int32, (2 * tq, tk), 1)
    rel = (row - col).astype(F32)

    def body(j, carry):
        m, l, acc = carry
        ksl = pl.ds(pl.multiple_of(j * tk, tk), tk)
        k = k_ref[0, ksl, :]
        v = v_ref[0, ksl, :]
        s = lax.dot_general(qs, k, _NT, preferred_element_type=F32) * (half ** -0.5)
        dist = rel + (qi * tq - j * tk).astype(F32)
        s = s + jnp.where(dist >= 0, -slope * dist, NEG_INF)
        m_new = jnp.maximum(m, jnp.max(s, axis=1, keepdims=True))
        alpha = jnp.exp(m - m_new)
        p = jnp.exp(s - m_new)
        l = alpha * l + jnp.sum(p, axis=1, keepdims=True)
        acc = alpha * acc + jnp.dot(p.astype(BF16), v, preferred_element_type=F32)
        return m_new, l, acc

    n_kv = (qi * tq + tq - 1) // tk + 1
    m0 = jnp.full((2 * tq, 1), NEG_INF, F32)
    l0 = jnp.zeros((2 * tq, 1), F32)
    a0 = jnp.zeros((2 * tq, HEAD_W), F32)
    _, l, acc = lax.fori_loop(0, n_kv, body, (m0, l0, a0))
    o = acc / l
    dy = o[:tq] - lam * o[tq:]
    o_ref[0] = (_rms(dy) * gsub_ref[...] * out_scale).astype(o_ref.dtype)


def _diff_attention(proj3, slopes, diff_lambda_l, gsub, out_scale, tq=128, tk=128):
    b, t, _ = proj3.shape
    h = DIFF_HEADS
    base = 4 * RET_HEADS
    kern = functools.partial(_diff_kernel, tq=tq, tk=tk, out_scale=out_scale)
    return pl.pallas_call(
        kern,
        out_shape=jax.ShapeDtypeStruct((b, t, h * HEAD_W), BF16),
        grid=(b, h, t // tq),
        in_specs=[pl.BlockSpec(memory_space=pltpu.SMEM),
                  pl.BlockSpec(diff_lambda_l.shape, lambda bi, hi, qi: (0, 0)),
                  pl.BlockSpec((1, tq, HEAD_W), lambda bi, hi, qi: (bi, qi, base + hi)),
                  pl.BlockSpec((1, t, HEAD_W), lambda bi, hi, qi: (bi, 0, base + h + hi)),
                  pl.BlockSpec((1, t, HEAD_W), lambda bi, hi, qi: (bi, 0, base + 2 * h + hi)),
                  pl.BlockSpec((1, HEAD_W), lambda bi, hi, qi: (0, 0))],
        out_specs=pl.BlockSpec((1, tq, HEAD_W), lambda bi, hi, qi: (bi, qi, hi)),
        compiler_params=pltpu.CompilerParams(
            dimension_semantics=("parallel", "parallel", "arbitrary"),
            vmem_limit_bytes=VMEM_LIMIT),
        name="diff_attention",
    )(slopes, diff_lambda_l, proj3, proj3, proj3, gsub)


def _outproj_kernel(x_ref, ret_ref, dif_ref, wo_ref, g_ref, wq_ref, x1_ref, xn_ref, q_ref, *, col_chunk):
    half = ret_ref.shape[1]
    mix = jnp.dot(ret_ref[...], wo_ref[:half, :], preferred_element_type=F32)
    mix = mix + jnp.dot(dif_ref[...], wo_ref[half:, :], preferred_element_type=F32)
    x1 = x_ref[...] + mix
    x1_ref[...] = x1
    xn = (_rms(x1) * g_ref[...]).astype(BF16)
    xn_ref[...] = xn
    for j in range(q_ref.shape[1] // col_chunk):
        sl = slice(j * col_chunk, (j + 1) * col_chunk)
        q_ref[:, sl] = jnp.dot(xn, wq_ref[:, sl], preferred_element_type=F32).astype(q_ref.dtype)


def _out_proj(x2, ret2, dif2, wo_bf16, g, wq_bf16, tm=512):
    n, d = x2.shape
    half = ret2.shape[1]
    qc = wq_bf16.shape[1]
    row = lambda w: pl.BlockSpec((tm, w), lambda i: (i, 0))
    full = lambda a: pl.BlockSpec(a.shape, lambda i: (0, 0))
    return pl.pallas_call(
        functools.partial(_outproj_kernel, col_chunk=512),
        out_shape=(jax.ShapeDtypeStruct((n, d), F32),
                   jax.ShapeDtypeStruct((n, d), BF16),
                   jax.ShapeDtypeStruct((n, qc), BF16)),
        grid=(n // tm,),
        in_specs=[row(d), row(half), row(half), full(wo_bf16), full(g), full(wq_bf16)],
        out_specs=(row(d), row(d), row(qc)),
        compiler_params=pltpu.CompilerParams(dimension_semantics=("parallel",),
                                             vmem_limit_bytes=VMEM_LIMIT),
        name="out_proj",
    )(x2, ret2, dif2, wo_bf16, g, wq_bf16)


def _spread_exact(v, sel_bf16):
    hi = v.astype(BF16)
    r1 = v - hi.astype(F32)
    mid = r1.astype(BF16)
    lo = (r1 - mid.astype(F32)).astype(BF16)
    d = lambda a: jnp.dot(a, sel_bf16, preferred_element_type=F32)
    return (d(hi) + d(mid)) + d(lo)


def _topk_kernel(q_ref, sk_ref, e_ref, g_ref, g2_ref):
    tb = q_ref.shape[0]
    k = PEER_TOPK
    kk = k * k
    n_sets = 2 * PEER_HEADS
    neg = -jnp.inf

    s = jnp.concatenate(
        [lax.dot_general(q_ref[:, j * N_KEYS:(j + 1) * N_KEYS], sk_ref[j // 2, j % 2], _NT,
                         preferred_element_type=F32) for j in range(n_sets)], axis=0)
    lane = lax.broadcasted_iota(jnp.int32, s.shape, 1)
    key_f = lane.astype(F32)
    v16 = jnp.zeros(s.shape, F32)
    i16 = jnp.zeros(s.shape, F32)
    for r in range(k):
        m = jnp.max(s, axis=1, keepdims=True)
        i = jnp.min(jnp.where(s == m, key_f, float(N_KEYS)), axis=1, keepdims=True)
        s = jnp.where(key_f == i, neg, s)
        v16 = jnp.where(lane == r, m, v16)
        i16 = jnp.where(lane == r, i, i16)

    sr = lax.broadcasted_iota(jnp.int32, (N_KEYS, kk), 0)
    sc = lax.broadcasted_iota(jnp.int32, (N_KEYS, kk), 1)
    sel_a = jnp.where(sc // k == sr, 1.0, 0.0).astype(BF16)
    sel_b = jnp.where(sc % k == sr, 1.0, 0.0).astype(BF16)
    cands, eids = [], []
    for h in range(PEER_HEADS):
        lo, mid, hi = 2 * h * tb, (2 * h + 1) * tb, (2 * h + 2) * tb
        cands.append(_spread_exact(v16[lo:mid], sel_a) + _spread_exact(v16[mid:hi], sel_b))
        eids.append(jnp.dot(i16[lo:mid].astype(BF16), sel_a, preferred_element_type=F32) * float(N_KEYS)
                    + jnp.dot(i16[mid:hi].astype(BF16), sel_b, preferred_element_type=F32))
    cand = jnp.concatenate(cands, axis=0)
    eid = jnp.concatenate(eids, axis=0)

    cl_f = lax.broadcasted_iota(jnp.int32, cand.shape, 1).astype(F32)
    top_c, top_e = [], []
    for r in range(k):
        m = jnp.max(cand, axis=1, keepdims=True)
        pos = jnp.min(jnp.where(cand == m, cl_f, float(kk)), axis=1, keepdims=True)
        hit = cl_f == pos
        top_e.append(jnp.sum(jnp.where(hit, eid, 0.0), axis=1, keepdims=True))
        cand = jnp.where(hit, neg, cand)
        top_c.append(m)
    ps = [jnp.exp(c - top_c[0]) for c in top_c]
    denom = ps[0]
    for p in ps[1:]:
        denom = denom + p

    ol = lax.broadcasted_iota(jnp.int32, (tb, N_SEL), 1)
    ol2 = lax.broadcasted_iota(jnp.int32, (tb, 2 * N_SEL), 1)
    e_out = jnp.zeros((tb, N_SEL), F32)
    g1 = jnp.zeros((tb, N_SEL), F32)
    g2 = jnp.zeros((tb, 2 * N_SEL), F32)
    for h in range(PEER_HEADS):
        rows = slice(h * tb, (h + 1) * tb)
        for r in range(k):
            c = h * k + r
            gate = (ps[r] / denom)[rows]
            e_out = jnp.where(ol == c, top_e[r][rows], e_out)
            g1 = jnp.where(ol == c, gate, g1)
            g2 = jnp.where(ol2 == 2 * c + 1, gate, g2)
    e_ref[...] = e_out.astype(jnp.int32)
    g_ref[...] = g1
    g2_ref[...] = g2


def _peer_topk(q2, sk_bf16, tb=128):
    n, qc = q2.shape
    return pl.pallas_call(
        _topk_kernel,
        out_shape=(jax.ShapeDtypeStruct((n, N_SEL), jnp.int32),
                   jax.ShapeDtypeStruct((n, N_SEL), F32),
                   jax.ShapeDtypeStruct((n, 2 * N_SEL), F32)),
        grid=(n // tb,),
        in_specs=[pl.BlockSpec((tb, qc), lambda i: (i, 0)),
                  pl.BlockSpec(sk_bf16.shape, lambda i: (0, 0, 0, 0))],
        out_specs=(pl.BlockSpec((tb, N_SEL), lambda i: (i, 0)),
                   pl.BlockSpec((tb, N_SEL), lambda i: (i, 0)),
                   pl.BlockSpec((tb, 2 * N_SEL), lambda i: (i, 0))),
        compiler_params=pltpu.CompilerParams(dimension_semantics=("parallel",),
                                             vmem_limit_bytes=VMEM_LIMIT),
        name="peer_topk",
    )(q2, sk_bf16)


def _bf16_bits(t):
    return lax.bitcast_convert_type(t.astype(BF16), jnp.uint16).astype(jnp.uint32)


def _pack_tables(u, v):
    e, d = u.shape
    return (_bf16_bits(u) | (_bf16_bits(v) << 16)).reshape(e, d // HEAD_W, HEAD_W)


def _pack_halves(t):
    b = _bf16_bits(t)
    half = t.shape[1] // 2
    return b[:, :half] | (b[:, half:] << 16)


def _mix_kernel(efirst_ref, enext_ref, g2_ref, xn_ref, x1_ref, gfin_ref, tab_hbm, after_ref, o_ref, buf, sem, *,
                final_norm):
    step = pl.program_id(0)
    tb = xn_ref.shape[0]
    n_chunks = tab_hbm.shape[1]

    def issue(e_ref, t, slot):
        for k in range(N_SEL):
            e = e_ref[t, k]
            pltpu.make_async_copy(tab_hbm.at[e], buf.at[slot, pl.ds(k * n_chunks, n_chunks), :],
                                  sem.at[slot]).start(priority=k % 2)

    def wait(slot):
        pltpu.make_async_copy(buf.at[slot], buf.at[slot], sem.at[slot]).wait()

    @pl.when(step == 0)
    def _():
        issue(efirst_ref, 0, 0)

    rid = lax.broadcasted_iota(jnp.int32, (SUBLANES, 2 * N_SEL), 0)

    def group(gi, c):
        base = pl.multiple_of(gi * SUBLANES, SUBLANES)
        xrows = xn_ref[pl.ds(base, SUBLANES), :]
        grows = g2_ref[pl.ds(base, SUBLANES), :]
        acc = x1_ref[pl.ds(base, SUBLANES), :]
        for j in range(SUBLANES):
            slot = j % 2
            issue(enext_ref, base + j, 1 - slot)
            wait(slot)
            uv = jnp.concatenate(
                [pltpu.bitcast(buf[slot, pl.ds(c, N_SEL, stride=n_chunks), :], BF16) for c in range(n_chunks)],
                axis=1)
            s2 = lax.dot_general(xrows, uv, _NT, preferred_element_type=F32)
            act = pltpu.roll(_gelu_exact(s2), 1, axis=1)
            w2 = jnp.where(rid == j, grows * act, 0.0)
            acc = acc + jnp.dot(w2.astype(BF16), uv, preferred_element_type=F32)
        o_ref[pl.ds(base, SUBLANES), :] = acc
        return c

    lax.fori_loop(0, tb // SUBLANES, group, 0)

    @pl.when(step == pl.num_programs(0) - 1)
    def _():
        wait(0)

    if final_norm:
        o_ref[...] = _rms(o_ref[...]) * gfin_ref[...]


def _peer_mix(eids, e_next, gates2, xn, x1, gfin, table, after, row0, rows, final_norm, tb=64):
    d = x1.shape[1]
    assert tb % (2 * SUBLANES) == 0 and row0 % tb == 0 and rows % tb == 0
    blk0 = row0 // tb
    row = lambda w: pl.BlockSpec((tb, w), lambda i: (i + blk0, 0))
    return pl.pallas_call(
        functools.partial(_mix_kernel, final_norm=final_norm),
        out_shape=jax.ShapeDtypeStruct((rows, d), F32),
        grid=(rows // tb,),
        in_specs=[pl.BlockSpec((SUBLANES, N_SEL), lambda i: (row0 // SUBLANES, 0), memory_space=pltpu.SMEM),
                  pl.BlockSpec((tb, N_SEL), lambda i: (i + blk0, 0), memory_space=pltpu.SMEM),
                  row(2 * N_SEL), row(d), row(d),
                  pl.BlockSpec((1, d), lambda i: (0, 0)),
                  pl.BlockSpec(memory_space=pl.ANY),
                  pl.BlockSpec(memory_space=pl.ANY)],
        out_specs=pl.BlockSpec((tb, d), lambda i: (i, 0)),
        scratch_shapes=[pltpu.VMEM((2, N_SEL * (d // HEAD_W), HEAD_W), jnp.uint32),
                        pltpu.SemaphoreType.DMA((2,))],
        compiler_params=pltpu.CompilerParams(dimension_semantics=("arbitrary",),
                                             vmem_limit_bytes=VMEM_LIMIT),
        name="peer_mix",
    )(eids, e_next, gates2, xn, x1, gfin, table, after)


SC_LANES = 16
SC_TILES = 32
SC_TBLK = 8
HI_MASK = 0xFFFF0000
_SC_PARAMS = pltpu.CompilerParams(needs_layout_passes=False)


def _sc_unpack(word):
    return plsc.bitcast(word << 16, F32), plsc.bitcast(word & jnp.uint32(HI_MASK), F32)


def _sc_pipelined_heads(tab_hbm, ev, rows, sem, compute):
    n_items = SC_TBLK * PEER_HEADS

    def gather(it, slot):
        ti = it // PEER_HEADS
        h = it % PEER_HEADS
        return pltpu.make_async_copy(tab_hbm.at[ev.at[ti, pl.ds(h * PEER_TOPK, PEER_TOPK)]], rows.at[slot],
                                     sem.at[slot])

    gather(0, 0).start()

    def pair(jj, carry):
        for slot in range(2):
            it = 2 * jj + slot
            gather(jnp.minimum(it + 1, n_items - 1), 1 - slot).start()
            gather(it, slot).wait()
            compute(it, slot)
        return carry

    lax.fori_loop(0, n_items // 2, pair, 0)
    gather(n_items - 1, 0).wait()


def _sc_scores(x, eids, utab, row0, rows):
    d = x.shape[1]
    dw = d // 2
    nw = dw // SC_LANES
    per_tile = rows // SC_TILES
    assert rows % (SC_TILES * SC_TBLK) == 0 and row0 % SC_TBLK == 0

    def body(x_hbm, e_hbm, tab_hbm, s_hbm, xv, ev, rbuf, sv, tmp, sem):
        tile = lax.axis_index("c") * 16 + lax.axis_index("s")
        lane = lax.iota(jnp.int32, SC_LANES)

        def compute(it, slot):
            ti = it // PEER_HEADS
            h = it % PEER_HEADS

            def chunk(w, accs):
                off = pl.multiple_of(w * SC_LANES, SC_LANES)
                xlo = xv[ti, pl.ds(off, SC_LANES)]
                xhi = xv[ti, pl.ds(dw + off, SC_LANES)]
                out = []
                for j in range(PEER_TOPK):
                    lo, hi = _sc_unpack(rbuf[slot, j, pl.ds(off, SC_LANES)])
                    out.append(accs[j] + lo * xlo + hi * xhi)
                return tuple(out)

            accs = lax.fori_loop(0, nw, chunk, tuple(jnp.zeros((SC_LANES,), F32) for _ in range(PEER_TOPK)))
            for j in range(PEER_TOPK):
                tmp[j, :] = accs[j]
            svec = jnp.zeros((SC_LANES,), F32)
            for l in range(SC_LANES):
                svec = svec + plsc.load_gather(tmp, [lane, jnp.full((SC_LANES,), l, jnp.int32)])
            sv[ti, pl.ds(pl.multiple_of(h * PEER_TOPK, PEER_TOPK), PEER_TOPK)] = svec

        def block(b, carry):
            t_out = pl.multiple_of(tile * per_tile + b * SC_TBLK, SC_TBLK)
            pltpu.sync_copy(x_hbm.at[pl.ds(row0 + t_out, SC_TBLK)], xv)
            pltpu.sync_copy(e_hbm.at[pl.ds(row0 + t_out, SC_TBLK)], ev)
            _sc_pipelined_heads(tab_hbm, ev, rbuf, sem, compute)
            pltpu.sync_copy(sv, s_hbm.at[pl.ds(t_out, SC_TBLK)])
            return carry

        lax.fori_loop(0, per_tile // SC_TBLK, block, 0)

    return pl.kernel(
        body,
        out_type=jax.ShapeDtypeStruct((rows, N_SEL), F32),
        mesh=plsc.VectorSubcoreMesh(core_axis_name="c", subcore_axis_name="s"),
        scratch_types=[pltpu.VMEM((SC_TBLK, d), F32), pltpu.VMEM((SC_TBLK, N_SEL), jnp.int32),
                       pltpu.VMEM((2, PEER_TOPK, dw), jnp.uint32), pltpu.VMEM((SC_TBLK, N_SEL), F32),
                       pltpu.VMEM((PEER_TOPK, SC_LANES), F32), pltpu.SemaphoreType.DMA((2,))],
        compiler_params=_SC_PARAMS,
        name="sc_scores",
    )(x, eids, utab)


def _sc_combine(w, eids, vtab, row0):
    rows = w.shape[0]
    dw = vtab.shape[1]
    d = 2 * dw
    nw = dw // SC_LANES
    per_tile = rows // SC_TILES
    assert rows % (SC_TILES * SC_TBLK) == 0 and row0 % SC_TBLK == 0

    def body(w_hbm, e_hbm, tab_hbm, y_hbm, wv, ev, rbuf, yv, sem):
        tile = lax.axis_index("c") * 16 + lax.axis_index("s")
        zero_i = jnp.zeros((SC_LANES,), jnp.int32)

        def compute(it, slot):
            ti = it // PEER_HEADS
            h = it % PEER_HEADS
            wk = [plsc.load_gather(wv, [zero_i + ti, zero_i + (h * PEER_TOPK + j)]) for j in range(PEER_TOPK)]

            def chunk(c, carry):
                off = pl.multiple_of(c * SC_LANES, SC_LANES)
                alo = jnp.zeros((SC_LANES,), F32)
                ahi = jnp.zeros((SC_LANES,), F32)
                for j in range(PEER_TOPK):
                    lo, hi = _sc_unpack(rbuf[slot, j, pl.ds(off, SC_LANES)])
                    alo = alo + wk[j] * lo
                    ahi = ahi + wk[j] * hi
                first = h == 0
                yv[ti, pl.ds(off, SC_LANES)] = jnp.where(first, 0.0, yv[ti, pl.ds(off, SC_LANES)]) + alo
                yv[ti, pl.ds(dw + off, SC_LANES)] = jnp.where(first, 0.0, yv[ti, pl.ds(dw + off, SC_LANES)]) + ahi
                return carry

            lax.fori_loop(0, nw, chunk, 0)

        def block(b, carry):
            t_out = pl.multiple_of(tile * per_tile + b * SC_TBLK, SC_TBLK)
            pltpu.sync_copy(w_hbm.at[pl.ds(t_out, SC_TBLK)], wv)
            pltpu.sync_copy(e_hbm.at[pl.ds(row0 + t_out, SC_TBLK)], ev)
            _sc_pipelined_heads(tab_hbm, ev, rbuf, sem, compute)
            pltpu.sync_copy(yv, y_hbm.at[pl.ds(t_out, SC_TBLK)])
            return carry

        lax.fori_loop(0, per_tile // SC_TBLK, block, 0)

    return pl.kernel(
        body,
        out_type=jax.ShapeDtypeStruct((rows, d), F32),
        mesh=plsc.VectorSubcoreMesh(core_axis_name="c", subcore_axis_name="s"),
        scratch_types=[pltpu.VMEM((SC_TBLK, N_SEL), F32), pltpu.VMEM((SC_TBLK, N_SEL), jnp.int32),
                       pltpu.VMEM((2, PEER_TOPK, dw), jnp.uint32), pltpu.VMEM((SC_TBLK, d), F32),
                       pltpu.SemaphoreType.DMA((2,))],
        compiler_params=_SC_PARAMS,
        name="sc_combine",
    )(w, eids, vtab)


def _gate_kernel(s_ref, g_ref, after_ref, w_ref):
    w_ref[...] = g_ref[...] * _gelu_exact(s_ref[...])


def _gate_act(s, gates, after, row0, tb=512):
    rows = s.shape[0]
    assert rows % tb == 0 and row0 % tb == 0
    blk0 = row0 // tb
    return pl.pallas_call(
        _gate_kernel,
        out_shape=jax.ShapeDtypeStruct((rows, N_SEL), F32),
        grid=(rows // tb,),
        in_specs=[pl.BlockSpec((tb, N_SEL), lambda i: (i, 0)),
                  pl.BlockSpec((tb, N_SEL), lambda i: (i + blk0, 0)),
                  pl.BlockSpec(memory_space=pl.ANY)],
        out_specs=pl.BlockSpec((tb, N_SEL), lambda i: (i, 0)),
        compiler_params=pltpu.CompilerParams(dimension_semantics=("parallel",)),
        name="peer_gate",
    )(s, gates, after)


def _resid_norm_kernel(x1_ref, y_ref, gfin_ref, o_ref, *, final_norm):
    z = x1_ref[...] + y_ref[...]
    o_ref[...] = _rms(z) * gfin_ref[...] if final_norm else z


def _resid_norm(x1, y, gfin, row0, final_norm, tb=512):
    rows, d = y.shape
    assert rows % tb == 0 and row0 % tb == 0
    blk0 = row0 // tb
    return pl.pallas_call(
        functools.partial(_resid_norm_kernel, final_norm=final_norm),
        out_shape=jax.ShapeDtypeStruct((rows, d), F32),
        grid=(rows // tb,),
        in_specs=[pl.BlockSpec((tb, d), lambda i: (i + blk0, 0)),
                  pl.BlockSpec((tb, d), lambda i: (i, 0)),
                  pl.BlockSpec((1, d), lambda i: (0, 0))],
        out_specs=pl.BlockSpec((tb, d), lambda i: (i, 0)),
        compiler_params=pltpu.CompilerParams(dimension_semantics=("parallel",)),
        name="peer_resid_norm",
    )(x1, y, gfin)


TC_SLICE_FRAC = 3 / 16


def _peer_mix_hybrid(eids, gates, gates2, xn, x1, gfin, u, v, final_norm):
    n, d = x1.shape
    unit = 512
    n1 = int(n * TC_SLICE_FRAC) // unit * unit
    n_tc = 2 * n1
    n_sc = n - n_tc
    e_next = jnp.roll(eids, -1, axis=0)
    table = _pack_tables(u, v)
    s = _sc_scores(xn.astype(F32), eids, _pack_halves(u), n_tc, n_sc)
    out_a = _peer_mix(eids, e_next, gates2, xn, x1, gfin, table, gfin, 0, n1, final_norm)
    w = _gate_act(s, gates, out_a, n_tc)
    y = _sc_combine(w, eids, _pack_halves(v), n_tc)
    out_b = _peer_mix(eids, e_next, gates2, xn, x1, gfin, table, w, n1, n1, final_norm)
    out_c = _resid_norm(x1, y, gfin, n_tc, final_norm)
    return jnp.concatenate([out_a, out_b, out_c], axis=0)


def _retention_tables(dk):
    c = RET_CHUNK
    h = RET_HEADS
    log_gamma = jnp.log1p(-jnp.exp2(-5.0 - jnp.arange(h, dtype=F32)))
    idx = jnp.arange(c, dtype=F32)
    rel = idx[:, None] - idx[None, :]
    intra = jnp.where(rel >= 0, jnp.exp(log_gamma[:, None, None] * jnp.maximum(rel, 0.0)), 0.0)
    cross = jnp.exp(log_gamma[:, None] * (idx + 1.0))
    kdec = jnp.exp(log_gamma[:, None] * (c - 1.0 - idx))
    cdec = jnp.exp(log_gamma * c)
    scale = dk ** -0.5
    return intra * scale, cross[:, :, None], (kdec * scale)[:, :, None], cdec[:, None, None]


def kernel(x, norm_mix_g, w_in, diff_lambda, diff_subln_g, w_out, norm_ffn_g, peer_wq, peer_subkeys,
           peer_u, peer_v, final_norm_g):
    b, t, d = x.shape
    depth = w_in.shape[0]
    n = b * t
    x2 = x.reshape(n, d)
    intra, cross, kdec, cdec = _retention_tables(HEAD_W)
    slopes = jnp.exp2(-8.0 * (jnp.arange(DIFF_HEADS, dtype=F32) + 1.0) / DIFF_HEADS)
    for l in range(depth):
        li = 0.8 - 0.6 * math.exp(-0.3 * l)
        proj = _in_proj(x2, norm_mix_g[l][None, :], w_in[l].astype(BF16))
        proj3 = proj.reshape(b, t, proj.shape[1])
        ret = _retention(proj3, intra, cross, kdec, cdec)
        dif = _diff_attention(proj3, slopes, diff_lambda[l], diff_subln_g[l][None, :], 1.0 - li)
        half = ret.shape[2]
        x1, xn, q = _out_proj(x2, ret.reshape(n, half), dif.reshape(n, half), w_out[l].astype(BF16),
                              norm_ffn_g[l][None, :], peer_wq[l].astype(BF16))
        eids, gates, gates2 = _peer_topk(q, peer_subkeys[l].astype(BF16))
        x2 = _peer_mix_hybrid(eids, gates, gates2, xn, x1, final_norm_g[None, :], peer_u[l], peer_v[l],
                              final_norm=(l == depth - 1))
    return x2.reshape(b, t, d)
```

```python
import functools
import math

import jax
import jax.numpy as jnp
from jax import lax
from jax.experimental import pallas as pl
from jax.experimental.pallas import tpu as pltpu
from jax.experimental.pallas import tpu_sc as plsc

F32 = jnp.float32
BF16 = jnp.bfloat16

RMS_EPS = 1e-6
NEG_INF = -1e30

RET_HEADS = 4
RET_CHUNK = 128
DIFF_HEADS = 4
HEAD_W = 128
PEER_HEADS = 8
PEER_TOPK = 16
N_KEYS = 128
N_SEL = PEER_HEADS * PEER_TOPK

SUBLANES = 8
VMEM_LIMIT = 56 * 1024 * 1024

_NT = (((1,), (1,)), ((), ()))


def _rms(x, eps=RMS_EPS):
    return x * lax.rsqrt(jnp.mean(x * x, axis=-1, keepdims=True) + eps)


def _gelu_exact(x):
    return 0.5 * x * (1.0 + lax.erf(x * (2.0 ** -0.5)))


def _inproj_kernel(x_ref, g_ref, w_ref, o_ref, *, col_chunk):
    h = (_rms(x_ref[...]) * g_ref[...]).astype(BF16)
    for j in range(o_ref.shape[1] // col_chunk):
        sl = slice(j * col_chunk, (j + 1) * col_chunk)
        o_ref[:, sl] = jnp.dot(h, w_ref[:, sl], preferred_element_type=F32).astype(o_ref.dtype)


def _in_proj(x2, g, w_bf16, tm=512):
    n, d = x2.shape
    cols = w_bf16.shape[1]
    return pl.pallas_call(
        functools.partial(_inproj_kernel, col_chunk=512),
        out_shape=jax.ShapeDtypeStruct((n, cols), BF16),
        grid=(n // tm,),
        in_specs=[pl.BlockSpec((tm, d), lambda i: (i, 0)),
                  pl.BlockSpec((1, d), lambda i: (0, 0)),
                  pl.BlockSpec((d, cols), lambda i: (0, 0))],
        out_specs=pl.BlockSpec((tm, cols), lambda i: (i, 0)),
        compiler_params=pltpu.CompilerParams(dimension_semantics=("parallel",),
                                             vmem_limit_bytes=VMEM_LIMIT),
        name="in_proj",
    )(x2, g, w_bf16)


def _ret_kernel(q_ref, k_ref, v_ref, g_ref, intra_ref, cross_ref, kdec_ref, cdec_ref, o_ref):
    c = RET_CHUNK
    n_chunks = q_ref.shape[1] // c
    intra = intra_ref[0]
    cross = cross_ref[0]
    kdec = kdec_ref[0]
    cdec = cdec_ref[0]

    def body(n, s):
        sl = pl.ds(pl.multiple_of(n * c, c), c)
        qi = q_ref[0, sl, :]
        ki = k_ref[0, sl, :]
        vi = v_ref[0, sl, :]
        a = lax.dot_general(qi, ki, _NT, preferred_element_type=F32) * intra
        o = jnp.dot(a.astype(BF16), vi, preferred_element_type=F32)
        o = o + jnp.dot(qi, s.astype(BF16), preferred_element_type=F32) * cross
        kd_t = (ki.astype(F32) * kdec).T.astype(BF16)
        s = s * cdec + jnp.dot(kd_t, vi, preferred_element_type=F32)
        gate = g_ref[0, sl, :].astype(F32)
        o_ref[0, sl, :] = (gate * jax.nn.sigmoid(gate) * _rms(o)).astype(o_ref.dtype)
        return s

    lax.fori_loop(0, n_chunks, body, jnp.zeros((HEAD_W, HEAD_W), F32))


def _retention(proj3, intra, cross, kdec, cdec):
    b, t, _ = proj3.shape
    h = RET_HEADS

    def col(off):
        return pl.BlockSpec((1, t, HEAD_W), lambda bi, hi: (bi, 0, off + hi))

    def per_head(shape):
        return pl.BlockSpec((1,) + shape, lambda bi, hi: (hi, 0, 0))

    return pl.pallas_call(
        _ret_kernel,
        out_shape=jax.ShapeDtypeStruct((b, t, h * HEAD_W), BF16),
        grid=(b, h),
        in_specs=[col(0), col(h), col(2 * h), col(3 * h),
                  per_head((RET_CHUNK, RET_CHUNK)), per_head((RET_CHUNK, 1)),
                  per_head((RET_CHUNK, 1)), per_head((1, 1))],
        out_specs=pl.BlockSpec((1, t, HEAD_W), lambda bi, hi: (bi, 0, hi)),
        compiler_params=pltpu.CompilerParams(dimension_semantics=("parallel", "parallel"),
                                             vmem_limit_bytes=VMEM_LIMIT),
        name="retention",
    )(proj3, proj3, proj3, proj3, intra, cross, kdec, cdec)


def _diff_kernel(slope_ref, lam_ref, q_ref, k_ref, v_ref, gsub_ref, o_ref, *, tq, tk, out_scale):
    hi = pl.program_id(1)
    qi = pl.program_id(2)
    half = HEAD_W // 2
    slope = slope_ref[hi]

    lp = lam_ref[...]
    lam = (jnp.exp(jnp.sum(lp[0:1] * lp[1:2], axis=1, keepdims=True))
           - jnp.exp(jnp.sum(lp[2:3] * lp[3:4], axis=1, keepdims=True))
           + (1.0 - out_scale))

    q = q_ref[0] * jnp.asarray(half ** -0.5, BF16)
    lane = lax.broadcasted_iota(jnp.int32, q.shape, 1)
    zero = jnp.zeros_like(q)
    qs = jnp.concatenate([jnp.where(lane < half, q, zero),
                          jnp.where(lane >= half, q, zero)], axis=0)

    row = lax.broadcasted_iota(jnp.int32, (2 * tq, tk), 0)
    row = jnp.where(row >= tq, row - tq, row)
    col = lax.broadcasted_iota(jnp.int32, (2 * tq, tk), 1)
    rel = (row - col).astype(F32)
    rel_bias = -slope * rel

    def body(j, carry):
        m, l, acc = carry
        ksl = pl.ds(pl.multiple_of(j * tk, tk), tk)
        k = k_ref[0, ksl, :]
        v = v_ref[0, ksl, :]
        off = (qi * tq - j * tk).astype(F32)
        s = lax.dot_general(qs, k, _NT, preferred_element_type=F32)
        s = s + jnp.where(rel >= -off, rel_bias - slope * off, NEG_INF)
        m_new = jnp.maximum(m, jnp.max(s, axis=1, keepdims=True))
        alpha = jnp.exp(m - m_new)
        p = jnp.exp(s - m_new)
        l = alpha * l + jnp.sum(p, axis=1, keepdims=True)
        acc = alpha * acc + jnp.dot(p.astype(BF16), v, preferred_element_type=F32)
        return m_new, l, acc

    n_kv = (qi * tq + tq - 1) // tk + 1
    m0 = jnp.full((2 * tq, 1), NEG_INF, F32)
    l0 = jnp.zeros((2 * tq, 1), F32)
    a0 = jnp.zeros((2 * tq, HEAD_W), F32)
    _, l, acc = lax.fori_loop(0, n_kv, body, (m0, l0, a0))
    o = acc / l
    dy = o[:tq] - lam * o[tq:]
    o_ref[0] = (_rms(dy) * gsub_ref[...] * out_scale).astype(o_ref.dtype)


def _diff_attention(proj3, slopes, diff_lambda_l, gsub, out_scale, tq=128, tk=512):
    b, t, _ = proj3.shape
    h = DIFF_HEADS
    base = 4 * RET_HEADS
    kern = functools.partial(_diff_kernel, tq=tq, tk=tk, out_scale=out_scale)
    return pl.pallas_call(
        kern,
        out_shape=jax.ShapeDtypeStruct((b, t, h * HEAD_W), BF16),
        grid=(b, h, t // tq),
        in_specs=[pl.BlockSpec(memory_space=pltpu.SMEM),
                  pl.BlockSpec(diff_lambda_l.shape, lambda bi, hi, qi: (0, 0)),
                  pl.BlockSpec((1, tq, HEAD_W), lambda bi, hi, qi: (bi, qi, base + hi)),
                  pl.BlockSpec((1, t, HEAD_W), lambda bi, hi, qi: (bi, 0, base + h + hi)),
                  pl.BlockSpec((1, t, HEAD_W), lambda bi, hi, qi: (bi, 0, base + 2 * h + hi)),
                  pl.BlockSpec((1, HEAD_W), lambda bi, hi, qi: (0, 0))],
        out_specs=pl.BlockSpec((1, tq, HEAD_W), lambda bi, hi, qi: (bi, qi, hi)),
        compiler_params=pltpu.CompilerParams(
            dimension_semantics=("parallel", "parallel", "arbitrary"),
            vmem_limit_bytes=VMEM_LIMIT),
        name="diff_attention",
    )(slopes, diff_lambda_l, proj3, proj3, proj3, gsub)


def _outproj_kernel(x_ref, ret_ref, dif_ref, wo_ref, g_ref, wq_ref, x1_ref, xn_ref, q_ref, *, col_chunk):
    half = ret_ref.shape[1]
    mix = jnp.dot(ret_ref[...], wo_ref[:half, :], preferred_element_type=F32)
    mix = mix + jnp.dot(dif_ref[...], wo_ref[half:, :], preferred_element_type=F32)
    x1 = x_ref[...] + mix
    x1_ref[...] = x1
    xn = (_rms(x1) * g_ref[...]).astype(BF16)
    xn_ref[...] = xn
    for j in range(q_ref.shape[1] // col_chunk):
        sl = slice(j * col_chunk, (j + 1) * col_chunk)
        q_ref[:, sl] = jnp.dot(xn, wq_ref[:, sl], preferred_element_type=F32).astype(q_ref.dtype)


def _out_proj(x2, ret2, dif2, wo_bf16, g, wq_bf16, tm=512):
    n, d = x2.shape
    half = ret2.shape[1]
    qc = wq_bf16.shape[1]
    row = lambda w: pl.BlockSpec((tm, w), lambda i: (i, 0))
    full = lambda a: pl.BlockSpec(a.shape, lambda i: (0, 0))
    return pl.pallas_call(
        functools.partial(_outproj_kernel, col_chunk=512),
        out_shape=(jax.ShapeDtypeStruct((n, d), F32),
                   jax.ShapeDtypeStruct((n, d), BF16),
                   jax.ShapeDtypeStruct((n, qc), BF16)),
        grid=(n // tm,),
        in_specs=[row(d), row(half), row(half), full(wo_bf16), full(g), full(wq_bf16)],
        out_specs=(row(d), row(d), row(qc)),
        compiler_params=pltpu.CompilerParams(dimension_semantics=("parallel",),
                                             vmem_limit_bytes=VMEM_LIMIT),
        name="out_proj",
    )(x2, ret2, dif2, wo_bf16, g, wq_bf16)


def _spread_exact(v, sel_bf16):
    hi = v.astype(BF16)
    r1 = v - hi.astype(F32)
    mid = r1.astype(BF16)
    lo = (r1 - mid.astype(F32)).astype(BF16)
    d = lambda a: jnp.dot(a, sel_bf16, preferred_element_type=F32)
    return (d(hi) + d(mid)) + d(lo)


def _topk_kernel(q_ref, sk_ref, e_ref, g_ref, g2_ref):
    tb = q_ref.shape[0]
    k = PEER_TOPK
    kk = k * k
    n_sets = 2 * PEER_HEADS
    neg = -jnp.inf

    s = jnp.concatenate(
        [lax.dot_general(q_ref[:, j * N_KEYS:(j + 1) * N_KEYS], sk_ref[j // 2, j % 2], _NT,
                         preferred_element_type=F32) for j in range(n_sets)], axis=0)
    lane = lax.broadcasted_iota(jnp.int32, s.shape, 1)
    key_f = lane.astype(F32)
    v16 = jnp.zeros(s.shape, F32)
    i16 = jnp.zeros(s.shape, F32)
    for r in range(k):
        m = jnp.max(s, axis=1, keepdims=True)
        i = jnp.min(jnp.where(s == m, key_f, float(N_KEYS)), axis=1, keepdims=True)
        s = jnp.where(key_f == i, neg, s)
        v16 = jnp.where(lane == r, m, v16)
        i16 = jnp.where(lane == r, i, i16)

    sr = lax.broadcasted_iota(jnp.int32, (N_KEYS, kk), 0)
    sc = lax.broadcasted_iota(jnp.int32, (N_KEYS, kk), 1)
    sel_a = jnp.where(sc // k == sr, 1.0, 0.0).astype(BF16)
    sel_b = jnp.where(sc % k == sr, 1.0, 0.0).astype(BF16)
    cands, eids = [], []
    for h in range(PEER_HEADS):
        lo, mid, hi = 2 * h * tb, (2 * h + 1) * tb, (2 * h + 2) * tb
        cands.append(_spread_exact(v16[lo:mid], sel_a) + _spread_exact(v16[mid:hi], sel_b))
        eids.append(jnp.dot(i16[lo:mid].astype(BF16), sel_a, preferred_element_type=F32) * float(N_KEYS)
                    + jnp.dot(i16[mid:hi].astype(BF16), sel_b, preferred_element_type=F32))
    cand = jnp.concatenate(cands, axis=0)
    eid = jnp.concatenate(eids, axis=0)

    cl_f = lax.broadcasted_iota(jnp.int32, cand.shape, 1).astype(F32)
    top_c, top_e = [], []
    for r in range(k):
        m = jnp.max(cand, axis=1, keepdims=True)
        pos = jnp.min(jnp.where(cand == m, cl_f, float(kk)), axis=1, keepdims=True)
        hit = cl_f == pos
        top_e.append(jnp.sum(jnp.where(hit, eid, 0.0), axis=1, keepdims=True))
        cand = jnp.where(hit, neg, cand)
        top_c.append(m)
    ps = [jnp.exp(c - top_c[0]) for c in top_c]
    denom = ps[0]
    for p in ps[1:]:
        denom = denom + p

    ol = lax.broadcasted_iota(jnp.int32, (tb, N_SEL), 1)
    ol2 = lax.broadcasted_iota(jnp.int32, (tb, 2 * N_SEL), 1)
    e_out = jnp.zeros((tb, N_SEL), F32)
    g1 = jnp.zeros((tb, N_SEL), F32)
    g2 = jnp.zeros((tb, 2 * N_SEL), F32)
    for h in range(PEER_HEADS):
        rows = slice(h * tb, (h + 1) * tb)
        for r in range(k):
            c = h * k + r
            gate = (ps[r] / denom)[rows]
            e_out = jnp.where(ol == c, top_e[r][rows], e_out)
            g1 = jnp.where(ol == c, gate, g1)
            g2 = jnp.where(ol2 == 2 * c + 1, gate, g2)
    e_ref[...] = e_out.astype(jnp.int32)
    g_ref[...] = g1
    g2_ref[...] = g2


def _peer_topk(q2, sk_bf16, tb=128):
    n, qc = q2.shape
    return pl.pallas_call(
        _topk_kernel,
        out_shape=(jax.ShapeDtypeStruct((n, N_SEL), jnp.int32),
                   jax.ShapeDtypeStruct((n, N_SEL), F32),
                   jax.ShapeDtypeStruct((n, 2 * N_SEL), F32)),
        grid=(n // tb,),
        in_specs=[pl.BlockSpec((tb, qc), lambda i: (i, 0)),
                  pl.BlockSpec(sk_bf16.shape, lambda i: (0, 0, 0, 0))],
        out_specs=(pl.BlockSpec((tb, N_SEL), lambda i: (i, 0)),
                   pl.BlockSpec((tb, N_SEL), lambda i: (i, 0)),
                   pl.BlockSpec((tb, 2 * N_SEL), lambda i: (i, 0))),
        compiler_params=pltpu.CompilerParams(dimension_semantics=("parallel",),
                                             vmem_limit_bytes=VMEM_LIMIT),
        name="peer_topk",
    )(q2, sk_bf16)


def _bf16_bits(t):
    return lax.bitcast_convert_type(t.astype(BF16), jnp.uint16).astype(jnp.uint32)


def _pack_tables(u, v):
    e, d = u.shape
    return (_bf16_bits(u) | (_bf16_bits(v) << 16)).reshape(e, d // HEAD_W, HEAD_W)


def _pack_halves(t):
    b = _bf16_bits(t)
    half = t.shape[1] // 2
    return b[:, :half] | (b[:, half:] << 16)


def _mix_kernel(efirst_ref, enext_ref, g2_ref, xn_ref, x1_ref, gfin_ref, tab_hbm, after_ref, o_ref, buf, sem, *,
                final_norm):
    step = pl.program_id(0)
    tb = xn_ref.shape[0]
    n_chunks = tab_hbm.shape[1]

    def issue(e_ref, t, slot):
        for k in range(N_SEL):
            e = e_ref[t, k]
            pltpu.make_async_copy(tab_hbm.at[e], buf.at[slot, pl.ds(k * n_chunks, n_chunks), :],
                                  sem.at[slot]).start(priority=k % 2)

    def wait(slot):
        pltpu.make_async_copy(buf.at[slot], buf.at[slot], sem.at[slot]).wait()

    @pl.when(step == 0)
    def _():
        issue(efirst_ref, 0, 0)

    rid = lax.broadcasted_iota(jnp.int32, (SUBLANES, 2 * N_SEL), 0)

    def group(gi, c):
        base = pl.multiple_of(gi * SUBLANES, SUBLANES)
        xrows = xn_ref[pl.ds(base, SUBLANES), :]
        grows = g2_ref[pl.ds(base, SUBLANES), :]
        acc = x1_ref[pl.ds(base, SUBLANES), :]
        for j in range(SUBLANES):
            slot = j % 2
            issue(enext_ref, base + j, 1 - slot)
            wait(slot)
            uv = jnp.concatenate(
                [pltpu.bitcast(buf[slot, pl.ds(c, N_SEL, stride=n_chunks), :], BF16) for c in range(n_chunks)],
                axis=1)
            s2 = lax.dot_general(xrows, uv, _NT, preferred_element_type=F32)
            act = pltpu.roll(_gelu_exact(s2), 1, axis=1)
            w2 = jnp.where(rid == j, grows * act, 0.0)
            acc = acc + jnp.dot(w2.astype(BF16), uv, preferred_element_type=F32)
        o_ref[pl.ds(base, SUBLANES), :] = acc
        return c

    lax.fori_loop(0, tb // SUBLANES, group, 0)

    @pl.when(step == pl.num_programs(0) - 1)
    def _():
        wait(0)

    if final_norm:
        o_ref[...] = _rms(o_ref[...]) * gfin_ref[...]


def _peer_mix(eids, e_next, gates2, xn, x1, gfin, table, after, row0, rows, final_norm, tb=64):
    d = x1.shape[1]
    assert tb % (2 * SUBLANES) == 0 and row0 % tb == 0 and rows % tb == 0
    blk0 = row0 // tb
    row = lambda w: pl.BlockSpec((tb, w), lambda i: (i + blk0, 0))
    return pl.pallas_call(
        functools.partial(_mix_kernel, final_norm=final_norm),
        out_shape=jax.ShapeDtypeStruct((rows, d), F32),
        grid=(rows // tb,),
        in_specs=[pl.BlockSpec((SUBLANES, N_SEL), lambda i: (row0 // SUBLANES, 0), memory_space=pltpu.SMEM),
                  pl.BlockSpec((tb, N_SEL), lambda i: (i + blk0, 0), memory_space=pltpu.SMEM),
                  row(2 * N_SEL), row(d), row(d),
                  pl.BlockSpec((1, d), lambda i: (0, 0)),
                  pl.BlockSpec(memory_space=pl.ANY),
                  pl.BlockSpec(memory_space=pl.ANY)],
        out_specs=pl.BlockSpec((tb, d), lambda i: (i, 0)),
        scratch_shapes=[pltpu.VMEM((2, N_SEL * (d // HEAD_W), HEAD_W), jnp.uint32),
                        pltpu.SemaphoreType.DMA((2,))],
        compiler_params=pltpu.CompilerParams(dimension_semantics=("arbitrary",),
                                             vmem_limit_bytes=VMEM_LIMIT),
        name="peer_mix",
    )(eids, e_next, gates2, xn, x1, gfin, table, after)


SC_LANES = 16
SC_TILES = 32
SC_TBLK = 8
HI_MASK = 0xFFFF0000
_SC_PARAMS = pltpu.CompilerParams(needs_layout_passes=False)


def _sc_unpack(word):
    return plsc.bitcast(word << 16, F32), plsc.bitcast(word & jnp.uint32(HI_MASK), F32)


def _sc_pipelined_heads(tab_hbm, ev, rows, sem, compute):
    n_items = SC_TBLK * PEER_HEADS

    def gather(it, slot):
        ti = it // PEER_HEADS
        h = it % PEER_HEADS
        return pltpu.make_async_copy(tab_hbm.at[ev.at[ti, pl.ds(h * PEER_TOPK, PEER_TOPK)]], rows.at[slot],
                                     sem.at[slot])

    gather(0, 0).start()

    def pair(jj, carry):
        for slot in range(2):
            it = 2 * jj + slot
            gather(jnp.minimum(it + 1, n_items - 1), 1 - slot).start()
            gather(it, slot).wait()
            compute(it, slot)
        return carry

    lax.fori_loop(0, n_items // 2, pair, 0)
    gather(n_items - 1, 0).wait()


def _sc_scores(x, eids, utab, row0, rows):
    d = x.shape[1]
    dw = d // 2
    nw = dw // SC_LANES
    per_tile = rows // SC_TILES
    assert rows % (SC_TILES * SC_TBLK) == 0 and row0 % SC_TBLK == 0

    def body(x_hbm, e_hbm, tab_hbm, s_hbm, xv, ev, rbuf, sv, tmp, sem):
        tile = lax.axis_index("c") * 16 + lax.axis_index("s")
        lane = lax.iota(jnp.int32, SC_LANES)

        def compute(it, slot):
            ti = it // PEER_HEADS
            h = it % PEER_HEADS

            def chunk(w, accs):
                off = pl.multiple_of(w * SC_LANES, SC_LANES)
                xlo = xv[ti, pl.ds(off, SC_LANES)]
                xhi = xv[ti, pl.ds(dw + off, SC_LANES)]
                out = []
                for j in range(PEER_TOPK):
                    lo, hi = _sc_unpack(rbuf[slot, j, pl.ds(off, SC_LANES)])
                    out.append(accs[j] + lo * xlo + hi * xhi)
                return tuple(out)

            accs = lax.fori_loop(0, nw, chunk, tuple(jnp.zeros((SC_LANES,), F32) for _ in range(PEER_TOPK)))
            for j in range(PEER_TOPK):
                tmp[j, :] = accs[j]
            svec = jnp.zeros((SC_LANES,), F32)
            for l in range(SC_LANES):
                svec = svec + plsc.load_gather(tmp, [lane, jnp.full((SC_LANES,), l, jnp.int32)])
            sv[ti, pl.ds(pl.multiple_of(h * PEER_TOPK, PEER_TOPK), PEER_TOPK)] = svec

        def block(b, carry):
            t_out = pl.multiple_of(tile * per_tile + b * SC_TBLK, SC_TBLK)
            pltpu.sync_copy(x_hbm.at[pl.ds(row0 + t_out, SC_TBLK)], xv)
            pltpu.sync_copy(e_hbm.at[pl.ds(row0 + t_out, SC_TBLK)], ev)
            _sc_pipelined_heads(tab_hbm, ev, rbuf, sem, compute)
            pltpu.sync_copy(sv, s_hbm.at[pl.ds(t_out, SC_TBLK)])
            return carry

        lax.fori_loop(0, per_tile // SC_TBLK, block, 0)

    return pl.kernel(
        body,
        out_type=jax.ShapeDtypeStruct((rows, N_SEL), F32),
        mesh=plsc.VectorSubcoreMesh(core_axis_name="c", subcore_axis_name="s"),
        scratch_types=[pltpu.VMEM((SC_TBLK, d), F32), pltpu.VMEM((SC_TBLK, N_SEL), jnp.int32),
                       pltpu.VMEM((2, PEER_TOPK, dw), jnp.uint32), pltpu.VMEM((SC_TBLK, N_SEL), F32),
                       pltpu.VMEM((PEER_TOPK, SC_LANES), F32), pltpu.SemaphoreType.DMA((2,))],
        compiler_params=_SC_PARAMS,
        name="sc_scores",
    )(x, eids, utab)


def _sc_combine(w, eids, vtab, row0):
    rows = w.shape[0]
    dw = vtab.shape[1]
    d = 2 * dw
    nw = dw // SC_LANES
    per_tile = rows // SC_TILES
    assert rows % (SC_TILES * SC_TBLK) == 0 and row0 % SC_TBLK == 0

    def body(w_hbm, e_hbm, tab_hbm, y_hbm, wv, ev, rbuf, yv, sem):
        tile = lax.axis_index("c") * 16 + lax.axis_index("s")
        zero_i = jnp.zeros((SC_LANES,), jnp.int32)

        def compute(it, slot):
            ti = it // PEER_HEADS
            h = it % PEER_HEADS
            wk = [plsc.load_gather(wv, [zero_i + ti, zero_i + (h * PEER_TOPK + j)]) for j in range(PEER_TOPK)]

            def chunk(c, carry):
                off = pl.multiple_of(c * SC_LANES, SC_LANES)
                alo = jnp.zeros((SC_LANES,), F32)
                ahi = jnp.zeros((SC_LANES,), F32)
                for j in range(PEER_TOPK):
                    lo, hi = _sc_unpack(rbuf[slot, j, pl.ds(off, SC_LANES)])
                    alo = alo + wk[j] * lo
                    ahi = ahi + wk[j] * hi
                first = h == 0
                yv[ti, pl.ds(off, SC_LANES)] = jnp.where(first, 0.0, yv[ti, pl.ds(off, SC_LANES)]) + alo
                yv[ti, pl.ds(dw + off, SC_LANES)] = jnp.where(first, 0.0, yv[ti, pl.ds(dw + off, SC_LANES)]) + ahi
                return carry

            lax.fori_loop(0, nw, chunk, 0)

        def block(b, carry):
            t_out = pl.multiple_of(tile * per_tile + b * SC_TBLK, SC_TBLK)
            pltpu.sync_copy(w_hbm.at[pl.ds(t_out, SC_TBLK)], wv)
            pltpu.sync_copy(e_hbm.at[pl.ds(row0 + t_out, SC_TBLK)], ev)
            _sc_pipelined_heads(tab_hbm, ev, rbuf, sem, compute)
            pltpu.sync_copy(yv, y_hbm.at[pl.ds(t_out, SC_TBLK)])
            return carry

        lax.fori_loop(0, per_tile // SC_TBLK, block, 0)

    return pl.kernel(
        body,
        out_type=jax.ShapeDtypeStruct((rows, d), F32),
        mesh=plsc.VectorSubcoreMesh(core_axis_name="c", subcore_axis_name="s"),
        scratch_types=[pltpu.VMEM((SC_TBLK, N_SEL), F32), pltpu.VMEM((SC_TBLK, N_SEL), jnp.int32),
                       pltpu.VMEM((2, PEER_TOPK, dw), jnp.uint32), pltpu.VMEM((SC_TBLK, d), F32),
                       pltpu.SemaphoreType.DMA((2,))],
        compiler_params=_SC_PARAMS,
        name="sc_combine",
    )(w, eids, vtab)


def _gate_kernel(s_ref, g_ref, after_ref, w_ref):
    w_ref[...] = g_ref[...] * _gelu_exact(s_ref[...])


def _gate_act(s, gates, after, row0, tb=512):
    rows = s.shape[0]
    assert rows % tb == 0 and row0 % tb == 0
    blk0 = row0 // tb
    return pl.pallas_call(
        _gate_kernel,
        out_shape=jax.ShapeDtypeStruct((rows, N_SEL), F32),
        grid=(rows // tb,),
        in_specs=[pl.BlockSpec((tb, N_SEL), lambda i: (i, 0)),
                  pl.BlockSpec((tb, N_SEL), lambda i: (i + blk0, 0)),
                  pl.BlockSpec(memory_space=pl.ANY)],
        out_specs=pl.BlockSpec((tb, N_SEL), lambda i: (i, 0)),
        compiler_params=pltpu.CompilerParams(dimension_semantics=("parallel",)),
        name="peer_gate",
    )(s, gates, after)


def _resid_norm_kernel(x1_ref, y_ref, gfin_ref, o_ref, *, final_norm):
    z = x1_ref[...] + y_ref[...]
    o_ref[...] = _rms(z) * gfin_ref[...] if final_norm else z


def _resid_norm(x1, y, gfin, row0, final_norm, tb=512):
    rows, d = y.shape
    assert rows % tb == 0 and row0 % tb == 0
    blk0 = row0 // tb
    return pl.pallas_call(
        functools.partial(_resid_norm_kernel, final_norm=final_norm),
        out_shape=jax.ShapeDtypeStruct((rows, d), F32),
        grid=(rows // tb,),
        in_specs=[pl.BlockSpec((tb, d), lambda i: (i + blk0, 0)),
                  pl.BlockSpec((tb, d), lambda i: (i, 0)),
                  pl.BlockSpec((1, d), lambda i: (0, 0))],
        out_specs=pl.BlockSpec((tb, d), lambda i: (i, 0)),
        compiler_params=pltpu.CompilerParams(dimension_semantics=("parallel",)),
        name="peer_resid_norm",
    )(x1, y, gfin)


TC_SLICE_FRAC = 3 / 16


def _peer_mix_hybrid(eids, gates, gates2, xn, x1, gfin, u, v, final_norm):
    n, d = x1.shape
    unit = 512
    n1 = int(n * TC_SLICE_FRAC) // unit * unit
    n_tc = 2 * n1
    n_sc = n - n_tc
    e_next = jnp.roll(eids, -1, axis=0)
    table = _pack_tables(u, v)
    s = _sc_scores(xn.astype(F32), eids, _pack_halves(u), n_tc, n_sc)
    out_a = _peer_mix(eids, e_next, gates2, xn, x1, gfin, table, gfin, 0, n1, final_norm)
    w = _gate_act(s, gates, out_a, n_tc)
    y = _sc_combine(w, eids, _pack_halves(v), n_tc)
    out_b = _peer_mix(eids, e_next, gates2, xn, x1, gfin, table, w, n1, n1, final_norm)
    out_c = _resid_norm(x1, y, gfin, n_tc, final_norm)
    return jnp.concatenate([out_a, out_b, out_c], axis=0)


def _retention_tables(dk):
    c = RET_CHUNK
    h = RET_HEADS
    log_gamma = jnp.log1p(-jnp.exp2(-5.0 - jnp.arange(h, dtype=F32)))
    idx = jnp.arange(c, dtype=F32)
    rel = idx[:, None] - idx[None, :]
    intra = jnp.where(rel >= 0, jnp.exp(log_gamma[:, None, None] * jnp.maximum(rel, 0.0)), 0.0)
    cross = jnp.exp(log_gamma[:, None] * (idx + 1.0))
    kdec = jnp.exp(log_gamma[:, None] * (c - 1.0 - idx))
    cdec = jnp.exp(log_gamma * c)
    scale = dk ** -0.5
    return intra * scale, cross[:, :, None], (kdec * scale)[:, :, None], cdec[:, None, None]


def kernel(x, norm_mix_g, w_in, diff_lambda, diff_subln_g, w_out, norm_ffn_g, peer_wq, peer_subkeys,
           peer_u, peer_v, final_norm_g):
    b, t, d = x.shape
    depth = w_in.shape[0]
    n = b * t
    x2 = x.reshape(n, d)
    intra, cross, kdec, cdec = _retention_tables(HEAD_W)
    slopes = jnp.exp2(-8.0 * (jnp.arange(DIFF_HEADS, dtype=F32) + 1.0) / DIFF_HEADS)
    for l in range(depth):
        li = 0.8 - 0.6 * math.exp(-0.3 * l)
        proj = _in_proj(x2, norm_mix_g[l][None, :], w_in[l].astype(BF16))
        proj3 = proj.reshape(b, t, proj.shape[1])
        ret = _retention(proj3, intra, cross, kdec, cdec)
        dif = _diff_attention(proj3, slopes, diff_lambda[l], diff_subln_g[l][None, :], 1.0 - li)
        half = ret.shape[2]
        x1, xn, q = _out_proj(x2, ret.reshape(n, half), dif.reshape(n, half), w_out[l].astype(BF16),
                              norm_ffn_g[l][None, :], peer_wq[l].astype(BF16))
        eids, gates, gates2 = _peer_topk(q, peer_subkeys[l].astype(BF16))
        x2 = _peer_mix_hybrid(eids, gates, gates2, xn, x1, final_norm_g[None, :], peer_u[l], peer_v[l],
                              final_norm=(l == depth - 1))
    return x2.reshape(b, t, d)
```

```python
import functools
import math

import jax
import jax.numpy as jnp
from jax import lax
from jax.experimental import pallas as pl
from jax.experimental.pallas import tpu as pltpu
from jax.experimental.pallas import tpu_sc as plsc

F32 = jnp.float32
BF16 = jnp.bfloat16

RMS_EPS = 1e-6
NEG_INF = -1e30

RET_HEADS = 4
RET_CHUNK = 128
DIFF_HEADS = 4
HEAD_W = 128
PEER_HEADS = 8
PEER_TOPK = 16
N_KEYS = 128
N_SEL = PEER_HEADS * PEER_TOPK

SUBLANES = 8
VMEM_LIMIT = 56 * 1024 * 1024

_NT = (((1,), (1,)), ((), ()))


def _rms(x, eps=RMS_EPS):
    return x * lax.rsqrt(jnp.mean(x * x, axis=-1, keepdims=True) + eps)


def _gelu_exact(x):
    return 0.5 * x * (1.0 + lax.erf(x * (2.0 ** -0.5)))


def _inproj_kernel(x_ref, g_ref, w_ref, after_ref, o_ref, *, col_chunk):
    h = (_rms(x_ref[...]) * g_ref[...]).astype(BF16)
    for j in range(o_ref.shape[1] // col_chunk):
        sl = slice(j * col_chunk, (j + 1) * col_chunk)
        o_ref[:, sl] = jnp.dot(h, w_ref[:, sl], preferred_element_type=F32).astype(o_ref.dtype)


def _in_proj(x2, g, w_bf16, after, row0, rows, tm=512):
    d = x2.shape[1]
    cols = w_bf16.shape[1]
    assert row0 % tm == 0 and rows % tm == 0
    blk0 = row0 // tm
    return pl.pallas_call(
        functools.partial(_inproj_kernel, col_chunk=512),
        out_shape=jax.ShapeDtypeStruct((rows, cols), BF16),
        grid=(rows // tm,),
        in_specs=[pl.BlockSpec((tm, d), lambda i: (i + blk0, 0)),
                  pl.BlockSpec((1, d), lambda i: (0, 0)),
                  pl.BlockSpec((d, cols), lambda i: (0, 0)),
                  pl.BlockSpec(memory_space=pl.ANY)],
        out_specs=pl.BlockSpec((tm, cols), lambda i: (i, 0)),
        compiler_params=pltpu.CompilerParams(dimension_semantics=("parallel",),
                                             vmem_limit_bytes=VMEM_LIMIT),
        name="in_proj",
    )(x2, g, w_bf16, after)


def _ret_kernel(q_ref, k_ref, v_ref, g_ref, intra_ref, cross_ref, kdec_ref, cdec_ref, o_ref):
    c = RET_CHUNK
    n_chunks = q_ref.shape[1] // c
    intra = intra_ref[0]
    cross = cross_ref[0]
    kdec = kdec_ref[0]
    cdec = cdec_ref[0]

    def body(n, s):
        sl = pl.ds(pl.multiple_of(n * c, c), c)
        qi = q_ref[0, sl, :]
        ki = k_ref[0, sl, :]
        vi = v_ref[0, sl, :]
        a = lax.dot_general(qi, ki, _NT, preferred_element_type=F32) * intra
        o = jnp.dot(a.astype(BF16), vi, preferred_element_type=F32)
        o = o + jnp.dot(qi, s.astype(BF16), preferred_element_type=F32) * cross
        kd_t = (ki.astype(F32) * kdec).T.astype(BF16)
        s = s * cdec + jnp.dot(kd_t, vi, preferred_element_type=F32)
        gate = g_ref[0, sl, :].astype(F32)
        o_ref[0, sl, :] = (gate * jax.nn.sigmoid(gate) * _rms(o)).astype(o_ref.dtype)
        return s

    lax.fori_loop(0, n_chunks, body, jnp.zeros((HEAD_W, HEAD_W), F32))


def _retention(proj3, intra, cross, kdec, cdec):
    b, t, _ = proj3.shape
    h = RET_HEADS

    def col(off):
        return pl.BlockSpec((1, t, HEAD_W), lambda bi, hi: (bi, 0, off + hi))

    def per_head(shape):
        return pl.BlockSpec((1,) + shape, lambda bi, hi: (hi, 0, 0))

    return pl.pallas_call(
        _ret_kernel,
        out_shape=jax.ShapeDtypeStruct((b, t, h * HEAD_W), BF16),
        grid=(b, h),
        in_specs=[col(0), col(h), col(2 * h), col(3 * h),
                  per_head((RET_CHUNK, RET_CHUNK)), per_head((RET_CHUNK, 1)),
                  per_head((RET_CHUNK, 1)), per_head((1, 1))],
        out_specs=pl.BlockSpec((1, t, HEAD_W), lambda bi, hi: (bi, 0, hi)),
        compiler_params=pltpu.CompilerParams(dimension_semantics=("parallel", "parallel"),
                                             vmem_limit_bytes=VMEM_LIMIT),
        name="retention",
    )(proj3, proj3, proj3, proj3, intra, cross, kdec, cdec)


def _diff_kernel(slope_ref, lam_ref, q_ref, k_ref, v_ref, gsub_ref, o_ref, *, tq, tk, out_scale):
    hi = pl.program_id(1)
    qi = pl.program_id(2)
    half = HEAD_W // 2
    slope = slope_ref[hi]

    lp = lam_ref[...]
    lam = (jnp.exp(jnp.sum(lp[0:1] * lp[1:2], axis=1, keepdims=True))
           - jnp.exp(jnp.sum(lp[2:3] * lp[3:4], axis=1, keepdims=True))
           + (1.0 - out_scale))

    q = q_ref[0] * jnp.asarray(half ** -0.5, BF16)
    lane = lax.broadcasted_iota(jnp.int32, q.shape, 1)
    zero = jnp.zeros_like(q)
    qs = jnp.concatenate([jnp.where(lane < half, q, zero),
                          jnp.where(lane >= half, q, zero)], axis=0)

    row = lax.broadcasted_iota(jnp.int32, (2 * tq, tk), 0)
    row = jnp.where(row >= tq, row - tq, row)
    col = lax.broadcasted_iota(jnp.int32, (2 * tq, tk), 1)
    rel = (row - col).astype(F32)
    rel_bias = -slope * rel

    def body(j, carry):
        m, l, acc = carry
        ksl = pl.ds(pl.multiple_of(j * tk, tk), tk)
        k = k_ref[0, ksl, :]
        v = v_ref[0, ksl, :]
        off = (qi * tq - j * tk).astype(F32)
        s = lax.dot_general(qs, k, _NT, preferred_element_type=F32)
        s = s + jnp.where(rel >= -off, rel_bias - slope * off, NEG_INF)
        m_new = jnp.maximum(m, jnp.max(s, axis=1, keepdims=True))
        alpha = jnp.exp(m - m_new)
        p = jnp.exp(s - m_new)
        l = alpha * l + jnp.sum(p, axis=1, keepdims=True)
        acc = alpha * acc + jnp.dot(p.astype(BF16), v, preferred_element_type=F32)
        return m_new, l, acc

    n_kv = (qi * tq + tq - 1) // tk + 1
    m0 = jnp.full((2 * tq, 1), NEG_INF, F32)
    l0 = jnp.zeros((2 * tq, 1), F32)
    a0 = jnp.zeros((2 * tq, HEAD_W), F32)
    _, l, acc = lax.fori_loop(0, n_kv, body, (m0, l0, a0))
    o = acc / l
    dy = o[:tq] - lam * o[tq:]
    o_ref[0] = (_rms(dy) * gsub_ref[...] * out_scale).astype(o_ref.dtype)


def _diff_attention(proj3, slopes, diff_lambda_l, gsub, out_scale, tq=128, tk=512):
    b, t, _ = proj3.shape
    h = DIFF_HEADS
    base = 4 * RET_HEADS
    kern = functools.partial(_diff_kernel, tq=tq, tk=tk, out_scale=out_scale)
    return pl.pallas_call(
        kern,
        out_shape=jax.ShapeDtypeStruct((b, t, h * HEAD_W), BF16),
        grid=(b, h, t // tq),
        in_specs=[pl.BlockSpec(memory_space=pltpu.SMEM),
                  pl.BlockSpec(diff_lambda_l.shape, lambda bi, hi, qi: (0, 0)),
                  pl.BlockSpec((1, tq, HEAD_W), lambda bi, hi, qi: (bi, qi, base + hi)),
                  pl.BlockSpec((1, t, HEAD_W), lambda bi, hi, qi: (bi, 0, base + h + hi)),
                  pl.BlockSpec((1, t, HEAD_W), lambda bi, hi, qi: (bi, 0, base + 2 * h + hi)),
                  pl.BlockSpec((1, HEAD_W), lambda bi, hi, qi: (0, 0))],
        out_specs=pl.BlockSpec((1, tq, HEAD_W), lambda bi, hi, qi: (bi, qi, hi)),
        compiler_params=pltpu.CompilerParams(
            dimension_semantics=("parallel", "parallel", "arbitrary"),
            vmem_limit_bytes=VMEM_LIMIT),
        name="diff_attention",
    )(slopes, diff_lambda_l, proj3, proj3, proj3, gsub)


def _outproj_kernel(x_ref, ret_ref, dif_ref, wo_ref, g_ref, wq_ref, x1_ref, xn_ref, q_ref, *, col_chunk):
    half = ret_ref.shape[1]
    mix = jnp.dot(ret_ref[...], wo_ref[:half, :], preferred_element_type=F32)
    mix = mix + jnp.dot(dif_ref[...], wo_ref[half:, :], preferred_element_type=F32)
    x1 = x_ref[...] + mix
    x1_ref[...] = x1
    xn = (_rms(x1) * g_ref[...]).astype(BF16)
    xn_ref[...] = xn
    for j in range(q_ref.shape[1] // col_chunk):
        sl = slice(j * col_chunk, (j + 1) * col_chunk)
        q_ref[:, sl] = jnp.dot(xn, wq_ref[:, sl], preferred_element_type=F32).astype(q_ref.dtype)


def _out_proj(x2, ret2, dif2, wo_bf16, g, wq_bf16, row0, tm=512):
    n, half = ret2.shape
    d = x2.shape[1]
    qc = wq_bf16.shape[1]
    assert row0 % tm == 0 and n % tm == 0
    blk0 = row0 // tm
    row = lambda w: pl.BlockSpec((tm, w), lambda i: (i, 0))
    full = lambda a: pl.BlockSpec(a.shape, lambda i: (0, 0))
    return pl.pallas_call(
        functools.partial(_outproj_kernel, col_chunk=512),
        out_shape=(jax.ShapeDtypeStruct((n, d), F32),
                   jax.ShapeDtypeStruct((n, d), BF16),
                   jax.ShapeDtypeStruct((n, qc), BF16)),
        grid=(n // tm,),
        in_specs=[pl.BlockSpec((tm, d), lambda i: (i + blk0, 0)), row(half), row(half),
                  full(wo_bf16), full(g), full(wq_bf16)],
        out_specs=(row(d), row(d), row(qc)),
        compiler_params=pltpu.CompilerParams(dimension_semantics=("parallel",),
                                             vmem_limit_bytes=VMEM_LIMIT),
        name="out_proj",
    )(x2, ret2, dif2, wo_bf16, g, wq_bf16)


def _spread_exact(v, sel_bf16):
    hi = v.astype(BF16)
    r1 = v - hi.astype(F32)
    mid = r1.astype(BF16)
    lo = (r1 - mid.astype(F32)).astype(BF16)
    d = lambda a: jnp.dot(a, sel_bf16, preferred_element_type=F32)
    return (d(hi) + d(mid)) + d(lo)


def _topk_kernel(q_ref, sk_ref, e_ref, g_ref, g2_ref):
    tb = q_ref.shape[0]
    k = PEER_TOPK
    kk = k * k
    n_sets = 2 * PEER_HEADS
    neg = -jnp.inf

    s = jnp.concatenate(
        [lax.dot_general(q_ref[:, j * N_KEYS:(j + 1) * N_KEYS], sk_ref[j // 2, j % 2], _NT,
                         preferred_element_type=F32) for j in range(n_sets)], axis=0)
    lane = lax.broadcasted_iota(jnp.int32, s.shape, 1)
    key_f = lane.astype(F32)
    v16 = jnp.zeros(s.shape, F32)
    i16 = jnp.zeros(s.shape, F32)
    for r in range(k):
        m = jnp.max(s, axis=1, keepdims=True)
        i = jnp.min(jnp.where(s == m, key_f, float(N_KEYS)), axis=1, keepdims=True)
        s = jnp.where(key_f == i, neg, s)
        v16 = jnp.where(lane == r, m, v16)
        i16 = jnp.where(lane == r, i, i16)

    sr = lax.broadcasted_iota(jnp.int32, (N_KEYS, kk), 0)
    sc = lax.broadcasted_iota(jnp.int32, (N_KEYS, kk), 1)
    sel_a = jnp.where(sc // k == sr, 1.0, 0.0).astype(BF16)
    sel_b = jnp.where(sc % k == sr, 1.0, 0.0).astype(BF16)
    cands, eids = [], []
    for h in range(PEER_HEADS):
        lo, mid, hi = 2 * h * tb, (2 * h + 1) * tb, (2 * h + 2) * tb
        cands.append(_spread_exact(v16[lo:mid], sel_a) + _spread_exact(v16[mid:hi], sel_b))
        eids.append(jnp.dot(i16[lo:mid].astype(BF16), sel_a, preferred_element_type=F32) * float(N_KEYS)
                    + jnp.dot(i16[mid:hi].astype(BF16), sel_b, preferred_element_type=F32))
    cand = jnp.concatenate(cands, axis=0)
    eid = jnp.concatenate(eids, axis=0)

    cl_f = lax.broadcasted_iota(jnp.int32, cand.shape, 1).astype(F32)
    top_c, top_e = [], []
    for r in range(k):
        m = jnp.max(cand, axis=1, keepdims=True)
        pos = jnp.min(jnp.where(cand == m, cl_f, float(kk)), axis=1, keepdims=True)
        hit = cl_f == pos
        top_e.append(jnp.sum(jnp.where(hit, eid, 0.0), axis=1, keepdims=True))
        cand = jnp.where(hit, neg, cand)
        top_c.append(m)
    ps = [jnp.exp(c - top_c[0]) for c in top_c]
    denom = ps[0]
    for p in ps[1:]:
        denom = denom + p

    ol = lax.broadcasted_iota(jnp.int32, (tb, N_SEL), 1)
    ol2 = lax.broadcasted_iota(jnp.int32, (tb, 2 * N_SEL), 1)
    e_out = jnp.zeros((tb, N_SEL), F32)
    g1 = jnp.zeros((tb, N_SEL), F32)
    g2 = jnp.zeros((tb, 2 * N_SEL), F32)
    for h in range(PEER_HEADS):
        rows = slice(h * tb, (h + 1) * tb)
        for r in range(k):
            c = h * k + r
            gate = (ps[r] / denom)[rows]
            e_out = jnp.where(ol == c, top_e[r][rows], e_out)
            g1 = jnp.where(ol == c, gate, g1)
            g2 = jnp.where(ol2 == 2 * c + 1, gate, g2)
    e_ref[...] = e_out.astype(jnp.int32)
    g_ref[...] = g1
    g2_ref[...] = g2


def _peer_topk(q2, sk_bf16, tb=128):
    n, qc = q2.shape
    return pl.pallas_call(
        _topk_kernel,
        out_shape=(jax.ShapeDtypeStruct((n, N_SEL), jnp.int32),
                   jax.ShapeDtypeStruct((n, N_SEL), F32),
                   jax.ShapeDtypeStruct((n, 2 * N_SEL), F32)),
        grid=(n // tb,),
        in_specs=[pl.BlockSpec((tb, qc), lambda i: (i, 0)),
                  pl.BlockSpec(sk_bf16.shape, lambda i: (0, 0, 0, 0))],
        out_specs=(pl.BlockSpec((tb, N_SEL), lambda i: (i, 0)),
                   pl.BlockSpec((tb, N_SEL), lambda i: (i, 0)),
                   pl.BlockSpec((tb, 2 * N_SEL), lambda i: (i, 0))),
        compiler_params=pltpu.CompilerParams(dimension_semantics=("parallel",),
                                             vmem_limit_bytes=VMEM_LIMIT),
        name="peer_topk",
    )(q2, sk_bf16)


def _bf16_bits(t):
    return lax.bitcast_convert_type(t.astype(BF16), jnp.uint16).astype(jnp.uint32)


def _pack_tables(u, v):
    e, d = u.shape
    return (_bf16_bits(u) | (_bf16_bits(v) << 16)).reshape(e, d // HEAD_W, HEAD_W)


def _pack_halves(t):
    b = _bf16_bits(t)
    half = t.shape[1] // 2
    return b[:, :half] | (b[:, half:] << 16)


def _mix_kernel(efirst_ref, enext_ref, g2_ref, xn_ref, x1_ref, gfin_ref, tab_hbm, after_ref, o_ref, buf, sem, *,
                final_norm):
    step = pl.program_id(0)
    tb = xn_ref.shape[0]
    n_chunks = tab_hbm.shape[1]

    def issue(e_ref, t, slot):
        for k in range(N_SEL):
            e = e_ref[t, k]
            pltpu.make_async_copy(tab_hbm.at[e], buf.at[slot, pl.ds(k * n_chunks, n_chunks), :],
                                  sem.at[slot]).start(priority=k % 2)

    def wait(slot):
        pltpu.make_async_copy(buf.at[slot], buf.at[slot], sem.at[slot]).wait()

    @pl.when(step == 0)
    def _():
        issue(efirst_ref, 0, 0)

    rid = lax.broadcasted_iota(jnp.int32, (SUBLANES, 2 * N_SEL), 0)

    def group(gi, c):
        base = pl.multiple_of(gi * SUBLANES, SUBLANES)
        xrows = xn_ref[pl.ds(base, SUBLANES), :]
        grows = g2_ref[pl.ds(base, SUBLANES), :]
        acc = x1_ref[pl.ds(base, SUBLANES), :]
        for j in range(SUBLANES):
            slot = j % 2
            issue(enext_ref, base + j, 1 - slot)
            wait(slot)
            uv = jnp.concatenate(
                [pltpu.bitcast(buf[slot, pl.ds(c, N_SEL, stride=n_chunks), :], BF16) for c in range(n_chunks)],
                axis=1)
            s2 = lax.dot_general(xrows, uv, _NT, preferred_element_type=F32)
            act = pltpu.roll(_gelu_exact(s2), 1, axis=1)
            w2 = jnp.where(rid == j, grows * act, 0.0)
            acc = acc + jnp.dot(w2.astype(BF16), uv, preferred_element_type=F32)
        o_ref[pl.ds(base, SUBLANES), :] = acc
        return c

    lax.fori_loop(0, tb // SUBLANES, group, 0)

    @pl.when(step == pl.num_programs(0) - 1)
    def _():
        wait(0)

    if final_norm:
        o_ref[...] = _rms(o_ref[...]) * gfin_ref[...]


def _peer_mix(eids, e_next, gates2, xn, x1, gfin, table, after, row0, rows, final_norm, tb=64):
    d = x1.shape[1]
    assert tb % (2 * SUBLANES) == 0 and row0 % tb == 0 and rows % tb == 0
    blk0 = row0 // tb
    row = lambda w: pl.BlockSpec((tb, w), lambda i: (i + blk0, 0))
    return pl.pallas_call(
        functools.partial(_mix_kernel, final_norm=final_norm),
        out_shape=jax.ShapeDtypeStruct((rows, d), F32),
        grid=(rows // tb,),
        in_specs=[pl.BlockSpec((SUBLANES, N_SEL), lambda i: (row0 // SUBLANES, 0), memory_space=pltpu.SMEM),
                  pl.BlockSpec((tb, N_SEL), lambda i: (i + blk0, 0), memory_space=pltpu.SMEM),
                  row(2 * N_SEL), row(d), row(d),
                  pl.BlockSpec((1, d), lambda i: (0, 0)),
                  pl.BlockSpec(memory_space=pl.ANY),
                  pl.BlockSpec(memory_space=pl.ANY)],
        out_specs=pl.BlockSpec((tb, d), lambda i: (i, 0)),
        scratch_shapes=[pltpu.VMEM((2, N_SEL * (d // HEAD_W), HEAD_W), jnp.uint32),
                        pltpu.SemaphoreType.DMA((2,))],
        compiler_params=pltpu.CompilerParams(dimension_semantics=("arbitrary",),
                                             vmem_limit_bytes=VMEM_LIMIT),
        name="peer_mix",
    )(eids, e_next, gates2, xn, x1, gfin, table, after)


SC_LANES = 16
SC_TILES = 32
SC_TBLK = 8
HI_MASK = 0xFFFF0000
_SC_PARAMS = pltpu.CompilerParams(needs_layout_passes=False)


def _sc_unpack(word):
    return plsc.bitcast(word << 16, F32), plsc.bitcast(word & jnp.uint32(HI_MASK), F32)


def _sc_pipelined_heads(tab_hbm, ev, rows, sem, compute):
    n_items = SC_TBLK * PEER_HEADS

    def gather(it, slot):
        ti = it // PEER_HEADS
        h = it % PEER_HEADS
        return pltpu.make_async_copy(tab_hbm.at[ev.at[ti, pl.ds(h * PEER_TOPK, PEER_TOPK)]], rows.at[slot],
                                     sem.at[slot])

    gather(0, 0).start()

    def pair(jj, carry):
        for slot in range(2):
            it = 2 * jj + slot
            gather(jnp.minimum(it + 1, n_items - 1), 1 - slot).start()
            gather(it, slot).wait()
            compute(it, slot)
        return carry

    lax.fori_loop(0, n_items // 2, pair, 0)
    gather(n_items - 1, 0).wait()


def _sc_scores(x, eids, utab, row0, rows):
    d = x.shape[1]
    dw = d // 2
    nw = dw // SC_LANES
    per_tile = rows // SC_TILES
    assert rows % (SC_TILES * SC_TBLK) == 0 and row0 % SC_TBLK == 0

    def body(x_hbm, e_hbm, tab_hbm, s_hbm, xv, ev, rbuf, sv, tmp, sem):
        tile = lax.axis_index("c") * 16 + lax.axis_index("s")
        lane = lax.iota(jnp.int32, SC_LANES)

        def compute(it, slot):
            ti = it // PEER_HEADS
            h = it % PEER_HEADS

            def chunk(w, accs):
                off = pl.multiple_of(w * SC_LANES, SC_LANES)
                xlo = xv[ti, pl.ds(off, SC_LANES)]
                xhi = xv[ti, pl.ds(dw + off, SC_LANES)]
                out = []
                for j in range(PEER_TOPK):
                    lo, hi = _sc_unpack(rbuf[slot, j, pl.ds(off, SC_LANES)])
                    out.append(accs[j] + lo * xlo + hi * xhi)
                return tuple(out)

            accs = lax.fori_loop(0, nw, chunk, tuple(jnp.zeros((SC_LANES,), F32) for _ in range(PEER_TOPK)))
            for j in range(PEER_TOPK):
                tmp[j, :] = accs[j]
            svec = jnp.zeros((SC_LANES,), F32)
            for l in range(SC_LANES):
                svec = svec + plsc.load_gather(tmp, [lane, jnp.full((SC_LANES,), l, jnp.int32)])
            sv[ti, pl.ds(pl.multiple_of(h * PEER_TOPK, PEER_TOPK), PEER_TOPK)] = svec

        def block(b, carry):
            t_out = pl.multiple_of(tile * per_tile + b * SC_TBLK, SC_TBLK)
            pltpu.sync_copy(x_hbm.at[pl.ds(row0 + t_out, SC_TBLK)], xv)
            pltpu.sync_copy(e_hbm.at[pl.ds(row0 + t_out, SC_TBLK)], ev)
            _sc_pipelined_heads(tab_hbm, ev, rbuf, sem, compute)
            pltpu.sync_copy(sv, s_hbm.at[pl.ds(t_out, SC_TBLK)])
            return carry

        lax.fori_loop(0, per_tile // SC_TBLK, block, 0)

    return pl.kernel(
        body,
        out_type=jax.ShapeDtypeStruct((rows, N_SEL), F32),
        mesh=plsc.VectorSubcoreMesh(core_axis_name="c", subcore_axis_name="s"),
        scratch_types=[pltpu.VMEM((SC_TBLK, d), F32), pltpu.VMEM((SC_TBLK, N_SEL), jnp.int32),
                       pltpu.VMEM((2, PEER_TOPK, dw), jnp.uint32), pltpu.VMEM((SC_TBLK, N_SEL), F32),
                       pltpu.VMEM((PEER_TOPK, SC_LANES), F32), pltpu.SemaphoreType.DMA((2,))],
        compiler_params=_SC_PARAMS,
        name="sc_scores",
    )(x, eids, utab)


def _sc_combine(w, eids, vtab, row0):
    rows = w.shape[0]
    dw = vtab.shape[1]
    d = 2 * dw
    nw = dw // SC_LANES
    per_tile = rows // SC_TILES
    assert rows % (SC_TILES * SC_TBLK) == 0 and row0 % SC_TBLK == 0

    def body(w_hbm, e_hbm, tab_hbm, y_hbm, wv, ev, rbuf, yv, sem):
        tile = lax.axis_index("c") * 16 + lax.axis_index("s")
        zero_i = jnp.zeros((SC_LANES,), jnp.int32)

        def compute(it, slot):
            ti = it // PEER_HEADS
            h = it % PEER_HEADS
            wk = [plsc.load_gather(wv, [zero_i + ti, zero_i + (h * PEER_TOPK + j)]) for j in range(PEER_TOPK)]

            def chunk(c, carry):
                off = pl.multiple_of(c * SC_LANES, SC_LANES)
                alo = jnp.zeros((SC_LANES,), F32)
                ahi = jnp.zeros((SC_LANES,), F32)
                for j in range(PEER_TOPK):
                    lo, hi = _sc_unpack(rbuf[slot, j, pl.ds(off, SC_LANES)])
                    alo = alo + wk[j] * lo
                    ahi = ahi + wk[j] * hi
                first = h == 0
                yv[ti, pl.ds(off, SC_LANES)] = jnp.where(first, 0.0, yv[ti, pl.ds(off, SC_LANES)]) + alo
                yv[ti, pl.ds(dw + off, SC_LANES)] = jnp.where(first, 0.0, yv[ti, pl.ds(dw + off, SC_LANES)]) + ahi
                return carry

            lax.fori_loop(0, nw, chunk, 0)

        def block(b, carry):
            t_out = pl.multiple_of(tile * per_tile + b * SC_TBLK, SC_TBLK)
            pltpu.sync_copy(w_hbm.at[pl.ds(t_out, SC_TBLK)], wv)
            pltpu.sync_copy(e_hbm.at[pl.ds(row0 + t_out, SC_TBLK)], ev)
            _sc_pipelined_heads(tab_hbm, ev, rbuf, sem, compute)
            pltpu.sync_copy(yv, y_hbm.at[pl.ds(t_out, SC_TBLK)])
            return carry

        lax.fori_loop(0, per_tile // SC_TBLK, block, 0)

    return pl.kernel(
        body,
        out_type=jax.ShapeDtypeStruct((rows, d), F32),
        mesh=plsc.VectorSubcoreMesh(core_axis_name="c", subcore_axis_name="s"),
        scratch_types=[pltpu.VMEM((SC_TBLK, N_SEL), F32), pltpu.VMEM((SC_TBLK, N_SEL), jnp.int32),
                       pltpu.VMEM((2, PEER_TOPK, dw), jnp.uint32), pltpu.VMEM((SC_TBLK, d), F32),
                       pltpu.SemaphoreType.DMA((2,))],
        compiler_params=_SC_PARAMS,
        name="sc_combine",
    )(w, eids, vtab)


def _gate_kernel(s_ref, g_ref, after_ref, w_ref):
    w_ref[...] = g_ref[...] * _gelu_exact(s_ref[...])


def _gate_act(s, gates, after, row0, tb=512):
    rows = s.shape[0]
    assert rows % tb == 0 and row0 % tb == 0
    blk0 = row0 // tb
    return pl.pallas_call(
        _gate_kernel,
        out_shape=jax.ShapeDtypeStruct((rows, N_SEL), F32),
        grid=(rows // tb,),
        in_specs=[pl.BlockSpec((tb, N_SEL), lambda i: (i, 0)),
                  pl.BlockSpec((tb, N_SEL), lambda i: (i + blk0, 0)),
                  pl.BlockSpec(memory_space=pl.ANY)],
        out_specs=pl.BlockSpec((tb, N_SEL), lambda i: (i, 0)),
        compiler_params=pltpu.CompilerParams(dimension_semantics=("parallel",)),
        name="peer_gate",
    )(s, gates, after)


def _resid_norm_kernel(x1_ref, y_ref, gfin_ref, after_ref, o_ref, *, final_norm):
    z = x1_ref[...] + y_ref[...]
    o_ref[...] = _rms(z) * gfin_ref[...] if final_norm else z


def _resid_norm(x1, y, gfin, after, final_norm, tb=512):
    rows, d = y.shape
    assert rows % tb == 0
    return pl.pallas_call(
        functools.partial(_resid_norm_kernel, final_norm=final_norm),
        out_shape=jax.ShapeDtypeStruct((rows, d), F32),
        grid=(rows // tb,),
        in_specs=[pl.BlockSpec((tb, d), lambda i: (i, 0)),
                  pl.BlockSpec((tb, d), lambda i: (i, 0)),
                  pl.BlockSpec((1, d), lambda i: (0, 0)),
                  pl.BlockSpec(memory_space=pl.ANY)],
        out_specs=pl.BlockSpec((tb, d), lambda i: (i, 0)),
        compiler_params=pltpu.CompilerParams(dimension_semantics=("parallel",)),
        name="peer_resid_norm",
    )(x1, y, gfin, after)


N_CHUNKS = 8
SC_CHUNKS = 6
GATE_SPLIT = 6


def _layer(x2, batch, seq, params, consts, li, final_norm):
    (g_mix, w_in, lam_p, g_sub, w_out, g_ffn, wq, sk, u, v, g_fin) = params
    intra, cross, kdec, cdec, slopes = consts
    n, d = x2.shape
    nc = n // N_CHUNKS
    bc = batch // N_CHUNKS
    assert batch % N_CHUNKS == 0
    table = _pack_tables(u, v)
    utab = _pack_halves(u)
    vtab = _pack_halves(v)

    chunks = [None] * N_CHUNKS
    scores = [None] * N_CHUNKS
    ys = [None] * N_CHUNKS
    outs = [None] * N_CHUNKS
    last = g_fin

    def front(c, last):
        proj = _in_proj(x2, g_mix, w_in, last, c * nc, nc)
        proj3 = proj.reshape(bc, seq, proj.shape[1])
        ret = _retention(proj3, intra, cross, kdec, cdec)
        dif = _diff_attention(proj3, slopes, lam_p, g_sub, 1.0 - li)
        half = ret.shape[2]
        x1, xn, q = _out_proj(x2, ret.reshape(nc, half), dif.reshape(nc, half), w_out, g_ffn, wq, c * nc)
        eids, gates, gates2 = _peer_topk(q, sk)
        chunks[c] = (x1, xn, eids, gates, gates2)
        if c < SC_CHUNKS:
            scores[c] = _sc_scores(xn.astype(F32), eids, utab, 0, nc)
        return gates2

    def gate(c, last):
        _, _, eids, gates, _ = chunks[c]
        w = _gate_act(scores[c], gates, last, 0)
        ys[c] = _sc_combine(w, eids, vtab, 0)
        return w

    first_gates = SC_CHUNKS // 2
    for c in range(GATE_SPLIT):
        last = front(c, last)
    for c in range(first_gates):
        last = gate(c, last)
    for c in range(GATE_SPLIT, N_CHUNKS):
        last = front(c, last)
    for c in range(first_gates, SC_CHUNKS):
        last = gate(c, last)
    for c in range(SC_CHUNKS, N_CHUNKS):
        x1, xn, eids, _, gates2 = chunks[c]
        outs[c] = _peer_mix(eids, jnp.roll(eids, -1, axis=0), gates2, xn, x1, g_fin, table, last, 0, nc, final_norm)
        last = outs[c]
    for c in range(SC_CHUNKS):
        outs[c] = _resid_norm(chunks[c][0], ys[c], g_fin, last, final_norm)
        last = outs[c]
    return jnp.concatenate(outs, axis=0)


def _retention_tables(dk):
    c = RET_CHUNK
    h = RET_HEADS
    log_gamma = jnp.log1p(-jnp.exp2(-5.0 - jnp.arange(h, dtype=F32)))
    idx = jnp.arange(c, dtype=F32)
    rel = idx[:, None] - idx[None, :]
    intra = jnp.where(rel >= 0, jnp.exp(log_gamma[:, None, None] * jnp.maximum(rel, 0.0)), 0.0)
    cross = jnp.exp(log_gamma[:, None] * (idx + 1.0))
    kdec = jnp.exp(log_gamma[:, None] * (c - 1.0 - idx))
    cdec = jnp.exp(log_gamma * c)
    scale = dk ** -0.5
    return intra * scale, cross[:, :, None], (kdec * scale)[:, :, None], cdec[:, None, None]


def kernel(x, norm_mix_g, w_in, diff_lambda, diff_subln_g, w_out, norm_ffn_g, peer_wq, peer_subkeys,
           peer_u, peer_v, final_norm_g):
    b, t, d = x.shape
    depth = w_in.shape[0]
    n = b * t
    x2 = x.reshape(n, d)
    consts = _retention_tables(HEAD_W) + (
        jnp.exp2(-8.0 * (jnp.arange(DIFF_HEADS, dtype=F32) + 1.0) / DIFF_HEADS),)
    for l in range(depth):
        li = 0.8 - 0.6 * math.exp(-0.3 * l)
        params = (norm_mix_g[l][None, :], w_in[l].astype(BF16), diff_lambda[l], diff_subln_g[l][None, :],
                  w_out[l].astype(BF16), norm_ffn_g[l][None, :], peer_wq[l].astype(BF16),
                  peer_subkeys[l].astype(BF16), peer_u[l], peer_v[l], final_norm_g[None, :])
        x2 = _layer(x2, b, t, params, consts, li, final_norm=(l == depth - 1))
    return x2.reshape(b, t, d)
```

```python
import functools
import math

import jax
import jax.numpy as jnp
import numpy as np
from jax import lax
from jax.experimental import pallas as pl
from jax.experimental.pallas import tpu as pltpu
from jax.experimental.pallas import tpu_sc as plsc

F32 = jnp.float32
BF16 = jnp.bfloat16

RMS_EPS = 1e-6
NEG_INF = -1e30

RET_HEADS = 4
RET_CHUNK = 128
DIFF_HEADS = 4
HEAD_W = 128
PEER_HEADS = 8
PEER_TOPK = 16
N_KEYS = 128
N_SEL = PEER_HEADS * PEER_TOPK

SUBLANES = 8
VMEM_LIMIT = 56 * 1024 * 1024

_NT = (((1,), (1,)), ((), ()))


def _rms(x, eps=RMS_EPS):
    return x * lax.rsqrt(jnp.mean(x * x, axis=-1, keepdims=True) + eps)


def _gelu_exact(x):
    return 0.5 * x * (1.0 + lax.erf(x * (2.0 ** -0.5)))


def _inproj_kernel(x_ref, g_ref, w_ref, after_ref, o_ref, *, col_chunk):
    h = (_rms(x_ref[...]) * g_ref[...]).astype(BF16)
    for j in range(o_ref.shape[1] // col_chunk):
        sl = slice(j * col_chunk, (j + 1) * col_chunk)
        o_ref[:, sl] = jnp.dot(h, w_ref[:, sl], preferred_element_type=F32).astype(o_ref.dtype)


def _in_proj(x2, g, w_bf16, after, row0, rows, tm=512):
    d = x2.shape[1]
    cols = w_bf16.shape[1]
    assert row0 % tm == 0 and rows % tm == 0
    blk0 = row0 // tm
    return pl.pallas_call(
        functools.partial(_inproj_kernel, col_chunk=512),
        out_shape=jax.ShapeDtypeStruct((rows, cols), BF16),
        grid=(rows // tm,),
        in_specs=[pl.BlockSpec((tm, d), lambda i: (i + blk0, 0)),
                  pl.BlockSpec((1, d), lambda i: (0, 0)),
                  pl.BlockSpec((d, cols), lambda i: (0, 0)),
                  pl.BlockSpec(memory_space=pl.ANY)],
        out_specs=pl.BlockSpec((tm, cols), lambda i: (i, 0)),
        compiler_params=pltpu.CompilerParams(dimension_semantics=("parallel",),
                                             vmem_limit_bytes=VMEM_LIMIT),
        name="in_proj",
    )(x2, g, w_bf16, after)


def _ret_kernel(q_ref, k_ref, v_ref, g_ref, intra_ref, cross_ref, kdec_ref, cdec_ref, o_ref):
    c = RET_CHUNK
    n_chunks = q_ref.shape[1] // c
    intra = intra_ref[0]
    cross = cross_ref[0]
    kdec = kdec_ref[0]
    cdec = cdec_ref[0]

    def body(n, s):
        sl = pl.ds(pl.multiple_of(n * c, c), c)
        qi = q_ref[0, sl, :]
        ki = k_ref[0, sl, :]
        vi = v_ref[0, sl, :]
        a = lax.dot_general(qi, ki, _NT, preferred_element_type=F32) * intra
        o = jnp.dot(a.astype(BF16), vi, preferred_element_type=F32)
        o = o + jnp.dot(qi, s.astype(BF16), preferred_element_type=F32) * cross
        kd_t = (ki.astype(F32) * kdec).T.astype(BF16)
        s = s * cdec + jnp.dot(kd_t, vi, preferred_element_type=F32)
        gate = g_ref[0, sl, :].astype(F32)
        o_ref[0, sl, :] = (gate * jax.nn.sigmoid(gate) * _rms(o)).astype(o_ref.dtype)
        return s

    lax.fori_loop(0, n_chunks, body, jnp.zeros((HEAD_W, HEAD_W), F32))


def _retention(proj3, intra, cross, kdec, cdec):
    b, t, _ = proj3.shape
    h = RET_HEADS

    def col(off):
        return pl.BlockSpec((1, t, HEAD_W), lambda bi, hi: (bi, 0, off + hi))

    def per_head(shape):
        return pl.BlockSpec((1,) + shape, lambda bi, hi: (hi, 0, 0))

    return pl.pallas_call(
        _ret_kernel,
        out_shape=jax.ShapeDtypeStruct((b, t, h * HEAD_W), BF16),
        grid=(b, h),
        in_specs=[col(0), col(h), col(2 * h), col(3 * h),
                  per_head((RET_CHUNK, RET_CHUNK)), per_head((RET_CHUNK, 1)),
                  per_head((RET_CHUNK, 1)), per_head((1, 1))],
        out_specs=pl.BlockSpec((1, t, HEAD_W), lambda bi, hi: (bi, 0, hi)),
        compiler_params=pltpu.CompilerParams(dimension_semantics=("parallel", "parallel"),
                                             vmem_limit_bytes=VMEM_LIMIT),
        name="retention",
    )(proj3, proj3, proj3, proj3, intra, cross, kdec, cdec)


def _diff_kernel(slope_ref, lam_ref, q_ref, k_ref, v_ref, gsub_ref, o_ref, *, tq, tk, out_scale):
    hi = pl.program_id(1)
    qi = pl.program_id(2)
    half = HEAD_W // 2
    slope = slope_ref[hi]

    lp = lam_ref[...]
    lam = (jnp.exp(jnp.sum(lp[0:1] * lp[1:2], axis=1, keepdims=True))
           - jnp.exp(jnp.sum(lp[2:3] * lp[3:4], axis=1, keepdims=True))
           + (1.0 - out_scale))

    q = q_ref[0] * jnp.asarray(half ** -0.5, BF16)
    lane = lax.broadcasted_iota(jnp.int32, q.shape, 1)
    zero = jnp.zeros_like(q)
    qs = jnp.concatenate([jnp.where(lane < half, q, zero),
                          jnp.where(lane >= half, q, zero)], axis=0)

    row = lax.broadcasted_iota(jnp.int32, (2 * tq, tk), 0)
    row = jnp.where(row >= tq, row - tq, row)
    col = lax.broadcasted_iota(jnp.int32, (2 * tq, tk), 1)
    rel = (row - col).astype(F32)
    rel_bias = -slope * rel

    def body(j, carry):
        m, l, acc = carry
        ksl = pl.ds(pl.multiple_of(j * tk, tk), tk)
        k = k_ref[0, ksl, :]
        v = v_ref[0, ksl, :]
        off = (qi * tq - j * tk).astype(F32)
        s = lax.dot_general(qs, k, _NT, preferred_element_type=F32)
        s = s + jnp.where(rel >= -off, rel_bias - slope * off, NEG_INF)
        m_new = jnp.maximum(m, jnp.max(s, axis=1, keepdims=True))
        alpha = jnp.exp(m - m_new)
        p = jnp.exp(s - m_new)
        l = alpha * l + jnp.sum(p, axis=1, keepdims=True)
        acc = alpha * acc + jnp.dot(p.astype(BF16), v, preferred_element_type=F32)
        return m_new, l, acc

    n_kv = (qi * tq + tq - 1) // tk + 1
    m0 = jnp.full((2 * tq, 1), NEG_INF, F32)
    l0 = jnp.zeros((2 * tq, 1), F32)
    a0 = jnp.zeros((2 * tq, HEAD_W), F32)
    _, l, acc = lax.fori_loop(0, n_kv, body, (m0, l0, a0))
    o = acc / l
    dy = o[:tq] - lam * o[tq:]
    o_ref[0] = (_rms(dy) * gsub_ref[...] * out_scale).astype(o_ref.dtype)


def _diff_attention(proj3, slopes, diff_lambda_l, gsub, out_scale, tq=128, tk=512):
    b, t, _ = proj3.shape
    h = DIFF_HEADS
    base = 4 * RET_HEADS
    kern = functools.partial(_diff_kernel, tq=tq, tk=tk, out_scale=out_scale)
    return pl.pallas_call(
        kern,
        out_shape=jax.ShapeDtypeStruct((b, t, h * HEAD_W), BF16),
        grid=(b, h, t // tq),
        in_specs=[pl.BlockSpec(memory_space=pltpu.SMEM),
                  pl.BlockSpec(diff_lambda_l.shape, lambda bi, hi, qi: (0, 0)),
                  pl.BlockSpec((1, tq, HEAD_W), lambda bi, hi, qi: (bi, qi, base + hi)),
                  pl.BlockSpec((1, t, HEAD_W), lambda bi, hi, qi: (bi, 0, base + h + hi)),
                  pl.BlockSpec((1, t, HEAD_W), lambda bi, hi, qi: (bi, 0, base + 2 * h + hi)),
                  pl.BlockSpec((1, HEAD_W), lambda bi, hi, qi: (0, 0))],
        out_specs=pl.BlockSpec((1, tq, HEAD_W), lambda bi, hi, qi: (bi, qi, hi)),
        compiler_params=pltpu.CompilerParams(
            dimension_semantics=("parallel", "parallel", "arbitrary"),
            vmem_limit_bytes=VMEM_LIMIT),
        name="diff_attention",
    )(slopes, diff_lambda_l, proj3, proj3, proj3, gsub)


def _outproj_kernel(x_ref, ret_ref, dif_ref, wo_ref, g_ref, wq_ref, x1_ref, xn_ref, q_ref, *, col_chunk):
    half = ret_ref.shape[1]
    mix = jnp.dot(ret_ref[...], wo_ref[:half, :], preferred_element_type=F32)
    mix = mix + jnp.dot(dif_ref[...], wo_ref[half:, :], preferred_element_type=F32)
    x1 = x_ref[...] + mix
    x1_ref[...] = x1
    xn = (_rms(x1) * g_ref[...]).astype(BF16)
    xn_ref[...] = xn
    for j in range(q_ref.shape[1] // col_chunk):
        sl = slice(j * col_chunk, (j + 1) * col_chunk)
        q_ref[:, sl] = jnp.dot(xn, wq_ref[:, sl], preferred_element_type=F32).astype(q_ref.dtype)


def _out_proj(x2, ret2, dif2, wo_bf16, g, wq_bf16, row0, tm=512):
    n, half = ret2.shape
    d = x2.shape[1]
    qc = wq_bf16.shape[1]
    assert row0 % tm == 0 and n % tm == 0
    blk0 = row0 // tm
    row = lambda w: pl.BlockSpec((tm, w), lambda i: (i, 0))
    full = lambda a: pl.BlockSpec(a.shape, lambda i: (0, 0))
    return pl.pallas_call(
        functools.partial(_outproj_kernel, col_chunk=512),
        out_shape=(jax.ShapeDtypeStruct((n, d), F32),
                   jax.ShapeDtypeStruct((n, d), BF16),
                   jax.ShapeDtypeStruct((n, qc), BF16)),
        grid=(n // tm,),
        in_specs=[pl.BlockSpec((tm, d), lambda i: (i + blk0, 0)), row(half), row(half),
                  full(wo_bf16), full(g), full(wq_bf16)],
        out_specs=(row(d), row(d), row(qc)),
        compiler_params=pltpu.CompilerParams(dimension_semantics=("parallel",),
                                             vmem_limit_bytes=VMEM_LIMIT),
        name="out_proj",
    )(x2, ret2, dif2, wo_bf16, g, wq_bf16)


def _split3(v):
    trunc = lambda x: pltpu.bitcast(pltpu.bitcast(x, jnp.uint32) & jnp.uint32(0xFFFF0000), F32)
    hi = trunc(v)
    r1 = v - hi
    mid = trunc(r1)
    return hi.astype(BF16), mid.astype(BF16), (r1 - mid).astype(BF16)


def _spread_exact(v, sel_bf16):
    hi, mid, lo = _split3(v)
    d = lambda a: jnp.dot(a, sel_bf16, preferred_element_type=F32)
    return (d(hi) + d(mid)) + d(lo)


def _spread_exact_left(sel_bf16, v):
    hi, mid, lo = _split3(v)
    d = lambda a: jnp.dot(sel_bf16, a, preferred_element_type=F32)
    return (d(hi) + d(mid)) + d(lo)


def _topk_kernel(q_ref, sk_ref, sel_ref, pad_ref, e_ref, g_ref, g2_ref):
    tb = q_ref.shape[0]
    k = PEER_TOPK
    n_sets = 2 * PEER_HEADS
    neg = -jnp.inf

    s = jnp.stack(
        [lax.dot_general(sk_ref[j // 2, j % 2], q_ref[:, j * N_KEYS:(j + 1) * N_KEYS], _NT,
                         preferred_element_type=F32) for j in range(n_sets)], axis=0)
    key_f = lax.broadcasted_iota(jnp.int32, s.shape, 1).astype(F32)
    rank = lax.broadcasted_iota(jnp.int32, (n_sets, k, tb), 1)
    top_v = jnp.zeros((n_sets, k, tb), F32)
    top_i = jnp.zeros((n_sets, k, tb), F32)
    for r in range(k):
        m = jnp.max(s, axis=1, keepdims=True)
        i = jnp.min(jnp.where(s == m, key_f, float(N_KEYS)), axis=1, keepdims=True)
        s = jnp.where(key_f == i, neg, s)
        top_v = jnp.where(rank == r, m, top_v)
        top_i = jnp.where(rank == r, i, top_i)

    pad = pad_ref[...]
    cands, eids = [], []
    for h in range(PEER_HEADS):
        cands.append(_spread_exact_left(sel_ref[0], top_v[2 * h]) + _spread_exact_left(sel_ref[1], top_v[2 * h + 1])
                     + pad)
        eids.append(jnp.dot(sel_ref[0], top_i[2 * h].astype(BF16), preferred_element_type=F32) * float(N_KEYS)
                    + jnp.dot(sel_ref[1], top_i[2 * h + 1].astype(BF16), preferred_element_type=F32))
    cand = jnp.stack(cands, axis=0)
    eid = jnp.stack(eids, axis=0)
    n_cand = cand.shape[1]

    c_f = lax.broadcasted_iota(jnp.int32, cand.shape, 1).astype(F32)
    rank2 = lax.broadcasted_iota(jnp.int32, (PEER_HEADS, k, tb), 1)
    top_c = []
    e_rank = jnp.zeros((PEER_HEADS, k, tb), F32)
    for r in range(k):
        m = jnp.max(cand, axis=1, keepdims=True)
        pos = jnp.min(jnp.where(cand == m, c_f, float(n_cand)), axis=1, keepdims=True)
        hit = c_f == pos
        e_rank = jnp.where(rank2 == r, jnp.sum(jnp.where(hit, eid, 0.0), axis=1, keepdims=True), e_rank)
        cand = jnp.where(hit, neg, cand)
        top_c.append(m)
    p_rank = jnp.zeros((PEER_HEADS, k, tb), F32)
    denom = jnp.zeros((PEER_HEADS, 1, tb), F32)
    for r in range(k):
        p = jnp.exp(top_c[r] - top_c[0])
        denom = denom + p
        p_rank = jnp.where(rank2 == r, p, p_rank)
    g_rank = p_rank / denom

    e_out = e_rank.reshape(PEER_HEADS * k, tb).T
    g1 = g_rank.reshape(PEER_HEADS * k, tb).T
    sr = lax.broadcasted_iota(jnp.int32, (N_SEL, 2 * N_SEL), 0)
    sc = lax.broadcasted_iota(jnp.int32, (N_SEL, 2 * N_SEL), 1)
    e_ref[...] = e_out.astype(jnp.int32)
    g_ref[...] = g1
    g2_ref[...] = _spread_exact(g1, jnp.where(sc == 2 * sr + 1, 1.0, 0.0).astype(BF16))


def _candidate_tables(tb):
    k = PEER_TOPK
    pairs = [(a, b) for a in range(k) for b in range(k) if (a + 1) * (b + 1) <= k]
    n_cand = -(-len(pairs) // SUBLANES) * SUBLANES
    sel = np.zeros((2, n_cand, k), np.float32)
    pad = np.full((n_cand, tb), -np.inf, np.float32)
    for c, (a, b) in enumerate(pairs):
        sel[0, c, a] = 1.0
        sel[1, c, b] = 1.0
        pad[c, :] = 0.0
    return jnp.asarray(sel, BF16), jnp.asarray(pad)


def _peer_topk(q2, sk_bf16, tb=128):
    n, qc = q2.shape
    sel, pad = _candidate_tables(tb)
    const = lambda a: pl.BlockSpec(a.shape, lambda i: (0,) * a.ndim)
    return pl.pallas_call(
        _topk_kernel,
        out_shape=(jax.ShapeDtypeStruct((n, N_SEL), jnp.int32),
                   jax.ShapeDtypeStruct((n, N_SEL), F32),
                   jax.ShapeDtypeStruct((n, 2 * N_SEL), F32)),
        grid=(n // tb,),
        in_specs=[pl.BlockSpec((tb, qc), lambda i: (i, 0)), const(sk_bf16), const(sel), const(pad)],
        out_specs=(pl.BlockSpec((tb, N_SEL), lambda i: (i, 0)),
                   pl.BlockSpec((tb, N_SEL), lambda i: (i, 0)),
                   pl.BlockSpec((tb, 2 * N_SEL), lambda i: (i, 0))),
        compiler_params=pltpu.CompilerParams(dimension_semantics=("parallel",),
                                             vmem_limit_bytes=VMEM_LIMIT),
        name="peer_topk",
    )(q2, sk_bf16, sel, pad)


def _bf16_bits(t):
    return lax.bitcast_convert_type(t.astype(BF16), jnp.uint16).astype(jnp.uint32)


def _pack_tables(u, v):
    e, d = u.shape
    return (_bf16_bits(u) | (_bf16_bits(v) << 16)).reshape(e, d // HEAD_W, HEAD_W)


def _pack_halves(t):
    b = _bf16_bits(t)
    half = t.shape[1] // 2
    return b[:, :half] | (b[:, half:] << 16)


def _mix_kernel(efirst_ref, enext_ref, g2_ref, xn_ref, x1_ref, gfin_ref, tab_hbm, after_ref, o_ref, buf, sem, *,
                final_norm):
    step = pl.program_id(0)
    tb = xn_ref.shape[0]
    n_chunks = tab_hbm.shape[1]

    def issue(e_ref, t, slot):
        for k in range(N_SEL):
            e = e_ref[t, k]
            pltpu.make_async_copy(tab_hbm.at[e], buf.at[slot, pl.ds(k * n_chunks, n_chunks), :],
                                  sem.at[slot]).start(priority=k % 2)

    def wait(slot):
        pltpu.make_async_copy(buf.at[slot], buf.at[slot], sem.at[slot]).wait()

    @pl.when(step == 0)
    def _():
        issue(efirst_ref, 0, 0)

    rid = lax.broadcasted_iota(jnp.int32, (SUBLANES, 2 * N_SEL), 0)

    def group(gi, c):
        base = pl.multiple_of(gi * SUBLANES, SUBLANES)
        xrows = xn_ref[pl.ds(base, SUBLANES), :]
        grows = g2_ref[pl.ds(base, SUBLANES), :]
        acc = x1_ref[pl.ds(base, SUBLANES), :]
        for j in range(SUBLANES):
            slot = j % 2
            issue(enext_ref, base + j, 1 - slot)
            wait(slot)
            uv = jnp.concatenate(
                [pltpu.bitcast(buf[slot, pl.ds(c, N_SEL, stride=n_chunks), :], BF16) for c in range(n_chunks)],
                axis=1)
            s2 = lax.dot_general(xrows, uv, _NT, preferred_element_type=F32)
            act = pltpu.roll(_gelu_exact(s2), 1, axis=1)
            w2 = jnp.where(rid == j, grows * act, 0.0)
            acc = acc + jnp.dot(w2.astype(BF16), uv, preferred_element_type=F32)
        o_ref[pl.ds(base, SUBLANES), :] = acc
        return c

    lax.fori_loop(0, tb // SUBLANES, group, 0)

    @pl.when(step == pl.num_programs(0) - 1)
    def _():
        wait(0)

    if final_norm:
        o_ref[...] = _rms(o_ref[...]) * gfin_ref[...]


def _peer_mix(eids, e_next, gates2, xn, x1, gfin, table, after, row0, rows, final_norm, tb=64):
    d = x1.shape[1]
    assert tb % (2 * SUBLANES) == 0 and row0 % tb == 0 and rows % tb == 0
    blk0 = row0 // tb
    row = lambda w: pl.BlockSpec((tb, w), lambda i: (i + blk0, 0))
    return pl.pallas_call(
        functools.partial(_mix_kernel, final_norm=final_norm),
        out_shape=jax.ShapeDtypeStruct((rows, d), F32),
        grid=(rows // tb,),
        in_specs=[pl.BlockSpec((SUBLANES, N_SEL), lambda i: (row0 // SUBLANES, 0), memory_space=pltpu.SMEM),
                  pl.BlockSpec((tb, N_SEL), lambda i: (i + blk0, 0), memory_space=pltpu.SMEM),
                  row(2 * N_SEL), row(d), row(d),
                  pl.BlockSpec((1, d), lambda i: (0, 0)),
                  pl.BlockSpec(memory_space=pl.ANY),
                  pl.BlockSpec(memory_space=pl.ANY)],
        out_specs=pl.BlockSpec((tb, d), lambda i: (i, 0)),
        scratch_shapes=[pltpu.VMEM((2, N_SEL * (d // HEAD_W), HEAD_W), jnp.uint32),
                        pltpu.SemaphoreType.DMA((2,))],
        compiler_params=pltpu.CompilerParams(dimension_semantics=("arbitrary",),
                                             vmem_limit_bytes=VMEM_LIMIT),
        name="peer_mix",
    )(eids, e_next, gates2, xn, x1, gfin, table, after)


SC_LANES = 16
SC_TILES = 32
SC_TBLK = 8
HI_MASK = 0xFFFF0000
_SC_PARAMS = pltpu.CompilerParams(needs_layout_passes=False)


def _sc_unpack(word):
    return plsc.bitcast(word << 16, F32), plsc.bitcast(word & jnp.uint32(HI_MASK), F32)


def _sc_pipelined_heads(tab_hbm, ev, rows, sem, compute):
    n_items = SC_TBLK * PEER_HEADS

    def gather(it, slot):
        ti = it // PEER_HEADS
        h = it % PEER_HEADS
        return pltpu.make_async_copy(tab_hbm.at[ev.at[ti, pl.ds(h * PEER_TOPK, PEER_TOPK)]], rows.at[slot],
                                     sem.at[slot])

    gather(0, 0).start()

    def pair(jj, carry):
        for slot in range(2):
            it = 2 * jj + slot
            gather(jnp.minimum(it + 1, n_items - 1), 1 - slot).start()
            gather(it, slot).wait()
            compute(it, slot)
        return carry

    lax.fori_loop(0, n_items // 2, pair, 0)
    gather(n_items - 1, 0).wait()


def _sc_scores(x, eids, utab, row0, rows):
    d = x.shape[1]
    dw = d // 2
    nw = dw // SC_LANES
    per_tile = rows // SC_TILES
    assert rows % (SC_TILES * SC_TBLK) == 0 and row0 % SC_TBLK == 0

    def body(x_hbm, e_hbm, tab_hbm, s_hbm, xv, ev, rbuf, sv, tmp, sem):
        tile = lax.axis_index("c") * 16 + lax.axis_index("s")
        lane = lax.iota(jnp.int32, SC_LANES)

        def compute(it, slot):
            ti = it // PEER_HEADS
            h = it % PEER_HEADS

            def chunk(w, accs):
                off = pl.multiple_of(w * SC_LANES, SC_LANES)
                xlo = xv[ti, pl.ds(off, SC_LANES)]
                xhi = xv[ti, pl.ds(dw + off, SC_LANES)]
                out = []
                for j in range(PEER_TOPK):
                    lo, hi = _sc_unpack(rbuf[slot, j, pl.ds(off, SC_LANES)])
                    out.append(accs[j] + lo * xlo + hi * xhi)
                return tuple(out)

            accs = lax.fori_loop(0, nw, chunk, tuple(jnp.zeros((SC_LANES,), F32) for _ in range(PEER_TOPK)))
            for j in range(PEER_TOPK):
                tmp[j, :] = accs[j]
            svec = jnp.zeros((SC_LANES,), F32)
            for l in range(SC_LANES):
                svec = svec + plsc.load_gather(tmp, [lane, jnp.full((SC_LANES,), l, jnp.int32)])
            sv[ti, pl.ds(pl.multiple_of(h * PEER_TOPK, PEER_TOPK), PEER_TOPK)] = svec

        def block(b, carry):
            t_out = pl.multiple_of(tile * per_tile + b * SC_TBLK, SC_TBLK)
            pltpu.sync_copy(x_hbm.at[pl.ds(row0 + t_out, SC_TBLK)], xv)
            pltpu.sync_copy(e_hbm.at[pl.ds(row0 + t_out, SC_TBLK)], ev)
            _sc_pipelined_heads(tab_hbm, ev, rbuf, sem, compute)
            pltpu.sync_copy(sv, s_hbm.at[pl.ds(t_out, SC_TBLK)])
            return carry

        lax.fori_loop(0, per_tile // SC_TBLK, block, 0)

    return pl.kernel(
        body,
        out_type=jax.ShapeDtypeStruct((rows, N_SEL), F32),
        mesh=plsc.VectorSubcoreMesh(core_axis_name="c", subcore_axis_name="s"),
        scratch_types=[pltpu.VMEM((SC_TBLK, d), F32), pltpu.VMEM((SC_TBLK, N_SEL), jnp.int32),
                       pltpu.VMEM((2, PEER_TOPK, dw), jnp.uint32), pltpu.VMEM((SC_TBLK, N_SEL), F32),
                       pltpu.VMEM((PEER_TOPK, SC_LANES), F32), pltpu.SemaphoreType.DMA((2,))],
        compiler_params=_SC_PARAMS,
        name="sc_scores",
    )(x, eids, utab)


def _sc_combine(w, eids, vtab, row0):
    rows = w.shape[0]
    dw = vtab.shape[1]
    d = 2 * dw
    nw = dw // SC_LANES
    per_tile = rows // SC_TILES
    assert rows % (SC_TILES * SC_TBLK) == 0 and row0 % SC_TBLK == 0

    def body(w_hbm, e_hbm, tab_hbm, y_hbm, wv, ev, rbuf, yv, sem):
        tile = lax.axis_index("c") * 16 + lax.axis_index("s")
        zero_i = jnp.zeros((SC_LANES,), jnp.int32)

        def compute(it, slot):
            ti = it // PEER_HEADS
            h = it % PEER_HEADS
            wk = [plsc.load_gather(wv, [zero_i + ti, zero_i + (h * PEER_TOPK + j)]) for j in range(PEER_TOPK)]

            def chunk(c, carry):
                off = pl.multiple_of(c * SC_LANES, SC_LANES)
                alo = jnp.zeros((SC_LANES,), F32)
                ahi = jnp.zeros((SC_LANES,), F32)
                for j in range(PEER_TOPK):
                    lo, hi = _sc_unpack(rbuf[slot, j, pl.ds(off, SC_LANES)])
                    alo = alo + wk[j] * lo
                    ahi = ahi + wk[j] * hi
                first = h == 0
                yv[ti, pl.ds(off, SC_LANES)] = jnp.where(first, 0.0, yv[ti, pl.ds(off, SC_LANES)]) + alo
                yv[ti, pl.ds(dw + off, SC_LANES)] = jnp.where(first, 0.0, yv[ti, pl.ds(dw + off, SC_LANES)]) + ahi
                return carry

            lax.fori_loop(0, nw, chunk, 0)

        def block(b, carry):
            t_out = pl.multiple_of(tile * per_tile + b * SC_TBLK, SC_TBLK)
            pltpu.sync_copy(w_hbm.at[pl.ds(t_out, SC_TBLK)], wv)
            pltpu.sync_copy(e_hbm.at[pl.ds(row0 + t_out, SC_TBLK)], ev)
            _sc_pipelined_heads(tab_hbm, ev, rbuf, sem, compute)
            pltpu.sync_copy(yv, y_hbm.at[pl.ds(t_out, SC_TBLK)])
            return carry

        lax.fori_loop(0, per_tile // SC_TBLK, block, 0)

    return pl.kernel(
        body,
        out_type=jax.ShapeDtypeStruct((rows, d), F32),
        mesh=plsc.VectorSubcoreMesh(core_axis_name="c", subcore_axis_name="s"),
        scratch_types=[pltpu.VMEM((SC_TBLK, N_SEL), F32), pltpu.VMEM((SC_TBLK, N_SEL), jnp.int32),
                       pltpu.VMEM((2, PEER_TOPK, dw), jnp.uint32), pltpu.VMEM((SC_TBLK, d), F32),
                       pltpu.SemaphoreType.DMA((2,))],
        compiler_params=_SC_PARAMS,
        name="sc_combine",
    )(w, eids, vtab)


def _gate_kernel(s_ref, g_ref, after_ref, w_ref):
    w_ref[...] = g_ref[...] * _gelu_exact(s_ref[...])


def _gate_act(s, gates, after, row0, tb=512):
    rows = s.shape[0]
    assert rows % tb == 0 and row0 % tb == 0
    blk0 = row0 // tb
    return pl.pallas_call(
        _gate_kernel,
        out_shape=jax.ShapeDtypeStruct((rows, N_SEL), F32),
        grid=(rows // tb,),
        in_specs=[pl.BlockSpec((tb, N_SEL), lambda i: (i, 0)),
                  pl.BlockSpec((tb, N_SEL), lambda i: (i + blk0, 0)),
                  pl.BlockSpec(memory_space=pl.ANY)],
        out_specs=pl.BlockSpec((tb, N_SEL), lambda i: (i, 0)),
        compiler_params=pltpu.CompilerParams(dimension_semantics=("parallel",)),
        name="peer_gate",
    )(s, gates, after)


def _resid_norm_kernel(x1_ref, y_ref, gfin_ref, after_ref, o_ref, *, final_norm):
    z = x1_ref[...] + y_ref[...]
    o_ref[...] = _rms(z) * gfin_ref[...] if final_norm else z


def _resid_norm(x1, y, gfin, after, final_norm, tb=512):
    rows, d = y.shape
    assert rows % tb == 0
    return pl.pallas_call(
        functools.partial(_resid_norm_kernel, final_norm=final_norm),
        out_shape=jax.ShapeDtypeStruct((rows, d), F32),
        grid=(rows // tb,),
        in_specs=[pl.BlockSpec((tb, d), lambda i: (i, 0)),
                  pl.BlockSpec((tb, d), lambda i: (i, 0)),
                  pl.BlockSpec((1, d), lambda i: (0, 0)),
                  pl.BlockSpec(memory_space=pl.ANY)],
        out_specs=pl.BlockSpec((tb, d), lambda i: (i, 0)),
        compiler_params=pltpu.CompilerParams(dimension_semantics=("parallel",)),
        name="peer_resid_norm",
    )(x1, y, gfin, after)


N_CHUNKS = 8
SC_CHUNKS = 6
GATE_SPLIT = 6


def _layer(x2, batch, seq, params, consts, li, final_norm):
    (g_mix, w_in, lam_p, g_sub, w_out, g_ffn, wq, sk, u, v, g_fin) = params
    intra, cross, kdec, cdec, slopes = consts
    n, d = x2.shape
    nc = n // N_CHUNKS
    bc = batch // N_CHUNKS
    assert batch % N_CHUNKS == 0
    table = _pack_tables(u, v)
    utab = _pack_halves(u)
    vtab = _pack_halves(v)

    chunks = [None] * N_CHUNKS
    scores = [None] * N_CHUNKS
    ys = [None] * N_CHUNKS
    outs = [None] * N_CHUNKS
    last = g_fin

    def front(c, last):
        proj = _in_proj(x2, g_mix, w_in, last, c * nc, nc)
        proj3 = proj.reshape(bc, seq, proj.shape[1])
        ret = _retention(proj3, intra, cross, kdec, cdec)
        dif = _diff_attention(proj3, slopes, lam_p, g_sub, 1.0 - li)
        half = ret.shape[2]
        x1, xn, q = _out_proj(x2, ret.reshape(nc, half), dif.reshape(nc, half), w_out, g_ffn, wq, c * nc)
        eids, gates, gates2 = _peer_topk(q, sk)
        chunks[c] = (x1, xn, eids, gates, gates2)
        if c < SC_CHUNKS:
            scores[c] = _sc_scores(xn.astype(F32), eids, utab, 0, nc)
        return gates2

    def gate(c, last):
        _, _, eids, gates, _ = chunks[c]
        w = _gate_act(scores[c], gates, last, 0)
        ys[c] = _sc_combine(w, eids, vtab, 0)
        return w

    def tc_mix(c, last):
        x1, xn, eids, _, gates2 = chunks[c]
        outs[c] = _peer_mix(eids, jnp.roll(eids, -1, axis=0), gates2, xn, x1, g_fin, table, last, 0, nc, final_norm)
        return outs[c]

    first_gates = SC_CHUNKS // 2
    for c in range(GATE_SPLIT):
        last = front(c, last)
    for c in range(first_gates):
        last = gate(c, last)
    for c in range(GATE_SPLIT, N_CHUNKS):
        last = front(c, last)
    last = tc_mix(SC_CHUNKS, last)
    for c in range(first_gates, SC_CHUNKS):
        last = gate(c, last)
    for c in range(SC_CHUNKS + 1, N_CHUNKS):
        last = tc_mix(c, last)
    for c in range(SC_CHUNKS):
        outs[c] = _resid_norm(chunks[c][0], ys[c], g_fin, last, final_norm)
        last = outs[c]
    return jnp.concatenate(outs, axis=0)


def _retention_tables(dk):
    c = RET_CHUNK
    h = RET_HEADS
    log_gamma = jnp.log1p(-jnp.exp2(-5.0 - jnp.arange(h, dtype=F32)))
    idx = jnp.arange(c, dtype=F32)
    rel = idx[:, None] - idx[None, :]
    intra = jnp.where(rel >= 0, jnp.exp(log_gamma[:, None, None] * jnp.maximum(rel, 0.0)), 0.0)
    cross = jnp.exp(log_gamma[:, None] * (idx + 1.0))
    kdec = jnp.exp(log_gamma[:, None] * (c - 1.0 - idx))
    cdec = jnp.exp(log_gamma * c)
    scale = dk ** -0.5
    return intra * scale, cross[:, :, None], (kdec * scale)[:, :, None], cdec[:, None, None]


def kernel(x, norm_mix_g, w_in, diff_lambda, diff_subln_g, w_out, norm_ffn_g, peer_wq, peer_subkeys,
           peer_u, peer_v, final_norm_g):
    b, t, d = x.shape
    depth = w_in.shape[0]
    n = b * t
    x2 = x.reshape(n, d)
    consts = _retention_tables(HEAD_W) + (
        jnp.exp2(-8.0 * (jnp.arange(DIFF_HEADS, dtype=F32) + 1.0) / DIFF_HEADS),)
    for l in range(depth):
        li = 0.8 - 0.6 * math.exp(-0.3 * l)
        params = (norm_mix_g[l][None, :], w_in[l].astype(BF16), diff_lambda[l], diff_subln_g[l][None, :],
                  w_out[l].astype(BF16), norm_ffn_g[l][None, :], peer_wq[l].astype(BF16),
                  peer_subkeys[l].astype(BF16), peer_u[l], peer_v[l], final_norm_g[None, :])
        x2 = _layer(x2, b, t, params, consts, li, final_norm=(l == depth - 1))
    return x2.reshape(b, t, d)
```

```python
import functools
import math

import jax
import jax.numpy as jnp
import numpy as np
from jax import lax
from jax.experimental import pallas as pl
from jax.experimental.pallas import tpu as pltpu
from jax.experimental.pallas import tpu_sc as plsc

F32 = jnp.float32
BF16 = jnp.bfloat16

RMS_EPS = 1e-6
NEG_INF = -1e30

RET_HEADS = 4
RET_CHUNK = 128
DIFF_HEADS = 4
HEAD_W = 128
PEER_HEADS = 8
PEER_TOPK = 16
N_KEYS = 128
N_SEL = PEER_HEADS * PEER_TOPK

SUBLANES = 8
VMEM_LIMIT = 56 * 1024 * 1024

_NT = (((1,), (1,)), ((), ()))


def _rms(x, eps=RMS_EPS):
    return x * lax.rsqrt(jnp.mean(x * x, axis=-1, keepdims=True) + eps)


def _gelu_exact(x):
    return 0.5 * x * (1.0 + lax.erf(x * (2.0 ** -0.5)))


def _inproj_kernel(x_ref, g_ref, w_ref, after_ref, o_ref, *, col_chunk):
    h = (_rms(x_ref[...]) * g_ref[...]).astype(BF16)
    for j in range(o_ref.shape[1] // col_chunk):
        sl = slice(j * col_chunk, (j + 1) * col_chunk)
        o_ref[:, sl] = jnp.dot(h, w_ref[:, sl], preferred_element_type=F32).astype(o_ref.dtype)


def _in_proj(x2, g, w_bf16, after, row0, rows, tm=512):
    d = x2.shape[1]
    cols = w_bf16.shape[1]
    assert row0 % tm == 0 and rows % tm == 0
    blk0 = row0 // tm
    return pl.pallas_call(
        functools.partial(_inproj_kernel, col_chunk=512),
        out_shape=jax.ShapeDtypeStruct((rows, cols), BF16),
        grid=(rows // tm,),
        in_specs=[pl.BlockSpec((tm, d), lambda i: (i + blk0, 0)),
                  pl.BlockSpec((1, d), lambda i: (0, 0)),
                  pl.BlockSpec((d, cols), lambda i: (0, 0)),
                  pl.BlockSpec(memory_space=pl.ANY)],
        out_specs=pl.BlockSpec((tm, cols), lambda i: (i, 0)),
        compiler_params=pltpu.CompilerParams(dimension_semantics=("parallel",),
                                             vmem_limit_bytes=VMEM_LIMIT),
        name="in_proj",
    )(x2, g, w_bf16, after)


def _ret_kernel(q_ref, k_ref, v_ref, g_ref, intra_ref, cross_ref, kdec_ref, cdec_ref, o_ref):
    c = RET_CHUNK
    n_chunks = q_ref.shape[1] // c
    intra = intra_ref[0]
    cross = cross_ref[0]
    kdec = kdec_ref[0]
    cdec = cdec_ref[0]

    def body(n, s):
        sl = pl.ds(pl.multiple_of(n * c, c), c)
        qi = q_ref[0, sl, :]
        ki = k_ref[0, sl, :]
        vi = v_ref[0, sl, :]
        a = lax.dot_general(qi, ki, _NT, preferred_element_type=F32) * intra
        o = jnp.dot(a.astype(BF16), vi, preferred_element_type=F32)
        o = o + jnp.dot(qi, s.astype(BF16), preferred_element_type=F32) * cross
        kd_t = (ki.astype(F32) * kdec).T.astype(BF16)
        s = s * cdec + jnp.dot(kd_t, vi, preferred_element_type=F32)
        gate = g_ref[0, sl, :].astype(F32)
        o_ref[0, sl, :] = (gate * jax.nn.sigmoid(gate) * _rms(o)).astype(o_ref.dtype)
        return s

    lax.fori_loop(0, n_chunks, body, jnp.zeros((HEAD_W, HEAD_W), F32))


def _retention(proj3, intra, cross, kdec, cdec):
    b, t, _ = proj3.shape
    h = RET_HEADS

    def col(off):
        return pl.BlockSpec((1, t, HEAD_W), lambda bi, hi: (bi, 0, off + hi))

    def per_head(shape):
        return pl.BlockSpec((1,) + shape, lambda bi, hi: (hi, 0, 0))

    return pl.pallas_call(
        _ret_kernel,
        out_shape=jax.ShapeDtypeStruct((b, t, h * HEAD_W), BF16),
        grid=(b, h),
        in_specs=[col(0), col(h), col(2 * h), col(3 * h),
                  per_head((RET_CHUNK, RET_CHUNK)), per_head((RET_CHUNK, 1)),
                  per_head((RET_CHUNK, 1)), per_head((1, 1))],
        out_specs=pl.BlockSpec((1, t, HEAD_W), lambda bi, hi: (bi, 0, hi)),
        compiler_params=pltpu.CompilerParams(dimension_semantics=("parallel", "parallel"),
                                             vmem_limit_bytes=VMEM_LIMIT),
        name="retention",
    )(proj3, proj3, proj3, proj3, intra, cross, kdec, cdec)


def _diff_kernel(slope_ref, lam_ref, q_ref, k_ref, v_ref, gsub_ref, o_ref, *, tq, tk, out_scale):
    hi = pl.program_id(1)
    qi = pl.program_id(2)
    half = HEAD_W // 2
    slope = slope_ref[hi]

    lp = lam_ref[...]
    lam = (jnp.exp(jnp.sum(lp[0:1] * lp[1:2], axis=1, keepdims=True))
           - jnp.exp(jnp.sum(lp[2:3] * lp[3:4], axis=1, keepdims=True))
           + (1.0 - out_scale))

    q = q_ref[0] * jnp.asarray(half ** -0.5, BF16)
    lane = lax.broadcasted_iota(jnp.int32, q.shape, 1)
    zero = jnp.zeros_like(q)
    qs = jnp.concatenate([jnp.where(lane < half, q, zero),
                          jnp.where(lane >= half, q, zero)], axis=0)

    row = lax.broadcasted_iota(jnp.int32, (2 * tq, tk), 0)
    row = jnp.where(row >= tq, row - tq, row)
    col = lax.broadcasted_iota(jnp.int32, (2 * tq, tk), 1)
    rel = (row - col).astype(F32)
    rel_bias = -slope * rel

    def body(j, carry):
        m, l, acc = carry
        ksl = pl.ds(pl.multiple_of(j * tk, tk), tk)
        k = k_ref[0, ksl, :]
        v = v_ref[0, ksl, :]
        off = (qi * tq - j * tk).astype(F32)
        s = lax.dot_general(qs, k, _NT, preferred_element_type=F32)
        s = s + jnp.where(rel >= -off, rel_bias - slope * off, NEG_INF)
        m_new = jnp.maximum(m, jnp.max(s, axis=1, keepdims=True))
        alpha = jnp.exp(m - m_new)
        p = jnp.exp(s - m_new)
        l = alpha * l + jnp.sum(p, axis=1, keepdims=True)
        acc = alpha * acc + jnp.dot(p.astype(BF16), v, preferred_element_type=F32)
        return m_new, l, acc

    n_kv = (qi * tq + tq - 1) // tk + 1
    m0 = jnp.full((2 * tq, 1), NEG_INF, F32)
    l0 = jnp.zeros((2 * tq, 1), F32)
    a0 = jnp.zeros((2 * tq, HEAD_W), F32)
    _, l, acc = lax.fori_loop(0, n_kv, body, (m0, l0, a0))
    o = acc / l
    dy = o[:tq] - lam * o[tq:]
    o_ref[0] = (_rms(dy) * gsub_ref[...] * out_scale).astype(o_ref.dtype)


def _diff_attention(proj3, slopes, diff_lambda_l, gsub, out_scale, tq=128, tk=512):
    b, t, _ = proj3.shape
    h = DIFF_HEADS
    base = 4 * RET_HEADS
    kern = functools.partial(_diff_kernel, tq=tq, tk=tk, out_scale=out_scale)
    return pl.pallas_call(
        kern,
        out_shape=jax.ShapeDtypeStruct((b, t, h * HEAD_W), BF16),
        grid=(b, h, t // tq),
        in_specs=[pl.BlockSpec(memory_space=pltpu.SMEM),
                  pl.BlockSpec(diff_lambda_l.shape, lambda bi, hi, qi: (0, 0)),
                  pl.BlockSpec((1, tq, HEAD_W), lambda bi, hi, qi: (bi, qi, base + hi)),
                  pl.BlockSpec((1, t, HEAD_W), lambda bi, hi, qi: (bi, 0, base + h + hi)),
                  pl.BlockSpec((1, t, HEAD_W), lambda bi, hi, qi: (bi, 0, base + 2 * h + hi)),
                  pl.BlockSpec((1, HEAD_W), lambda bi, hi, qi: (0, 0))],
        out_specs=pl.BlockSpec((1, tq, HEAD_W), lambda bi, hi, qi: (bi, qi, hi)),
        compiler_params=pltpu.CompilerParams(
            dimension_semantics=("parallel", "parallel", "arbitrary"),
            vmem_limit_bytes=VMEM_LIMIT),
        name="diff_attention",
    )(slopes, diff_lambda_l, proj3, proj3, proj3, gsub)


def _outproj_kernel(x_ref, ret_ref, dif_ref, wo_ref, g_ref, wq_ref, x1_ref, xn_ref, q_ref, *, col_chunk):
    half = ret_ref.shape[1]
    mix = jnp.dot(ret_ref[...], wo_ref[:half, :], preferred_element_type=F32)
    mix = mix + jnp.dot(dif_ref[...], wo_ref[half:, :], preferred_element_type=F32)
    x1 = x_ref[...] + mix
    x1_ref[...] = x1
    xn = (_rms(x1) * g_ref[...]).astype(BF16)
    xn_ref[...] = xn
    for j in range(q_ref.shape[1] // col_chunk):
        sl = slice(j * col_chunk, (j + 1) * col_chunk)
        q_ref[:, sl] = jnp.dot(xn, wq_ref[:, sl], preferred_element_type=F32).astype(q_ref.dtype)


def _out_proj(x2, ret2, dif2, wo_bf16, g, wq_bf16, row0, tm=512):
    n, half = ret2.shape
    d = x2.shape[1]
    qc = wq_bf16.shape[1]
    assert row0 % tm == 0 and n % tm == 0
    blk0 = row0 // tm
    row = lambda w: pl.BlockSpec((tm, w), lambda i: (i, 0))
    full = lambda a: pl.BlockSpec(a.shape, lambda i: (0, 0))
    return pl.pallas_call(
        functools.partial(_outproj_kernel, col_chunk=512),
        out_shape=(jax.ShapeDtypeStruct((n, d), F32),
                   jax.ShapeDtypeStruct((n, d), BF16),
                   jax.ShapeDtypeStruct((n, qc), BF16)),
        grid=(n // tm,),
        in_specs=[pl.BlockSpec((tm, d), lambda i: (i + blk0, 0)), row(half), row(half),
                  full(wo_bf16), full(g), full(wq_bf16)],
        out_specs=(row(d), row(d), row(qc)),
        compiler_params=pltpu.CompilerParams(dimension_semantics=("parallel",),
                                             vmem_limit_bytes=VMEM_LIMIT),
        name="out_proj",
    )(x2, ret2, dif2, wo_bf16, g, wq_bf16)


def _split3(v):
    trunc = lambda x: pltpu.bitcast(pltpu.bitcast(x, jnp.uint32) & jnp.uint32(0xFFFF0000), F32)
    hi = trunc(v)
    r1 = v - hi
    mid = trunc(r1)
    return hi.astype(BF16), mid.astype(BF16), (r1 - mid).astype(BF16)


def _spread_exact(v, sel_bf16):
    hi, mid, lo = _split3(v)
    d = lambda a: jnp.dot(a, sel_bf16, preferred_element_type=F32)
    return (d(hi) + d(mid)) + d(lo)


def _spread_exact_left(sel_bf16, v):
    hi, mid, lo = _split3(v)
    d = lambda a: jnp.dot(sel_bf16, a, preferred_element_type=F32)
    return (d(hi) + d(mid)) + d(lo)


def _topk_kernel(q_ref, sk_ref, sel_ref, pad_ref, e_ref, g_ref, g2_ref):
    tb = q_ref.shape[0]
    k = PEER_TOPK
    n_sets = 2 * PEER_HEADS
    neg = -jnp.inf

    s = jnp.stack(
        [lax.dot_general(sk_ref[j // 2, j % 2], q_ref[:, j * N_KEYS:(j + 1) * N_KEYS], _NT,
                         preferred_element_type=F32) for j in range(n_sets)], axis=0)
    key_f = lax.broadcasted_iota(jnp.int32, s.shape, 1).astype(F32)
    rank = lax.broadcasted_iota(jnp.int32, (n_sets, k, tb), 1)
    top_v = jnp.zeros((n_sets, k, tb), F32)
    top_i = jnp.zeros((n_sets, k, tb), F32)
    for r in range(k):
        m = jnp.max(s, axis=1, keepdims=True)
        i = jnp.min(jnp.where(s == m, key_f, float(N_KEYS)), axis=1, keepdims=True)
        s = jnp.where(key_f == i, neg, s)
        top_v = jnp.where(rank == r, m, top_v)
        top_i = jnp.where(rank == r, i, top_i)

    pad = pad_ref[...]
    cands, eids = [], []
    for h in range(PEER_HEADS):
        cands.append(_spread_exact_left(sel_ref[0], top_v[2 * h]) + _spread_exact_left(sel_ref[1], top_v[2 * h + 1])
                     + pad)
        eids.append(jnp.dot(sel_ref[0], top_i[2 * h].astype(BF16), preferred_element_type=F32) * float(N_KEYS)
                    + jnp.dot(sel_ref[1], top_i[2 * h + 1].astype(BF16), preferred_element_type=F32))
    cand = jnp.stack(cands, axis=0)
    eid = jnp.stack(eids, axis=0)
    n_cand = cand.shape[1]

    c_f = lax.broadcasted_iota(jnp.int32, cand.shape, 1).astype(F32)
    rank2 = lax.broadcasted_iota(jnp.int32, (PEER_HEADS, k, tb), 1)
    top_c = []
    e_rank = jnp.zeros((PEER_HEADS, k, tb), F32)
    for r in range(k):
        m = jnp.max(cand, axis=1, keepdims=True)
        pos = jnp.min(jnp.where(cand == m, c_f, float(n_cand)), axis=1, keepdims=True)
        hit = c_f == pos
        e_rank = jnp.where(rank2 == r, jnp.sum(jnp.where(hit, eid, 0.0), axis=1, keepdims=True), e_rank)
        cand = jnp.where(hit, neg, cand)
        top_c.append(m)
    p_rank = jnp.zeros((PEER_HEADS, k, tb), F32)
    denom = jnp.zeros((PEER_HEADS, 1, tb), F32)
    for r in range(k):
        p = jnp.exp(top_c[r] - top_c[0])
        denom = denom + p
        p_rank = jnp.where(rank2 == r, p, p_rank)
    g_rank = p_rank / denom

    e_out = e_rank.reshape(PEER_HEADS * k, tb).T
    g1 = g_rank.reshape(PEER_HEADS * k, tb).T
    sr = lax.broadcasted_iota(jnp.int32, (N_SEL, 2 * N_SEL), 0)
    sc = lax.broadcasted_iota(jnp.int32, (N_SEL, 2 * N_SEL), 1)
    e_ref[...] = e_out.astype(jnp.int32)
    g_ref[...] = g1
    g2_ref[...] = _spread_exact(g1, jnp.where(sc == 2 * sr + 1, 1.0, 0.0).astype(BF16))


def _candidate_tables(tb):
    k = PEER_TOPK
    pairs = [(a, b) for a in range(k) for b in range(k) if (a + 1) * (b + 1) <= k]
    n_cand = -(-len(pairs) // SUBLANES) * SUBLANES
    sel = np.zeros((2, n_cand, k), np.float32)
    pad = np.full((n_cand, tb), -np.inf, np.float32)
    for c, (a, b) in enumerate(pairs):
        sel[0, c, a] = 1.0
        sel[1, c, b] = 1.0
        pad[c, :] = 0.0
    return jnp.asarray(sel, BF16), jnp.asarray(pad)


def _peer_topk(q2, sk_bf16, tb=128):
    n, qc = q2.shape
    sel, pad = _candidate_tables(tb)
    const = lambda a: pl.BlockSpec(a.shape, lambda i: (0,) * a.ndim)
    return pl.pallas_call(
        _topk_kernel,
        out_shape=(jax.ShapeDtypeStruct((n, N_SEL), jnp.int32),
                   jax.ShapeDtypeStruct((n, N_SEL), F32),
                   jax.ShapeDtypeStruct((n, 2 * N_SEL), F32)),
        grid=(n // tb,),
        in_specs=[pl.BlockSpec((tb, qc), lambda i: (i, 0)), const(sk_bf16), const(sel), const(pad)],
        out_specs=(pl.BlockSpec((tb, N_SEL), lambda i: (i, 0)),
                   pl.BlockSpec((tb, N_SEL), lambda i: (i, 0)),
                   pl.BlockSpec((tb, 2 * N_SEL), lambda i: (i, 0))),
        compiler_params=pltpu.CompilerParams(dimension_semantics=("parallel",),
                                             vmem_limit_bytes=VMEM_LIMIT),
        name="peer_topk",
    )(q2, sk_bf16, sel, pad)


def _bf16_bits(t):
    return lax.bitcast_convert_type(t.astype(BF16), jnp.uint16).astype(jnp.uint32)


def _pack_tables(u, v):
    e, d = u.shape
    return (_bf16_bits(u) | (_bf16_bits(v) << 16)).reshape(e, d // HEAD_W, HEAD_W)


def _pack_halves(t):
    b = _bf16_bits(t)
    half = t.shape[1] // 2
    return b[:, :half] | (b[:, half:] << 16)


def _mix_kernel(efirst_ref, enext_ref, g2_ref, xn_ref, x1_ref, gfin_ref, tab_hbm, after_ref, o_ref, buf, sem, *,
                final_norm):
    step = pl.program_id(0)
    tb = xn_ref.shape[0]
    n_chunks = tab_hbm.shape[1]

    def issue(e_ref, t, slot):
        for k in range(N_SEL):
            e = e_ref[t, k]
            pltpu.make_async_copy(tab_hbm.at[e], buf.at[slot, pl.ds(k * n_chunks, n_chunks), :],
                                  sem.at[slot]).start(priority=k % 2)

    def wait(slot):
        pltpu.make_async_copy(buf.at[slot], buf.at[slot], sem.at[slot]).wait()

    @pl.when(step == 0)
    def _():
        issue(efirst_ref, 0, 0)

    rid = lax.broadcasted_iota(jnp.int32, (SUBLANES, 2 * N_SEL), 0)

    def group(gi, c):
        base = pl.multiple_of(gi * SUBLANES, SUBLANES)
        xrows = xn_ref[pl.ds(base, SUBLANES), :]
        grows = g2_ref[pl.ds(base, SUBLANES), :]
        acc = x1_ref[pl.ds(base, SUBLANES), :]
        for j in range(SUBLANES):
            slot = j % 2
            issue(enext_ref, base + j, 1 - slot)
            wait(slot)
            uv = jnp.concatenate(
                [pltpu.bitcast(buf[slot, pl.ds(c, N_SEL, stride=n_chunks), :], BF16) for c in range(n_chunks)],
                axis=1)
            s2 = lax.dot_general(xrows, uv, _NT, preferred_element_type=F32)
            act = pltpu.roll(_gelu_exact(s2), 1, axis=1)
            w2 = jnp.where(rid == j, grows * act, 0.0)
            acc = acc + jnp.dot(w2.astype(BF16), uv, preferred_element_type=F32)
        o_ref[pl.ds(base, SUBLANES), :] = acc
        return c

    lax.fori_loop(0, tb // SUBLANES, group, 0)

    @pl.when(step == pl.num_programs(0) - 1)
    def _():
        wait(0)

    if final_norm:
        o_ref[...] = _rms(o_ref[...]) * gfin_ref[...]


def _peer_mix(eids, e_next, gates2, xn, x1, gfin, table, after, row0, rows, final_norm, tb=64):
    d = x1.shape[1]
    assert tb % (2 * SUBLANES) == 0 and row0 % tb == 0 and rows % tb == 0
    blk0 = row0 // tb
    row = lambda w: pl.BlockSpec((tb, w), lambda i: (i + blk0, 0))
    return pl.pallas_call(
        functools.partial(_mix_kernel, final_norm=final_norm),
        out_shape=jax.ShapeDtypeStruct((rows, d), F32),
        grid=(rows // tb,),
        in_specs=[pl.BlockSpec((SUBLANES, N_SEL), lambda i: (row0 // SUBLANES, 0), memory_space=pltpu.SMEM),
                  pl.BlockSpec((tb, N_SEL), lambda i: (i + blk0, 0), memory_space=pltpu.SMEM),
                  row(2 * N_SEL), row(d), row(d),
                  pl.BlockSpec((1, d), lambda i: (0, 0)),
                  pl.BlockSpec(memory_space=pl.ANY),
                  pl.BlockSpec(memory_space=pl.ANY)],
        out_specs=pl.BlockSpec((tb, d), lambda i: (i, 0)),
        scratch_shapes=[pltpu.VMEM((2, N_SEL * (d // HEAD_W), HEAD_W), jnp.uint32),
                        pltpu.SemaphoreType.DMA((2,))],
        compiler_params=pltpu.CompilerParams(dimension_semantics=("arbitrary",),
                                             vmem_limit_bytes=VMEM_LIMIT),
        name="peer_mix",
    )(eids, e_next, gates2, xn, x1, gfin, table, after)


SC_LANES = 16
SC_TILES = 32
SC_TBLK = 8
SC_SLOTS = 4
HI_MASK = 0xFFFF0000
_SC_PARAMS = pltpu.CompilerParams(needs_layout_passes=False)


def _sc_unpack(word):
    return plsc.bitcast(word << 16, F32), plsc.bitcast(word & jnp.uint32(HI_MASK), F32)


def _sc_pipelined_heads(tab_hbm, ev, rows, sem, compute):
    n_items = SC_TBLK * PEER_HEADS
    ahead = SC_SLOTS - 1
    assert n_items % SC_SLOTS == 0

    def gather(it, slot):
        ti = it // PEER_HEADS
        h = it % PEER_HEADS
        return pltpu.make_async_copy(tab_hbm.at[ev.at[ti, pl.ds(h * PEER_TOPK, PEER_TOPK)]], rows.at[slot],
                                     sem.at[slot])

    for a in range(ahead):
        gather(a, a).start()

    def group(jj, carry):
        for slot in range(SC_SLOTS):
            it = SC_SLOTS * jj + slot
            gather(jnp.minimum(it + ahead, n_items - 1), (slot + ahead) % SC_SLOTS).start()
            gather(it, slot).wait()
            compute(it, slot)
        return carry

    lax.fori_loop(0, n_items // SC_SLOTS, group, 0)
    for a in range(ahead):
        gather(n_items - 1, (n_items + a) % SC_SLOTS).wait()


def _sc_scores(x, eids, utab, row0, rows):
    d = x.shape[1]
    dw = d // 2
    nw = dw // SC_LANES
    per_tile = rows // SC_TILES
    assert rows % (SC_TILES * SC_TBLK) == 0 and row0 % SC_TBLK == 0

    def body(x_hbm, e_hbm, tab_hbm, s_hbm, xv, ev, rbuf, sv, tmp, sem):
        tile = lax.axis_index("c") * 16 + lax.axis_index("s")
        lane = lax.iota(jnp.int32, SC_LANES)

        def compute(it, slot):
            ti = it // PEER_HEADS
            h = it % PEER_HEADS

            def chunk(w, accs):
                off = pl.multiple_of(w * SC_LANES, SC_LANES)
                xlo = xv[ti, pl.ds(off, SC_LANES)]
                xhi = xv[ti, pl.ds(dw + off, SC_LANES)]
                out = []
                for j in range(PEER_TOPK):
                    lo, hi = _sc_unpack(rbuf[slot, j, pl.ds(off, SC_LANES)])
                    out.append(accs[j] + lo * xlo + hi * xhi)
                return tuple(out)

            accs = lax.fori_loop(0, nw, chunk, tuple(jnp.zeros((SC_LANES,), F32) for _ in range(PEER_TOPK)))
            for j in range(PEER_TOPK):
                tmp[j, :] = accs[j]
            svec = jnp.zeros((SC_LANES,), F32)
            for l in range(SC_LANES):
                svec = svec + plsc.load_gather(tmp, [lane, jnp.full((SC_LANES,), l, jnp.int32)])
            sv[ti, pl.ds(pl.multiple_of(h * PEER_TOPK, PEER_TOPK), PEER_TOPK)] = svec

        def block(b, carry):
            t_out = pl.multiple_of(tile * per_tile + b * SC_TBLK, SC_TBLK)
            pltpu.sync_copy(x_hbm.at[pl.ds(row0 + t_out, SC_TBLK)], xv)
            pltpu.sync_copy(e_hbm.at[pl.ds(row0 + t_out, SC_TBLK)], ev)
            _sc_pipelined_heads(tab_hbm, ev, rbuf, sem, compute)
            pltpu.sync_copy(sv, s_hbm.at[pl.ds(t_out, SC_TBLK)])
            return carry

        lax.fori_loop(0, per_tile // SC_TBLK, block, 0)

    return pl.kernel(
        body,
        out_type=jax.ShapeDtypeStruct((rows, N_SEL), F32),
        mesh=plsc.VectorSubcoreMesh(core_axis_name="c", subcore_axis_name="s"),
        scratch_types=[pltpu.VMEM((SC_TBLK, d), F32), pltpu.VMEM((SC_TBLK, N_SEL), jnp.int32),
                       pltpu.VMEM((SC_SLOTS, PEER_TOPK, dw), jnp.uint32), pltpu.VMEM((SC_TBLK, N_SEL), F32),
                       pltpu.VMEM((PEER_TOPK, SC_LANES), F32), pltpu.SemaphoreType.DMA((SC_SLOTS,))],
        compiler_params=_SC_PARAMS,
        name="sc_scores",
    )(x, eids, utab)


def _sc_combine(w, eids, vtab, row0):
    rows = w.shape[0]
    dw = vtab.shape[1]
    d = 2 * dw
    nw = dw // SC_LANES
    per_tile = rows // SC_TILES
    assert rows % (SC_TILES * SC_TBLK) == 0 and row0 % SC_TBLK == 0

    def body(w_hbm, e_hbm, tab_hbm, y_hbm, wv, ev, rbuf, yv, sem):
        tile = lax.axis_index("c") * 16 + lax.axis_index("s")
        zero_i = jnp.zeros((SC_LANES,), jnp.int32)

        def compute(it, slot):
            ti = it // PEER_HEADS
            h = it % PEER_HEADS
            wk = [plsc.load_gather(wv, [zero_i + ti, zero_i + (h * PEER_TOPK + j)]) for j in range(PEER_TOPK)]

            def chunk(c, carry):
                off = pl.multiple_of(c * SC_LANES, SC_LANES)
                alo = jnp.zeros((SC_LANES,), F32)
                ahi = jnp.zeros((SC_LANES,), F32)
                for j in range(PEER_TOPK):
                    lo, hi = _sc_unpack(rbuf[slot, j, pl.ds(off, SC_LANES)])
                    alo = alo + wk[j] * lo
                    ahi = ahi + wk[j] * hi
                plsc.addupdate(yv.at[ti, pl.ds(off, SC_LANES)], alo)
                plsc.addupdate(yv.at[ti, pl.ds(dw + off, SC_LANES)], ahi)
                return carry

            lax.fori_loop(0, nw, chunk, 0)

        def block(b, carry):
            t_out = pl.multiple_of(tile * per_tile + b * SC_TBLK, SC_TBLK)
            pltpu.sync_copy(w_hbm.at[pl.ds(t_out, SC_TBLK)], wv)
            pltpu.sync_copy(e_hbm.at[pl.ds(row0 + t_out, SC_TBLK)], ev)

            def zero(i, carry):
                for ti in range(SC_TBLK):
                    yv[ti, pl.ds(pl.multiple_of(i * SC_LANES, SC_LANES), SC_LANES)] = jnp.zeros((SC_LANES,), F32)
                return carry

            lax.fori_loop(0, d // SC_LANES, zero, 0)
            _sc_pipelined_heads(tab_hbm, ev, rbuf, sem, compute)
            pltpu.sync_copy(yv, y_hbm.at[pl.ds(t_out, SC_TBLK)])
            return carry

        lax.fori_loop(0, per_tile // SC_TBLK, block, 0)

    return pl.kernel(
        body,
        out_type=jax.ShapeDtypeStruct((rows, d), F32),
        mesh=plsc.VectorSubcoreMesh(core_axis_name="c", subcore_axis_name="s"),
        scratch_types=[pltpu.VMEM((SC_TBLK, N_SEL), F32), pltpu.VMEM((SC_TBLK, N_SEL), jnp.int32),
                       pltpu.VMEM((SC_SLOTS, PEER_TOPK, dw), jnp.uint32), pltpu.VMEM((SC_TBLK, d), F32),
                       pltpu.SemaphoreType.DMA((SC_SLOTS,))],
        compiler_params=_SC_PARAMS,
        name="sc_combine",
    )(w, eids, vtab)


def _gate_kernel(s_ref, g_ref, after_ref, w_ref):
    w_ref[...] = g_ref[...] * _gelu_exact(s_ref[...])


def _gate_act(s, gates, after, row0, tb=512):
    rows = s.shape[0]
    assert rows % tb == 0 and row0 % tb == 0
    blk0 = row0 // tb
    return pl.pallas_call(
        _gate_kernel,
        out_shape=jax.ShapeDtypeStruct((rows, N_SEL), F32),
        grid=(rows // tb,),
        in_specs=[pl.BlockSpec((tb, N_SEL), lambda i: (i, 0)),
                  pl.BlockSpec((tb, N_SEL), lambda i: (i + blk0, 0)),
                  pl.BlockSpec(memory_space=pl.ANY)],
        out_specs=pl.BlockSpec((tb, N_SEL), lambda i: (i, 0)),
        compiler_params=pltpu.CompilerParams(dimension_semantics=("parallel",)),
        name="peer_gate",
    )(s, gates, after)


def _resid_norm_kernel(x1_ref, y_ref, gfin_ref, after_ref, o_ref, *, final_norm):
    z = x1_ref[...] + y_ref[...]
    o_ref[...] = _rms(z) * gfin_ref[...] if final_norm else z


def _resid_norm(x1, y, gfin, after, final_norm, tb=512):
    rows, d = y.shape
    assert rows % tb == 0
    return pl.pallas_call(
        functools.partial(_resid_norm_kernel, final_norm=final_norm),
        out_shape=jax.ShapeDtypeStruct((rows, d), F32),
        grid=(rows // tb,),
        in_specs=[pl.BlockSpec((tb, d), lambda i: (i, 0)),
                  pl.BlockSpec((tb, d), lambda i: (i, 0)),
                  pl.BlockSpec((1, d), lambda i: (0, 0)),
                  pl.BlockSpec(memory_space=pl.ANY)],
        out_specs=pl.BlockSpec((tb, d), lambda i: (i, 0)),
        compiler_params=pltpu.CompilerParams(dimension_semantics=("parallel",)),
        name="peer_resid_norm",
    )(x1, y, gfin, after)


N_CHUNKS = 8
SC_CHUNKS = 6
GATE_SPLIT = 6


def _layer(x2, batch, seq, params, consts, li, final_norm):
    (g_mix, w_in, lam_p, g_sub, w_out, g_ffn, wq, sk, u, v, g_fin) = params
    intra, cross, kdec, cdec, slopes = consts
    n, d = x2.shape
    nc = n // N_CHUNKS
    bc = batch // N_CHUNKS
    assert batch % N_CHUNKS == 0
    table = _pack_tables(u, v)
    utab = _pack_halves(u)
    vtab = _pack_halves(v)

    chunks = [None] * N_CHUNKS
    scores = [None] * N_CHUNKS
    ys = [None] * N_CHUNKS
    outs = [None] * N_CHUNKS
    last = g_fin

    def front(c, last):
        proj = _in_proj(x2, g_mix, w_in, last, c * nc, nc)
        proj3 = proj.reshape(bc, seq, proj.shape[1])
        ret = _retention(proj3, intra, cross, kdec, cdec)
        dif = _diff_attention(proj3, slopes, lam_p, g_sub, 1.0 - li)
        half = ret.shape[2]
        x1, xn, q = _out_proj(x2, ret.reshape(nc, half), dif.reshape(nc, half), w_out, g_ffn, wq, c * nc)
        eids, gates, gates2 = _peer_topk(q, sk)
        chunks[c] = (x1, xn, eids, gates, gates2)
        if c < SC_CHUNKS:
            scores[c] = _sc_scores(xn.astype(F32), eids, utab, 0, nc)
        return gates2

    def gate(c, last):
        _, _, eids, gates, _ = chunks[c]
        w = _gate_act(scores[c], gates, last, 0)
        ys[c] = _sc_combine(w, eids, vtab, 0)
        return w

    def tc_mix(c, last):
        x1, xn, eids, _, gates2 = chunks[c]
        outs[c] = _peer_mix(eids, jnp.roll(eids, -1, axis=0), gates2, xn, x1, g_fin, table, last, 0, nc, final_norm)
        return outs[c]

    first_gates = SC_CHUNKS // 2
    for c in range(GATE_SPLIT):
        last = front(c, last)
    for c in range(first_gates):
        last = gate(c, last)
    for c in range(GATE_SPLIT, N_CHUNKS):
        last = front(c, last)
    last = tc_mix(SC_CHUNKS, last)
    for c in range(first_gates, SC_CHUNKS):
        last = gate(c, last)
    for c in range(SC_CHUNKS + 1, N_CHUNKS):
        last = tc_mix(c, last)
    for c in range(SC_CHUNKS):
        outs[c] = _resid_norm(chunks[c][0], ys[c], g_fin, last, final_norm)
        last = outs[c]
    return jnp.concatenate(outs, axis=0)


def _retention_tables(dk):
    c = RET_CHUNK
    h = RET_HEADS
    log_gamma = jnp.log1p(-jnp.exp2(-5.0 - jnp.arange(h, dtype=F32)))
    idx = jnp.arange(c, dtype=F32)
    rel = idx[:, None] - idx[None, :]
    intra = jnp.where(rel >= 0, jnp.exp(log_gamma[:, None, None] * jnp.maximum(rel, 0.0)), 0.0)
    cross = jnp.exp(log_gamma[:, None] * (idx + 1.0))
    kdec = jnp.exp(log_gamma[:, None] * (c - 1.0 - idx))
    cdec = jnp.exp(log_gamma * c)
    scale = dk ** -0.5
    return intra * scale, cross[:, :, None], (kdec * scale)[:, :, None], cdec[:, None, None]


def kernel(x, norm_mix_g, w_in, diff_lambda, diff_subln_g, w_out, norm_ffn_g, peer_wq, peer_subkeys,
           peer_u, peer_v, final_norm_g):
    b, t, d = x.shape
    depth = w_in.shape[0]
    n = b * t
    x2 = x.reshape(n, d)
    consts = _retention_tables(HEAD_W) + (
        jnp.exp2(-8.0 * (jnp.arange(DIFF_HEADS, dtype=F32) + 1.0) / DIFF_HEADS),)
    for l in range(depth):
        li = 0.8 - 0.6 * math.exp(-0.3 * l)
        params = (norm_mix_g[l][None, :], w_in[l].astype(BF16), diff_lambda[l], diff_subln_g[l][None, :],
                  w_out[l].astype(BF16), norm_ffn_g[l][None, :], peer_wq[l].astype(BF16),
                  peer_subkeys[l].astype(BF16), peer_u[l], peer_v[l], final_norm_g[None, :])
        x2 = _layer(x2, b, t, params, consts, li, final_norm=(l == depth - 1))
    return x2.reshape(b, t, d)
```

```python
import functools
import math

import jax
import jax.numpy as jnp
import numpy as np
from jax import lax
from jax.experimental import pallas as pl
from jax.experimental.pallas import tpu as pltpu
from jax.experimental.pallas import tpu_sc as plsc

F32 = jnp.float32
BF16 = jnp.bfloat16

RMS_EPS = 1e-6
NEG_INF = -1e30

RET_HEADS = 4
RET_CHUNK = 128
DIFF_HEADS = 4
HEAD_W = 128
PEER_HEADS = 8
PEER_TOPK = 16
N_KEYS = 128
N_SEL = PEER_HEADS * PEER_TOPK

SUBLANES = 8
VMEM_LIMIT = 56 * 1024 * 1024

_NT = (((1,), (1,)), ((), ()))


def _rms(x, eps=RMS_EPS):
    return x * lax.rsqrt(jnp.mean(x * x, axis=-1, keepdims=True) + eps)


def _gelu_exact(x):
    return 0.5 * x * (1.0 + lax.erf(x * (2.0 ** -0.5)))


def _inproj_kernel(x_ref, g_ref, w_ref, after_ref, o_ref, *, col_chunk):
    h = (_rms(x_ref[...]) * g_ref[...]).astype(BF16)
    for j in range(o_ref.shape[1] // col_chunk):
        sl = slice(j * col_chunk, (j + 1) * col_chunk)
        o_ref[:, sl] = jnp.dot(h, w_ref[:, sl], preferred_element_type=F32).astype(o_ref.dtype)


def _in_proj(x2, g, w_bf16, after, row0, rows, tm=512):
    d = x2.shape[1]
    cols = w_bf16.shape[1]
    assert row0 % tm == 0 and rows % tm == 0
    blk0 = row0 // tm
    return pl.pallas_call(
        functools.partial(_inproj_kernel, col_chunk=512),
        out_shape=jax.ShapeDtypeStruct((rows, cols), BF16),
        grid=(rows // tm,),
        in_specs=[pl.BlockSpec((tm, d), lambda i: (i + blk0, 0)),
                  pl.BlockSpec((1, d), lambda i: (0, 0)),
                  pl.BlockSpec((d, cols), lambda i: (0, 0)),
                  pl.BlockSpec(memory_space=pl.ANY)],
        out_specs=pl.BlockSpec((tm, cols), lambda i: (i, 0)),
        compiler_params=pltpu.CompilerParams(dimension_semantics=("parallel",),
                                             vmem_limit_bytes=VMEM_LIMIT),
        name="in_proj",
    )(x2, g, w_bf16, after)


def _ret_kernel(q_ref, k_ref, v_ref, g_ref, intra_ref, cross_ref, kdec_ref, cdec_ref, o_ref):
    c = RET_CHUNK
    n_chunks = q_ref.shape[1] // c
    intra = intra_ref[0]
    cross = cross_ref[0]
    kdec = kdec_ref[0]
    cdec = cdec_ref[0]

    def body(n, s):
        sl = pl.ds(pl.multiple_of(n * c, c), c)
        qi = q_ref[0, sl, :]
        ki = k_ref[0, sl, :]
        vi = v_ref[0, sl, :]
        a = lax.dot_general(qi, ki, _NT, preferred_element_type=F32) * intra
        o = jnp.dot(a.astype(BF16), vi, preferred_element_type=F32)
        o = o + jnp.dot(qi, s.astype(BF16), preferred_element_type=F32) * cross
        kd_t = (ki.astype(F32) * kdec).T.astype(BF16)
        s = s * cdec + jnp.dot(kd_t, vi, preferred_element_type=F32)
        gate = g_ref[0, sl, :].astype(F32)
        o_ref[0, sl, :] = (gate * jax.nn.sigmoid(gate) * _rms(o)).astype(o_ref.dtype)
        return s

    lax.fori_loop(0, n_chunks, body, jnp.zeros((HEAD_W, HEAD_W), F32))


def _retention(proj3, intra, cross, kdec, cdec):
    b, t, _ = proj3.shape
    h = RET_HEADS

    def col(off):
        return pl.BlockSpec((1, t, HEAD_W), lambda bi, hi: (bi, 0, off + hi))

    def per_head(shape):
        return pl.BlockSpec((1,) + shape, lambda bi, hi: (hi, 0, 0))

    return pl.pallas_call(
        _ret_kernel,
        out_shape=jax.ShapeDtypeStruct((b, t, h * HEAD_W), BF16),
        grid=(b, h),
        in_specs=[col(0), col(h), col(2 * h), col(3 * h),
                  per_head((RET_CHUNK, RET_CHUNK)), per_head((RET_CHUNK, 1)),
                  per_head((RET_CHUNK, 1)), per_head((1, 1))],
        out_specs=pl.BlockSpec((1, t, HEAD_W), lambda bi, hi: (bi, 0, hi)),
        compiler_params=pltpu.CompilerParams(dimension_semantics=("parallel", "parallel"),
                                             vmem_limit_bytes=VMEM_LIMIT),
        name="retention",
    )(proj3, proj3, proj3, proj3, intra, cross, kdec, cdec)


def _diff_kernel(slope_ref, lam_ref, q_ref, k_ref, v_ref, gsub_ref, o_ref, *, tq, tk, out_scale):
    hi = pl.program_id(1)
    qi = pl.program_id(2)
    half = HEAD_W // 2
    slope = slope_ref[hi]

    lp = lam_ref[...]
    lam = (jnp.exp(jnp.sum(lp[0:1] * lp[1:2], axis=1, keepdims=True))
           - jnp.exp(jnp.sum(lp[2:3] * lp[3:4], axis=1, keepdims=True))
           + (1.0 - out_scale))

    q = q_ref[0] * jnp.asarray(half ** -0.5, BF16)
    lane = lax.broadcasted_iota(jnp.int32, q.shape, 1)
    zero = jnp.zeros_like(q)
    qs = jnp.concatenate([jnp.where(lane < half, q, zero),
                          jnp.where(lane >= half, q, zero)], axis=0)

    row = lax.broadcasted_iota(jnp.int32, (2 * tq, tk), 0)
    row = jnp.where(row >= tq, row - tq, row)
    col = lax.broadcasted_iota(jnp.int32, (2 * tq, tk), 1)
    rel = (row - col).astype(F32)
    rel_bias = -slope * rel

    def body(j, carry):
        m, l, acc = carry
        ksl = pl.ds(pl.multiple_of(j * tk, tk), tk)
        k = k_ref[0, ksl, :]
        v = v_ref[0, ksl, :]
        off = (qi * tq - j * tk).astype(F32)
        s = lax.dot_general(qs, k, _NT, preferred_element_type=F32)
        s = s + jnp.where(rel >= -off, rel_bias - slope * off, NEG_INF)
        m_new = jnp.maximum(m, jnp.max(s, axis=1, keepdims=True))
        alpha = jnp.exp(m - m_new)
        p = jnp.exp(s - m_new)
        l = alpha * l + jnp.sum(p, axis=1, keepdims=True)
        acc = alpha * acc + jnp.dot(p.astype(BF16), v, preferred_element_type=F32)
        return m_new, l, acc

    n_kv = (qi * tq + tq - 1) // tk + 1
    m0 = jnp.full((2 * tq, 1), NEG_INF, F32)
    l0 = jnp.zeros((2 * tq, 1), F32)
    a0 = jnp.zeros((2 * tq, HEAD_W), F32)
    _, l, acc = lax.fori_loop(0, n_kv, body, (m0, l0, a0))
    o = acc / l
    dy = o[:tq] - lam * o[tq:]
    o_ref[0] = (_rms(dy) * gsub_ref[...] * out_scale).astype(o_ref.dtype)


def _diff_attention(proj3, slopes, diff_lambda_l, gsub, out_scale, tq=256, tk=512):
    b, t, _ = proj3.shape
    h = DIFF_HEADS
    base = 4 * RET_HEADS
    kern = functools.partial(_diff_kernel, tq=tq, tk=tk, out_scale=out_scale)
    return pl.pallas_call(
        kern,
        out_shape=jax.ShapeDtypeStruct((b, t, h * HEAD_W), BF16),
        grid=(b, h, t // tq),
        in_specs=[pl.BlockSpec(memory_space=pltpu.SMEM),
                  pl.BlockSpec(diff_lambda_l.shape, lambda bi, hi, qi: (0, 0)),
                  pl.BlockSpec((1, tq, HEAD_W), lambda bi, hi, qi: (bi, qi, base + hi)),
                  pl.BlockSpec((1, t, HEAD_W), lambda bi, hi, qi: (bi, 0, base + h + hi)),
                  pl.BlockSpec((1, t, HEAD_W), lambda bi, hi, qi: (bi, 0, base + 2 * h + hi)),
                  pl.BlockSpec((1, HEAD_W), lambda bi, hi, qi: (0, 0))],
        out_specs=pl.BlockSpec((1, tq, HEAD_W), lambda bi, hi, qi: (bi, qi, hi)),
        compiler_params=pltpu.CompilerParams(
            dimension_semantics=("parallel", "parallel", "arbitrary"),
            vmem_limit_bytes=VMEM_LIMIT),
        name="diff_attention",
    )(slopes, diff_lambda_l, proj3, proj3, proj3, gsub)


def _outproj_kernel(x_ref, ret_ref, dif_ref, wo_ref, g_ref, wq_ref, x1_ref, xn_ref, q_ref, *, col_chunk):
    half = ret_ref.shape[1]
    mix = jnp.dot(ret_ref[...], wo_ref[:half, :], preferred_element_type=F32)
    mix = mix + jnp.dot(dif_ref[...], wo_ref[half:, :], preferred_element_type=F32)
    x1 = x_ref[...] + mix
    x1_ref[...] = x1
    xn = (_rms(x1) * g_ref[...]).astype(BF16)
    xn_ref[...] = xn
    for j in range(q_ref.shape[1] // col_chunk):
        sl = slice(j * col_chunk, (j + 1) * col_chunk)
        q_ref[:, sl] = jnp.dot(xn, wq_ref[:, sl], preferred_element_type=F32).astype(q_ref.dtype)


def _out_proj(x2, ret2, dif2, wo_bf16, g, wq_bf16, row0, tm=512):
    n, half = ret2.shape
    d = x2.shape[1]
    qc = wq_bf16.shape[1]
    assert row0 % tm == 0 and n % tm == 0
    blk0 = row0 // tm
    row = lambda w: pl.BlockSpec((tm, w), lambda i: (i, 0))
    full = lambda a: pl.BlockSpec(a.shape, lambda i: (0, 0))
    return pl.pallas_call(
        functools.partial(_outproj_kernel, col_chunk=512),
        out_shape=(jax.ShapeDtypeStruct((n, d), F32),
                   jax.ShapeDtypeStruct((n, d), BF16),
                   jax.ShapeDtypeStruct((n, qc), BF16)),
        grid=(n // tm,),
        in_specs=[pl.BlockSpec((tm, d), lambda i: (i + blk0, 0)), row(half), row(half),
                  full(wo_bf16), full(g), full(wq_bf16)],
        out_specs=(row(d), row(d), row(qc)),
        compiler_params=pltpu.CompilerParams(dimension_semantics=("parallel",),
                                             vmem_limit_bytes=VMEM_LIMIT),
        name="out_proj",
    )(x2, ret2, dif2, wo_bf16, g, wq_bf16)


def _split3(v):
    trunc = lambda x: pltpu.bitcast(pltpu.bitcast(x, jnp.uint32) & jnp.uint32(0xFFFF0000), F32)
    hi = trunc(v)
    r1 = v - hi
    mid = trunc(r1)
    return hi.astype(BF16), mid.astype(BF16), (r1 - mid).astype(BF16)


def _spread_exact(v, sel_bf16):
    hi, mid, lo = _split3(v)
    d = lambda a: jnp.dot(a, sel_bf16, preferred_element_type=F32)
    return (d(hi) + d(mid)) + d(lo)


def _spread_exact_left(sel_bf16, v):
    hi, mid, lo = _split3(v)
    d = lambda a: jnp.dot(sel_bf16, a, preferred_element_type=F32)
    return (d(hi) + d(mid)) + d(lo)


def _topk_kernel(q_ref, sk_ref, sel_ref, pad_ref, e_ref, g_ref, g2_ref):
    tb = q_ref.shape[0]
    k = PEER_TOPK
    n_sets = 2 * PEER_HEADS
    neg = -jnp.inf

    s = jnp.stack(
        [lax.dot_general(sk_ref[j // 2, j % 2], q_ref[:, j * N_KEYS:(j + 1) * N_KEYS], _NT,
                         preferred_element_type=F32) for j in range(n_sets)], axis=0)
    key_f = lax.broadcasted_iota(jnp.int32, s.shape, 1).astype(F32)
    rank = lax.broadcasted_iota(jnp.int32, (n_sets, k, tb), 1)
    top_v = jnp.zeros((n_sets, k, tb), F32)
    top_i = jnp.zeros((n_sets, k, tb), F32)
    for r in range(k):
        m = jnp.max(s, axis=1, keepdims=True)
        i = jnp.min(jnp.where(s == m, key_f, float(N_KEYS)), axis=1, keepdims=True)
        s = jnp.where(key_f == i, neg, s)
        top_v = jnp.where(rank == r, m, top_v)
        top_i = jnp.where(rank == r, i, top_i)

    pad = pad_ref[...]
    cands, eids = [], []
    for h in range(PEER_HEADS):
        cands.append(_spread_exact_left(sel_ref[0], top_v[2 * h]) + _spread_exact_left(sel_ref[1], top_v[2 * h + 1])
                     + pad)
        eids.append(jnp.dot(sel_ref[0], top_i[2 * h].astype(BF16), preferred_element_type=F32) * float(N_KEYS)
                    + jnp.dot(sel_ref[1], top_i[2 * h + 1].astype(BF16), preferred_element_type=F32))
    cand = jnp.stack(cands, axis=0)
    eid = jnp.stack(eids, axis=0)
    n_cand = cand.shape[1]

    c_f = lax.broadcasted_iota(jnp.int32, cand.shape, 1).astype(F32)
    rank2 = lax.broadcasted_iota(jnp.int32, (PEER_HEADS, k, tb), 1)
    top_c = []
    e_rank = jnp.zeros((PEER_HEADS, k, tb), F32)
    for r in range(k):
        m = jnp.max(cand, axis=1, keepdims=True)
        pos = jnp.min(jnp.where(cand == m, c_f, float(n_cand)), axis=1, keepdims=True)
        hit = c_f == pos
        e_rank = jnp.where(rank2 == r, jnp.sum(jnp.where(hit, eid, 0.0), axis=1, keepdims=True), e_rank)
        cand = jnp.where(hit, neg, cand)
        top_c.append(m)
    p_rank = jnp.zeros((PEER_HEADS, k, tb), F32)
    denom = jnp.zeros((PEER_HEADS, 1, tb), F32)
    for r in range(k):
        p = jnp.exp(top_c[r] - top_c[0])
        denom = denom + p
        p_rank = jnp.where(rank2 == r, p, p_rank)
    g_rank = p_rank / denom

    e_out = e_rank.reshape(PEER_HEADS * k, tb).T
    g1 = g_rank.reshape(PEER_HEADS * k, tb).T
    sr = lax.broadcasted_iota(jnp.int32, (N_SEL, 2 * N_SEL), 0)
    sc = lax.broadcasted_iota(jnp.int32, (N_SEL, 2 * N_SEL), 1)
    e_ref[...] = e_out.astype(jnp.int32)
    g_ref[...] = g1
    g2_ref[...] = _spread_exact(g1, jnp.where(sc == 2 * sr + 1, 1.0, 0.0).astype(BF16))


def _candidate_tables(tb):
    k = PEER_TOPK
    pairs = [(a, b) for a in range(k) for b in range(k) if (a + 1) * (b + 1) <= k]
    n_cand = -(-len(pairs) // SUBLANES) * SUBLANES
    sel = np.zeros((2, n_cand, k), np.float32)
    pad = np.full((n_cand, tb), -np.inf, np.float32)
    for c, (a, b) in enumerate(pairs):
        sel[0, c, a] = 1.0
        sel[1, c, b] = 1.0
        pad[c, :] = 0.0
    return jnp.asarray(sel, BF16), jnp.asarray(pad)


def _peer_topk(q2, sk_bf16, tb=128):
    n, qc = q2.shape
    sel, pad = _candidate_tables(tb)
    const = lambda a: pl.BlockSpec(a.shape, lambda i: (0,) * a.ndim)
    return pl.pallas_call(
        _topk_kernel,
        out_shape=(jax.ShapeDtypeStruct((n, N_SEL), jnp.int32),
                   jax.ShapeDtypeStruct((n, N_SEL), F32),
                   jax.ShapeDtypeStruct((n, 2 * N_SEL), F32)),
        grid=(n // tb,),
        in_specs=[pl.BlockSpec((tb, qc), lambda i: (i, 0)), const(sk_bf16), const(sel), const(pad)],
        out_specs=(pl.BlockSpec((tb, N_SEL), lambda i: (i, 0)),
                   pl.BlockSpec((tb, N_SEL), lambda i: (i, 0)),
                   pl.BlockSpec((tb, 2 * N_SEL), lambda i: (i, 0))),
        compiler_params=pltpu.CompilerParams(dimension_semantics=("parallel",),
                                             vmem_limit_bytes=VMEM_LIMIT),
        name="peer_topk",
    )(q2, sk_bf16, sel, pad)


def _bf16_bits(t):
    return lax.bitcast_convert_type(t.astype(BF16), jnp.uint16).astype(jnp.uint32)


def _pack_tables(u, v):
    e, d = u.shape
    return (_bf16_bits(u) | (_bf16_bits(v) << 16)).reshape(e, d // HEAD_W, HEAD_W)


def _pack_halves(t):
    b = _bf16_bits(t)
    half = t.shape[1] // 2
    return b[:, :half] | (b[:, half:] << 16)


def _mix_kernel(efirst_ref, enext_ref, g2_ref, xn_ref, x1_ref, gfin_ref, tab_hbm, after_ref, o_ref, buf, sem, *,
                final_norm):
    step = pl.program_id(0)
    tb = xn_ref.shape[0]
    n_chunks = tab_hbm.shape[1]

    def issue(e_ref, t, slot):
        for k in range(N_SEL):
            e = e_ref[t, k]
            pltpu.make_async_copy(tab_hbm.at[e], buf.at[slot, pl.ds(k * n_chunks, n_chunks), :],
                                  sem.at[slot]).start(priority=k % 2)

    def wait(slot):
        pltpu.make_async_copy(buf.at[slot], buf.at[slot], sem.at[slot]).wait()

    @pl.when(step == 0)
    def _():
        issue(efirst_ref, 0, 0)

    rid = lax.broadcasted_iota(jnp.int32, (SUBLANES, 2 * N_SEL), 0)

    def group(gi, c):
        base = pl.multiple_of(gi * SUBLANES, SUBLANES)
        xrows = xn_ref[pl.ds(base, SUBLANES), :]
        grows = g2_ref[pl.ds(base, SUBLANES), :]
        acc = x1_ref[pl.ds(base, SUBLANES), :]
        for j in range(SUBLANES):
            slot = j % 2
            issue(enext_ref, base + j, 1 - slot)
            wait(slot)
            uv = jnp.concatenate(
                [pltpu.bitcast(buf[slot, pl.ds(c, N_SEL, stride=n_chunks), :], BF16) for c in range(n_chunks)],
                axis=1)
            s2 = lax.dot_general(xrows, uv, _NT, preferred_element_type=F32)
            act = pltpu.roll(_gelu_exact(s2), 1, axis=1)
            w2 = jnp.where(rid == j, grows * act, 0.0)
            acc = acc + jnp.dot(w2.astype(BF16), uv, preferred_element_type=F32)
        o_ref[pl.ds(base, SUBLANES), :] = acc
        return c

    lax.fori_loop(0, tb // SUBLANES, group, 0)

    @pl.when(step == pl.num_programs(0) - 1)
    def _():
        wait(0)

    if final_norm:
        o_ref[...] = _rms(o_ref[...]) * gfin_ref[...]


def _peer_mix(eids, e_next, gates2, xn, x1, gfin, table, after, row0, rows, final_norm, tb=64):
    d = x1.shape[1]
    assert tb % (2 * SUBLANES) == 0 and row0 % tb == 0 and rows % tb == 0
    blk0 = row0 // tb
    row = lambda w: pl.BlockSpec((tb, w), lambda i: (i + blk0, 0))
    return pl.pallas_call(
        functools.partial(_mix_kernel, final_norm=final_norm),
        out_shape=jax.ShapeDtypeStruct((rows, d), F32),
        grid=(rows // tb,),
        in_specs=[pl.BlockSpec((SUBLANES, N_SEL), lambda i: (row0 // SUBLANES, 0), memory_space=pltpu.SMEM),
                  pl.BlockSpec((tb, N_SEL), lambda i: (i + blk0, 0), memory_space=pltpu.SMEM),
                  row(2 * N_SEL), row(d), row(d),
                  pl.BlockSpec((1, d), lambda i: (0, 0)),
                  pl.BlockSpec(memory_space=pl.ANY),
                  pl.BlockSpec(memory_space=pl.ANY)],
        out_specs=pl.BlockSpec((tb, d), lambda i: (i, 0)),
        scratch_shapes=[pltpu.VMEM((2, N_SEL * (d // HEAD_W), HEAD_W), jnp.uint32),
                        pltpu.SemaphoreType.DMA((2,))],
        compiler_params=pltpu.CompilerParams(dimension_semantics=("arbitrary",),
                                             vmem_limit_bytes=VMEM_LIMIT),
        name="peer_mix",
    )(eids, e_next, gates2, xn, x1, gfin, table, after)


SC_LANES = 16
SC_TILES = 32
SC_TBLK = 8
SC_SLOTS = 4
HI_MASK = 0xFFFF0000
_SC_PARAMS = pltpu.CompilerParams(needs_layout_passes=False)


def _sc_unpack(word):
    return plsc.bitcast(word << 16, F32), plsc.bitcast(word & jnp.uint32(HI_MASK), F32)


def _sc_pipelined_heads(tab_hbm, ev, rows, sem, compute):
    n_items = SC_TBLK * PEER_HEADS
    ahead = SC_SLOTS - 1
    assert n_items % SC_SLOTS == 0

    def gather(it, slot):
        ti = it // PEER_HEADS
        h = it % PEER_HEADS
        return pltpu.make_async_copy(tab_hbm.at[ev.at[ti, pl.ds(h * PEER_TOPK, PEER_TOPK)]], rows.at[slot],
                                     sem.at[slot])

    for a in range(ahead):
        gather(a, a).start()

    def group(jj, carry):
        for slot in range(SC_SLOTS):
            it = SC_SLOTS * jj + slot
            gather(jnp.minimum(it + ahead, n_items - 1), (slot + ahead) % SC_SLOTS).start()
            gather(it, slot).wait()
            compute(it, slot)
        return carry

    lax.fori_loop(0, n_items // SC_SLOTS, group, 0)
    for a in range(ahead):
        gather(n_items - 1, (n_items + a) % SC_SLOTS).wait()


def _sc_scores(x, eids, utab, row0, rows):
    d = x.shape[1]
    dw = d // 2
    nw = dw // SC_LANES
    per_tile = rows // SC_TILES
    assert rows % (SC_TILES * SC_TBLK) == 0 and row0 % SC_TBLK == 0

    def body(x_hbm, e_hbm, tab_hbm, s_hbm, xv, ev, rbuf, sv, tmp, sem):
        tile = lax.axis_index("c") * 16 + lax.axis_index("s")
        lane = lax.iota(jnp.int32, SC_LANES)

        def compute(it, slot):
            ti = it // PEER_HEADS
            h = it % PEER_HEADS

            def chunk(w, accs):
                off = pl.multiple_of(w * SC_LANES, SC_LANES)
                xlo = xv[ti, pl.ds(off, SC_LANES)]
                xhi = xv[ti, pl.ds(dw + off, SC_LANES)]
                out = []
                for j in range(PEER_TOPK):
                    lo, hi = _sc_unpack(rbuf[slot, j, pl.ds(off, SC_LANES)])
                    out.append(accs[j] + lo * xlo + hi * xhi)
                return tuple(out)

            accs = lax.fori_loop(0, nw, chunk, tuple(jnp.zeros((SC_LANES,), F32) for _ in range(PEER_TOPK)))
            for j in range(PEER_TOPK):
                tmp[j, :] = accs[j]
            svec = jnp.zeros((SC_LANES,), F32)
            for l in range(SC_LANES):
                svec = svec + plsc.load_gather(tmp, [lane, jnp.full((SC_LANES,), l, jnp.int32)])
            sv[ti, pl.ds(pl.multiple_of(h * PEER_TOPK, PEER_TOPK), PEER_TOPK)] = svec

        def block(b, carry):
            t_out = pl.multiple_of(tile * per_tile + b * SC_TBLK, SC_TBLK)
            pltpu.sync_copy(x_hbm.at[pl.ds(row0 + t_out, SC_TBLK)], xv)
            pltpu.sync_copy(e_hbm.at[pl.ds(row0 + t_out, SC_TBLK)], ev)
            _sc_pipelined_heads(tab_hbm, ev, rbuf, sem, compute)
            pltpu.sync_copy(sv, s_hbm.at[pl.ds(t_out, SC_TBLK)])
            return carry

        lax.fori_loop(0, per_tile // SC_TBLK, block, 0)

    return pl.kernel(
        body,
        out_type=jax.ShapeDtypeStruct((rows, N_SEL), F32),
        mesh=plsc.VectorSubcoreMesh(core_axis_name="c", subcore_axis_name="s"),
        scratch_types=[pltpu.VMEM((SC_TBLK, d), F32), pltpu.VMEM((SC_TBLK, N_SEL), jnp.int32),
                       pltpu.VMEM((SC_SLOTS, PEER_TOPK, dw), jnp.uint32), pltpu.VMEM((SC_TBLK, N_SEL), F32),
                       pltpu.VMEM((PEER_TOPK, SC_LANES), F32), pltpu.SemaphoreType.DMA((SC_SLOTS,))],
        compiler_params=_SC_PARAMS,
        name="sc_scores",
    )(x, eids, utab)


def _sc_combine(w, eids, vtab, row0):
    rows = w.shape[0]
    dw = vtab.shape[1]
    d = 2 * dw
    nw = dw // SC_LANES
    per_tile = rows // SC_TILES
    assert rows % (SC_TILES * SC_TBLK) == 0 and row0 % SC_TBLK == 0

    def body(w_hbm, e_hbm, tab_hbm, y_hbm, wv, ev, rbuf, yv, sem):
        tile = lax.axis_index("c") * 16 + lax.axis_index("s")
        zero_i = jnp.zeros((SC_LANES,), jnp.int32)

        def compute(it, slot):
            ti = it // PEER_HEADS
            h = it % PEER_HEADS
            wk = [plsc.load_gather(wv, [zero_i + ti, zero_i + (h * PEER_TOPK + j)]) for j in range(PEER_TOPK)]

            def chunk(c, carry):
                off = pl.multiple_of(c * SC_LANES, SC_LANES)
                alo = jnp.zeros((SC_LANES,), F32)
                ahi = jnp.zeros((SC_LANES,), F32)
                for j in range(PEER_TOPK):
                    lo, hi = _sc_unpack(rbuf[slot, j, pl.ds(off, SC_LANES)])
                    alo = alo + wk[j] * lo
                    ahi = ahi + wk[j] * hi
                plsc.addupdate(yv.at[ti, pl.ds(off, SC_LANES)], alo)
                plsc.addupdate(yv.at[ti, pl.ds(dw + off, SC_LANES)], ahi)
                return carry

            lax.fori_loop(0, nw, chunk, 0)

        def block(b, carry):
            t_out = pl.multiple_of(tile * per_tile + b * SC_TBLK, SC_TBLK)
            pltpu.sync_copy(w_hbm.at[pl.ds(t_out, SC_TBLK)], wv)
            pltpu.sync_copy(e_hbm.at[pl.ds(row0 + t_out, SC_TBLK)], ev)

            def zero(i, carry):
                for ti in range(SC_TBLK):
                    yv[ti, pl.ds(pl.multiple_of(i * SC_LANES, SC_LANES), SC_LANES)] = jnp.zeros((SC_LANES,), F32)
                return carry

            lax.fori_loop(0, d // SC_LANES, zero, 0)
            _sc_pipelined_heads(tab_hbm, ev, rbuf, sem, compute)
            pltpu.sync_copy(yv, y_hbm.at[pl.ds(t_out, SC_TBLK)])
            return carry

        lax.fori_loop(0, per_tile // SC_TBLK, block, 0)

    return pl.kernel(
        body,
        out_type=jax.ShapeDtypeStruct((rows, d), F32),
        mesh=plsc.VectorSubcoreMesh(core_axis_name="c", subcore_axis_name="s"),
        scratch_types=[pltpu.VMEM((SC_TBLK, N_SEL), F32), pltpu.VMEM((SC_TBLK, N_SEL), jnp.int32),
                       pltpu.VMEM((SC_SLOTS, PEER_TOPK, dw), jnp.uint32), pltpu.VMEM((SC_TBLK, d), F32),
                       pltpu.SemaphoreType.DMA((SC_SLOTS,))],
        compiler_params=_SC_PARAMS,
        name="sc_combine",
    )(w, eids, vtab)


def _gate_kernel(s_ref, g_ref, after_ref, w_ref):
    w_ref[...] = g_ref[...] * _gelu_exact(s_ref[...])


def _gate_act(s, gates, after, row0, tb=512):
    rows = s.shape[0]
    assert rows % tb == 0 and row0 % tb == 0
    blk0 = row0 // tb
    return pl.pallas_call(
        _gate_kernel,
        out_shape=jax.ShapeDtypeStruct((rows, N_SEL), F32),
        grid=(rows // tb,),
        in_specs=[pl.BlockSpec((tb, N_SEL), lambda i: (i, 0)),
                  pl.BlockSpec((tb, N_SEL), lambda i: (i + blk0, 0)),
                  pl.BlockSpec(memory_space=pl.ANY)],
        out_specs=pl.BlockSpec((tb, N_SEL), lambda i: (i, 0)),
        compiler_params=pltpu.CompilerParams(dimension_semantics=("parallel",)),
        name="peer_gate",
    )(s, gates, after)


def _resid_norm_kernel(x1_ref, y_ref, gfin_ref, after_ref, o_ref, *, final_norm):
    z = x1_ref[...] + y_ref[...]
    o_ref[...] = _rms(z) * gfin_ref[...] if final_norm else z


def _resid_norm(x1, y, gfin, after, final_norm, tb=512):
    rows, d = y.shape
    assert rows % tb == 0
    return pl.pallas_call(
        functools.partial(_resid_norm_kernel, final_norm=final_norm),
        out_shape=jax.ShapeDtypeStruct((rows, d), F32),
        grid=(rows // tb,),
        in_specs=[pl.BlockSpec((tb, d), lambda i: (i, 0)),
                  pl.BlockSpec((tb, d), lambda i: (i, 0)),
                  pl.BlockSpec((1, d), lambda i: (0, 0)),
                  pl.BlockSpec(memory_space=pl.ANY)],
        out_specs=pl.BlockSpec((tb, d), lambda i: (i, 0)),
        compiler_params=pltpu.CompilerParams(dimension_semantics=("parallel",)),
        name="peer_resid_norm",
    )(x1, y, gfin, after)


N_CHUNKS = 8
SC_CHUNKS = 6
COST_FRONT, COST_SC_SCORES, COST_SC_COMBINE, COST_TC_MIX = 0.13, 0.21, 0.22, 0.85


def _tc_schedule(tokens_per_chunk):
    t_front, t_scores, t_combine, t_mix = (c * tokens_per_chunk for c in
                                           (COST_FRONT, COST_SC_SCORES, COST_SC_COMBINE, COST_TC_MIX))
    base = [("front", c) for c in range(N_CHUNKS)] + [("mix", c) for c in range(SC_CHUNKS, N_CHUNKS)]
    order = []
    tc = sc = 0.0
    scores_done, combine_done = {}, {}
    next_gate = next_norm = 0
    for kind, c in base + [("end", None)]:
        while next_gate < SC_CHUNKS and (kind == "end"
                                         or scores_done.get(next_gate, float("inf")) + 0.25 * t_scores <= tc):
            tc = max(tc, scores_done[next_gate])
            sc = max(sc, tc) + t_combine
            combine_done[next_gate] = sc
            order.append(("gate", next_gate))
            next_gate += 1
        while next_norm < SC_CHUNKS and (kind == "end" or combine_done.get(next_norm, float("inf")) <= tc):
            order.append(("norm", next_norm))
            next_norm += 1
        if kind == "front":
            tc += t_front
            if c < SC_CHUNKS:
                sc = max(sc, tc) + t_scores
                scores_done[c] = sc
        elif kind == "mix":
            tc += t_mix
        if kind != "end":
            order.append((kind, c))
    return order


def _layer(x2, batch, seq, params, consts, li, final_norm):
    (g_mix, w_in, lam_p, g_sub, w_out, g_ffn, wq, sk, u, v, g_fin) = params
    intra, cross, kdec, cdec, slopes = consts
    n, d = x2.shape
    nc = n // N_CHUNKS
    bc = batch // N_CHUNKS
    assert batch % N_CHUNKS == 0
    table = _pack_tables(u, v)
    utab = _pack_halves(u)
    vtab = _pack_halves(v)

    chunks = [None] * N_CHUNKS
    scores = [None] * N_CHUNKS
    ys = [None] * N_CHUNKS
    outs = [None] * N_CHUNKS
    last = g_fin

    def front(c, last):
        proj = _in_proj(x2, g_mix, w_in, last, c * nc, nc)
        proj3 = proj.reshape(bc, seq, proj.shape[1])
        ret = _retention(proj3, intra, cross, kdec, cdec)
        dif = _diff_attention(proj3, slopes, lam_p, g_sub, 1.0 - li)
        half = ret.shape[2]
        x1, xn, q = _out_proj(x2, ret.reshape(nc, half), dif.reshape(nc, half), w_out, g_ffn, wq, c * nc)
        eids, gates, gates2 = _peer_topk(q, sk)
        chunks[c] = (x1, xn, eids, gates, gates2)
        if c < SC_CHUNKS:
            scores[c] = _sc_scores(xn.astype(F32), eids, utab, 0, nc)
        return gates2

    def gate(c, last):
        _, _, eids, gates, _ = chunks[c]
        w = _gate_act(scores[c], gates, last, 0)
        ys[c] = _sc_combine(w, eids, vtab, 0)
        return w

    def tc_mix(c, last):
        x1, xn, eids, _, gates2 = chunks[c]
        outs[c] = _peer_mix(eids, jnp.roll(eids, -1, axis=0), gates2, xn, x1, g_fin, table, last, 0, nc, final_norm)
        return outs[c]

    def norm(c, last):
        outs[c] = _resid_norm(chunks[c][0], ys[c], g_fin, last, final_norm)
        return outs[c]

    run = {"front": front, "gate": gate, "mix": tc_mix, "norm": norm}
    for kind, c in _tc_schedule(nc):
        last = run[kind](c, last)
    return jnp.concatenate(outs, axis=0)


def _retention_tables(dk):
    c = RET_CHUNK
    h = RET_HEADS
    log_gamma = jnp.log1p(-jnp.exp2(-5.0 - jnp.arange(h, dtype=F32)))
    idx = jnp.arange(c, dtype=F32)
    rel = idx[:, None] - idx[None, :]
    intra = jnp.where(rel >= 0, jnp.exp(log_gamma[:, None, None] * jnp.maximum(rel, 0.0)), 0.0)
    cross = jnp.exp(log_gamma[:, None] * (idx + 1.0))
    kdec = jnp.exp(log_gamma[:, None] * (c - 1.0 - idx))
    cdec = jnp.exp(log_gamma * c)
    scale = dk ** -0.5
    return intra * scale, cross[:, :, None], (kdec * scale)[:, :, None], cdec[:, None, None]


def kernel(x, norm_mix_g, w_in, diff_lambda, diff_subln_g, w_out, norm_ffn_g, peer_wq, peer_subkeys,
           peer_u, peer_v, final_norm_g):
    b, t, d = x.shape
    depth = w_in.shape[0]
    n = b * t
    x2 = x.reshape(n, d)
    consts = _retention_tables(HEAD_W) + (
        jnp.exp2(-8.0 * (jnp.arange(DIFF_HEADS, dtype=F32) + 1.0) / DIFF_HEADS),)
    for l in range(depth):
        li = 0.8 - 0.6 * math.exp(-0.3 * l)
        params = (norm_mix_g[l][None, :], w_in[l].astype(BF16), diff_lambda[l], diff_subln_g[l][None, :],
                  w_out[l].astype(BF16), norm_ffn_g[l][None, :], peer_wq[l].astype(BF16),
                  peer_subkeys[l].astype(BF16), peer_u[l], peer_v[l], final_norm_g[None, :])
        x2 = _layer(x2, b, t, params, consts, li, final_norm=(l == depth - 1))
    return x2.reshape(b, t, d)
```

```python
import functools
import math

import jax
import jax.numpy as jnp
import numpy as np
from jax import lax
from jax.experimental import pallas as pl
from jax.experimental.pallas import tpu as pltpu
from jax.experimental.pallas import tpu_sc as plsc

F32 = jnp.float32
BF16 = jnp.bfloat16

RMS_EPS = 1e-6
NEG_INF = -1e30

RET_HEADS = 4
RET_CHUNK = 128
DIFF_HEADS = 4
HEAD_W = 128
PEER_HEADS = 8
PEER_TOPK = 16
N_KEYS = 128
N_SEL = PEER_HEADS * PEER_TOPK

SUBLANES = 8
VMEM_LIMIT = 56 * 1024 * 1024

_NT = (((1,), (1,)), ((), ()))


def _rms(x, eps=RMS_EPS):
    return x * lax.rsqrt(jnp.mean(x * x, axis=-1, keepdims=True) + eps)


def _gelu_exact(x):
    return 0.5 * x * (1.0 + lax.erf(x * (2.0 ** -0.5)))


def _inproj_kernel(x_ref, g_ref, w_ref, after_ref, o_ref, *, col_chunk):
    h = (_rms(x_ref[...]) * g_ref[...]).astype(BF16)
    for j in range(o_ref.shape[1] // col_chunk):
        sl = slice(j * col_chunk, (j + 1) * col_chunk)
        o_ref[:, sl] = jnp.dot(h, w_ref[:, sl], preferred_element_type=F32).astype(o_ref.dtype)


def _in_proj(x2, g, w_bf16, after, row0, rows, tm=512):
    d = x2.shape[1]
    cols = w_bf16.shape[1]
    assert row0 % tm == 0 and rows % tm == 0
    blk0 = row0 // tm
    return pl.pallas_call(
        functools.partial(_inproj_kernel, col_chunk=512),
        out_shape=jax.ShapeDtypeStruct((rows, cols), BF16),
        grid=(rows // tm,),
        in_specs=[pl.BlockSpec((tm, d), lambda i: (i + blk0, 0)),
                  pl.BlockSpec((1, d), lambda i: (0, 0)),
                  pl.BlockSpec((d, cols), lambda i: (0, 0)),
                  pl.BlockSpec(memory_space=pl.ANY)],
        out_specs=pl.BlockSpec((tm, cols), lambda i: (i, 0)),
        compiler_params=pltpu.CompilerParams(dimension_semantics=("parallel",),
                                             vmem_limit_bytes=VMEM_LIMIT),
        name="in_proj",
    )(x2, g, w_bf16, after)


def _ret_kernel(q_ref, k_ref, v_ref, g_ref, intra_ref, cross_ref, kdec_ref, cdec_ref, o_ref):
    c = RET_CHUNK
    n_chunks = q_ref.shape[1] // c
    intra = intra_ref[0]
    cross = cross_ref[0]
    kdec = kdec_ref[0]
    cdec = cdec_ref[0]

    def body(n, s):
        sl = pl.ds(pl.multiple_of(n * c, c), c)
        qi = q_ref[0, sl, :]
        ki = k_ref[0, sl, :]
        vi = v_ref[0, sl, :]
        a = lax.dot_general(qi, ki, _NT, preferred_element_type=F32) * intra
        o = jnp.dot(a.astype(BF16), vi, preferred_element_type=F32)
        o = o + jnp.dot(qi, s.astype(BF16), preferred_element_type=F32) * cross
        kd_t = (ki.astype(F32) * kdec).T.astype(BF16)
        s = s * cdec + jnp.dot(kd_t, vi, preferred_element_type=F32)
        gate = g_ref[0, sl, :].astype(F32)
        o_ref[0, sl, :] = (gate * jax.nn.sigmoid(gate) * _rms(o)).astype(o_ref.dtype)
        return s

    lax.fori_loop(0, n_chunks, body, jnp.zeros((HEAD_W, HEAD_W), F32))


def _retention(proj3, intra, cross, kdec, cdec):
    b, t, _ = proj3.shape
    h = RET_HEADS

    def col(off):
        return pl.BlockSpec((1, t, HEAD_W), lambda bi, hi: (bi, 0, off + hi))

    def per_head(shape):
        return pl.BlockSpec((1,) + shape, lambda bi, hi: (hi, 0, 0))

    return pl.pallas_call(
        _ret_kernel,
        out_shape=jax.ShapeDtypeStruct((b, t, h * HEAD_W), BF16),
        grid=(b, h),
        in_specs=[col(0), col(h), col(2 * h), col(3 * h),
                  per_head((RET_CHUNK, RET_CHUNK)), per_head((RET_CHUNK, 1)),
                  per_head((RET_CHUNK, 1)), per_head((1, 1))],
        out_specs=pl.BlockSpec((1, t, HEAD_W), lambda bi, hi: (bi, 0, hi)),
        compiler_params=pltpu.CompilerParams(dimension_semantics=("parallel", "parallel"),
                                             vmem_limit_bytes=VMEM_LIMIT),
        name="retention",
    )(proj3, proj3, proj3, proj3, intra, cross, kdec, cdec)


def _diff_kernel(slope_ref, lam_ref, q_ref, k_ref, v_ref, gsub_ref, o_ref, *, tq, tk, out_scale):
    hi = pl.program_id(1)
    qi = pl.program_id(2)
    half = HEAD_W // 2
    slope = slope_ref[hi]

    lp = lam_ref[...]
    lam = (jnp.exp(jnp.sum(lp[0:1] * lp[1:2], axis=1, keepdims=True))
           - jnp.exp(jnp.sum(lp[2:3] * lp[3:4], axis=1, keepdims=True))
           + (1.0 - out_scale))

    q = q_ref[0] * jnp.asarray(half ** -0.5, BF16)
    lane = lax.broadcasted_iota(jnp.int32, q.shape, 1)
    zero = jnp.zeros_like(q)
    qs = jnp.concatenate([jnp.where(lane < half, q, zero),
                          jnp.where(lane >= half, q, zero)], axis=0)

    row = lax.broadcasted_iota(jnp.int32, (2 * tq, tk), 0)
    row = jnp.where(row >= tq, row - tq, row)
    col = lax.broadcasted_iota(jnp.int32, (2 * tq, tk), 1)
    rel = (row - col).astype(F32)
    rel_bias = -slope * rel

    def body(j, carry):
        m, l, acc = carry
        ksl = pl.ds(pl.multiple_of(j * tk, tk), tk)
        k = k_ref[0, ksl, :]
        v = v_ref[0, ksl, :]
        off = (qi * tq - j * tk).astype(F32)
        s = lax.dot_general(qs, k, _NT, preferred_element_type=F32)
        s = s + jnp.where(rel >= -off, rel_bias - slope * off, NEG_INF)
        m_new = jnp.maximum(m, jnp.max(s, axis=1, keepdims=True))
        alpha = jnp.exp(m - m_new)
        p = jnp.exp(s - m_new)
        l = alpha * l + jnp.sum(p, axis=1, keepdims=True)
        acc = alpha * acc + jnp.dot(p.astype(BF16), v, preferred_element_type=F32)
        return m_new, l, acc

    n_kv = (qi * tq + tq - 1) // tk + 1
    m0 = jnp.full((2 * tq, 1), NEG_INF, F32)
    l0 = jnp.zeros((2 * tq, 1), F32)
    a0 = jnp.zeros((2 * tq, HEAD_W), F32)
    _, l, acc = lax.fori_loop(0, n_kv, body, (m0, l0, a0))
    o = acc / l
    dy = o[:tq] - lam * o[tq:]
    o_ref[0] = (_rms(dy) * gsub_ref[...] * out_scale).astype(o_ref.dtype)


def _diff_attention(proj3, slopes, diff_lambda_l, gsub, out_scale, tq=256, tk=512):
    b, t, _ = proj3.shape
    h = DIFF_HEADS
    base = 4 * RET_HEADS
    kern = functools.partial(_diff_kernel, tq=tq, tk=tk, out_scale=out_scale)
    return pl.pallas_call(
        kern,
        out_shape=jax.ShapeDtypeStruct((b, t, h * HEAD_W), BF16),
        grid=(b, h, t // tq),
        in_specs=[pl.BlockSpec(memory_space=pltpu.SMEM),
                  pl.BlockSpec(diff_lambda_l.shape, lambda bi, hi, qi: (0, 0)),
                  pl.BlockSpec((1, tq, HEAD_W), lambda bi, hi, qi: (bi, qi, base + hi)),
                  pl.BlockSpec((1, t, HEAD_W), lambda bi, hi, qi: (bi, 0, base + h + hi)),
                  pl.BlockSpec((1, t, HEAD_W), lambda bi, hi, qi: (bi, 0, base + 2 * h + hi)),
                  pl.BlockSpec((1, HEAD_W), lambda bi, hi, qi: (0, 0))],
        out_specs=pl.BlockSpec((1, tq, HEAD_W), lambda bi, hi, qi: (bi, qi, hi)),
        compiler_params=pltpu.CompilerParams(
            dimension_semantics=("parallel", "parallel", "arbitrary"),
            vmem_limit_bytes=VMEM_LIMIT),
        name="diff_attention",
    )(slopes, diff_lambda_l, proj3, proj3, proj3, gsub)


def _outproj_kernel(x_ref, ret_ref, dif_ref, wo_ref, g_ref, wq_ref, x1_ref, xn_ref, q_ref, *, col_chunk):
    half = ret_ref.shape[1]
    mix = jnp.dot(ret_ref[...], wo_ref[:half, :], preferred_element_type=F32)
    mix = mix + jnp.dot(dif_ref[...], wo_ref[half:, :], preferred_element_type=F32)
    x1 = x_ref[...] + mix
    x1_ref[...] = x1
    xn = (_rms(x1) * g_ref[...]).astype(BF16)
    xn_ref[...] = xn
    for j in range(q_ref.shape[1] // col_chunk):
        sl = slice(j * col_chunk, (j + 1) * col_chunk)
        q_ref[:, sl] = jnp.dot(xn, wq_ref[:, sl], preferred_element_type=F32).astype(q_ref.dtype)


def _out_proj(x2, ret2, dif2, wo_bf16, g, wq_bf16, row0, tm=512):
    n, half = ret2.shape
    d = x2.shape[1]
    qc = wq_bf16.shape[1]
    assert row0 % tm == 0 and n % tm == 0
    blk0 = row0 // tm
    row = lambda w: pl.BlockSpec((tm, w), lambda i: (i, 0))
    full = lambda a: pl.BlockSpec(a.shape, lambda i: (0, 0))
    return pl.pallas_call(
        functools.partial(_outproj_kernel, col_chunk=512),
        out_shape=(jax.ShapeDtypeStruct((n, d), F32),
                   jax.ShapeDtypeStruct((n, d), BF16),
                   jax.ShapeDtypeStruct((n, qc), BF16)),
        grid=(n // tm,),
        in_specs=[pl.BlockSpec((tm, d), lambda i: (i + blk0, 0)), row(half), row(half),
                  full(wo_bf16), full(g), full(wq_bf16)],
        out_specs=(row(d), row(d), row(qc)),
        compiler_params=pltpu.CompilerParams(dimension_semantics=("parallel",),
                                             vmem_limit_bytes=VMEM_LIMIT),
        name="out_proj",
    )(x2, ret2, dif2, wo_bf16, g, wq_bf16)


def _split3(v):
    trunc = lambda x: pltpu.bitcast(pltpu.bitcast(x, jnp.uint32) & jnp.uint32(0xFFFF0000), F32)
    hi = trunc(v)
    r1 = v - hi
    mid = trunc(r1)
    return hi.astype(BF16), mid.astype(BF16), (r1 - mid).astype(BF16)


def _spread_exact(v, sel_bf16):
    hi, mid, lo = _split3(v)
    d = lambda a: jnp.dot(a, sel_bf16, preferred_element_type=F32)
    return (d(hi) + d(mid)) + d(lo)


def _spread_exact_left(sel_bf16, v):
    hi, mid, lo = _split3(v)
    d = lambda a: jnp.dot(sel_bf16, a, preferred_element_type=F32)
    return (d(hi) + d(mid)) + d(lo)


def _topk_kernel(q_ref, sk_ref, sel_ref, pad_ref, e_ref, g_ref, g2_ref):
    tb = q_ref.shape[0]
    k = PEER_TOPK
    n_sets = 2 * PEER_HEADS
    neg = -jnp.inf

    s = jnp.stack(
        [lax.dot_general(sk_ref[j // 2, j % 2], q_ref[:, j * N_KEYS:(j + 1) * N_KEYS], _NT,
                         preferred_element_type=F32) for j in range(n_sets)], axis=0)
    key_f = lax.broadcasted_iota(jnp.int32, s.shape, 1).astype(F32)
    rank = lax.broadcasted_iota(jnp.int32, (n_sets, k, tb), 1)
    top_v = jnp.zeros((n_sets, k, tb), F32)
    top_i = jnp.zeros((n_sets, k, tb), F32)
    for r in range(k):
        m = jnp.max(s, axis=1, keepdims=True)
        i = jnp.min(jnp.where(s == m, key_f, float(N_KEYS)), axis=1, keepdims=True)
        s = jnp.where(key_f == i, neg, s)
        top_v = jnp.where(rank == r, m, top_v)
        top_i = jnp.where(rank == r, i, top_i)

    pad = pad_ref[...]
    cands, eids = [], []
    for h in range(PEER_HEADS):
        cands.append(_spread_exact_left(sel_ref[0], top_v[2 * h]) + _spread_exact_left(sel_ref[1], top_v[2 * h + 1])
                     + pad)
        eids.append(jnp.dot(sel_ref[0], top_i[2 * h].astype(BF16), preferred_element_type=F32) * float(N_KEYS)
                    + jnp.dot(sel_ref[1], top_i[2 * h + 1].astype(BF16), preferred_element_type=F32))
    cand = jnp.stack(cands, axis=0)
    eid = jnp.stack(eids, axis=0)
    n_cand = cand.shape[1]

    c_f = lax.broadcasted_iota(jnp.int32, cand.shape, 1).astype(F32)
    rank2 = lax.broadcasted_iota(jnp.int32, (PEER_HEADS, k, tb), 1)
    top_c = []
    e_rank = jnp.zeros((PEER_HEADS, k, tb), F32)
    for r in range(k):
        m = jnp.max(cand, axis=1, keepdims=True)
        pos = jnp.min(jnp.where(cand == m, c_f, float(n_cand)), axis=1, keepdims=True)
        hit = c_f == pos
        e_rank = jnp.where(rank2 == r, jnp.sum(jnp.where(hit, eid, 0.0), axis=1, keepdims=True), e_rank)
        cand = jnp.where(hit, neg, cand)
        top_c.append(m)
    p_rank = jnp.zeros((PEER_HEADS, k, tb), F32)
    denom = jnp.zeros((PEER_HEADS, 1, tb), F32)
    for r in range(k):
        p = jnp.exp(top_c[r] - top_c[0])
        denom = denom + p
        p_rank = jnp.where(rank2 == r, p, p_rank)
    g_rank = p_rank / denom

    e_out = e_rank.reshape(PEER_HEADS * k, tb).T
    g1 = g_rank.reshape(PEER_HEADS * k, tb).T
    sr = lax.broadcasted_iota(jnp.int32, (N_SEL, 2 * N_SEL), 0)
    sc = lax.broadcasted_iota(jnp.int32, (N_SEL, 2 * N_SEL), 1)
    e_ref[...] = e_out.astype(jnp.int32)
    g_ref[...] = g1
    g2_ref[...] = _spread_exact(g1, jnp.where(sc == 2 * sr + 1, 1.0, 0.0).astype(BF16))


def _candidate_tables(tb):
    k = PEER_TOPK
    pairs = [(a, b) for a in range(k) for b in range(k) if (a + 1) * (b + 1) <= k]
    n_cand = -(-len(pairs) // SUBLANES) * SUBLANES
    sel = np.zeros((2, n_cand, k), np.float32)
    pad = np.full((n_cand, tb), -np.inf, np.float32)
    for c, (a, b) in enumerate(pairs):
        sel[0, c, a] = 1.0
        sel[1, c, b] = 1.0
        pad[c, :] = 0.0
    return jnp.asarray(sel, BF16), jnp.asarray(pad)


def _peer_topk(q2, sk_bf16, tb=128):
    n, qc = q2.shape
    sel, pad = _candidate_tables(tb)
    const = lambda a: pl.BlockSpec(a.shape, lambda i: (0,) * a.ndim)
    return pl.pallas_call(
        _topk_kernel,
        out_shape=(jax.ShapeDtypeStruct((n, N_SEL), jnp.int32),
                   jax.ShapeDtypeStruct((n, N_SEL), F32),
                   jax.ShapeDtypeStruct((n, 2 * N_SEL), F32)),
        grid=(n // tb,),
        in_specs=[pl.BlockSpec((tb, qc), lambda i: (i, 0)), const(sk_bf16), const(sel), const(pad)],
        out_specs=(pl.BlockSpec((tb, N_SEL), lambda i: (i, 0)),
                   pl.BlockSpec((tb, N_SEL), lambda i: (i, 0)),
                   pl.BlockSpec((tb, 2 * N_SEL), lambda i: (i, 0))),
        compiler_params=pltpu.CompilerParams(dimension_semantics=("parallel",),
                                             vmem_limit_bytes=VMEM_LIMIT),
        name="peer_topk",
    )(q2, sk_bf16, sel, pad)


def _bf16_bits(t):
    return lax.bitcast_convert_type(t.astype(BF16), jnp.uint16).astype(jnp.uint32)


def _pack_tables(u, v):
    e, d = u.shape
    return (_bf16_bits(u) | (_bf16_bits(v) << 16)).reshape(e, d // HEAD_W, HEAD_W)


def _pack_halves(t):
    b = _bf16_bits(t)
    half = t.shape[1] // 2
    return b[:, :half] | (b[:, half:] << 16)


def _mix_kernel(efirst_ref, enext_ref, g2_ref, xn_ref, x1_ref, gfin_ref, tab_hbm, after_ref, o_ref, buf, sem, *,
                final_norm):
    step = pl.program_id(0)
    tb = xn_ref.shape[0]
    n_chunks = tab_hbm.shape[1]

    def issue(e_ref, t, slot):
        for k in range(N_SEL):
            e = e_ref[t, k]
            pltpu.make_async_copy(tab_hbm.at[e], buf.at[slot, pl.ds(k * n_chunks, n_chunks), :],
                                  sem.at[slot]).start(priority=k % 2)

    def wait(slot):
        pltpu.make_async_copy(buf.at[slot], buf.at[slot], sem.at[slot]).wait()

    @pl.when(step == 0)
    def _():
        issue(efirst_ref, 0, 0)

    rid = lax.broadcasted_iota(jnp.int32, (SUBLANES, 2 * N_SEL), 0)

    def group(gi, c):
        base = pl.multiple_of(gi * SUBLANES, SUBLANES)
        xrows = xn_ref[pl.ds(base, SUBLANES), :]
        grows = g2_ref[pl.ds(base, SUBLANES), :]
        acc = x1_ref[pl.ds(base, SUBLANES), :]
        for j in range(SUBLANES):
            slot = j % 2
            issue(enext_ref, base + j, 1 - slot)
            wait(slot)
            uv = jnp.concatenate(
                [pltpu.bitcast(buf[slot, pl.ds(c, N_SEL, stride=n_chunks), :], BF16) for c in range(n_chunks)],
                axis=1)
            s2 = lax.dot_general(xrows, uv, _NT, preferred_element_type=F32)
            act = pltpu.roll(_gelu_exact(s2), 1, axis=1)
            w2 = jnp.where(rid == j, grows * act, 0.0)
            acc = acc + jnp.dot(w2.astype(BF16), uv, preferred_element_type=F32)
        o_ref[pl.ds(base, SUBLANES), :] = acc
        return c

    lax.fori_loop(0, tb // SUBLANES, group, 0)

    @pl.when(step == pl.num_programs(0) - 1)
    def _():
        wait(0)

    if final_norm:
        o_ref[...] = _rms(o_ref[...]) * gfin_ref[...]


def _peer_mix(eids, e_next, gates2, xn, x1, gfin, table, after, row0, rows, final_norm, tb=64):
    d = x1.shape[1]
    assert tb % (2 * SUBLANES) == 0 and row0 % tb == 0 and rows % tb == 0
    blk0 = row0 // tb
    row = lambda w: pl.BlockSpec((tb, w), lambda i: (i + blk0, 0))
    return pl.pallas_call(
        functools.partial(_mix_kernel, final_norm=final_norm),
        out_shape=jax.ShapeDtypeStruct((rows, d), F32),
        grid=(rows // tb,),
        in_specs=[pl.BlockSpec((SUBLANES, N_SEL), lambda i: (row0 // SUBLANES, 0), memory_space=pltpu.SMEM),
                  pl.BlockSpec((tb, N_SEL), lambda i: (i + blk0, 0), memory_space=pltpu.SMEM),
                  row(2 * N_SEL), row(d), row(d),
                  pl.BlockSpec((1, d), lambda i: (0, 0)),
                  pl.BlockSpec(memory_space=pl.ANY),
                  pl.BlockSpec(memory_space=pl.ANY)],
        out_specs=pl.BlockSpec((tb, d), lambda i: (i, 0)),
        scratch_shapes=[pltpu.VMEM((2, N_SEL * (d // HEAD_W), HEAD_W), jnp.uint32),
                        pltpu.SemaphoreType.DMA((2,))],
        compiler_params=pltpu.CompilerParams(dimension_semantics=("arbitrary",),
                                             vmem_limit_bytes=VMEM_LIMIT),
        name="peer_mix",
    )(eids, e_next, gates2, xn, x1, gfin, table, after)


SC_LANES = 16
SC_TILES = 32
SC_TBLK = 8
SC_SLOTS = 4
HI_MASK = 0xFFFF0000
_SC_PARAMS = pltpu.CompilerParams(needs_layout_passes=False)


def _sc_unpack(word):
    return plsc.bitcast(word << 16, F32), plsc.bitcast(word & jnp.uint32(HI_MASK), F32)


def _sc_pipelined_heads(tab_hbm, ev, rows, sem, compute):
    n_items = SC_TBLK * PEER_HEADS
    ahead = SC_SLOTS - 1
    assert n_items % SC_SLOTS == 0

    def gather(it, slot):
        ti = it // PEER_HEADS
        h = it % PEER_HEADS
        return pltpu.make_async_copy(tab_hbm.at[ev.at[ti, pl.ds(h * PEER_TOPK, PEER_TOPK)]], rows.at[slot],
                                     sem.at[slot])

    for a in range(ahead):
        gather(a, a).start()

    def group(jj, carry):
        for slot in range(SC_SLOTS):
            it = SC_SLOTS * jj + slot
            gather(jnp.minimum(it + ahead, n_items - 1), (slot + ahead) % SC_SLOTS).start()
            gather(it, slot).wait()
            compute(it, slot)
        return carry

    lax.fori_loop(0, n_items // SC_SLOTS, group, 0)
    for a in range(ahead):
        gather(n_items - 1, (n_items + a) % SC_SLOTS).wait()


def _sc_scores(x, eids, utab, row0, rows):
    d = x.shape[1]
    dw = d // 2
    nw = dw // SC_LANES
    per_tile = rows // SC_TILES
    assert rows % (SC_TILES * SC_TBLK) == 0 and row0 % SC_TBLK == 0

    def body(x_hbm, e_hbm, tab_hbm, s_hbm, xv, ev, rbuf, sv, tmp, sem):
        tile = lax.axis_index("c") * 16 + lax.axis_index("s")
        lane = lax.iota(jnp.int32, SC_LANES)

        def compute(it, slot):
            ti = it // PEER_HEADS
            h = it % PEER_HEADS

            def chunk(w, accs):
                off = pl.multiple_of(w * SC_LANES, SC_LANES)
                xlo = xv[ti, pl.ds(off, SC_LANES)]
                xhi = xv[ti, pl.ds(dw + off, SC_LANES)]
                out = []
                for j in range(PEER_TOPK):
                    lo, hi = _sc_unpack(rbuf[slot, j, pl.ds(off, SC_LANES)])
                    out.append(accs[j] + lo * xlo + hi * xhi)
                return tuple(out)

            accs = lax.fori_loop(0, nw, chunk, tuple(jnp.zeros((SC_LANES,), F32) for _ in range(PEER_TOPK)))
            for j in range(PEER_TOPK):
                tmp[j, :] = accs[j]
            svec = jnp.zeros((SC_LANES,), F32)
            for l in range(SC_LANES):
                svec = svec + plsc.load_gather(tmp, [lane, jnp.full((SC_LANES,), l, jnp.int32)])
            sv[ti, pl.ds(pl.multiple_of(h * PEER_TOPK, PEER_TOPK), PEER_TOPK)] = svec

        def block(b, carry):
            t_out = pl.multiple_of(tile * per_tile + b * SC_TBLK, SC_TBLK)
            pltpu.sync_copy(x_hbm.at[pl.ds(row0 + t_out, SC_TBLK)], xv)
            pltpu.sync_copy(e_hbm.at[pl.ds(row0 + t_out, SC_TBLK)], ev)
            _sc_pipelined_heads(tab_hbm, ev, rbuf, sem, compute)
            pltpu.sync_copy(sv, s_hbm.at[pl.ds(t_out, SC_TBLK)])
            return carry

        lax.fori_loop(0, per_tile // SC_TBLK, block, 0)

    return pl.kernel(
        body,
        out_type=jax.ShapeDtypeStruct((rows, N_SEL), F32),
        mesh=plsc.VectorSubcoreMesh(core_axis_name="c", subcore_axis_name="s"),
        scratch_types=[pltpu.VMEM((SC_TBLK, d), F32), pltpu.VMEM((SC_TBLK, N_SEL), jnp.int32),
                       pltpu.VMEM((SC_SLOTS, PEER_TOPK, dw), jnp.uint32), pltpu.VMEM((SC_TBLK, N_SEL), F32),
                       pltpu.VMEM((PEER_TOPK, SC_LANES), F32), pltpu.SemaphoreType.DMA((SC_SLOTS,))],
        compiler_params=_SC_PARAMS,
        name="sc_scores",
    )(x, eids, utab)


def _sc_combine(w, eids, vtab, row0):
    rows = w.shape[0]
    dw = vtab.shape[1]
    d = 2 * dw
    nw = dw // SC_LANES
    per_tile = rows // SC_TILES
    assert rows % (SC_TILES * SC_TBLK) == 0 and row0 % SC_TBLK == 0

    def body(w_hbm, e_hbm, tab_hbm, y_hbm, wv, ev, rbuf, yv, sem):
        tile = lax.axis_index("c") * 16 + lax.axis_index("s")
        zero_i = jnp.zeros((SC_LANES,), jnp.int32)

        def compute(it, slot):
            ti = it // PEER_HEADS
            h = it % PEER_HEADS
            wk = [plsc.load_gather(wv, [zero_i + ti, zero_i + (h * PEER_TOPK + j)]) for j in range(PEER_TOPK)]

            def chunk(c, carry):
                off = pl.multiple_of(c * SC_LANES, SC_LANES)
                alo = jnp.zeros((SC_LANES,), F32)
                ahi = jnp.zeros((SC_LANES,), F32)
                for j in range(PEER_TOPK):
                    lo, hi = _sc_unpack(rbuf[slot, j, pl.ds(off, SC_LANES)])
                    alo = alo + wk[j] * lo
                    ahi = ahi + wk[j] * hi
                plsc.addupdate(yv.at[ti, pl.ds(off, SC_LANES)], alo)
                plsc.addupdate(yv.at[ti, pl.ds(dw + off, SC_LANES)], ahi)
                return carry

            plsc.parallel_loop(0, nw, 1, carry=jnp.int32(0))(chunk)

        def block(b, carry):
            t_out = pl.multiple_of(tile * per_tile + b * SC_TBLK, SC_TBLK)
            pltpu.sync_copy(w_hbm.at[pl.ds(t_out, SC_TBLK)], wv)
            pltpu.sync_copy(e_hbm.at[pl.ds(row0 + t_out, SC_TBLK)], ev)

            def zero(i, carry):
                for ti in range(SC_TBLK):
                    yv[ti, pl.ds(pl.multiple_of(i * SC_LANES, SC_LANES), SC_LANES)] = jnp.zeros((SC_LANES,), F32)
                return carry

            lax.fori_loop(0, d // SC_LANES, zero, 0)
            _sc_pipelined_heads(tab_hbm, ev, rbuf, sem, compute)
            pltpu.sync_copy(yv, y_hbm.at[pl.ds(t_out, SC_TBLK)])
            return carry

        lax.fori_loop(0, per_tile // SC_TBLK, block, 0)

    return pl.kernel(
        body,
        out_type=jax.ShapeDtypeStruct((rows, d), F32),
        mesh=plsc.VectorSubcoreMesh(core_axis_name="c", subcore_axis_name="s"),
        scratch_types=[pltpu.VMEM((SC_TBLK, N_SEL), F32), pltpu.VMEM((SC_TBLK, N_SEL), jnp.int32),
                       pltpu.VMEM((SC_SLOTS, PEER_TOPK, dw), jnp.uint32), pltpu.VMEM((SC_TBLK, d), F32),
                       pltpu.SemaphoreType.DMA((SC_SLOTS,))],
        compiler_params=_SC_PARAMS,
        name="sc_combine",
    )(w, eids, vtab)


def _gate_kernel(s_ref, g_ref, after_ref, w_ref):
    w_ref[...] = g_ref[...] * _gelu_exact(s_ref[...])


def _gate_act(s, gates, after, row0, tb=512):
    rows = s.shape[0]
    assert rows % tb == 0 and row0 % tb == 0
    blk0 = row0 // tb
    return pl.pallas_call(
        _gate_kernel,
        out_shape=jax.ShapeDtypeStruct((rows, N_SEL), F32),
        grid=(rows // tb,),
        in_specs=[pl.BlockSpec((tb, N_SEL), lambda i: (i, 0)),
                  pl.BlockSpec((tb, N_SEL), lambda i: (i + blk0, 0)),
                  pl.BlockSpec(memory_space=pl.ANY)],
        out_specs=pl.BlockSpec((tb, N_SEL), lambda i: (i, 0)),
        compiler_params=pltpu.CompilerParams(dimension_semantics=("parallel",)),
        name="peer_gate",
    )(s, gates, after)


def _resid_norm_kernel(x1_ref, y_ref, gfin_ref, after_ref, o_ref, *, final_norm):
    z = x1_ref[...] + y_ref[...]
    o_ref[...] = _rms(z) * gfin_ref[...] if final_norm else z


def _resid_norm(x1, y, gfin, after, final_norm, tb=512):
    rows, d = y.shape
    assert rows % tb == 0
    return pl.pallas_call(
        functools.partial(_resid_norm_kernel, final_norm=final_norm),
        out_shape=jax.ShapeDtypeStruct((rows, d), F32),
        grid=(rows // tb,),
        in_specs=[pl.BlockSpec((tb, d), lambda i: (i, 0)),
                  pl.BlockSpec((tb, d), lambda i: (i, 0)),
                  pl.BlockSpec((1, d), lambda i: (0, 0)),
                  pl.BlockSpec(memory_space=pl.ANY)],
        out_specs=pl.BlockSpec((tb, d), lambda i: (i, 0)),
        compiler_params=pltpu.CompilerParams(dimension_semantics=("parallel",)),
        name="peer_resid_norm",
    )(x1, y, gfin, after)


N_CHUNKS = 16
SC_CHUNKS = 13
COST_FRONT, COST_SC_SCORES, COST_SC_COMBINE, COST_TC_MIX = 0.13, 0.21, 0.18, 0.85


def _tc_schedule(tokens_per_chunk):
    t_front, t_scores, t_combine, t_mix = (c * tokens_per_chunk for c in
                                           (COST_FRONT, COST_SC_SCORES, COST_SC_COMBINE, COST_TC_MIX))
    base = [("front", c) for c in range(N_CHUNKS)] + [("mix", c) for c in range(SC_CHUNKS, N_CHUNKS)]
    order = []
    tc = sc = 0.0
    scores_done, combine_done = {}, {}
    next_gate = next_norm = 0
    for kind, c in base + [("end", None)]:
        while next_gate < SC_CHUNKS and (kind == "end"
                                         or scores_done.get(next_gate, float("inf")) + 0.25 * t_scores <= tc):
            tc = max(tc, scores_done[next_gate])
            sc = max(sc, tc) + t_combine
            combine_done[next_gate] = sc
            order.append(("gate", next_gate))
            next_gate += 1
        while next_norm < SC_CHUNKS and (kind == "end" or combine_done.get(next_norm, float("inf")) <= tc):
            order.append(("norm", next_norm))
            next_norm += 1
        if kind == "front":
            tc += t_front
            if c < SC_CHUNKS:
                sc = max(sc, tc) + t_scores
                scores_done[c] = sc
        elif kind == "mix":
            tc += t_mix
        if kind != "end":
            order.append((kind, c))
    return order


def _layer(x2, batch, seq, params, consts, li, final_norm):
    (g_mix, w_in, lam_p, g_sub, w_out, g_ffn, wq, sk, u, v, g_fin) = params
    intra, cross, kdec, cdec, slopes = consts
    n, d = x2.shape
    nc = n // N_CHUNKS
    bc = batch // N_CHUNKS
    assert batch % N_CHUNKS == 0
    table = _pack_tables(u, v)
    utab = _pack_halves(u)
    vtab = _pack_halves(v)

    chunks = [None] * N_CHUNKS
    scores = [None] * N_CHUNKS
    ys = [None] * N_CHUNKS
    outs = [None] * N_CHUNKS
    last = g_fin

    def front(c, last):
        proj = _in_proj(x2, g_mix, w_in, last, c * nc, nc)
        proj3 = proj.reshape(bc, seq, proj.shape[1])
        ret = _retention(proj3, intra, cross, kdec, cdec)
        dif = _diff_attention(proj3, slopes, lam_p, g_sub, 1.0 - li)
        half = ret.shape[2]
        x1, xn, q = _out_proj(x2, ret.reshape(nc, half), dif.reshape(nc, half), w_out, g_ffn, wq, c * nc)
        eids, gates, gates2 = _peer_topk(q, sk)
        chunks[c] = (x1, xn, eids, gates, gates2)
        if c < SC_CHUNKS:
            scores[c] = _sc_scores(xn.astype(F32), eids, utab, 0, nc)
        return gates2

    def gate(c, last):
        _, _, eids, gates, _ = chunks[c]
        w = _gate_act(scores[c], gates, last, 0)
        ys[c] = _sc_combine(w, eids, vtab, 0)
        return w

    def tc_mix(c, last):
        x1, xn, eids, _, gates2 = chunks[c]
        outs[c] = _peer_mix(eids, jnp.roll(eids, -1, axis=0), gates2, xn, x1, g_fin, table, last, 0, nc, final_norm)
        return outs[c]

    def norm(c, last):
        outs[c] = _resid_norm(chunks[c][0], ys[c], g_fin, last, final_norm)
        return outs[c]

    run = {"front": front, "gate": gate, "mix": tc_mix, "norm": norm}
    for kind, c in _tc_schedule(nc):
        last = run[kind](c, last)
    return jnp.concatenate(outs, axis=0)


def _retention_tables(dk):
    c = RET_CHUNK
    h = RET_HEADS
    log_gamma = jnp.log1p(-jnp.exp2(-5.0 - jnp.arange(h, dtype=F32)))
    idx = jnp.arange(c, dtype=F32)
    rel = idx[:, None] - idx[None, :]
    intra = jnp.where(rel >= 0, jnp.exp(log_gamma[:, None, None] * jnp.maximum(rel, 0.0)), 0.0)
    cross = jnp.exp(log_gamma[:, None] * (idx + 1.0))
    kdec = jnp.exp(log_gamma[:, None] * (c - 1.0 - idx))
    cdec = jnp.exp(log_gamma * c)
    scale = dk ** -0.5
    return intra * scale, cross[:, :, None], (kdec * scale)[:, :, None], cdec[:, None, None]


def kernel(x, norm_mix_g, w_in, diff_lambda, diff_subln_g, w_out, norm_ffn_g, peer_wq, peer_subkeys,
           peer_u, peer_v, final_norm_g):
    b, t, d = x.shape
    depth = w_in.shape[0]
    n = b * t
    x2 = x.reshape(n, d)
    consts = _retention_tables(HEAD_W) + (
        jnp.exp2(-8.0 * (jnp.arange(DIFF_HEADS, dtype=F32) + 1.0) / DIFF_HEADS),)
    for l in range(depth):
        li = 0.8 - 0.6 * math.exp(-0.3 * l)
        params = (norm_mix_g[l][None, :], w_in[l].astype(BF16), diff_lambda[l], diff_subln_g[l][None, :],
                  w_out[l].astype(BF16), norm_ffn_g[l][None, :], peer_wq[l].astype(BF16),
                  peer_subkeys[l].astype(BF16), peer_u[l], peer_v[l], final_norm_g[None, :])
        x2 = _layer(x2, b, t, params, consts, li, final_norm=(l == depth - 1))
    return x2.reshape(b, t, d)
```

```python
import functools
import math

import jax
import jax.numpy as jnp
import numpy as np
from jax import lax
from jax.experimental import pallas as pl
from jax.experimental.pallas import tpu as pltpu
from jax.experimental.pallas import tpu_sc as plsc

F32 = jnp.float32
BF16 = jnp.bfloat16

RMS_EPS = 1e-6
NEG_INF = -1e30

RET_HEADS = 4
RET_CHUNK = 128
DIFF_HEADS = 4
HEAD_W = 128
PEER_HEADS = 8
PEER_TOPK = 16
N_KEYS = 128
N_SEL = PEER_HEADS * PEER_TOPK

SUBLANES = 8
VMEM_LIMIT = 56 * 1024 * 1024

_NT = (((1,), (1,)), ((), ()))


def _rms(x, eps=RMS_EPS):
    return x * lax.rsqrt(jnp.mean(x * x, axis=-1, keepdims=True) + eps)


def _gelu_exact(x):
    return 0.5 * x * (1.0 + lax.erf(x * (2.0 ** -0.5)))


def _inproj_kernel(x_ref, g_ref, w_ref, after_ref, o_ref, *, col_chunk):
    h = (_rms(x_ref[...]) * g_ref[...]).astype(BF16)
    for j in range(o_ref.shape[1] // col_chunk):
        sl = slice(j * col_chunk, (j + 1) * col_chunk)
        o_ref[:, sl] = jnp.dot(h, w_ref[:, sl], preferred_element_type=F32).astype(o_ref.dtype)


def _in_proj(x2, g, w_bf16, after, row0, rows, tm=512):
    d = x2.shape[1]
    cols = w_bf16.shape[1]
    assert row0 % tm == 0 and rows % tm == 0
    blk0 = row0 // tm
    return pl.pallas_call(
        functools.partial(_inproj_kernel, col_chunk=512),
        out_shape=jax.ShapeDtypeStruct((rows, cols), BF16),
        grid=(rows // tm,),
        in_specs=[pl.BlockSpec((tm, d), lambda i: (i + blk0, 0)),
                  pl.BlockSpec((1, d), lambda i: (0, 0)),
                  pl.BlockSpec((d, cols), lambda i: (0, 0)),
                  pl.BlockSpec(memory_space=pl.ANY)],
        out_specs=pl.BlockSpec((tm, cols), lambda i: (i, 0)),
        compiler_params=pltpu.CompilerParams(dimension_semantics=("parallel",),
                                             vmem_limit_bytes=VMEM_LIMIT),
        name="in_proj",
    )(x2, g, w_bf16, after)


def _ret_kernel(q_ref, k_ref, v_ref, g_ref, intra_ref, cross_ref, kdec_ref, cdec_ref, o_ref):
    c = RET_CHUNK
    n_chunks = q_ref.shape[1] // c
    intra = intra_ref[0]
    cross = cross_ref[0]
    kdec = kdec_ref[0]
    cdec = cdec_ref[0]

    def body(n, s):
        sl = pl.ds(pl.multiple_of(n * c, c), c)
        qi = q_ref[0, sl, :]
        ki = k_ref[0, sl, :]
        vi = v_ref[0, sl, :]
        a = lax.dot_general(qi, ki, _NT, preferred_element_type=F32) * intra
        o = jnp.dot(a.astype(BF16), vi, preferred_element_type=F32)
        o = o + jnp.dot(qi, s.astype(BF16), preferred_element_type=F32) * cross
        kd_t = (ki.astype(F32) * kdec).T.astype(BF16)
        s = s * cdec + jnp.dot(kd_t, vi, preferred_element_type=F32)
        gate = g_ref[0, sl, :].astype(F32)
        o_ref[0, sl, :] = (gate * jax.nn.sigmoid(gate) * _rms(o)).astype(o_ref.dtype)
        return s

    lax.fori_loop(0, n_chunks, body, jnp.zeros((HEAD_W, HEAD_W), F32))


def _retention(proj3, intra, cross, kdec, cdec):
    b, t, _ = proj3.shape
    h = RET_HEADS

    def col(off):
        return pl.BlockSpec((1, t, HEAD_W), lambda bi, hi: (bi, 0, off + hi))

    def per_head(shape):
        return pl.BlockSpec((1,) + shape, lambda bi, hi: (hi, 0, 0))

    return pl.pallas_call(
        _ret_kernel,
        out_shape=jax.ShapeDtypeStruct((b, t, h * HEAD_W), BF16),
        grid=(b, h),
        in_specs=[col(0), col(h), col(2 * h), col(3 * h),
                  per_head((RET_CHUNK, RET_CHUNK)), per_head((RET_CHUNK, 1)),
                  per_head((RET_CHUNK, 1)), per_head((1, 1))],
        out_specs=pl.BlockSpec((1, t, HEAD_W), lambda bi, hi: (bi, 0, hi)),
        compiler_params=pltpu.CompilerParams(dimension_semantics=("parallel", "parallel"),
                                             vmem_limit_bytes=VMEM_LIMIT),
        name="retention",
    )(proj3, proj3, proj3, proj3, intra, cross, kdec, cdec)


def _diff_kernel(slope_ref, lam_ref, q_ref, k_ref, v_ref, gsub_ref, o_ref, *, tq, tk, out_scale):
    hi = pl.program_id(1)
    qi = pl.program_id(2)
    half = HEAD_W // 2
    slope = slope_ref[hi]

    lp = lam_ref[...]
    lam = (jnp.exp(jnp.sum(lp[0:1] * lp[1:2], axis=1, keepdims=True))
           - jnp.exp(jnp.sum(lp[2:3] * lp[3:4], axis=1, keepdims=True))
           + (1.0 - out_scale))

    q = q_ref[0] * jnp.asarray(half ** -0.5, BF16)
    lane = lax.broadcasted_iota(jnp.int32, q.shape, 1)
    zero = jnp.zeros_like(q)
    qs = jnp.concatenate([jnp.where(lane < half, q, zero),
                          jnp.where(lane >= half, q, zero)], axis=0)

    row = lax.broadcasted_iota(jnp.int32, (2 * tq, tk), 0)
    row = jnp.where(row >= tq, row - tq, row)
    col = lax.broadcasted_iota(jnp.int32, (2 * tq, tk), 1)
    rel = (row - col).astype(F32)
    rel_bias = -slope * rel

    def body(j, carry):
        m, l, acc = carry
        ksl = pl.ds(pl.multiple_of(j * tk, tk), tk)
        k = k_ref[0, ksl, :]
        v = v_ref[0, ksl, :]
        off = (qi * tq - j * tk).astype(F32)
        s = lax.dot_general(qs, k, _NT, preferred_element_type=F32)
        s = s + jnp.where(rel >= -off, rel_bias - slope * off, NEG_INF)
        m_new = jnp.maximum(m, jnp.max(s, axis=1, keepdims=True))
        alpha = jnp.exp(m - m_new)
        p = jnp.exp(s - m_new)
        l = alpha * l + jnp.sum(p, axis=1, keepdims=True)
        acc = alpha * acc + jnp.dot(p.astype(BF16), v, preferred_element_type=F32)
        return m_new, l, acc

    n_kv = (qi * tq + tq - 1) // tk + 1
    m0 = jnp.full((2 * tq, 1), NEG_INF, F32)
    l0 = jnp.zeros((2 * tq, 1), F32)
    a0 = jnp.zeros((2 * tq, HEAD_W), F32)
    _, l, acc = lax.fori_loop(0, n_kv, body, (m0, l0, a0))
    o = acc / l
    dy = o[:tq] - lam * o[tq:]
    o_ref[0] = (_rms(dy) * gsub_ref[...] * out_scale).astype(o_ref.dtype)


def _diff_attention(proj3, slopes, diff_lambda_l, gsub, out_scale, tq=256, tk=512):
    b, t, _ = proj3.shape
    h = DIFF_HEADS
    base = 4 * RET_HEADS
    kern = functools.partial(_diff_kernel, tq=tq, tk=tk, out_scale=out_scale)
    return pl.pallas_call(
        kern,
        out_shape=jax.ShapeDtypeStruct((b, t, h * HEAD_W), BF16),
        grid=(b, h, t // tq),
        in_specs=[pl.BlockSpec(memory_space=pltpu.SMEM),
                  pl.BlockSpec(diff_lambda_l.shape, lambda bi, hi, qi: (0, 0)),
                  pl.BlockSpec((1, tq, HEAD_W), lambda bi, hi, qi: (bi, qi, base + hi)),
                  pl.BlockSpec((1, t, HEAD_W), lambda bi, hi, qi: (bi, 0, base + h + hi)),
                  pl.BlockSpec((1, t, HEAD_W), lambda bi, hi, qi: (bi, 0, base + 2 * h + hi)),
                  pl.BlockSpec((1, HEAD_W), lambda bi, hi, qi: (0, 0))],
        out_specs=pl.BlockSpec((1, tq, HEAD_W), lambda bi, hi, qi: (bi, qi, hi)),
        compiler_params=pltpu.CompilerParams(
            dimension_semantics=("parallel", "parallel", "arbitrary"),
            vmem_limit_bytes=VMEM_LIMIT),
        name="diff_attention",
    )(slopes, diff_lambda_l, proj3, proj3, proj3, gsub)


def _outproj_kernel(x_ref, ret_ref, dif_ref, wo_ref, g_ref, wq_ref, x1_ref, xn_ref, q_ref, *, col_chunk):
    half = ret_ref.shape[1]
    mix = jnp.dot(ret_ref[...], wo_ref[:half, :], preferred_element_type=F32)
    mix = mix + jnp.dot(dif_ref[...], wo_ref[half:, :], preferred_element_type=F32)
    x1 = x_ref[...] + mix
    x1_ref[...] = x1
    xn = (_rms(x1) * g_ref[...]).astype(BF16)
    xn_ref[...] = xn
    for j in range(q_ref.shape[1] // col_chunk):
        sl = slice(j * col_chunk, (j + 1) * col_chunk)
        q_ref[:, sl] = jnp.dot(xn, wq_ref[:, sl], preferred_element_type=F32).astype(q_ref.dtype)


def _out_proj(x2, ret2, dif2, wo_bf16, g, wq_bf16, row0, tm=512):
    n, half = ret2.shape
    d = x2.shape[1]
    qc = wq_bf16.shape[1]
    assert row0 % tm == 0 and n % tm == 0
    blk0 = row0 // tm
    row = lambda w: pl.BlockSpec((tm, w), lambda i: (i, 0))
    full = lambda a: pl.BlockSpec(a.shape, lambda i: (0, 0))
    return pl.pallas_call(
        functools.partial(_outproj_kernel, col_chunk=512),
        out_shape=(jax.ShapeDtypeStruct((n, d), F32),
                   jax.ShapeDtypeStruct((n, d), BF16),
                   jax.ShapeDtypeStruct((n, qc), BF16)),
        grid=(n // tm,),
        in_specs=[pl.BlockSpec((tm, d), lambda i: (i + blk0, 0)), row(half), row(half),
                  full(wo_bf16), full(g), full(wq_bf16)],
        out_specs=(row(d), row(d), row(qc)),
        compiler_params=pltpu.CompilerParams(dimension_semantics=("parallel",),
                                             vmem_limit_bytes=VMEM_LIMIT),
        name="out_proj",
    )(x2, ret2, dif2, wo_bf16, g, wq_bf16)


def _split3(v):
    trunc = lambda x: pltpu.bitcast(pltpu.bitcast(x, jnp.uint32) & jnp.uint32(0xFFFF0000), F32)
    hi = trunc(v)
    r1 = v - hi
    mid = trunc(r1)
    return hi.astype(BF16), mid.astype(BF16), (r1 - mid).astype(BF16)


def _spread_exact(v, sel_bf16):
    hi, mid, lo = _split3(v)
    d = lambda a: jnp.dot(a, sel_bf16, preferred_element_type=F32)
    return (d(hi) + d(mid)) + d(lo)


def _spread_exact_left(sel_bf16, v):
    hi, mid, lo = _split3(v)
    d = lambda a: jnp.dot(sel_bf16, a, preferred_element_type=F32)
    return (d(hi) + d(mid)) + d(lo)


def _topk_kernel(q_ref, sk_ref, sel_ref, pad_ref, e_ref, g_ref, g2_ref):
    tb = q_ref.shape[0]
    k = PEER_TOPK
    n_sets = 2 * PEER_HEADS
    neg = -jnp.inf

    s = jnp.stack(
        [lax.dot_general(sk_ref[j // 2, j % 2], q_ref[:, j * N_KEYS:(j + 1) * N_KEYS], _NT,
                         preferred_element_type=F32) for j in range(n_sets)], axis=0)
    key_f = lax.broadcasted_iota(jnp.int32, s.shape, 1).astype(F32)
    rank = lax.broadcasted_iota(jnp.int32, (n_sets, k, tb), 1)
    top_v = jnp.zeros((n_sets, k, tb), F32)
    top_i = jnp.zeros((n_sets, k, tb), F32)
    for r in range(k):
        m = jnp.max(s, axis=1, keepdims=True)
        i = jnp.min(jnp.where(s == m, key_f, float(N_KEYS)), axis=1, keepdims=True)
        s = jnp.where(key_f == i, neg, s)
        top_v = jnp.where(rank == r, m, top_v)
        top_i = jnp.where(rank == r, i, top_i)

    pad = pad_ref[...]
    cands, eids = [], []
    for h in range(PEER_HEADS):
        cands.append(_spread_exact_left(sel_ref[0], top_v[2 * h]) + _spread_exact_left(sel_ref[1], top_v[2 * h + 1])
                     + pad)
        eids.append(jnp.dot(sel_ref[0], top_i[2 * h].astype(BF16), preferred_element_type=F32) * float(N_KEYS)
                    + jnp.dot(sel_ref[1], top_i[2 * h + 1].astype(BF16), preferred_element_type=F32))
    cand = jnp.stack(cands, axis=0)
    eid = jnp.stack(eids, axis=0)
    n_cand = cand.shape[1]

    c_f = lax.broadcasted_iota(jnp.int32, cand.shape, 1).astype(F32)
    rank2 = lax.broadcasted_iota(jnp.int32, (PEER_HEADS, k, tb), 1)
    top_c = []
    e_rank = jnp.zeros((PEER_HEADS, k, tb), F32)
    for r in range(k):
        m = jnp.max(cand, axis=1, keepdims=True)
        pos = jnp.min(jnp.where(cand == m, c_f, float(n_cand)), axis=1, keepdims=True)
        hit = c_f == pos
        e_rank = jnp.where(rank2 == r, jnp.sum(jnp.where(hit, eid, 0.0), axis=1, keepdims=True), e_rank)
        cand = jnp.where(hit, neg, cand)
        top_c.append(m)
    p_rank = jnp.zeros((PEER_HEADS, k, tb), F32)
    denom = jnp.zeros((PEER_HEADS, 1, tb), F32)
    for r in range(k):
        p = jnp.exp(top_c[r] - top_c[0])
        denom = denom + p
        p_rank = jnp.where(rank2 == r, p, p_rank)
    g_rank = p_rank / denom

    e_out = e_rank.reshape(PEER_HEADS * k, tb).T
    g1 = g_rank.reshape(PEER_HEADS * k, tb).T
    sr = lax.broadcasted_iota(jnp.int32, (N_SEL, 2 * N_SEL), 0)
    sc = lax.broadcasted_iota(jnp.int32, (N_SEL, 2 * N_SEL), 1)
    e_ref[...] = e_out.astype(jnp.int32)
    g_ref[...] = g1
    g2_ref[...] = _spread_exact(g1, jnp.where(sc == 2 * sr + 1, 1.0, 0.0).astype(BF16))


def _candidate_tables(tb):
    k = PEER_TOPK
    pairs = [(a, b) for a in range(k) for b in range(k) if (a + 1) * (b + 1) <= k]
    n_cand = -(-len(pairs) // SUBLANES) * SUBLANES
    sel = np.zeros((2, n_cand, k), np.float32)
    pad = np.full((n_cand, tb), -np.inf, np.float32)
    for c, (a, b) in enumerate(pairs):
        sel[0, c, a] = 1.0
        sel[1, c, b] = 1.0
        pad[c, :] = 0.0
    return jnp.asarray(sel, BF16), jnp.asarray(pad)


def _peer_topk(q2, sk_bf16, tb=128):
    n, qc = q2.shape
    sel, pad = _candidate_tables(tb)
    const = lambda a: pl.BlockSpec(a.shape, lambda i: (0,) * a.ndim)
    return pl.pallas_call(
        _topk_kernel,
        out_shape=(jax.ShapeDtypeStruct((n, N_SEL), jnp.int32),
                   jax.ShapeDtypeStruct((n, N_SEL), F32),
                   jax.ShapeDtypeStruct((n, 2 * N_SEL), F32)),
        grid=(n // tb,),
        in_specs=[pl.BlockSpec((tb, qc), lambda i: (i, 0)), const(sk_bf16), const(sel), const(pad)],
        out_specs=(pl.BlockSpec((tb, N_SEL), lambda i: (i, 0)),
                   pl.BlockSpec((tb, N_SEL), lambda i: (i, 0)),
                   pl.BlockSpec((tb, 2 * N_SEL), lambda i: (i, 0))),
        compiler_params=pltpu.CompilerParams(dimension_semantics=("parallel",),
                                             vmem_limit_bytes=VMEM_LIMIT),
        name="peer_topk",
    )(q2, sk_bf16, sel, pad)


def _bf16_bits(t):
    return lax.bitcast_convert_type(t.astype(BF16), jnp.uint16).astype(jnp.uint32)


def _pack_tables(u, v):
    e, d = u.shape
    return (_bf16_bits(u) | (_bf16_bits(v) << 16)).reshape(e, d // HEAD_W, HEAD_W)


def _pack_halves(t):
    b = _bf16_bits(t)
    half = t.shape[1] // 2
    return b[:, :half] | (b[:, half:] << 16)


def _mix_kernel(efirst_ref, enext_ref, g2_ref, xn_ref, x1_ref, gfin_ref, tab_hbm, after_ref, o_ref, buf, sem, *,
                final_norm):
    step = pl.program_id(0)
    tb = xn_ref.shape[0]
    n_chunks = tab_hbm.shape[1]

    def issue(e_ref, t, slot):
        for k in range(N_SEL):
            e = e_ref[t, k]
            pltpu.make_async_copy(tab_hbm.at[e], buf.at[slot, pl.ds(k * n_chunks, n_chunks), :],
                                  sem.at[slot]).start(priority=k % 2)

    def wait(slot):
        pltpu.make_async_copy(buf.at[slot], buf.at[slot], sem.at[slot]).wait()

    @pl.when(step == 0)
    def _():
        issue(efirst_ref, 0, 0)

    rid = lax.broadcasted_iota(jnp.int32, (SUBLANES, 2 * N_SEL), 0)

    def group(gi, c):
        base = pl.multiple_of(gi * SUBLANES, SUBLANES)
        xrows = xn_ref[pl.ds(base, SUBLANES), :]
        grows = g2_ref[pl.ds(base, SUBLANES), :]
        acc = x1_ref[pl.ds(base, SUBLANES), :]
        for j in range(SUBLANES):
            slot = j % 2
            issue(enext_ref, base + j, 1 - slot)
            wait(slot)
            uv = jnp.concatenate(
                [pltpu.bitcast(buf[slot, pl.ds(c, N_SEL, stride=n_chunks), :], BF16) for c in range(n_chunks)],
                axis=1)
            s2 = lax.dot_general(xrows, uv, _NT, preferred_element_type=F32)
            act = pltpu.roll(_gelu_exact(s2), 1, axis=1)
            w2 = jnp.where(rid == j, grows * act, 0.0)
            acc = acc + jnp.dot(w2.astype(BF16), uv, preferred_element_type=F32)
        o_ref[pl.ds(base, SUBLANES), :] = acc
        return c

    lax.fori_loop(0, tb // SUBLANES, group, 0)

    @pl.when(step == pl.num_programs(0) - 1)
    def _():
        wait(0)

    if final_norm:
        o_ref[...] = _rms(o_ref[...]) * gfin_ref[...]


def _peer_mix(eids, e_next, gates2, xn, x1, gfin, table, after, row0, rows, final_norm, tb=64):
    d = x1.shape[1]
    assert tb % (2 * SUBLANES) == 0 and row0 % tb == 0 and rows % tb == 0
    blk0 = row0 // tb
    row = lambda w: pl.BlockSpec((tb, w), lambda i: (i + blk0, 0))
    return pl.pallas_call(
        functools.partial(_mix_kernel, final_norm=final_norm),
        out_shape=jax.ShapeDtypeStruct((rows, d), F32),
        grid=(rows // tb,),
        in_specs=[pl.BlockSpec((SUBLANES, N_SEL), lambda i: (row0 // SUBLANES, 0), memory_space=pltpu.SMEM),
                  pl.BlockSpec((tb, N_SEL), lambda i: (i + blk0, 0), memory_space=pltpu.SMEM),
                  row(2 * N_SEL), row(d), row(d),
                  pl.BlockSpec((1, d), lambda i: (0, 0)),
                  pl.BlockSpec(memory_space=pl.ANY),
                  pl.BlockSpec(memory_space=pl.ANY)],
        out_specs=pl.BlockSpec((tb, d), lambda i: (i, 0)),
        scratch_shapes=[pltpu.VMEM((2, N_SEL * (d // HEAD_W), HEAD_W), jnp.uint32),
                        pltpu.SemaphoreType.DMA((2,))],
        compiler_params=pltpu.CompilerParams(dimension_semantics=("arbitrary",),
                                             vmem_limit_bytes=VMEM_LIMIT),
        name="peer_mix",
    )(eids, e_next, gates2, xn, x1, gfin, table, after)


SC_LANES = 16
SC_TILES = 32
SC_TBLK = 8
SC_SLOTS = 4
HI_MASK = 0xFFFF0000
_SC_PARAMS = pltpu.CompilerParams(needs_layout_passes=False)


def _sc_unpack(word):
    return plsc.bitcast(word << 16, F32), plsc.bitcast(word & jnp.uint32(HI_MASK), F32)


def _sc_pipelined_heads(tab_hbm, ev, rows, sem, compute):
    n_items = SC_TBLK * PEER_HEADS
    ahead = SC_SLOTS - 1
    assert n_items % SC_SLOTS == 0

    def gather(it, slot):
        ti = it // PEER_HEADS
        h = it % PEER_HEADS
        return pltpu.make_async_copy(tab_hbm.at[ev.at[ti, pl.ds(h * PEER_TOPK, PEER_TOPK)]], rows.at[slot],
                                     sem.at[slot])

    for a in range(ahead):
        gather(a, a).start()

    def group(jj, carry):
        for slot in range(SC_SLOTS):
            it = SC_SLOTS * jj + slot
            gather(jnp.minimum(it + ahead, n_items - 1), (slot + ahead) % SC_SLOTS).start()
            gather(it, slot).wait()
            compute(it, slot)
        return carry

    lax.fori_loop(0, n_items // SC_SLOTS, group, 0)
    for a in range(ahead):
        gather(n_items - 1, (n_items + a) % SC_SLOTS).wait()


def _sc_scores(x, eids, utab, row0, rows):
    d = x.shape[1]
    dw = d // 2
    nw = dw // SC_LANES
    per_tile = rows // SC_TILES
    assert rows % (SC_TILES * SC_TBLK) == 0 and row0 % SC_TBLK == 0

    def body(x_hbm, e_hbm, tab_hbm, s_hbm, xv, ev, rbuf, sv, tmp, sem):
        tile = lax.axis_index("c") * 16 + lax.axis_index("s")
        lane = lax.iota(jnp.int32, SC_LANES)

        def compute(it, slot):
            ti = it // PEER_HEADS
            h = it % PEER_HEADS

            def chunk(w, accs):
                off = pl.multiple_of(w * SC_LANES, SC_LANES)
                xlo = xv[ti, pl.ds(off, SC_LANES)]
                xhi = xv[ti, pl.ds(dw + off, SC_LANES)]
                out = []
                for j in range(PEER_TOPK):
                    lo, hi = _sc_unpack(rbuf[slot, j, pl.ds(off, SC_LANES)])
                    out.append(accs[j] + lo * xlo + hi * xhi)
                return tuple(out)

            accs = lax.fori_loop(0, nw, chunk, tuple(jnp.zeros((SC_LANES,), F32) for _ in range(PEER_TOPK)))
            for j in range(PEER_TOPK):
                tmp[j, :] = accs[j]
            svec = jnp.zeros((SC_LANES,), F32)
            for l in range(SC_LANES):
                svec = svec + plsc.load_gather(tmp, [lane, jnp.full((SC_LANES,), l, jnp.int32)])
            sv[ti, pl.ds(pl.multiple_of(h * PEER_TOPK, PEER_TOPK), PEER_TOPK)] = svec

        def block(b, carry):
            t_out = pl.multiple_of(tile * per_tile + b * SC_TBLK, SC_TBLK)
            pltpu.sync_copy(x_hbm.at[pl.ds(row0 + t_out, SC_TBLK)], xv)
            pltpu.sync_copy(e_hbm.at[pl.ds(row0 + t_out, SC_TBLK)], ev)
            _sc_pipelined_heads(tab_hbm, ev, rbuf, sem, compute)
            pltpu.sync_copy(sv, s_hbm.at[pl.ds(t_out, SC_TBLK)])
            return carry

        lax.fori_loop(0, per_tile // SC_TBLK, block, 0)

    return pl.kernel(
        body,
        out_type=jax.ShapeDtypeStruct((rows, N_SEL), F32),
        mesh=plsc.VectorSubcoreMesh(core_axis_name="c", subcore_axis_name="s"),
        scratch_types=[pltpu.VMEM((SC_TBLK, d), F32), pltpu.VMEM((SC_TBLK, N_SEL), jnp.int32),
                       pltpu.VMEM((SC_SLOTS, PEER_TOPK, dw), jnp.uint32), pltpu.VMEM((SC_TBLK, N_SEL), F32),
                       pltpu.VMEM((PEER_TOPK, SC_LANES), F32), pltpu.SemaphoreType.DMA((SC_SLOTS,))],
        compiler_params=_SC_PARAMS,
        name="sc_scores",
    )(x, eids, utab)


def _sc_combine(w, eids, vtab, row0):
    rows = w.shape[0]
    dw = vtab.shape[1]
    d = 2 * dw
    nw = dw // SC_LANES
    per_tile = rows // SC_TILES
    assert rows % (SC_TILES * SC_TBLK) == 0 and row0 % SC_TBLK == 0

    def body(w_hbm, e_hbm, tab_hbm, y_hbm, wv, ev, rbuf, yv, sem):
        tile = lax.axis_index("c") * 16 + lax.axis_index("s")
        zero_i = jnp.zeros((SC_LANES,), jnp.int32)

        def compute(it, slot):
            ti = it // PEER_HEADS
            h = it % PEER_HEADS
            wk = [plsc.load_gather(wv, [zero_i + ti, zero_i + (h * PEER_TOPK + j)]) for j in range(PEER_TOPK)]

            def chunk(c, carry):
                off = pl.multiple_of(c * SC_LANES, SC_LANES)
                alo = jnp.zeros((SC_LANES,), F32)
                ahi = jnp.zeros((SC_LANES,), F32)
                for j in range(PEER_TOPK):
                    lo, hi = _sc_unpack(rbuf[slot, j, pl.ds(off, SC_LANES)])
                    alo = alo + wk[j] * lo
                    ahi = ahi + wk[j] * hi
                plsc.addupdate(yv.at[ti, pl.ds(off, SC_LANES)], alo)
                plsc.addupdate(yv.at[ti, pl.ds(dw + off, SC_LANES)], ahi)
                return carry

            plsc.parallel_loop(0, nw, 1, carry=jnp.int32(0))(chunk)

        def block(b, carry):
            t_out = pl.multiple_of(tile * per_tile + b * SC_TBLK, SC_TBLK)
            pltpu.sync_copy(w_hbm.at[pl.ds(t_out, SC_TBLK)], wv)
            pltpu.sync_copy(e_hbm.at[pl.ds(row0 + t_out, SC_TBLK)], ev)

            def zero(i, carry):
                for ti in range(SC_TBLK):
                    yv[ti, pl.ds(pl.multiple_of(i * SC_LANES, SC_LANES), SC_LANES)] = jnp.zeros((SC_LANES,), F32)
                return carry

            lax.fori_loop(0, d // SC_LANES, zero, 0)
            _sc_pipelined_heads(tab_hbm, ev, rbuf, sem, compute)
            pltpu.sync_copy(yv, y_hbm.at[pl.ds(t_out, SC_TBLK)])
            return carry

        lax.fori_loop(0, per_tile // SC_TBLK, block, 0)

    return pl.kernel(
        body,
        out_type=jax.ShapeDtypeStruct((rows, d), F32),
        mesh=plsc.VectorSubcoreMesh(core_axis_name="c", subcore_axis_name="s"),
        scratch_types=[pltpu.VMEM((SC_TBLK, N_SEL), F32), pltpu.VMEM((SC_TBLK, N_SEL), jnp.int32),
                       pltpu.VMEM((SC_SLOTS, PEER_TOPK, dw), jnp.uint32), pltpu.VMEM((SC_TBLK, d), F32),
                       pltpu.SemaphoreType.DMA((SC_SLOTS,))],
        compiler_params=_SC_PARAMS,
        name="sc_combine",
    )(w, eids, vtab)


def _gate_kernel(s_ref, g_ref, after_ref, w_ref):
    w_ref[...] = g_ref[...] * _gelu_exact(s_ref[...])


def _gate_act(s, gates, after, row0, tb=512):
    rows = s.shape[0]
    assert rows % tb == 0 and row0 % tb == 0
    blk0 = row0 // tb
    return pl.pallas_call(
        _gate_kernel,
        out_shape=jax.ShapeDtypeStruct((rows, N_SEL), F32),
        grid=(rows // tb,),
        in_specs=[pl.BlockSpec((tb, N_SEL), lambda i: (i, 0)),
                  pl.BlockSpec((tb, N_SEL), lambda i: (i + blk0, 0)),
                  pl.BlockSpec(memory_space=pl.ANY)],
        out_specs=pl.BlockSpec((tb, N_SEL), lambda i: (i, 0)),
        compiler_params=pltpu.CompilerParams(dimension_semantics=("parallel",)),
        name="peer_gate",
    )(s, gates, after)


def _resid_norm_kernel(x1_ref, y_ref, gfin_ref, after_ref, o_ref, *, final_norm):
    z = x1_ref[...] + y_ref[...]
    o_ref[...] = _rms(z) * gfin_ref[...] if final_norm else z


def _resid_norm(x1, y, gfin, after, final_norm, tb=512):
    rows, d = y.shape
    assert rows % tb == 0
    return pl.pallas_call(
        functools.partial(_resid_norm_kernel, final_norm=final_norm),
        out_shape=jax.ShapeDtypeStruct((rows, d), F32),
        grid=(rows // tb,),
        in_specs=[pl.BlockSpec((tb, d), lambda i: (i, 0)),
                  pl.BlockSpec((tb, d), lambda i: (i, 0)),
                  pl.BlockSpec((1, d), lambda i: (0, 0)),
                  pl.BlockSpec(memory_space=pl.ANY)],
        out_specs=pl.BlockSpec((tb, d), lambda i: (i, 0)),
        compiler_params=pltpu.CompilerParams(dimension_semantics=("parallel",)),
        name="peer_resid_norm",
    )(x1, y, gfin, after)


N_CHUNKS = 16
SC_TOKEN_FRAC = 51 / 64
SPLIT_UNIT = 512
COST_FRONT, COST_SC_SCORES, COST_SC_COMBINE, COST_TC_MIX = 0.13, 0.21, 0.18, 0.85


def _sc_rows(n, tokens_per_chunk):
    n_sc = int(n * SC_TOKEN_FRAC) // SPLIT_UNIT * SPLIT_UNIT
    return [min(max(n_sc - c * tokens_per_chunk, 0), tokens_per_chunk) for c in range(N_CHUNKS)]


def _tc_schedule(tokens_per_chunk, sc_rows):
    sc_chunks = [c for c in range(N_CHUNKS) if sc_rows[c]]
    base = [("front", c) for c in range(N_CHUNKS)] + [("mix", c) for c in range(N_CHUNKS)
                                                       if sc_rows[c] < tokens_per_chunk]
    order = []
    tc = sc = 0.0
    scores_done, combine_done = {}, {}
    gates = list(sc_chunks)
    norms = list(sc_chunks)
    for kind, c in base + [("end", None)]:
        while gates and (kind == "end" or scores_done.get(gates[0], float("inf"))
                         + 0.25 * COST_SC_SCORES * sc_rows[gates[0]] <= tc):
            g = gates.pop(0)
            tc = max(tc, scores_done[g])
            sc = max(sc, tc) + COST_SC_COMBINE * sc_rows[g]
            combine_done[g] = sc
            order.append(("gate", g))
        while norms and (kind == "end" or combine_done.get(norms[0], float("inf")) <= tc):
            order.append(("norm", norms.pop(0)))
        if kind == "front":
            tc += COST_FRONT * tokens_per_chunk
            if sc_rows[c]:
                sc = max(sc, tc) + COST_SC_SCORES * sc_rows[c]
                scores_done[c] = sc
        elif kind == "mix":
            tc += COST_TC_MIX * (tokens_per_chunk - sc_rows[c])
        if kind != "end":
            order.append((kind, c))
    return order


def _layer(x2, batch, seq, params, consts, li, final_norm):
    (g_mix, w_in, lam_p, g_sub, w_out, g_ffn, wq, sk, u, v, g_fin) = params
    intra, cross, kdec, cdec, slopes = consts
    n, d = x2.shape
    nc = n // N_CHUNKS
    bc = batch // N_CHUNKS
    assert batch % N_CHUNKS == 0
    table = _pack_tables(u, v)
    utab = _pack_halves(u)
    vtab = _pack_halves(v)

    sc_rows = _sc_rows(n, nc)
    chunks = [None] * N_CHUNKS
    scores = [None] * N_CHUNKS
    ys = [None] * N_CHUNKS
    outs_sc = [None] * N_CHUNKS
    outs_tc = [None] * N_CHUNKS
    last = g_fin

    def front(c, last):
        proj = _in_proj(x2, g_mix, w_in, last, c * nc, nc)
        proj3 = proj.reshape(bc, seq, proj.shape[1])
        ret = _retention(proj3, intra, cross, kdec, cdec)
        dif = _diff_attention(proj3, slopes, lam_p, g_sub, 1.0 - li)
        half = ret.shape[2]
        x1, xn, q = _out_proj(x2, ret.reshape(nc, half), dif.reshape(nc, half), w_out, g_ffn, wq, c * nc)
        eids, gates, gates2 = _peer_topk(q, sk)
        chunks[c] = (x1, xn, eids, gates, gates2)
        if sc_rows[c]:
            scores[c] = _sc_scores(xn.astype(F32), eids, utab, 0, sc_rows[c])
        return gates2

    def gate(c, last):
        _, _, eids, gates, _ = chunks[c]
        w = _gate_act(scores[c], gates, last, 0)
        ys[c] = _sc_combine(w, eids, vtab, 0)
        return w

    def tc_mix(c, last):
        x1, xn, eids, _, gates2 = chunks[c]
        outs_tc[c] = _peer_mix(eids, jnp.roll(eids, -1, axis=0), gates2, xn, x1, g_fin, table, last,
                               sc_rows[c], nc - sc_rows[c], final_norm)
        return outs_tc[c]

    def norm(c, last):
        outs_sc[c] = _resid_norm(chunks[c][0], ys[c], g_fin, last, final_norm)
        return outs_sc[c]

    run = {"front": front, "gate": gate, "mix": tc_mix, "norm": norm}
    for kind, c in _tc_schedule(nc, sc_rows):
        last = run[kind](c, last)
    pieces = [o for c in range(N_CHUNKS) for o in (outs_sc[c], outs_tc[c]) if o is not None]
    return jnp.concatenate(pieces, axis=0)


def _retention_tables(dk):
    c = RET_CHUNK
    h = RET_HEADS
    log_gamma = jnp.log1p(-jnp.exp2(-5.0 - jnp.arange(h, dtype=F32)))
    idx = jnp.arange(c, dtype=F32)
    rel = idx[:, None] - idx[None, :]
    intra = jnp.where(rel >= 0, jnp.exp(log_gamma[:, None, None] * jnp.maximum(rel, 0.0)), 0.0)
    cross = jnp.exp(log_gamma[:, None] * (idx + 1.0))
    kdec = jnp.exp(log_gamma[:, None] * (c - 1.0 - idx))
    cdec = jnp.exp(log_gamma * c)
    scale = dk ** -0.5
    return intra * scale, cross[:, :, None], (kdec * scale)[:, :, None], cdec[:, None, None]


def kernel(x, norm_mix_g, w_in, diff_lambda, diff_subln_g, w_out, norm_ffn_g, peer_wq, peer_subkeys,
           peer_u, peer_v, final_norm_g):
    b, t, d = x.shape
    depth = w_in.shape[0]
    n = b * t
    x2 = x.reshape(n, d)
    consts = _retention_tables(HEAD_W) + (
        jnp.exp2(-8.0 * (jnp.arange(DIFF_HEADS, dtype=F32) + 1.0) / DIFF_HEADS),)
    for l in range(depth):
        li = 0.8 - 0.6 * math.exp(-0.3 * l)
        params = (norm_mix_g[l][None, :], w_in[l].astype(BF16), diff_lambda[l], diff_subln_g[l][None, :],
                  w_out[l].astype(BF16), norm_ffn_g[l][None, :], peer_wq[l].astype(BF16),
                  peer_subkeys[l].astype(BF16), peer_u[l], peer_v[l], final_norm_g[None, :])
        x2 = _layer(x2, b, t, params, consts, li, final_norm=(l == depth - 1))
    return x2.reshape(b, t, d)
```

```python
import functools
import math

import jax
import jax.numpy as jnp
import numpy as np
from jax import lax
from jax.experimental import pallas as pl
from jax.experimental.pallas import tpu as pltpu
from jax.experimental.pallas import tpu_sc as plsc

F32 = jnp.float32
BF16 = jnp.bfloat16

RMS_EPS = 1e-6
NEG_INF = -1e30

RET_HEADS = 4
RET_CHUNK = 128
DIFF_HEADS = 4
HEAD_W = 128
PEER_HEADS = 8
PEER_TOPK = 16
N_KEYS = 128
N_SEL = PEER_HEADS * PEER_TOPK

SUBLANES = 8
VMEM_LIMIT = 56 * 1024 * 1024

_NT = (((1,), (1,)), ((), ()))


def _rms(x, eps=RMS_EPS):
    return x * lax.rsqrt(jnp.mean(x * x, axis=-1, keepdims=True) + eps)


def _gelu_exact(x):
    return 0.5 * x * (1.0 + lax.erf(x * (2.0 ** -0.5)))


def _inproj_kernel(x_ref, g_ref, w_ref, after_ref, o_ref, *, col_chunk):
    h = (_rms(x_ref[...]) * g_ref[...]).astype(BF16)
    for j in range(o_ref.shape[1] // col_chunk):
        sl = slice(j * col_chunk, (j + 1) * col_chunk)
        o_ref[:, sl] = jnp.dot(h, w_ref[:, sl], preferred_element_type=F32).astype(o_ref.dtype)


def _in_proj(x2, g, w_bf16, after, row0, rows, tm=512):
    d = x2.shape[1]
    cols = w_bf16.shape[1]
    assert row0 % tm == 0 and rows % tm == 0
    blk0 = row0 // tm
    return pl.pallas_call(
        functools.partial(_inproj_kernel, col_chunk=512),
        out_shape=jax.ShapeDtypeStruct((rows, cols), BF16),
        grid=(rows // tm,),
        in_specs=[pl.BlockSpec((tm, d), lambda i: (i + blk0, 0)),
                  pl.BlockSpec((1, d), lambda i: (0, 0)),
                  pl.BlockSpec((d, cols), lambda i: (0, 0)),
                  pl.BlockSpec(memory_space=pl.ANY)],
        out_specs=pl.BlockSpec((tm, cols), lambda i: (i, 0)),
        compiler_params=pltpu.CompilerParams(dimension_semantics=("parallel",),
                                             vmem_limit_bytes=VMEM_LIMIT),
        name="in_proj",
    )(x2, g, w_bf16, after)


def _ret_kernel(q_ref, k_ref, v_ref, g_ref, intra_ref, cross_ref, kdec_ref, cdec_ref, o_ref):
    c = RET_CHUNK
    n_chunks = q_ref.shape[1] // c
    intra = intra_ref[0]
    cross = cross_ref[0]
    kdec = kdec_ref[0]
    cdec = cdec_ref[0]

    def body(n, s):
        sl = pl.ds(pl.multiple_of(n * c, c), c)
        qi = q_ref[0, sl, :]
        ki = k_ref[0, sl, :]
        vi = v_ref[0, sl, :]
        a = lax.dot_general(qi, ki, _NT, preferred_element_type=F32) * intra
        o = jnp.dot(a.astype(BF16), vi, preferred_element_type=F32)
        o = o + jnp.dot(qi, s.astype(BF16), preferred_element_type=F32) * cross
        kd_t = (ki.astype(F32) * kdec).T.astype(BF16)
        s = s * cdec + jnp.dot(kd_t, vi, preferred_element_type=F32)
        gate = g_ref[0, sl, :].astype(F32)
        o_ref[0, sl, :] = (gate * jax.nn.sigmoid(gate) * _rms(o)).astype(o_ref.dtype)
        return s

    lax.fori_loop(0, n_chunks, body, jnp.zeros((HEAD_W, HEAD_W), F32))


def _retention(proj3, intra, cross, kdec, cdec):
    b, t, _ = proj3.shape
    h = RET_HEADS

    def col(off):
        return pl.BlockSpec((1, t, HEAD_W), lambda bi, hi: (bi, 0, off + hi))

    def per_head(shape):
        return pl.BlockSpec((1,) + shape, lambda bi, hi: (hi, 0, 0))

    return pl.pallas_call(
        _ret_kernel,
        out_shape=jax.ShapeDtypeStruct((b, t, h * HEAD_W), BF16),
        grid=(b, h),
        in_specs=[col(0), col(h), col(2 * h), col(3 * h),
                  per_head((RET_CHUNK, RET_CHUNK)), per_head((RET_CHUNK, 1)),
                  per_head((RET_CHUNK, 1)), per_head((1, 1))],
        out_specs=pl.BlockSpec((1, t, HEAD_W), lambda bi, hi: (bi, 0, hi)),
        compiler_params=pltpu.CompilerParams(dimension_semantics=("parallel", "parallel"),
                                             vmem_limit_bytes=VMEM_LIMIT),
        name="retention",
    )(proj3, proj3, proj3, proj3, intra, cross, kdec, cdec)


def _diff_kernel(slope_ref, lam_ref, q_ref, k_ref, v_ref, gsub_ref, o_ref, *, tq, tk, out_scale):
    hi = pl.program_id(1)
    qi = pl.program_id(2)
    half = HEAD_W // 2
    slope = slope_ref[hi]

    lp = lam_ref[...]
    lam = (jnp.exp(jnp.sum(lp[0:1] * lp[1:2], axis=1, keepdims=True))
           - jnp.exp(jnp.sum(lp[2:3] * lp[3:4], axis=1, keepdims=True))
           + (1.0 - out_scale))

    q = q_ref[0] * jnp.asarray(half ** -0.5, BF16)
    lane = lax.broadcasted_iota(jnp.int32, q.shape, 1)
    zero = jnp.zeros_like(q)
    qs = jnp.concatenate([jnp.where(lane < half, q, zero),
                          jnp.where(lane >= half, q, zero)], axis=0)

    row = lax.broadcasted_iota(jnp.int32, (2 * tq, tk), 0)
    row = jnp.where(row >= tq, row - tq, row)
    col = lax.broadcasted_iota(jnp.int32, (2 * tq, tk), 1)
    rel = (row - col).astype(F32)
    rel_bias = -slope * rel

    def body(j, carry):
        m, l, acc = carry
        ksl = pl.ds(pl.multiple_of(j * tk, tk), tk)
        k = k_ref[0, ksl, :]
        v = v_ref[0, ksl, :]
        off = (qi * tq - j * tk).astype(F32)
        s = lax.dot_general(qs, k, _NT, preferred_element_type=F32)
        s = s + jnp.where(rel >= -off, rel_bias - slope * off, NEG_INF)
        m_new = jnp.maximum(m, jnp.max(s, axis=1, keepdims=True))
        alpha = jnp.exp(m - m_new)
        p = jnp.exp(s - m_new)
        l = alpha * l + jnp.sum(p, axis=1, keepdims=True)
        acc = alpha * acc + jnp.dot(p.astype(BF16), v, preferred_element_type=F32)
        return m_new, l, acc

    n_kv = (qi * tq + tq - 1) // tk + 1
    m0 = jnp.full((2 * tq, 1), NEG_INF, F32)
    l0 = jnp.zeros((2 * tq, 1), F32)
    a0 = jnp.zeros((2 * tq, HEAD_W), F32)
    _, l, acc = lax.fori_loop(0, n_kv, body, (m0, l0, a0))
    o = acc / l
    dy = o[:tq] - lam * o[tq:]
    o_ref[0] = (_rms(dy) * gsub_ref[...] * out_scale).astype(o_ref.dtype)


def _diff_attention(proj3, slopes, diff_lambda_l, gsub, out_scale, tq=256, tk=512):
    b, t, _ = proj3.shape
    h = DIFF_HEADS
    base = 4 * RET_HEADS
    kern = functools.partial(_diff_kernel, tq=tq, tk=tk, out_scale=out_scale)
    return pl.pallas_call(
        kern,
        out_shape=jax.ShapeDtypeStruct((b, t, h * HEAD_W), BF16),
        grid=(b, h, t // tq),
        in_specs=[pl.BlockSpec(memory_space=pltpu.SMEM),
                  pl.BlockSpec(diff_lambda_l.shape, lambda bi, hi, qi: (0, 0)),
                  pl.BlockSpec((1, tq, HEAD_W), lambda bi, hi, qi: (bi, qi, base + hi)),
                  pl.BlockSpec((1, t, HEAD_W), lambda bi, hi, qi: (bi, 0, base + h + hi)),
                  pl.BlockSpec((1, t, HEAD_W), lambda bi, hi, qi: (bi, 0, base + 2 * h + hi)),
                  pl.BlockSpec((1, HEAD_W), lambda bi, hi, qi: (0, 0))],
        out_specs=pl.BlockSpec((1, tq, HEAD_W), lambda bi, hi, qi: (bi, qi, hi)),
        compiler_params=pltpu.CompilerParams(
            dimension_semantics=("parallel", "parallel", "arbitrary"),
            vmem_limit_bytes=VMEM_LIMIT),
        name="diff_attention",
    )(slopes, diff_lambda_l, proj3, proj3, proj3, gsub)


def _outproj_kernel(x_ref, ret_ref, dif_ref, wo_ref, g_ref, wq_ref, x1_ref, xn_ref, q_ref, *, col_chunk):
    half = ret_ref.shape[1]
    mix = jnp.dot(ret_ref[...], wo_ref[:half, :], preferred_element_type=F32)
    mix = mix + jnp.dot(dif_ref[...], wo_ref[half:, :], preferred_element_type=F32)
    x1 = x_ref[...] + mix
    x1_ref[...] = x1
    xn = (_rms(x1) * g_ref[...]).astype(BF16)
    xn_ref[...] = xn
    for j in range(q_ref.shape[1] // col_chunk):
        sl = slice(j * col_chunk, (j + 1) * col_chunk)
        q_ref[:, sl] = jnp.dot(xn, wq_ref[:, sl], preferred_element_type=F32).astype(q_ref.dtype)


def _out_proj(x2, ret2, dif2, wo_bf16, g, wq_bf16, row0, tm=512):
    n, half = ret2.shape
    d = x2.shape[1]
    qc = wq_bf16.shape[1]
    assert row0 % tm == 0 and n % tm == 0
    blk0 = row0 // tm
    row = lambda w: pl.BlockSpec((tm, w), lambda i: (i, 0))
    full = lambda a: pl.BlockSpec(a.shape, lambda i: (0, 0))
    return pl.pallas_call(
        functools.partial(_outproj_kernel, col_chunk=512),
        out_shape=(jax.ShapeDtypeStruct((n, d), F32),
                   jax.ShapeDtypeStruct((n, d), BF16),
                   jax.ShapeDtypeStruct((n, qc), BF16)),
        grid=(n // tm,),
        in_specs=[pl.BlockSpec((tm, d), lambda i: (i + blk0, 0)), row(half), row(half),
                  full(wo_bf16), full(g), full(wq_bf16)],
        out_specs=(row(d), row(d), row(qc)),
        compiler_params=pltpu.CompilerParams(dimension_semantics=("parallel",),
                                             vmem_limit_bytes=VMEM_LIMIT),
        name="out_proj",
    )(x2, ret2, dif2, wo_bf16, g, wq_bf16)


def _split3(v):
    trunc = lambda x: pltpu.bitcast(pltpu.bitcast(x, jnp.uint32) & jnp.uint32(0xFFFF0000), F32)
    hi = trunc(v)
    r1 = v - hi
    mid = trunc(r1)
    return hi.astype(BF16), mid.astype(BF16), (r1 - mid).astype(BF16)


def _spread_exact(v, sel_bf16):
    hi, mid, lo = _split3(v)
    d = lambda a: jnp.dot(a, sel_bf16, preferred_element_type=F32)
    return (d(hi) + d(mid)) + d(lo)


def _spread_exact_left(sel_bf16, v):
    hi, mid, lo = _split3(v)
    d = lambda a: jnp.dot(sel_bf16, a, preferred_element_type=F32)
    return (d(hi) + d(mid)) + d(lo)


def _topk_kernel(q_ref, sk_ref, sel_ref, pad_ref, e_ref, g_ref, g2_ref):
    tb = q_ref.shape[0]
    k = PEER_TOPK
    n_sets = 2 * PEER_HEADS
    neg = -jnp.inf

    s = jnp.stack(
        [lax.dot_general(sk_ref[j // 2, j % 2], q_ref[:, j * N_KEYS:(j + 1) * N_KEYS], _NT,
                         preferred_element_type=F32) for j in range(n_sets)], axis=0)
    key_f = lax.broadcasted_iota(jnp.int32, s.shape, 1).astype(F32)
    rank = lax.broadcasted_iota(jnp.int32, (n_sets, k, tb), 1)
    top_v = jnp.zeros((n_sets, k, tb), F32)
    top_i = jnp.zeros((n_sets, k, tb), F32)
    for r in range(k):
        m = jnp.max(s, axis=1, keepdims=True)
        i = jnp.min(jnp.where(s == m, key_f, float(N_KEYS)), axis=1, keepdims=True)
        s = jnp.where(key_f == i, neg, s)
        top_v = jnp.where(rank == r, m, top_v)
        top_i = jnp.where(rank == r, i, top_i)

    pad = pad_ref[...]
    cands, eids = [], []
    for h in range(PEER_HEADS):
        cands.append(_spread_exact_left(sel_ref[0], top_v[2 * h]) + _spread_exact_left(sel_ref[1], top_v[2 * h + 1])
                     + pad)
        eids.append(jnp.dot(sel_ref[0], top_i[2 * h].astype(BF16), preferred_element_type=F32) * float(N_KEYS)
                    + jnp.dot(sel_ref[1], top_i[2 * h + 1].astype(BF16), preferred_element_type=F32))
    cand = jnp.stack(cands, axis=0)
    eid = jnp.stack(eids, axis=0)
    n_cand = cand.shape[1]

    c_f = lax.broadcasted_iota(jnp.int32, cand.shape, 1).astype(F32)
    rank2 = lax.broadcasted_iota(jnp.int32, (PEER_HEADS, k, tb), 1)
    top_c = []
    e_rank = jnp.zeros((PEER_HEADS, k, tb), F32)
    for r in range(k):
        m = jnp.max(cand, axis=1, keepdims=True)
        pos = jnp.min(jnp.where(cand == m, c_f, float(n_cand)), axis=1, keepdims=True)
        hit = c_f == pos
        e_rank = jnp.where(rank2 == r, jnp.sum(jnp.where(hit, eid, 0.0), axis=1, keepdims=True), e_rank)
        cand = jnp.where(hit, neg, cand)
        top_c.append(m)
    p_rank = jnp.zeros((PEER_HEADS, k, tb), F32)
    denom = jnp.zeros((PEER_HEADS, 1, tb), F32)
    for r in range(k):
        p = jnp.exp(top_c[r] - top_c[0])
        denom = denom + p
        p_rank = jnp.where(rank2 == r, p, p_rank)
    g_rank = p_rank / denom

    e_out = e_rank.reshape(PEER_HEADS * k, tb).T
    g1 = g_rank.reshape(PEER_HEADS * k, tb).T
    sr = lax.broadcasted_iota(jnp.int32, (N_SEL, 2 * N_SEL), 0)
    sc = lax.broadcasted_iota(jnp.int32, (N_SEL, 2 * N_SEL), 1)
    e_ref[...] = e_out.astype(jnp.int32)
    g_ref[...] = g1
    g2_ref[...] = _spread_exact(g1, jnp.where(sc == 2 * sr + 1, 1.0, 0.0).astype(BF16))


def _candidate_tables(tb):
    k = PEER_TOPK
    pairs = [(a, b) for a in range(k) for b in range(k) if (a + 1) * (b + 1) <= k]
    n_cand = -(-len(pairs) // SUBLANES) * SUBLANES
    sel = np.zeros((2, n_cand, k), np.float32)
    pad = np.full((n_cand, tb), -np.inf, np.float32)
    for c, (a, b) in enumerate(pairs):
        sel[0, c, a] = 1.0
        sel[1, c, b] = 1.0
        pad[c, :] = 0.0
    return jnp.asarray(sel, BF16), jnp.asarray(pad)


def _peer_topk(q2, sk_bf16, tb=128):
    n, qc = q2.shape
    sel, pad = _candidate_tables(tb)
    const = lambda a: pl.BlockSpec(a.shape, lambda i: (0,) * a.ndim)
    return pl.pallas_call(
        _topk_kernel,
        out_shape=(jax.ShapeDtypeStruct((n, N_SEL), jnp.int32),
                   jax.ShapeDtypeStruct((n, N_SEL), F32),
                   jax.ShapeDtypeStruct((n, 2 * N_SEL), F32)),
        grid=(n // tb,),
        in_specs=[pl.BlockSpec((tb, qc), lambda i: (i, 0)), const(sk_bf16), const(sel), const(pad)],
        out_specs=(pl.BlockSpec((tb, N_SEL), lambda i: (i, 0)),
                   pl.BlockSpec((tb, N_SEL), lambda i: (i, 0)),
                   pl.BlockSpec((tb, 2 * N_SEL), lambda i: (i, 0))),
        compiler_params=pltpu.CompilerParams(dimension_semantics=("parallel",),
                                             vmem_limit_bytes=VMEM_LIMIT),
        name="peer_topk",
    )(q2, sk_bf16, sel, pad)


def _bf16_bits(t):
    return lax.bitcast_convert_type(t.astype(BF16), jnp.uint16).astype(jnp.uint32)


def _pack_tables(u, v):
    e, d = u.shape
    return (_bf16_bits(u) | (_bf16_bits(v) << 16)).reshape(e, d // HEAD_W, HEAD_W)


def _pack_halves(t):
    b = _bf16_bits(t)
    half = t.shape[1] // 2
    return b[:, :half] | (b[:, half:] << 16)


def _mix_kernel(efirst_ref, enext_ref, g2_ref, xn_ref, x1_ref, gfin_ref, tab_hbm, after_ref, o_ref, buf, sem, *,
                final_norm):
    step = pl.program_id(0)
    tb = xn_ref.shape[0]
    n_chunks = tab_hbm.shape[1]

    def issue(e_ref, t, slot):
        for k in range(N_SEL):
            e = e_ref[t, k]
            pltpu.make_async_copy(tab_hbm.at[e], buf.at[slot, pl.ds(k * n_chunks, n_chunks), :],
                                  sem.at[slot]).start()

    def wait(slot):
        pltpu.make_async_copy(buf.at[slot], buf.at[slot], sem.at[slot]).wait()

    @pl.when(step == 0)
    def _():
        issue(efirst_ref, 0, 0)

    rid = lax.broadcasted_iota(jnp.int32, (SUBLANES, 2 * N_SEL), 0)

    def group(gi, c):
        base = pl.multiple_of(gi * SUBLANES, SUBLANES)
        xrows = xn_ref[pl.ds(base, SUBLANES), :]
        grows = g2_ref[pl.ds(base, SUBLANES), :]
        acc = x1_ref[pl.ds(base, SUBLANES), :]
        for j in range(SUBLANES):
            slot = j % 2
            issue(enext_ref, base + j, 1 - slot)
            wait(slot)
            uv = jnp.concatenate(
                [pltpu.bitcast(buf[slot, pl.ds(c, N_SEL, stride=n_chunks), :], BF16) for c in range(n_chunks)],
                axis=1)
            s2 = lax.dot_general(xrows, uv, _NT, preferred_element_type=F32)
            act = pltpu.roll(_gelu_exact(s2), 1, axis=1)
            w2 = jnp.where(rid == j, grows * act, 0.0)
            acc = acc + jnp.dot(w2.astype(BF16), uv, preferred_element_type=F32)
        o_ref[pl.ds(base, SUBLANES), :] = acc
        return c

    lax.fori_loop(0, tb // SUBLANES, group, 0)

    @pl.when(step == pl.num_programs(0) - 1)
    def _():
        wait(0)

    if final_norm:
        o_ref[...] = _rms(o_ref[...]) * gfin_ref[...]


def _peer_mix(eids, e_next, gates2, xn, x1, gfin, table, after, row0, rows, final_norm, tb=64):
    d = x1.shape[1]
    assert tb % (2 * SUBLANES) == 0 and row0 % tb == 0 and rows % tb == 0
    blk0 = row0 // tb
    row = lambda w: pl.BlockSpec((tb, w), lambda i: (i + blk0, 0))
    return pl.pallas_call(
        functools.partial(_mix_kernel, final_norm=final_norm),
        out_shape=jax.ShapeDtypeStruct((rows, d), F32),
        grid=(rows // tb,),
        in_specs=[pl.BlockSpec((SUBLANES, N_SEL), lambda i: (row0 // SUBLANES, 0), memory_space=pltpu.SMEM),
                  pl.BlockSpec((tb, N_SEL), lambda i: (i + blk0, 0), memory_space=pltpu.SMEM),
                  row(2 * N_SEL), row(d), row(d),
                  pl.BlockSpec((1, d), lambda i: (0, 0)),
                  pl.BlockSpec(memory_space=pl.ANY),
                  pl.BlockSpec(memory_space=pl.ANY)],
        out_specs=pl.BlockSpec((tb, d), lambda i: (i, 0)),
        scratch_shapes=[pltpu.VMEM((2, N_SEL * (d // HEAD_W), HEAD_W), jnp.uint32),
                        pltpu.SemaphoreType.DMA((2,))],
        compiler_params=pltpu.CompilerParams(dimension_semantics=("arbitrary",),
                                             vmem_limit_bytes=VMEM_LIMIT),
        name="peer_mix",
    )(eids, e_next, gates2, xn, x1, gfin, table, after)


SC_LANES = 16
SC_TILES = 32
SC_TBLK = 8
SC_SLOTS = 4
HI_MASK = 0xFFFF0000
_SC_PARAMS = pltpu.CompilerParams(needs_layout_passes=False)


def _sc_unpack(word):
    return plsc.bitcast(word << 16, F32), plsc.bitcast(word & jnp.uint32(HI_MASK), F32)


def _sc_pipelined_heads(tab_hbm, ev, rows, sem, compute):
    n_items = SC_TBLK * PEER_HEADS
    ahead = SC_SLOTS - 1
    assert n_items % SC_SLOTS == 0

    def gather(it, slot):
        ti = it // PEER_HEADS
        h = it % PEER_HEADS
        return pltpu.make_async_copy(tab_hbm.at[ev.at[ti, pl.ds(h * PEER_TOPK, PEER_TOPK)]], rows.at[slot],
                                     sem.at[slot])

    for a in range(ahead):
        gather(a, a).start()

    def group(jj, carry):
        for slot in range(SC_SLOTS):
            it = SC_SLOTS * jj + slot
            gather(jnp.minimum(it + ahead, n_items - 1), (slot + ahead) % SC_SLOTS).start()
            gather(it, slot).wait()
            compute(it, slot)
        return carry

    lax.fori_loop(0, n_items // SC_SLOTS, group, 0)
    for a in range(ahead):
        gather(n_items - 1, (n_items + a) % SC_SLOTS).wait()


def _sc_scores(x, eids, utab, row0, rows):
    d = x.shape[1]
    dw = d // 2
    nw = dw // SC_LANES
    per_tile = rows // SC_TILES
    assert rows % (SC_TILES * SC_TBLK) == 0 and row0 % SC_TBLK == 0

    def body(x_hbm, e_hbm, tab_hbm, s_hbm, xv, ev, rbuf, sv, tmp, sem):
        tile = lax.axis_index("c") * 16 + lax.axis_index("s")
        lane = lax.iota(jnp.int32, SC_LANES)

        def compute(it, slot):
            ti = it // PEER_HEADS
            h = it % PEER_HEADS

            def chunk(w, accs):
                off = pl.multiple_of(w * SC_LANES, SC_LANES)
                xlo = xv[ti, pl.ds(off, SC_LANES)]
                xhi = xv[ti, pl.ds(dw + off, SC_LANES)]
                out = []
                for j in range(PEER_TOPK):
                    lo, hi = _sc_unpack(rbuf[slot, j, pl.ds(off, SC_LANES)])
                    out.append(accs[j] + lo * xlo + hi * xhi)
                return tuple(out)

            accs = lax.fori_loop(0, nw, chunk, tuple(jnp.zeros((SC_LANES,), F32) for _ in range(PEER_TOPK)))
            for j in range(PEER_TOPK):
                tmp[j, :] = accs[j]
            svec = jnp.zeros((SC_LANES,), F32)
            for l in range(SC_LANES):
                svec = svec + plsc.load_gather(tmp, [lane, jnp.full((SC_LANES,), l, jnp.int32)])
            sv[ti, pl.ds(pl.multiple_of(h * PEER_TOPK, PEER_TOPK), PEER_TOPK)] = svec

        def block(b, carry):
            t_out = pl.multiple_of(tile * per_tile + b * SC_TBLK, SC_TBLK)
            pltpu.sync_copy(x_hbm.at[pl.ds(row0 + t_out, SC_TBLK)], xv)
            pltpu.sync_copy(e_hbm.at[pl.ds(row0 + t_out, SC_TBLK)], ev)
            _sc_pipelined_heads(tab_hbm, ev, rbuf, sem, compute)
            pltpu.sync_copy(sv, s_hbm.at[pl.ds(t_out, SC_TBLK)])
            return carry

        lax.fori_loop(0, per_tile // SC_TBLK, block, 0)

    return pl.kernel(
        body,
        out_type=jax.ShapeDtypeStruct((rows, N_SEL), F32),
        mesh=plsc.VectorSubcoreMesh(core_axis_name="c", subcore_axis_name="s"),
        scratch_types=[pltpu.VMEM((SC_TBLK, d), F32), pltpu.VMEM((SC_TBLK, N_SEL), jnp.int32),
                       pltpu.VMEM((SC_SLOTS, PEER_TOPK, dw), jnp.uint32), pltpu.VMEM((SC_TBLK, N_SEL), F32),
                       pltpu.VMEM((PEER_TOPK, SC_LANES), F32), pltpu.SemaphoreType.DMA((SC_SLOTS,))],
        compiler_params=_SC_PARAMS,
        name="sc_scores",
    )(x, eids, utab)


def _sc_combine(w, eids, vtab, row0):
    rows = w.shape[0]
    dw = vtab.shape[1]
    d = 2 * dw
    nw = dw // SC_LANES
    per_tile = rows // SC_TILES
    assert rows % (SC_TILES * SC_TBLK) == 0 and row0 % SC_TBLK == 0

    def body(w_hbm, e_hbm, tab_hbm, y_hbm, wv, ev, rbuf, yv, sem):
        tile = lax.axis_index("c") * 16 + lax.axis_index("s")
        zero_i = jnp.zeros((SC_LANES,), jnp.int32)

        def compute(it, slot):
            ti = it // PEER_HEADS
            h = it % PEER_HEADS
            wk = [plsc.load_gather(wv, [zero_i + ti, zero_i + (h * PEER_TOPK + j)]) for j in range(PEER_TOPK)]

            def chunk(c, carry):
                off = pl.multiple_of(c * SC_LANES, SC_LANES)
                alo = jnp.zeros((SC_LANES,), F32)
                ahi = jnp.zeros((SC_LANES,), F32)
                for j in range(PEER_TOPK):
                    lo, hi = _sc_unpack(rbuf[slot, j, pl.ds(off, SC_LANES)])
                    alo = alo + wk[j] * lo
                    ahi = ahi + wk[j] * hi
                plsc.addupdate(yv.at[ti, pl.ds(off, SC_LANES)], alo)
                plsc.addupdate(yv.at[ti, pl.ds(dw + off, SC_LANES)], ahi)
                return carry

            plsc.parallel_loop(0, nw, 1, carry=jnp.int32(0))(chunk)

        def block(b, carry):
            t_out = pl.multiple_of(tile * per_tile + b * SC_TBLK, SC_TBLK)
            pltpu.sync_copy(w_hbm.at[pl.ds(t_out, SC_TBLK)], wv)
            pltpu.sync_copy(e_hbm.at[pl.ds(row0 + t_out, SC_TBLK)], ev)

            def zero(i, carry):
                for ti in range(SC_TBLK):
                    yv[ti, pl.ds(pl.multiple_of(i * SC_LANES, SC_LANES), SC_LANES)] = jnp.zeros((SC_LANES,), F32)
                return carry

            lax.fori_loop(0, d // SC_LANES, zero, 0)
            _sc_pipelined_heads(tab_hbm, ev, rbuf, sem, compute)
            pltpu.sync_copy(yv, y_hbm.at[pl.ds(t_out, SC_TBLK)])
            return carry

        lax.fori_loop(0, per_tile // SC_TBLK, block, 0)

    return pl.kernel(
        body,
        out_type=jax.ShapeDtypeStruct((rows, d), F32),
        mesh=plsc.VectorSubcoreMesh(core_axis_name="c", subcore_axis_name="s"),
        scratch_types=[pltpu.VMEM((SC_TBLK, N_SEL), F32), pltpu.VMEM((SC_TBLK, N_SEL), jnp.int32),
                       pltpu.VMEM((SC_SLOTS, PEER_TOPK, dw), jnp.uint32), pltpu.VMEM((SC_TBLK, d), F32),
                       pltpu.SemaphoreType.DMA((SC_SLOTS,))],
        compiler_params=_SC_PARAMS,
        name="sc_combine",
    )(w, eids, vtab)


def _gate_kernel(s_ref, g_ref, after_ref, w_ref):
    w_ref[...] = g_ref[...] * _gelu_exact(s_ref[...])


def _gate_act(s, gates, after, row0, tb=512):
    rows = s.shape[0]
    assert rows % tb == 0 and row0 % tb == 0
    blk0 = row0 // tb
    return pl.pallas_call(
        _gate_kernel,
        out_shape=jax.ShapeDtypeStruct((rows, N_SEL), F32),
        grid=(rows // tb,),
        in_specs=[pl.BlockSpec((tb, N_SEL), lambda i: (i, 0)),
                  pl.BlockSpec((tb, N_SEL), lambda i: (i + blk0, 0)),
                  pl.BlockSpec(memory_space=pl.ANY)],
        out_specs=pl.BlockSpec((tb, N_SEL), lambda i: (i, 0)),
        compiler_params=pltpu.CompilerParams(dimension_semantics=("parallel",)),
        name="peer_gate",
    )(s, gates, after)


def _resid_norm_kernel(x1_ref, y_ref, gfin_ref, after_ref, o_ref, *, final_norm):
    z = x1_ref[...] + y_ref[...]
    o_ref[...] = _rms(z) * gfin_ref[...] if final_norm else z


def _resid_norm(x1, y, gfin, after, final_norm, tb=512):
    rows, d = y.shape
    assert rows % tb == 0
    return pl.pallas_call(
        functools.partial(_resid_norm_kernel, final_norm=final_norm),
        out_shape=jax.ShapeDtypeStruct((rows, d), F32),
        grid=(rows // tb,),
        in_specs=[pl.BlockSpec((tb, d), lambda i: (i, 0)),
                  pl.BlockSpec((tb, d), lambda i: (i, 0)),
                  pl.BlockSpec((1, d), lambda i: (0, 0)),
                  pl.BlockSpec(memory_space=pl.ANY)],
        out_specs=pl.BlockSpec((tb, d), lambda i: (i, 0)),
        compiler_params=pltpu.CompilerParams(dimension_semantics=("parallel",)),
        name="peer_resid_norm",
    )(x1, y, gfin, after)


N_CHUNKS = 16
SC_TOKEN_FRAC = 51 / 64
SPLIT_UNIT = 512
COST_FRONT, COST_SC_SCORES, COST_SC_COMBINE, COST_TC_MIX = 0.13, 0.21, 0.18, 0.85


def _sc_rows(n, tokens_per_chunk):
    n_sc = int(n * SC_TOKEN_FRAC) // SPLIT_UNIT * SPLIT_UNIT
    return [min(max(n_sc - c * tokens_per_chunk, 0), tokens_per_chunk) for c in range(N_CHUNKS)]


def _tc_schedule(tokens_per_chunk, sc_rows):
    sc_chunks = [c for c in range(N_CHUNKS) if sc_rows[c]]
    base = [("front", c) for c in range(N_CHUNKS)] + [("mix", c) for c in range(N_CHUNKS)
                                                       if sc_rows[c] < tokens_per_chunk]
    order = []
    tc = sc = 0.0
    scores_done, combine_done = {}, {}
    gates = list(sc_chunks)
    norms = list(sc_chunks)
    for kind, c in base + [("end", None)]:
        while gates and (kind == "end" or scores_done.get(gates[0], float("inf"))
                         + 0.25 * COST_SC_SCORES * sc_rows[gates[0]] <= tc):
            g = gates.pop(0)
            tc = max(tc, scores_done[g])
            sc = max(sc, tc) + COST_SC_COMBINE * sc_rows[g]
            combine_done[g] = sc
            order.append(("gate", g))
        while norms and (kind == "end" or combine_done.get(norms[0], float("inf")) <= tc):
            order.append(("norm", norms.pop(0)))
        if kind == "front":
            tc += COST_FRONT * tokens_per_chunk
            if sc_rows[c]:
                sc = max(sc, tc) + COST_SC_SCORES * sc_rows[c]
                scores_done[c] = sc
        elif kind == "mix":
            tc += COST_TC_MIX * (tokens_per_chunk - sc_rows[c])
        if kind != "end":
            order.append((kind, c))
    return order


def _layer(x2, batch, seq, params, consts, li, final_norm):
    (g_mix, w_in, lam_p, g_sub, w_out, g_ffn, wq, sk, u, v, g_fin) = params
    intra, cross, kdec, cdec, slopes = consts
    n, d = x2.shape
    nc = n // N_CHUNKS
    bc = batch // N_CHUNKS
    assert batch % N_CHUNKS == 0
    table = _pack_tables(u, v)
    utab = _pack_halves(u)
    vtab = _pack_halves(v)

    sc_rows = _sc_rows(n, nc)
    chunks = [None] * N_CHUNKS
    scores = [None] * N_CHUNKS
    ys = [None] * N_CHUNKS
    outs_sc = [None] * N_CHUNKS
    outs_tc = [None] * N_CHUNKS
    last = g_fin

    def front(c, last):
        proj = _in_proj(x2, g_mix, w_in, last, c * nc, nc)
        proj3 = proj.reshape(bc, seq, proj.shape[1])
        ret = _retention(proj3, intra, cross, kdec, cdec)
        dif = _diff_attention(proj3, slopes, lam_p, g_sub, 1.0 - li)
        half = ret.shape[2]
        x1, xn, q = _out_proj(x2, ret.reshape(nc, half), dif.reshape(nc, half), w_out, g_ffn, wq, c * nc)
        eids, gates, gates2 = _peer_topk(q, sk)
        chunks[c] = (x1, xn, eids, gates, gates2)
        if sc_rows[c]:
            scores[c] = _sc_scores(xn.astype(F32), eids, utab, 0, sc_rows[c])
        return gates2

    def gate(c, last):
        _, _, eids, gates, _ = chunks[c]
        w = _gate_act(scores[c], gates, last, 0)
        ys[c] = _sc_combine(w, eids, vtab, 0)
        return w

    def tc_mix(c, last):
        x1, xn, eids, _, gates2 = chunks[c]
        outs_tc[c] = _peer_mix(eids, jnp.roll(eids, -1, axis=0), gates2, xn, x1, g_fin, table, last,
                               sc_rows[c], nc - sc_rows[c], final_norm)
        return outs_tc[c]

    def norm(c, last):
        outs_sc[c] = _resid_norm(chunks[c][0], ys[c], g_fin, last, final_norm)
        return outs_sc[c]

    run = {"front": front, "gate": gate, "mix": tc_mix, "norm": norm}
    for kind, c in _tc_schedule(nc, sc_rows):
        last = run[kind](c, last)
    pieces = [o for c in range(N_CHUNKS) for o in (outs_sc[c], outs_tc[c]) if o is not None]
    return jnp.concatenate(pieces, axis=0)


def _retention_tables(dk):
    c = RET_CHUNK
    h = RET_HEADS
    log_gamma = jnp.log1p(-jnp.exp2(-5.0 - jnp.arange(h, dtype=F32)))
    idx = jnp.arange(c, dtype=F32)
    rel = idx[:, None] - idx[None, :]
    intra = jnp.where(rel >= 0, jnp.exp(log_gamma[:, None, None] * jnp.maximum(rel, 0.0)), 0.0)
    cross = jnp.exp(log_gamma[:, None] * (idx + 1.0))
    kdec = jnp.exp(log_gamma[:, None] * (c - 1.0 - idx))
    cdec = jnp.exp(log_gamma * c)
    scale = dk ** -0.5
    return intra * scale, cross[:, :, None], (kdec * scale)[:, :, None], cdec[:, None, None]


def kernel(x, norm_mix_g, w_in, diff_lambda, diff_subln_g, w_out, norm_ffn_g, peer_wq, peer_subkeys,
           peer_u, peer_v, final_norm_g):
    b, t, d = x.shape
    depth = w_in.shape[0]
    n = b * t
    x2 = x.reshape(n, d)
    consts = _retention_tables(HEAD_W) + (
        jnp.exp2(-8.0 * (jnp.arange(DIFF_HEADS, dtype=F32) + 1.0) / DIFF_HEADS),)
    for l in range(depth):
        li = 0.8 - 0.6 * math.exp(-0.3 * l)
        params = (norm_mix_g[l][None, :], w_in[l].astype(BF16), diff_lambda[l], diff_subln_g[l][None, :],
                  w_out[l].astype(BF16), norm_ffn_g[l][None, :], peer_wq[l].astype(BF16),
                  peer_subkeys[l].astype(BF16), peer_u[l], peer_v[l], final_norm_g[None, :])
        x2 = _layer(x2, b, t, params, consts, li, final_norm=(l == depth - 1))
    return x2.reshape(b, t, d)
```

```python
import functools
import math

import jax
import jax.numpy as jnp
import numpy as np
from jax import lax
from jax.experimental import pallas as pl
from jax.experimental.pallas import tpu as pltpu
from jax.experimental.pallas import tpu_sc as plsc

F32 = jnp.float32
BF16 = jnp.bfloat16

RMS_EPS = 1e-6
NEG_INF = -1e30

RET_HEADS = 4
RET_CHUNK = 128
DIFF_HEADS = 4
HEAD_W = 128
PEER_HEADS = 8
PEER_TOPK = 16
N_KEYS = 128
N_SEL = PEER_HEADS * PEER_TOPK

SUBLANES = 8
VMEM_LIMIT = 56 * 1024 * 1024

_NT = (((1,), (1,)), ((), ()))


def _rms(x, eps=RMS_EPS):
    return x * lax.rsqrt(jnp.mean(x * x, axis=-1, keepdims=True) + eps)


def _gelu_exact(x):
    return 0.5 * x * (1.0 + lax.erf(x * (2.0 ** -0.5)))


def _inproj_kernel(x_ref, g_ref, w_ref, after_ref, o_ref, *, col_chunk):
    h = (_rms(x_ref[...]) * g_ref[...]).astype(BF16)
    for j in range(o_ref.shape[1] // col_chunk):
        sl = slice(j * col_chunk, (j + 1) * col_chunk)
        o_ref[:, sl] = jnp.dot(h, w_ref[:, sl], preferred_element_type=F32).astype(o_ref.dtype)


def _in_proj(x2, g, w_bf16, after, row0, rows, tm=512):
    d = x2.shape[1]
    cols = w_bf16.shape[1]
    assert row0 % tm == 0 and rows % tm == 0
    blk0 = row0 // tm
    return pl.pallas_call(
        functools.partial(_inproj_kernel, col_chunk=512),
        out_shape=jax.ShapeDtypeStruct((rows, cols), BF16),
        grid=(rows // tm,),
        in_specs=[pl.BlockSpec((tm, d), lambda i: (i + blk0, 0)),
                  pl.BlockSpec((1, d), lambda i: (0, 0)),
                  pl.BlockSpec((d, cols), lambda i: (0, 0)),
                  pl.BlockSpec(memory_space=pl.ANY)],
        out_specs=pl.BlockSpec((tm, cols), lambda i: (i, 0)),
        compiler_params=pltpu.CompilerParams(dimension_semantics=("parallel",),
                                             vmem_limit_bytes=VMEM_LIMIT),
        name="in_proj",
    )(x2, g, w_bf16, after)


def _ret_kernel(q_ref, k_ref, v_ref, g_ref, intra_ref, cross_ref, kdec_ref, cdec_ref, o_ref):
    c = RET_CHUNK
    n_chunks = q_ref.shape[1] // c
    intra = intra_ref[0]
    cross = cross_ref[0]
    kdec = kdec_ref[0]
    cdec = cdec_ref[0]

    def body(n, s):
        sl = pl.ds(pl.multiple_of(n * c, c), c)
        qi = q_ref[0, sl, :]
        ki = k_ref[0, sl, :]
        vi = v_ref[0, sl, :]
        a = lax.dot_general(qi, ki, _NT, preferred_element_type=F32) * intra
        o = jnp.dot(a.astype(BF16), vi, preferred_element_type=F32)
        o = o + jnp.dot(qi, s.astype(BF16), preferred_element_type=F32) * cross
        kd_t = (ki.astype(F32) * kdec).T.astype(BF16)
        s = s * cdec + jnp.dot(kd_t, vi, preferred_element_type=F32)
        gate = g_ref[0, sl, :].astype(F32)
        o_ref[0, sl, :] = (gate * jax.nn.sigmoid(gate) * _rms(o)).astype(o_ref.dtype)
        return s

    lax.fori_loop(0, n_chunks, body, jnp.zeros((HEAD_W, HEAD_W), F32))


def _retention(proj3, intra, cross, kdec, cdec):
    b, t, _ = proj3.shape
    h = RET_HEADS

    def col(off):
        return pl.BlockSpec((1, t, HEAD_W), lambda bi, hi: (bi, 0, off + hi))

    def per_head(shape):
        return pl.BlockSpec((1,) + shape, lambda bi, hi: (hi, 0, 0))

    return pl.pallas_call(
        _ret_kernel,
        out_shape=jax.ShapeDtypeStruct((b, t, h * HEAD_W), BF16),
        grid=(b, h),
        in_specs=[col(0), col(h), col(2 * h), col(3 * h),
                  per_head((RET_CHUNK, RET_CHUNK)), per_head((RET_CHUNK, 1)),
                  per_head((RET_CHUNK, 1)), per_head((1, 1))],
        out_specs=pl.BlockSpec((1, t, HEAD_W), lambda bi, hi: (bi, 0, hi)),
        compiler_params=pltpu.CompilerParams(dimension_semantics=("parallel", "parallel"),
                                             vmem_limit_bytes=VMEM_LIMIT),
        name="retention",
    )(proj3, proj3, proj3, proj3, intra, cross, kdec, cdec)


def _diff_kernel(slope_ref, lam_ref, rel_ref, q_ref, k_ref, v_ref, gsub_ref, o_ref, *, tq, tk, out_scale):
    hi = pl.program_id(1)
    qi = pl.program_id(2)
    half = HEAD_W // 2
    slope = slope_ref[hi]

    lp = lam_ref[...]
    lam = (jnp.exp(jnp.sum(lp[0:1] * lp[1:2], axis=1, keepdims=True))
           - jnp.exp(jnp.sum(lp[2:3] * lp[3:4], axis=1, keepdims=True))
           + (1.0 - out_scale))

    q = q_ref[0] * jnp.asarray(half ** -0.5, BF16)
    lane = lax.broadcasted_iota(jnp.int32, q.shape, 1)
    zero = jnp.zeros_like(q)
    qs = jnp.concatenate([jnp.where(lane < half, q, zero),
                          jnp.where(lane >= half, q, zero)], axis=0)

    rel = rel_ref[...]
    rel_bias = -slope * rel

    def body(j, carry):
        m, l, acc = carry
        ksl = pl.ds(pl.multiple_of(j * tk, tk), tk)
        k = k_ref[0, ksl, :]
        v = v_ref[0, ksl, :]
        off = (qi * tq - j * tk).astype(F32)
        s = lax.dot_general(qs, k, _NT, preferred_element_type=F32)
        s = s + jnp.where(rel >= -off, rel_bias - slope * off, NEG_INF)
        m_new = jnp.maximum(m, jnp.max(s, axis=1, keepdims=True))
        alpha = jnp.exp(m - m_new)
        p = jnp.exp(s - m_new)
        l = alpha * l + jnp.sum(p, axis=1, keepdims=True)
        acc = alpha * acc + jnp.dot(p.astype(BF16), v, preferred_element_type=F32)
        return m_new, l, acc

    n_kv = (qi * tq + tq - 1) // tk + 1
    m0 = jnp.full((2 * tq, 1), NEG_INF, F32)
    l0 = jnp.zeros((2 * tq, 1), F32)
    a0 = jnp.zeros((2 * tq, HEAD_W), F32)
    _, l, acc = lax.fori_loop(0, n_kv, body, (m0, l0, a0))
    o = acc / l
    dy = o[:tq] - lam * o[tq:]
    o_ref[0] = (_rms(dy) * gsub_ref[...] * out_scale).astype(o_ref.dtype)


def _diff_attention(proj3, slopes, diff_lambda_l, gsub, out_scale, tq=256, tk=512):
    b, t, _ = proj3.shape
    h = DIFF_HEADS
    base = 4 * RET_HEADS
    kern = functools.partial(_diff_kernel, tq=tq, tk=tk, out_scale=out_scale)
    rel = jnp.asarray((np.arange(2 * tq) % tq)[:, None] - np.arange(tk)[None, :], F32)
    return pl.pallas_call(
        kern,
        out_shape=jax.ShapeDtypeStruct((b, t, h * HEAD_W), BF16),
        grid=(b, h, t // tq),
        in_specs=[pl.BlockSpec(memory_space=pltpu.SMEM),
                  pl.BlockSpec(diff_lambda_l.shape, lambda bi, hi, qi: (0, 0)),
                  pl.BlockSpec((2 * tq, tk), lambda bi, hi, qi: (0, 0)),
                  pl.BlockSpec((1, tq, HEAD_W), lambda bi, hi, qi: (bi, qi, base + hi)),
                  pl.BlockSpec((1, t, HEAD_W), lambda bi, hi, qi: (bi, 0, base + h + hi)),
                  pl.BlockSpec((1, t, HEAD_W), lambda bi, hi, qi: (bi, 0, base + 2 * h + hi)),
                  pl.BlockSpec((1, HEAD_W), lambda bi, hi, qi: (0, 0))],
        out_specs=pl.BlockSpec((1, tq, HEAD_W), lambda bi, hi, qi: (bi, qi, hi)),
        compiler_params=pltpu.CompilerParams(
            dimension_semantics=("parallel", "parallel", "arbitrary"),
            vmem_limit_bytes=VMEM_LIMIT),
        name="diff_attention",
    )(slopes, diff_lambda_l, rel, proj3, proj3, proj3, gsub)


def _outproj_kernel(x_ref, ret_ref, dif_ref, wo_ref, g_ref, wq_ref, x1_ref, xn_ref, xnf_ref, q_ref, *, col_chunk):
    half = ret_ref.shape[1]
    mix = jnp.dot(ret_ref[...], wo_ref[:half, :], preferred_element_type=F32)
    mix = mix + jnp.dot(dif_ref[...], wo_ref[half:, :], preferred_element_type=F32)
    x1 = x_ref[...] + mix
    x1_ref[...] = x1
    xn = (_rms(x1) * g_ref[...]).astype(BF16)
    xn_ref[...] = xn
    xnf_ref[...] = xn.astype(F32)
    for j in range(q_ref.shape[1] // col_chunk):
        sl = slice(j * col_chunk, (j + 1) * col_chunk)
        q_ref[:, sl] = jnp.dot(xn, wq_ref[:, sl], preferred_element_type=F32).astype(q_ref.dtype)


def _out_proj(x2, ret2, dif2, wo_bf16, g, wq_bf16, row0, tm=512):
    n, half = ret2.shape
    d = x2.shape[1]
    qc = wq_bf16.shape[1]
    assert row0 % tm == 0 and n % tm == 0
    blk0 = row0 // tm
    row = lambda w: pl.BlockSpec((tm, w), lambda i: (i, 0))
    full = lambda a: pl.BlockSpec(a.shape, lambda i: (0, 0))
    return pl.pallas_call(
        functools.partial(_outproj_kernel, col_chunk=512),
        out_shape=(jax.ShapeDtypeStruct((n, d), F32),
                   jax.ShapeDtypeStruct((n, d), BF16),
                   jax.ShapeDtypeStruct((n, d), F32),
                   jax.ShapeDtypeStruct((n, qc), BF16)),
        grid=(n // tm,),
        in_specs=[pl.BlockSpec((tm, d), lambda i: (i + blk0, 0)), row(half), row(half),
                  full(wo_bf16), full(g), full(wq_bf16)],
        out_specs=(row(d), row(d), row(d), row(qc)),
        compiler_params=pltpu.CompilerParams(dimension_semantics=("parallel",),
                                             vmem_limit_bytes=VMEM_LIMIT),
        name="out_proj",
    )(x2, ret2, dif2, wo_bf16, g, wq_bf16)


def _split3(v):
    trunc = lambda x: pltpu.bitcast(pltpu.bitcast(x, jnp.uint32) & jnp.uint32(0xFFFF0000), F32)
    hi = trunc(v)
    r1 = v - hi
    mid = trunc(r1)
    return hi.astype(BF16), mid.astype(BF16), (r1 - mid).astype(BF16)


def _spread_exact(v, sel_bf16):
    hi, mid, lo = _split3(v)
    d = lambda a: jnp.dot(a, sel_bf16, preferred_element_type=F32)
    return (d(hi) + d(mid)) + d(lo)


def _spread_exact_left(sel_bf16, v):
    hi, mid, lo = _split3(v)
    d = lambda a: jnp.dot(sel_bf16, a, preferred_element_type=F32)
    return (d(hi) + d(mid)) + d(lo)


def _topk_kernel(q_ref, sk_ref, sel_ref, pad_ref, e_ref, g_ref, g2_ref):
    tb = q_ref.shape[0]
    k = PEER_TOPK
    n_sets = 2 * PEER_HEADS
    neg = -jnp.inf

    s = jnp.stack(
        [lax.dot_general(sk_ref[j // 2, j % 2], q_ref[:, j * N_KEYS:(j + 1) * N_KEYS], _NT,
                         preferred_element_type=F32) for j in range(n_sets)], axis=0)
    key_f = lax.broadcasted_iota(jnp.int32, s.shape, 1).astype(F32)
    rank = lax.broadcasted_iota(jnp.int32, (n_sets, k, tb), 1)
    top_v = jnp.zeros((n_sets, k, tb), F32)
    top_i = jnp.zeros((n_sets, k, tb), F32)
    for r in range(k):
        m = jnp.max(s, axis=1, keepdims=True)
        i = jnp.min(jnp.where(s == m, key_f, float(N_KEYS)), axis=1, keepdims=True)
        s = jnp.where(key_f == i, neg, s)
        top_v = jnp.where(rank == r, m, top_v)
        top_i = jnp.where(rank == r, i, top_i)

    pad = pad_ref[...]
    cands, eids = [], []
    for h in range(PEER_HEADS):
        cands.append(_spread_exact_left(sel_ref[0], top_v[2 * h]) + _spread_exact_left(sel_ref[1], top_v[2 * h + 1])
                     + pad)
        eids.append(jnp.dot(sel_ref[0], top_i[2 * h].astype(BF16), preferred_element_type=F32) * float(N_KEYS)
                    + jnp.dot(sel_ref[1], top_i[2 * h + 1].astype(BF16), preferred_element_type=F32))
    cand = jnp.stack(cands, axis=0)
    eid = jnp.stack(eids, axis=0)
    n_cand = cand.shape[1]

    c_f = lax.broadcasted_iota(jnp.int32, cand.shape, 1).astype(F32)
    rank2 = lax.broadcasted_iota(jnp.int32, (PEER_HEADS, k, tb), 1)
    top_c = []
    e_rank = jnp.zeros((PEER_HEADS, k, tb), F32)
    for r in range(k):
        m = jnp.max(cand, axis=1, keepdims=True)
        pos = jnp.min(jnp.where(cand == m, c_f, float(n_cand)), axis=1, keepdims=True)
        hit = c_f == pos
        e_rank = jnp.where(rank2 == r, jnp.sum(jnp.where(hit, eid, 0.0), axis=1, keepdims=True), e_rank)
        cand = jnp.where(hit, neg, cand)
        top_c.append(m)
    p_rank = jnp.zeros((PEER_HEADS, k, tb), F32)
    denom = jnp.zeros((PEER_HEADS, 1, tb), F32)
    for r in range(k):
        p = jnp.exp(top_c[r] - top_c[0])
        denom = denom + p
        p_rank = jnp.where(rank2 == r, p, p_rank)
    g_rank = p_rank / denom

    e_out = e_rank.reshape(PEER_HEADS * k, tb).T
    g1 = g_rank.reshape(PEER_HEADS * k, tb).T
    sr = lax.broadcasted_iota(jnp.int32, (N_SEL, 2 * N_SEL), 0)
    sc = lax.broadcasted_iota(jnp.int32, (N_SEL, 2 * N_SEL), 1)
    e_ref[...] = e_out.astype(jnp.int32)
    g_ref[...] = g1
    g2_ref[...] = _spread_exact(g1, jnp.where(sc == 2 * sr + 1, 1.0, 0.0).astype(BF16))


def _candidate_tables(tb):
    k = PEER_TOPK
    pairs = [(a, b) for a in range(k) for b in range(k) if (a + 1) * (b + 1) <= k]
    n_cand = -(-len(pairs) // SUBLANES) * SUBLANES
    sel = np.zeros((2, n_cand, k), np.float32)
    pad = np.full((n_cand, tb), -np.inf, np.float32)
    for c, (a, b) in enumerate(pairs):
        sel[0, c, a] = 1.0
        sel[1, c, b] = 1.0
        pad[c, :] = 0.0
    return jnp.asarray(sel, BF16), jnp.asarray(pad)


def _peer_topk(q2, sk_bf16, tb=128):
    n, qc = q2.shape
    sel, pad = _candidate_tables(tb)
    const = lambda a: pl.BlockSpec(a.shape, lambda i: (0,) * a.ndim)
    return pl.pallas_call(
        _topk_kernel,
        out_shape=(jax.ShapeDtypeStruct((n, N_SEL), jnp.int32),
                   jax.ShapeDtypeStruct((n, N_SEL), F32),
                   jax.ShapeDtypeStruct((n, 2 * N_SEL), F32)),
        grid=(n // tb,),
        in_specs=[pl.BlockSpec((tb, qc), lambda i: (i, 0)), const(sk_bf16), const(sel), const(pad)],
        out_specs=(pl.BlockSpec((tb, N_SEL), lambda i: (i, 0)),
                   pl.BlockSpec((tb, N_SEL), lambda i: (i, 0)),
                   pl.BlockSpec((tb, 2 * N_SEL), lambda i: (i, 0))),
        compiler_params=pltpu.CompilerParams(dimension_semantics=("parallel",),
                                             vmem_limit_bytes=VMEM_LIMIT),
        name="peer_topk",
    )(q2, sk_bf16, sel, pad)


def _bf16_bits(t):
    return lax.bitcast_convert_type(t.astype(BF16), jnp.uint16).astype(jnp.uint32)


def _pack_tables(u, v):
    e, d = u.shape
    return (_bf16_bits(u) | (_bf16_bits(v) << 16)).reshape(e, d // HEAD_W, HEAD_W)


def _pack_halves(t):
    b = _bf16_bits(t)
    half = t.shape[1] // 2
    return b[:, :half] | (b[:, half:] << 16)


def _mix_kernel(efirst_ref, enext_ref, g2_ref, xn_ref, x1_ref, gfin_ref, tab_hbm, after_ref, o_ref, buf, sem, *,
                final_norm):
    step = pl.program_id(0)
    tb = xn_ref.shape[0]
    n_chunks = tab_hbm.shape[1]

    def issue(e_ref, t, slot):
        for k in range(N_SEL):
            e = e_ref[t, k]
            pltpu.make_async_copy(tab_hbm.at[e], buf.at[slot, pl.ds(k * n_chunks, n_chunks), :],
                                  sem.at[slot]).start(priority=k % 2)

    def wait(slot):
        pltpu.make_async_copy(buf.at[slot], buf.at[slot], sem.at[slot]).wait()

    @pl.when(step == 0)
    def _():
        issue(efirst_ref, 0, 0)

    rid = lax.broadcasted_iota(jnp.int32, (SUBLANES, 2 * N_SEL), 0)

    def group(gi, c):
        base = pl.multiple_of(gi * SUBLANES, SUBLANES)
        xrows = xn_ref[pl.ds(base, SUBLANES), :]
        grows = g2_ref[pl.ds(base, SUBLANES), :]
        acc = x1_ref[pl.ds(base, SUBLANES), :]
        for j in range(SUBLANES):
            slot = j % 2
            issue(enext_ref, base + j, 1 - slot)
            wait(slot)
            uv = jnp.concatenate(
                [pltpu.bitcast(buf[slot, pl.ds(c, N_SEL, stride=n_chunks), :], BF16) for c in range(n_chunks)],
                axis=1)
            s2 = lax.dot_general(xrows, uv, _NT, preferred_element_type=F32)
            act = pltpu.roll(_gelu_exact(s2), 1, axis=1)
            w2 = jnp.where(rid == j, grows * act, 0.0)
            acc = acc + jnp.dot(w2.astype(BF16), uv, preferred_element_type=F32)
        o_ref[pl.ds(base, SUBLANES), :] = acc
        return c

    lax.fori_loop(0, tb // SUBLANES, group, 0)

    @pl.when(step == pl.num_programs(0) - 1)
    def _():
        wait(0)

    if final_norm:
        o_ref[...] = _rms(o_ref[...]) * gfin_ref[...]


def _peer_mix(eids, e_next, gates2, xn, x1, gfin, table, after, row0, rows, final_norm, tb=64):
    d = x1.shape[1]
    assert tb % (2 * SUBLANES) == 0 and row0 % tb == 0 and rows % tb == 0
    blk0 = row0 // tb
    row = lambda w: pl.BlockSpec((tb, w), lambda i: (i + blk0, 0))
    return pl.pallas_call(
        functools.partial(_mix_kernel, final_norm=final_norm),
        out_shape=jax.ShapeDtypeStruct((rows, d), F32),
        grid=(rows // tb,),
        in_specs=[pl.BlockSpec((SUBLANES, N_SEL), lambda i: (row0 // SUBLANES, 0), memory_space=pltpu.SMEM),
                  pl.BlockSpec((tb, N_SEL), lambda i: (i + blk0, 0), memory_space=pltpu.SMEM),
                  row(2 * N_SEL), row(d), row(d),
                  pl.BlockSpec((1, d), lambda i: (0, 0)),
                  pl.BlockSpec(memory_space=pl.ANY),
                  pl.BlockSpec(memory_space=pl.ANY)],
        out_specs=pl.BlockSpec((tb, d), lambda i: (i, 0)),
        scratch_shapes=[pltpu.VMEM((2, N_SEL * (d // HEAD_W), HEAD_W), jnp.uint32),
                        pltpu.SemaphoreType.DMA((2,))],
        compiler_params=pltpu.CompilerParams(dimension_semantics=("arbitrary",),
                                             vmem_limit_bytes=VMEM_LIMIT),
        name="peer_mix",
    )(eids, e_next, gates2, xn, x1, gfin, table, after)


SC_LANES = 16
SC_TILES = 32
SC_TBLK = 8
SC_SLOTS = 4
HI_MASK = 0xFFFF0000
_SC_PARAMS = pltpu.CompilerParams(needs_layout_passes=False)


def _sc_unpack(word):
    return plsc.bitcast(word << 16, F32), plsc.bitcast(word & jnp.uint32(HI_MASK), F32)


def _sc_pipelined_heads(tab_hbm, ev, rows, sem, compute):
    n_items = SC_TBLK * PEER_HEADS
    ahead = SC_SLOTS - 1
    assert n_items % SC_SLOTS == 0

    def gather(it, slot):
        ti = it // PEER_HEADS
        h = it % PEER_HEADS
        return pltpu.make_async_copy(tab_hbm.at[ev.at[ti, pl.ds(h * PEER_TOPK, PEER_TOPK)]], rows.at[slot],
                                     sem.at[slot])

    for a in range(ahead):
        gather(a, a).start()

    def group(jj, carry):
        for slot in range(SC_SLOTS):
            it = SC_SLOTS * jj + slot
            gather(jnp.minimum(it + ahead, n_items - 1), (slot + ahead) % SC_SLOTS).start()
            gather(it, slot).wait()
            compute(it, slot)
        return carry

    lax.fori_loop(0, n_items // SC_SLOTS, group, 0)
    for a in range(ahead):
        gather(n_items - 1, (n_items + a) % SC_SLOTS).wait()


def _sc_scores(x, eids, utab, row0, rows):
    d = x.shape[1]
    dw = d // 2
    nw = dw // SC_LANES
    per_tile = rows // SC_TILES
    assert rows % (SC_TILES * SC_TBLK) == 0 and row0 % SC_TBLK == 0

    def body(x_hbm, e_hbm, tab_hbm, s_hbm, xv, ev, rbuf, sv, tmp, sem):
        tile = lax.axis_index("c") * 16 + lax.axis_index("s")
        lane = lax.iota(jnp.int32, SC_LANES)

        def compute(it, slot):
            ti = it // PEER_HEADS
            h = it % PEER_HEADS

            def chunk(w, accs):
                off = pl.multiple_of(w * SC_LANES, SC_LANES)
                xlo = xv[ti, pl.ds(off, SC_LANES)]
                xhi = xv[ti, pl.ds(dw + off, SC_LANES)]
                out = []
                for j in range(PEER_TOPK):
                    lo, hi = _sc_unpack(rbuf[slot, j, pl.ds(off, SC_LANES)])
                    out.append(accs[j] + lo * xlo + hi * xhi)
                return tuple(out)

            accs = lax.fori_loop(0, nw, chunk, tuple(jnp.zeros((SC_LANES,), F32) for _ in range(PEER_TOPK)))
            for j in range(PEER_TOPK):
                tmp[j, :] = accs[j]
            svec = jnp.zeros((SC_LANES,), F32)
            for l in range(SC_LANES):
                svec = svec + plsc.load_gather(tmp, [lane, jnp.full((SC_LANES,), l, jnp.int32)])
            sv[ti, pl.ds(pl.multiple_of(h * PEER_TOPK, PEER_TOPK), PEER_TOPK)] = svec

        def block(b, carry):
            t_out = pl.multiple_of(tile * per_tile + b * SC_TBLK, SC_TBLK)
            pltpu.sync_copy(x_hbm.at[pl.ds(row0 + t_out, SC_TBLK)], xv)
            pltpu.sync_copy(e_hbm.at[pl.ds(row0 + t_out, SC_TBLK)], ev)
            _sc_pipelined_heads(tab_hbm, ev, rbuf, sem, compute)
            pltpu.sync_copy(sv, s_hbm.at[pl.ds(t_out, SC_TBLK)])
            return carry

        lax.fori_loop(0, per_tile // SC_TBLK, block, 0)

    return pl.kernel(
        body,
        out_type=jax.ShapeDtypeStruct((rows, N_SEL), F32),
        mesh=plsc.VectorSubcoreMesh(core_axis_name="c", subcore_axis_name="s"),
        scratch_types=[pltpu.VMEM((SC_TBLK, d), F32), pltpu.VMEM((SC_TBLK, N_SEL), jnp.int32),
                       pltpu.VMEM((SC_SLOTS, PEER_TOPK, dw), jnp.uint32), pltpu.VMEM((SC_TBLK, N_SEL), F32),
                       pltpu.VMEM((PEER_TOPK, SC_LANES), F32), pltpu.SemaphoreType.DMA((SC_SLOTS,))],
        compiler_params=_SC_PARAMS,
        name="sc_scores",
    )(x, eids, utab)


def _sc_combine(w, eids, vtab, row0):
    rows = w.shape[0]
    dw = vtab.shape[1]
    d = 2 * dw
    nw = dw // SC_LANES
    per_tile = rows // SC_TILES
    assert rows % (SC_TILES * SC_TBLK) == 0 and row0 % SC_TBLK == 0

    def body(w_hbm, e_hbm, tab_hbm, y_hbm, wv, ev, rbuf, yv, sem):
        tile = lax.axis_index("c") * 16 + lax.axis_index("s")
        zero_i = jnp.zeros((SC_LANES,), jnp.int32)

        def compute(it, slot):
            ti = it // PEER_HEADS
            h = it % PEER_HEADS
            wk = [plsc.load_gather(wv, [zero_i + ti, zero_i + (h * PEER_TOPK + j)]) for j in range(PEER_TOPK)]

            def chunk(c, carry):
                off = pl.multiple_of(c * SC_LANES, SC_LANES)
                alo = jnp.zeros((SC_LANES,), F32)
                ahi = jnp.zeros((SC_LANES,), F32)
                for j in range(PEER_TOPK):
                    lo, hi = _sc_unpack(rbuf[slot, j, pl.ds(off, SC_LANES)])
                    alo = alo + wk[j] * lo
                    ahi = ahi + wk[j] * hi
                plsc.addupdate(yv.at[ti, pl.ds(off, SC_LANES)], alo)
                plsc.addupdate(yv.at[ti, pl.ds(dw + off, SC_LANES)], ahi)
                return carry

            plsc.parallel_loop(0, nw, 1, carry=jnp.int32(0))(chunk)

        def block(b, carry):
            t_out = pl.multiple_of(tile * per_tile + b * SC_TBLK, SC_TBLK)
            pltpu.sync_copy(w_hbm.at[pl.ds(t_out, SC_TBLK)], wv)
            pltpu.sync_copy(e_hbm.at[pl.ds(row0 + t_out, SC_TBLK)], ev)

            def zero(i, carry):
                for ti in range(SC_TBLK):
                    yv[ti, pl.ds(pl.multiple_of(i * SC_LANES, SC_LANES), SC_LANES)] = jnp.zeros((SC_LANES,), F32)
                return carry

            lax.fori_loop(0, d // SC_LANES, zero, 0)
            _sc_pipelined_heads(tab_hbm, ev, rbuf, sem, compute)
            pltpu.sync_copy(yv, y_hbm.at[pl.ds(t_out, SC_TBLK)])
            return carry

        lax.fori_loop(0, per_tile // SC_TBLK, block, 0)

    return pl.kernel(
        body,
        out_type=jax.ShapeDtypeStruct((rows, d), F32),
        mesh=plsc.VectorSubcoreMesh(core_axis_name="c", subcore_axis_name="s"),
        scratch_types=[pltpu.VMEM((SC_TBLK, N_SEL), F32), pltpu.VMEM((SC_TBLK, N_SEL), jnp.int32),
                       pltpu.VMEM((SC_SLOTS, PEER_TOPK, dw), jnp.uint32), pltpu.VMEM((SC_TBLK, d), F32),
                       pltpu.SemaphoreType.DMA((SC_SLOTS,))],
        compiler_params=_SC_PARAMS,
        name="sc_combine",
    )(w, eids, vtab)


def _gate_kernel(s_ref, g_ref, after_ref, w_ref):
    w_ref[...] = g_ref[...] * _gelu_exact(s_ref[...])


def _gate_act(s, gates, after, row0, tb=512):
    rows = s.shape[0]
    assert rows % tb == 0 and row0 % tb == 0
    blk0 = row0 // tb
    return pl.pallas_call(
        _gate_kernel,
        out_shape=jax.ShapeDtypeStruct((rows, N_SEL), F32),
        grid=(rows // tb,),
        in_specs=[pl.BlockSpec((tb, N_SEL), lambda i: (i, 0)),
                  pl.BlockSpec((tb, N_SEL), lambda i: (i + blk0, 0)),
                  pl.BlockSpec(memory_space=pl.ANY)],
        out_specs=pl.BlockSpec((tb, N_SEL), lambda i: (i, 0)),
        compiler_params=pltpu.CompilerParams(dimension_semantics=("parallel",)),
        name="peer_gate",
    )(s, gates, after)


def _resid_norm_kernel(x1_ref, y_ref, gfin_ref, after_ref, o_ref, *, final_norm):
    z = x1_ref[...] + y_ref[...]
    o_ref[...] = _rms(z) * gfin_ref[...] if final_norm else z


def _resid_norm(x1, y, gfin, after, final_norm, tb=512):
    rows, d = y.shape
    assert rows % tb == 0
    return pl.pallas_call(
        functools.partial(_resid_norm_kernel, final_norm=final_norm),
        out_shape=jax.ShapeDtypeStruct((rows, d), F32),
        grid=(rows // tb,),
        in_specs=[pl.BlockSpec((tb, d), lambda i: (i, 0)),
                  pl.BlockSpec((tb, d), lambda i: (i, 0)),
                  pl.BlockSpec((1, d), lambda i: (0, 0)),
                  pl.BlockSpec(memory_space=pl.ANY)],
        out_specs=pl.BlockSpec((tb, d), lambda i: (i, 0)),
        compiler_params=pltpu.CompilerParams(dimension_semantics=("parallel",)),
        name="peer_resid_norm",
    )(x1, y, gfin, after)


N_CHUNKS = 16
SC_TOKEN_FRAC = 51 / 64
SPLIT_UNIT = 512
COST_FRONT, COST_SC_SCORES, COST_SC_COMBINE, COST_TC_MIX = 0.13, 0.21, 0.18, 0.85


def _sc_rows(n, tokens_per_chunk):
    n_sc = int(n * SC_TOKEN_FRAC) // SPLIT_UNIT * SPLIT_UNIT
    return [min(max(n_sc - c * tokens_per_chunk, 0), tokens_per_chunk) for c in range(N_CHUNKS)]


def _tc_schedule(tokens_per_chunk, sc_rows):
    sc_chunks = [c for c in range(N_CHUNKS) if sc_rows[c]]
    base = [("front", c) for c in range(N_CHUNKS)] + [("mix", c) for c in range(N_CHUNKS)
                                                       if sc_rows[c] < tokens_per_chunk]
    order = []
    tc = sc = 0.0
    scores_done, combine_done = {}, {}
    gates = list(sc_chunks)
    norms = list(sc_chunks)
    for kind, c in base + [("end", None)]:
        while gates and (kind == "end" or scores_done.get(gates[0], float("inf"))
                         + 0.25 * COST_SC_SCORES * sc_rows[gates[0]] <= tc):
            g = gates.pop(0)
            tc = max(tc, scores_done[g])
            sc = max(sc, tc) + COST_SC_COMBINE * sc_rows[g]
            combine_done[g] = sc
            order.append(("gate", g))
        while norms and (kind == "end" or combine_done.get(norms[0], float("inf")) <= tc):
            order.append(("norm", norms.pop(0)))
        if kind == "front":
            tc += COST_FRONT * tokens_per_chunk
            if sc_rows[c]:
                sc = max(sc, tc) + COST_SC_SCORES * sc_rows[c]
                scores_done[c] = sc
        elif kind == "mix":
            tc += COST_TC_MIX * (tokens_per_chunk - sc_rows[c])
        if kind != "end":
            order.append((kind, c))
    return order


def _layer(x2, batch, seq, params, consts, li, final_norm):
    (g_mix, w_in, lam_p, g_sub, w_out, g_ffn, wq, sk, u, v, g_fin) = params
    intra, cross, kdec, cdec, slopes = consts
    n, d = x2.shape
    nc = n // N_CHUNKS
    bc = batch // N_CHUNKS
    assert batch % N_CHUNKS == 0
    table = _pack_tables(u, v)
    utab = _pack_halves(u)
    vtab = _pack_halves(v)

    sc_rows = _sc_rows(n, nc)
    chunks = [None] * N_CHUNKS
    scores = [None] * N_CHUNKS
    ys = [None] * N_CHUNKS
    outs_sc = [None] * N_CHUNKS
    outs_tc = [None] * N_CHUNKS
    last = g_fin

    def front(c, last):
        proj = _in_proj(x2, g_mix, w_in, last, c * nc, nc)
        proj3 = proj.reshape(bc, seq, proj.shape[1])
        ret = _retention(proj3, intra, cross, kdec, cdec)
        dif = _diff_attention(proj3, slopes, lam_p, g_sub, 1.0 - li)
        half = ret.shape[2]
        x1, xn, xn_f32, q = _out_proj(x2, ret.reshape(nc, half), dif.reshape(nc, half), w_out, g_ffn, wq, c * nc)
        eids, gates, gates2 = _peer_topk(q, sk)
        chunks[c] = (x1, xn, eids, gates, gates2)
        if sc_rows[c]:
            scores[c] = _sc_scores(xn_f32, eids, utab, 0, sc_rows[c])
        return gates2

    def gate(c, last):
        _, _, eids, gates, _ = chunks[c]
        w = _gate_act(scores[c], gates, last, 0)
        ys[c] = _sc_combine(w, eids, vtab, 0)
        return w

    def tc_mix(c, last):
        x1, xn, eids, _, gates2 = chunks[c]
        outs_tc[c] = _peer_mix(eids, jnp.roll(eids, -1, axis=0), gates2, xn, x1, g_fin, table, last,
                               sc_rows[c], nc - sc_rows[c], final_norm)
        return outs_tc[c]

    def norm(c, last):
        outs_sc[c] = _resid_norm(chunks[c][0], ys[c], g_fin, last, final_norm)
        return outs_sc[c]

    run = {"front": front, "gate": gate, "mix": tc_mix, "norm": norm}
    for kind, c in _tc_schedule(nc, sc_rows):
        last = run[kind](c, last)
    pieces = [o for c in range(N_CHUNKS) for o in (outs_sc[c], outs_tc[c]) if o is not None]
    return jnp.concatenate(pieces, axis=0)


def _retention_tables(dk):
    c = RET_CHUNK
    h = RET_HEADS
    log_gamma = jnp.log1p(-jnp.exp2(-5.0 - jnp.arange(h, dtype=F32)))
    idx = jnp.arange(c, dtype=F32)
    rel = idx[:, None] - idx[None, :]
    intra = jnp.where(rel >= 0, jnp.exp(log_gamma[:, None, None] * jnp.maximum(rel, 0.0)), 0.0)
    cross = jnp.exp(log_gamma[:, None] * (idx + 1.0))
    kdec = jnp.exp(log_gamma[:, None] * (c - 1.0 - idx))
    cdec = jnp.exp(log_gamma * c)
    scale = dk ** -0.5
    return intra * scale, cross[:, :, None], (kdec * scale)[:, :, None], cdec[:, None, None]


def kernel(x, norm_mix_g, w_in, diff_lambda, diff_subln_g, w_out, norm_ffn_g, peer_wq, peer_subkeys,
           peer_u, peer_v, final_norm_g):
    b, t, d = x.shape
    depth = w_in.shape[0]
    n = b * t
    x2 = x.reshape(n, d)
    consts = _retention_tables(HEAD_W) + (
        jnp.exp2(-8.0 * (jnp.arange(DIFF_HEADS, dtype=F32) + 1.0) / DIFF_HEADS),)
    for l in range(depth):
        li = 0.8 - 0.6 * math.exp(-0.3 * l)
        params = (norm_mix_g[l][None, :], w_in[l].astype(BF16), diff_lambda[l], diff_subln_g[l][None, :],
                  w_out[l].astype(BF16), norm_ffn_g[l][None, :], peer_wq[l].astype(BF16),
                  peer_subkeys[l].astype(BF16), peer_u[l], peer_v[l], final_norm_g[None, :])
        x2 = _layer(x2, b, t, params, consts, li, final_norm=(l == depth - 1))
    return x2.reshape(b, t, d)
```

```python
import functools
import math

import jax
import jax.numpy as jnp
import numpy as np
from jax import lax
from jax.experimental import pallas as pl
from jax.experimental.pallas import tpu as pltpu
from jax.experimental.pallas import tpu_sc as plsc

F32 = jnp.float32
BF16 = jnp.bfloat16

RMS_EPS = 1e-6
NEG_INF = -1e30

RET_HEADS = 4
RET_CHUNK = 128
DIFF_HEADS = 4
HEAD_W = 128
PEER_HEADS = 8
PEER_TOPK = 16
N_KEYS = 128
N_SEL = PEER_HEADS * PEER_TOPK

SUBLANES = 8
VMEM_LIMIT = 56 * 1024 * 1024

_NT = (((1,), (1,)), ((), ()))


def _rms(x, eps=RMS_EPS):
    return x * lax.rsqrt(jnp.mean(x * x, axis=-1, keepdims=True) + eps)


def _gelu_exact(x):
    return 0.5 * x * (1.0 + lax.erf(x * (2.0 ** -0.5)))


def _inproj_kernel(x_ref, g_ref, w_ref, after_ref, o_ref, *, col_chunk):
    h = (_rms(x_ref[...]) * g_ref[...]).astype(BF16)
    for j in range(o_ref.shape[1] // col_chunk):
        sl = slice(j * col_chunk, (j + 1) * col_chunk)
        o_ref[:, sl] = jnp.dot(h, w_ref[:, sl], preferred_element_type=F32).astype(o_ref.dtype)


def _in_proj(x2, g, w_bf16, after, row0, rows, tm=512):
    d = x2.shape[1]
    cols = w_bf16.shape[1]
    assert row0 % tm == 0 and rows % tm == 0
    blk0 = row0 // tm
    return pl.pallas_call(
        functools.partial(_inproj_kernel, col_chunk=512),
        out_shape=jax.ShapeDtypeStruct((rows, cols), BF16),
        grid=(rows // tm,),
        in_specs=[pl.BlockSpec((tm, d), lambda i: (i + blk0, 0)),
                  pl.BlockSpec((1, d), lambda i: (0, 0)),
                  pl.BlockSpec((d, cols), lambda i: (0, 0)),
                  pl.BlockSpec(memory_space=pl.ANY)],
        out_specs=pl.BlockSpec((tm, cols), lambda i: (i, 0)),
        compiler_params=pltpu.CompilerParams(dimension_semantics=("parallel",),
                                             vmem_limit_bytes=VMEM_LIMIT),
        name="in_proj",
    )(x2, g, w_bf16, after)


def _ret_kernel(q_ref, k_ref, v_ref, g_ref, intra_ref, cross_ref, kdec_ref, cdec_ref, o_ref):
    c = RET_CHUNK
    n_chunks = q_ref.shape[1] // c
    intra = intra_ref[0]
    cross = cross_ref[0]
    kdec = kdec_ref[0]
    cdec = cdec_ref[0]

    def body(n, s):
        sl = pl.ds(pl.multiple_of(n * c, c), c)
        qi = q_ref[0, sl, :]
        ki = k_ref[0, sl, :]
        vi = v_ref[0, sl, :]
        a = lax.dot_general(qi, ki, _NT, preferred_element_type=F32) * intra
        o = jnp.dot(a.astype(BF16), vi, preferred_element_type=F32)
        o = o + jnp.dot(qi, s.astype(BF16), preferred_element_type=F32) * cross
        kd_t = (ki.astype(F32) * kdec).T.astype(BF16)
        s = s * cdec + jnp.dot(kd_t, vi, preferred_element_type=F32)
        gate = g_ref[0, sl, :].astype(F32)
        o_ref[0, sl, :] = (gate * jax.nn.sigmoid(gate) * _rms(o)).astype(o_ref.dtype)
        return s

    lax.fori_loop(0, n_chunks, body, jnp.zeros((HEAD_W, HEAD_W), F32), unroll=8)


def _retention(proj3, intra, cross, kdec, cdec):
    b, t, _ = proj3.shape
    h = RET_HEADS

    def col(off):
        return pl.BlockSpec((1, t, HEAD_W), lambda bi, hi: (bi, 0, off + hi))

    def per_head(shape):
        return pl.BlockSpec((1,) + shape, lambda bi, hi: (hi, 0, 0))

    return pl.pallas_call(
        _ret_kernel,
        out_shape=jax.ShapeDtypeStruct((b, t, h * HEAD_W), BF16),
        grid=(b, h),
        in_specs=[col(0), col(h), col(2 * h), col(3 * h),
                  per_head((RET_CHUNK, RET_CHUNK)), per_head((RET_CHUNK, 1)),
                  per_head((RET_CHUNK, 1)), per_head((1, 1))],
        out_specs=pl.BlockSpec((1, t, HEAD_W), lambda bi, hi: (bi, 0, hi)),
        compiler_params=pltpu.CompilerParams(dimension_semantics=("parallel", "parallel"),
                                             vmem_limit_bytes=VMEM_LIMIT),
        name="retention",
    )(proj3, proj3, proj3, proj3, intra, cross, kdec, cdec)


def _diff_kernel(slope_ref, lam_ref, rel_ref, q_ref, k_ref, v_ref, gsub_ref, o_ref, *, tq, tk, out_scale):
    hi = pl.program_id(1)
    qi = pl.program_id(2)
    half = HEAD_W // 2
    slope = slope_ref[hi]

    lp = lam_ref[...]
    lam = (jnp.exp(jnp.sum(lp[0:1] * lp[1:2], axis=1, keepdims=True))
           - jnp.exp(jnp.sum(lp[2:3] * lp[3:4], axis=1, keepdims=True))
           + (1.0 - out_scale))

    q = q_ref[0] * jnp.asarray(half ** -0.5, BF16)
    lane = lax.broadcasted_iota(jnp.int32, q.shape, 1)
    zero = jnp.zeros_like(q)
    qs = jnp.concatenate([jnp.where(lane < half, q, zero),
                          jnp.where(lane >= half, q, zero)], axis=0)

    rel = rel_ref[...]
    rel_bias = -slope * rel

    def body(j, carry):
        m, l, acc = carry
        ksl = pl.ds(pl.multiple_of(j * tk, tk), tk)
        k = k_ref[0, ksl, :]
        v = v_ref[0, ksl, :]
        off = (qi * tq - j * tk).astype(F32)
        s = lax.dot_general(qs, k, _NT, preferred_element_type=F32)
        s = s + jnp.where(rel >= -off, rel_bias - slope * off, NEG_INF)
        m_new = jnp.maximum(m, jnp.max(s, axis=1, keepdims=True))
        alpha = jnp.exp(m - m_new)
        p = jnp.exp(s - m_new)
        l = alpha * l + jnp.sum(p, axis=1, keepdims=True)
        acc = alpha * acc + jnp.dot(p.astype(BF16), v, preferred_element_type=F32)
        return m_new, l, acc

    n_kv = (qi * tq + tq - 1) // tk + 1
    m0 = jnp.full((2 * tq, 1), NEG_INF, F32)
    l0 = jnp.zeros((2 * tq, 1), F32)
    a0 = jnp.zeros((2 * tq, HEAD_W), F32)
    _, l, acc = lax.fori_loop(0, n_kv, body, (m0, l0, a0))
    o = acc / l
    dy = o[:tq] - lam * o[tq:]
    o_ref[0] = (_rms(dy) * gsub_ref[...] * out_scale).astype(o_ref.dtype)


def _diff_attention(proj3, slopes, diff_lambda_l, gsub, out_scale, tq=256, tk=512):
    b, t, _ = proj3.shape
    h = DIFF_HEADS
    base = 4 * RET_HEADS
    kern = functools.partial(_diff_kernel, tq=tq, tk=tk, out_scale=out_scale)
    rel = jnp.asarray((np.arange(2 * tq) % tq)[:, None] - np.arange(tk)[None, :], F32)
    return pl.pallas_call(
        kern,
        out_shape=jax.ShapeDtypeStruct((b, t, h * HEAD_W), BF16),
        grid=(b, h, t // tq),
        in_specs=[pl.BlockSpec(memory_space=pltpu.SMEM),
                  pl.BlockSpec(diff_lambda_l.shape, lambda bi, hi, qi: (0, 0)),
                  pl.BlockSpec((2 * tq, tk), lambda bi, hi, qi: (0, 0)),
                  pl.BlockSpec((1, tq, HEAD_W), lambda bi, hi, qi: (bi, qi, base + hi)),
                  pl.BlockSpec((1, t, HEAD_W), lambda bi, hi, qi: (bi, 0, base + h + hi)),
                  pl.BlockSpec((1, t, HEAD_W), lambda bi, hi, qi: (bi, 0, base + 2 * h + hi)),
                  pl.BlockSpec((1, HEAD_W), lambda bi, hi, qi: (0, 0))],
        out_specs=pl.BlockSpec((1, tq, HEAD_W), lambda bi, hi, qi: (bi, qi, hi)),
        compiler_params=pltpu.CompilerParams(
            dimension_semantics=("parallel", "parallel", "arbitrary"),
            vmem_limit_bytes=VMEM_LIMIT),
        name="diff_attention",
    )(slopes, diff_lambda_l, rel, proj3, proj3, proj3, gsub)


def _outproj_kernel(x_ref, ret_ref, dif_ref, wo_ref, g_ref, wq_ref, x1_ref, xn_ref, xnf_ref, q_ref, *, col_chunk):
    half = ret_ref.shape[1]
    mix = jnp.dot(ret_ref[...], wo_ref[:half, :], preferred_element_type=F32)
    mix = mix + jnp.dot(dif_ref[...], wo_ref[half:, :], preferred_element_type=F32)
    x1 = x_ref[...] + mix
    x1_ref[...] = x1
    xn = (_rms(x1) * g_ref[...]).astype(BF16)
    xn_ref[...] = xn
    xnf_ref[...] = xn.astype(F32)
    for j in range(q_ref.shape[1] // col_chunk):
        sl = slice(j * col_chunk, (j + 1) * col_chunk)
        q_ref[:, sl] = jnp.dot(xn, wq_ref[:, sl], preferred_element_type=F32).astype(q_ref.dtype)


def _out_proj(x2, ret2, dif2, wo_bf16, g, wq_bf16, row0, tm=512):
    n, half = ret2.shape
    d = x2.shape[1]
    qc = wq_bf16.shape[1]
    assert row0 % tm == 0 and n % tm == 0
    blk0 = row0 // tm
    row = lambda w: pl.BlockSpec((tm, w), lambda i: (i, 0))
    full = lambda a: pl.BlockSpec(a.shape, lambda i: (0, 0))
    return pl.pallas_call(
        functools.partial(_outproj_kernel, col_chunk=512),
        out_shape=(jax.ShapeDtypeStruct((n, d), F32),
                   jax.ShapeDtypeStruct((n, d), BF16),
                   jax.ShapeDtypeStruct((n, d), F32),
                   jax.ShapeDtypeStruct((n, qc), BF16)),
        grid=(n // tm,),
        in_specs=[pl.BlockSpec((tm, d), lambda i: (i + blk0, 0)), row(half), row(half),
                  full(wo_bf16), full(g), full(wq_bf16)],
        out_specs=(row(d), row(d), row(d), row(qc)),
        compiler_params=pltpu.CompilerParams(dimension_semantics=("parallel",),
                                             vmem_limit_bytes=VMEM_LIMIT),
        name="out_proj",
    )(x2, ret2, dif2, wo_bf16, g, wq_bf16)


def _split3(v):
    trunc = lambda x: pltpu.bitcast(pltpu.bitcast(x, jnp.uint32) & jnp.uint32(0xFFFF0000), F32)
    hi = trunc(v)
    r1 = v - hi
    mid = trunc(r1)
    return hi.astype(BF16), mid.astype(BF16), (r1 - mid).astype(BF16)


def _spread_exact(v, sel_bf16):
    hi, mid, lo = _split3(v)
    d = lambda a: jnp.dot(a, sel_bf16, preferred_element_type=F32)
    return (d(hi) + d(mid)) + d(lo)


def _spread_exact_left(sel_bf16, v):
    hi, mid, lo = _split3(v)
    d = lambda a: jnp.dot(sel_bf16, a, preferred_element_type=F32)
    return (d(hi) + d(mid)) + d(lo)


def _topk_kernel(q_ref, sk_ref, sel_ref, pad_ref, e_ref, g_ref, g2_ref):
    tb = q_ref.shape[0]
    k = PEER_TOPK
    n_sets = 2 * PEER_HEADS
    neg = -jnp.inf

    s = jnp.stack(
        [lax.dot_general(sk_ref[j // 2, j % 2], q_ref[:, j * N_KEYS:(j + 1) * N_KEYS], _NT,
                         preferred_element_type=F32) for j in range(n_sets)], axis=0)
    key_f = lax.broadcasted_iota(jnp.int32, s.shape, 1).astype(F32)
    rank = lax.broadcasted_iota(jnp.int32, (n_sets, k, tb), 1)
    top_v = jnp.zeros((n_sets, k, tb), F32)
    top_i = jnp.zeros((n_sets, k, tb), F32)
    for r in range(k):
        m = jnp.max(s, axis=1, keepdims=True)
        i = jnp.min(jnp.where(s == m, key_f, float(N_KEYS)), axis=1, keepdims=True)
        s = jnp.where(key_f == i, neg, s)
        top_v = jnp.where(rank == r, m, top_v)
        top_i = jnp.where(rank == r, i, top_i)

    pad = pad_ref[...]
    cands, eids = [], []
    for h in range(PEER_HEADS):
        cands.append(_spread_exact_left(sel_ref[0], top_v[2 * h]) + _spread_exact_left(sel_ref[1], top_v[2 * h + 1])
                     + pad)
        eids.append(jnp.dot(sel_ref[0], top_i[2 * h].astype(BF16), preferred_element_type=F32) * float(N_KEYS)
                    + jnp.dot(sel_ref[1], top_i[2 * h + 1].astype(BF16), preferred_element_type=F32))
    cand = jnp.stack(cands, axis=0)
    eid = jnp.stack(eids, axis=0)
    n_cand = cand.shape[1]

    c_f = lax.broadcasted_iota(jnp.int32, cand.shape, 1).astype(F32)
    rank2 = lax.broadcasted_iota(jnp.int32, (PEER_HEADS, k, tb), 1)
    top_c = []
    e_rank = jnp.zeros((PEER_HEADS, k, tb), F32)
    for r in range(k):
        m = jnp.max(cand, axis=1, keepdims=True)
        pos = jnp.min(jnp.where(cand == m, c_f, float(n_cand)), axis=1, keepdims=True)
        hit = c_f == pos
        e_rank = jnp.where(rank2 == r, jnp.sum(jnp.where(hit, eid, 0.0), axis=1, keepdims=True), e_rank)
        cand = jnp.where(hit, neg, cand)
        top_c.append(m)
    p_rank = jnp.zeros((PEER_HEADS, k, tb), F32)
    denom = jnp.zeros((PEER_HEADS, 1, tb), F32)
    for r in range(k):
        p = jnp.exp(top_c[r] - top_c[0])
        denom = denom + p
        p_rank = jnp.where(rank2 == r, p, p_rank)
    g_rank = p_rank / denom

    e_out = e_rank.reshape(PEER_HEADS * k, tb).T
    g1 = g_rank.reshape(PEER_HEADS * k, tb).T
    sr = lax.broadcasted_iota(jnp.int32, (N_SEL, 2 * N_SEL), 0)
    sc = lax.broadcasted_iota(jnp.int32, (N_SEL, 2 * N_SEL), 1)
    e_ref[...] = e_out.astype(jnp.int32)
    g_ref[...] = g1
    g2_ref[...] = _spread_exact(g1, jnp.where(sc == 2 * sr + 1, 1.0, 0.0).astype(BF16))


def _candidate_tables(tb):
    k = PEER_TOPK
    pairs = [(a, b) for a in range(k) for b in range(k) if (a + 1) * (b + 1) <= k]
    n_cand = -(-len(pairs) // SUBLANES) * SUBLANES
    sel = np.zeros((2, n_cand, k), np.float32)
    pad = np.full((n_cand, tb), -np.inf, np.float32)
    for c, (a, b) in enumerate(pairs):
        sel[0, c, a] = 1.0
        sel[1, c, b] = 1.0
        pad[c, :] = 0.0
    return jnp.asarray(sel, BF16), jnp.asarray(pad)


def _peer_topk(q2, sk_bf16, tb=128):
    n, qc = q2.shape
    sel, pad = _candidate_tables(tb)
    const = lambda a: pl.BlockSpec(a.shape, lambda i: (0,) * a.ndim)
    return pl.pallas_call(
        _topk_kernel,
        out_shape=(jax.ShapeDtypeStruct((n, N_SEL), jnp.int32),
                   jax.ShapeDtypeStruct((n, N_SEL), F32),
                   jax.ShapeDtypeStruct((n, 2 * N_SEL), F32)),
        grid=(n // tb,),
        in_specs=[pl.BlockSpec((tb, qc), lambda i: (i, 0)), const(sk_bf16), const(sel), const(pad)],
        out_specs=(pl.BlockSpec((tb, N_SEL), lambda i: (i, 0)),
                   pl.BlockSpec((tb, N_SEL), lambda i: (i, 0)),
                   pl.BlockSpec((tb, 2 * N_SEL), lambda i: (i, 0))),
        compiler_params=pltpu.CompilerParams(dimension_semantics=("parallel",),
                                             vmem_limit_bytes=VMEM_LIMIT),
        name="peer_topk",
    )(q2, sk_bf16, sel, pad)


def _bf16_bits(t):
    return lax.bitcast_convert_type(t.astype(BF16), jnp.uint16).astype(jnp.uint32)


def _pack_tables(u, v):
    e, d = u.shape
    return (_bf16_bits(u) | (_bf16_bits(v) << 16)).reshape(e, d // HEAD_W, HEAD_W)


def _pack_halves(t):
    b = _bf16_bits(t)
    half = t.shape[1] // 2
    return b[:, :half] | (b[:, half:] << 16)


def _mix_kernel(efirst_ref, enext_ref, g2_ref, xn_ref, x1_ref, gfin_ref, tab_hbm, after_ref, o_ref, buf, sem, *,
                final_norm):
    step = pl.program_id(0)
    tb = xn_ref.shape[0]
    n_chunks = tab_hbm.shape[1]

    def issue(e_ref, t, slot):
        for k in range(N_SEL):
            e = e_ref[t, k]
            pltpu.make_async_copy(tab_hbm.at[e], buf.at[slot, pl.ds(k * n_chunks, n_chunks), :],
                                  sem.at[slot]).start(priority=k % 2)

    def wait(slot):
        pltpu.make_async_copy(buf.at[slot], buf.at[slot], sem.at[slot]).wait()

    @pl.when(step == 0)
    def _():
        issue(efirst_ref, 0, 0)

    rid = lax.broadcasted_iota(jnp.int32, (SUBLANES, 2 * N_SEL), 0)

    def group(gi, c):
        base = pl.multiple_of(gi * SUBLANES, SUBLANES)
        xrows = xn_ref[pl.ds(base, SUBLANES), :]
        grows = g2_ref[pl.ds(base, SUBLANES), :]
        acc = x1_ref[pl.ds(base, SUBLANES), :]
        for j in range(SUBLANES):
            slot = j % 2
            issue(enext_ref, base + j, 1 - slot)
            wait(slot)
            uv = jnp.concatenate(
                [pltpu.bitcast(buf[slot, pl.ds(c, N_SEL, stride=n_chunks), :], BF16) for c in range(n_chunks)],
                axis=1)
            s2 = lax.dot_general(xrows, uv, _NT, preferred_element_type=F32)
            act = pltpu.roll(_gelu_exact(s2), 1, axis=1)
            w2 = jnp.where(rid == j, grows * act, 0.0)
            acc = acc + jnp.dot(w2.astype(BF16), uv, preferred_element_type=F32)
        o_ref[pl.ds(base, SUBLANES), :] = acc
        return c

    lax.fori_loop(0, tb // SUBLANES, group, 0)

    @pl.when(step == pl.num_programs(0) - 1)
    def _():
        wait(0)

    if final_norm:
        o_ref[...] = _rms(o_ref[...]) * gfin_ref[...]


def _peer_mix(eids, e_next, gates2, xn, x1, gfin, table, after, row0, rows, final_norm, tb=64):
    d = x1.shape[1]
    assert tb % (2 * SUBLANES) == 0 and row0 % tb == 0 and rows % tb == 0
    blk0 = row0 // tb
    row = lambda w: pl.BlockSpec((tb, w), lambda i: (i + blk0, 0))
    return pl.pallas_call(
        functools.partial(_mix_kernel, final_norm=final_norm),
        out_shape=jax.ShapeDtypeStruct((rows, d), F32),
        grid=(rows // tb,),
        in_specs=[pl.BlockSpec((SUBLANES, N_SEL), lambda i: (row0 // SUBLANES, 0), memory_space=pltpu.SMEM),
                  pl.BlockSpec((tb, N_SEL), lambda i: (i + blk0, 0), memory_space=pltpu.SMEM),
                  row(2 * N_SEL), row(d), row(d),
                  pl.BlockSpec((1, d), lambda i: (0, 0)),
                  pl.BlockSpec(memory_space=pl.ANY),
                  pl.BlockSpec(memory_space=pl.ANY)],
        out_specs=pl.BlockSpec((tb, d), lambda i: (i, 0)),
        scratch_shapes=[pltpu.VMEM((2, N_SEL * (d // HEAD_W), HEAD_W), jnp.uint32),
                        pltpu.SemaphoreType.DMA((2,))],
        compiler_params=pltpu.CompilerParams(dimension_semantics=("arbitrary",),
                                             vmem_limit_bytes=VMEM_LIMIT),
        name="peer_mix",
    )(eids, e_next, gates2, xn, x1, gfin, table, after)


SC_LANES = 16
SC_TILES = 32
SC_TBLK = 8
SC_SLOTS = 4
HI_MASK = 0xFFFF0000
_SC_PARAMS = pltpu.CompilerParams(needs_layout_passes=False)


def _sc_unpack(word):
    return plsc.bitcast(word << 16, F32), plsc.bitcast(word & jnp.uint32(HI_MASK), F32)


def _sc_pipelined_heads(tab_hbm, ev, rows, sem, compute):
    n_items = SC_TBLK * PEER_HEADS
    ahead = SC_SLOTS - 1
    assert n_items % SC_SLOTS == 0

    def gather(it, slot):
        ti = it // PEER_HEADS
        h = it % PEER_HEADS
        return pltpu.make_async_copy(tab_hbm.at[ev.at[ti, pl.ds(h * PEER_TOPK, PEER_TOPK)]], rows.at[slot],
                                     sem.at[slot])

    for a in range(ahead):
        gather(a, a).start()

    def group(jj, carry):
        for slot in range(SC_SLOTS):
            it = SC_SLOTS * jj + slot
            gather(jnp.minimum(it + ahead, n_items - 1), (slot + ahead) % SC_SLOTS).start()
            gather(it, slot).wait()
            compute(it, slot)
        return carry

    lax.fori_loop(0, n_items // SC_SLOTS, group, 0)
    for a in range(ahead):
        gather(n_items - 1, (n_items + a) % SC_SLOTS).wait()


def _sc_scores(x, eids, utab, row0, rows):
    d = x.shape[1]
    dw = d // 2
    nw = dw // SC_LANES
    per_tile = rows // SC_TILES
    assert rows % (SC_TILES * SC_TBLK) == 0 and row0 % SC_TBLK == 0

    def body(x_hbm, e_hbm, tab_hbm, s_hbm, xv, ev, rbuf, sv, tmp, sem):
        tile = lax.axis_index("c") * 16 + lax.axis_index("s")
        lane = lax.iota(jnp.int32, SC_LANES)

        def compute(it, slot):
            ti = it // PEER_HEADS
            h = it % PEER_HEADS

            def chunk(w, accs):
                off = pl.multiple_of(w * SC_LANES, SC_LANES)
                xlo = xv[ti, pl.ds(off, SC_LANES)]
                xhi = xv[ti, pl.ds(dw + off, SC_LANES)]
                out = []
                for j in range(PEER_TOPK):
                    lo, hi = _sc_unpack(rbuf[slot, j, pl.ds(off, SC_LANES)])
                    out.append(accs[j] + lo * xlo + hi * xhi)
                return tuple(out)

            accs = lax.fori_loop(0, nw, chunk, tuple(jnp.zeros((SC_LANES,), F32) for _ in range(PEER_TOPK)))
            for j in range(PEER_TOPK):
                tmp[j, :] = accs[j]
            svec = jnp.zeros((SC_LANES,), F32)
            for l in range(SC_LANES):
                svec = svec + plsc.load_gather(tmp, [lane, jnp.full((SC_LANES,), l, jnp.int32)])
            sv[ti, pl.ds(pl.multiple_of(h * PEER_TOPK, PEER_TOPK), PEER_TOPK)] = svec

        def block(b, carry):
            t_out = pl.multiple_of(tile * per_tile + b * SC_TBLK, SC_TBLK)
            pltpu.sync_copy(x_hbm.at[pl.ds(row0 + t_out, SC_TBLK)], xv)
            pltpu.sync_copy(e_hbm.at[pl.ds(row0 + t_out, SC_TBLK)], ev)
            _sc_pipelined_heads(tab_hbm, ev, rbuf, sem, compute)
            pltpu.sync_copy(sv, s_hbm.at[pl.ds(t_out, SC_TBLK)])
            return carry

        lax.fori_loop(0, per_tile // SC_TBLK, block, 0)

    return pl.kernel(
        body,
        out_type=jax.ShapeDtypeStruct((rows, N_SEL), F32),
        mesh=plsc.VectorSubcoreMesh(core_axis_name="c", subcore_axis_name="s"),
        scratch_types=[pltpu.VMEM((SC_TBLK, d), F32), pltpu.VMEM((SC_TBLK, N_SEL), jnp.int32),
                       pltpu.VMEM((SC_SLOTS, PEER_TOPK, dw), jnp.uint32), pltpu.VMEM((SC_TBLK, N_SEL), F32),
                       pltpu.VMEM((PEER_TOPK, SC_LANES), F32), pltpu.SemaphoreType.DMA((SC_SLOTS,))],
        compiler_params=_SC_PARAMS,
        name="sc_scores",
    )(x, eids, utab)


def _sc_combine(w, eids, vtab, row0):
    rows = w.shape[0]
    dw = vtab.shape[1]
    d = 2 * dw
    nw = dw // SC_LANES
    per_tile = rows // SC_TILES
    assert rows % (SC_TILES * SC_TBLK) == 0 and row0 % SC_TBLK == 0

    def body(w_hbm, e_hbm, tab_hbm, y_hbm, wv, ev, rbuf, yv, sem):
        tile = lax.axis_index("c") * 16 + lax.axis_index("s")
        zero_i = jnp.zeros((SC_LANES,), jnp.int32)

        def compute(it, slot):
            ti = it // PEER_HEADS
            h = it % PEER_HEADS
            wk = [plsc.load_gather(wv, [zero_i + ti, zero_i + (h * PEER_TOPK + j)]) for j in range(PEER_TOPK)]

            def chunk(c, carry):
                off = pl.multiple_of(c * SC_LANES, SC_LANES)
                alo = jnp.zeros((SC_LANES,), F32)
                ahi = jnp.zeros((SC_LANES,), F32)
                for j in range(PEER_TOPK):
                    lo, hi = _sc_unpack(rbuf[slot, j, pl.ds(off, SC_LANES)])
                    alo = alo + wk[j] * lo
                    ahi = ahi + wk[j] * hi
                plsc.addupdate(yv.at[ti, pl.ds(off, SC_LANES)], alo)
                plsc.addupdate(yv.at[ti, pl.ds(dw + off, SC_LANES)], ahi)
                return carry

            plsc.parallel_loop(0, nw, 1, carry=jnp.int32(0))(chunk)

        def block(b, carry):
            t_out = pl.multiple_of(tile * per_tile + b * SC_TBLK, SC_TBLK)
            pltpu.sync_copy(w_hbm.at[pl.ds(t_out, SC_TBLK)], wv)
            pltpu.sync_copy(e_hbm.at[pl.ds(row0 + t_out, SC_TBLK)], ev)

            def zero(i, carry):
                for ti in range(SC_TBLK):
                    yv[ti, pl.ds(pl.multiple_of(i * SC_LANES, SC_LANES), SC_LANES)] = jnp.zeros((SC_LANES,), F32)
                return carry

            lax.fori_loop(0, d // SC_LANES, zero, 0)
            _sc_pipelined_heads(tab_hbm, ev, rbuf, sem, compute)
            pltpu.sync_copy(yv, y_hbm.at[pl.ds(t_out, SC_TBLK)])
            return carry

        lax.fori_loop(0, per_tile // SC_TBLK, block, 0)

    return pl.kernel(
        body,
        out_type=jax.ShapeDtypeStruct((rows, d), F32),
        mesh=plsc.VectorSubcoreMesh(core_axis_name="c", subcore_axis_name="s"),
        scratch_types=[pltpu.VMEM((SC_TBLK, N_SEL), F32), pltpu.VMEM((SC_TBLK, N_SEL), jnp.int32),
                       pltpu.VMEM((SC_SLOTS, PEER_TOPK, dw), jnp.uint32), pltpu.VMEM((SC_TBLK, d), F32),
                       pltpu.SemaphoreType.DMA((SC_SLOTS,))],
        compiler_params=_SC_PARAMS,
        name="sc_combine",
    )(w, eids, vtab)


def _gate_kernel(s_ref, g_ref, after_ref, w_ref):
    w_ref[...] = g_ref[...] * _gelu_exact(s_ref[...])


def _gate_act(s, gates, after, row0, tb=512):
    rows = s.shape[0]
    assert rows % tb == 0 and row0 % tb == 0
    blk0 = row0 // tb
    return pl.pallas_call(
        _gate_kernel,
        out_shape=jax.ShapeDtypeStruct((rows, N_SEL), F32),
        grid=(rows // tb,),
        in_specs=[pl.BlockSpec((tb, N_SEL), lambda i: (i, 0)),
                  pl.BlockSpec((tb, N_SEL), lambda i: (i + blk0, 0)),
                  pl.BlockSpec(memory_space=pl.ANY)],
        out_specs=pl.BlockSpec((tb, N_SEL), lambda i: (i, 0)),
        compiler_params=pltpu.CompilerParams(dimension_semantics=("parallel",)),
        name="peer_gate",
    )(s, gates, after)


def _resid_norm_kernel(x1_ref, y_ref, gfin_ref, after_ref, o_ref, *, final_norm):
    z = x1_ref[...] + y_ref[...]
    o_ref[...] = _rms(z) * gfin_ref[...] if final_norm else z


def _resid_norm(x1, y, gfin, after, final_norm, tb=512):
    rows, d = y.shape
    assert rows % tb == 0
    return pl.pallas_call(
        functools.partial(_resid_norm_kernel, final_norm=final_norm),
        out_shape=jax.ShapeDtypeStruct((rows, d), F32),
        grid=(rows // tb,),
        in_specs=[pl.BlockSpec((tb, d), lambda i: (i, 0)),
                  pl.BlockSpec((tb, d), lambda i: (i, 0)),
                  pl.BlockSpec((1, d), lambda i: (0, 0)),
                  pl.BlockSpec(memory_space=pl.ANY)],
        out_specs=pl.BlockSpec((tb, d), lambda i: (i, 0)),
        compiler_params=pltpu.CompilerParams(dimension_semantics=("parallel",)),
        name="peer_resid_norm",
    )(x1, y, gfin, after)


N_CHUNKS = 16
SC_TOKEN_FRAC = 51 / 64
SPLIT_UNIT = 512
COST_FRONT, COST_SC_SCORES, COST_SC_COMBINE, COST_TC_MIX = 0.13, 0.21, 0.18, 0.85


def _sc_rows(n, tokens_per_chunk):
    n_sc = int(n * SC_TOKEN_FRAC) // SPLIT_UNIT * SPLIT_UNIT
    return [min(max(n_sc - c * tokens_per_chunk, 0), tokens_per_chunk) for c in range(N_CHUNKS)]


def _tc_schedule(tokens_per_chunk, sc_rows):
    sc_chunks = [c for c in range(N_CHUNKS) if sc_rows[c]]
    base = [("front", c) for c in range(N_CHUNKS)] + [("mix", c) for c in range(N_CHUNKS)
                                                       if sc_rows[c] < tokens_per_chunk]
    order = []
    tc = sc = 0.0
    scores_done, combine_done = {}, {}
    gates = list(sc_chunks)
    norms = list(sc_chunks)
    for kind, c in base + [("end", None)]:
        while gates and (kind == "end" or scores_done.get(gates[0], float("inf"))
                         + 0.25 * COST_SC_SCORES * sc_rows[gates[0]] <= tc):
            g = gates.pop(0)
            tc = max(tc, scores_done[g])
            sc = max(sc, tc) + COST_SC_COMBINE * sc_rows[g]
            combine_done[g] = sc
            order.append(("gate", g))
        while norms and (kind == "end" or combine_done.get(norms[0], float("inf")) <= tc):
            order.append(("norm", norms.pop(0)))
        if kind == "front":
            tc += COST_FRONT * tokens_per_chunk
            if sc_rows[c]:
                sc = max(sc, tc) + COST_SC_SCORES * sc_rows[c]
                scores_done[c] = sc
        elif kind == "mix":
            tc += COST_TC_MIX * (tokens_per_chunk - sc_rows[c])
        if kind != "end":
            order.append((kind, c))
    return order


def _layer(x2, batch, seq, params, consts, li, final_norm):
    (g_mix, w_in, lam_p, g_sub, w_out, g_ffn, wq, sk, u, v, g_fin) = params
    intra, cross, kdec, cdec, slopes = consts
    n, d = x2.shape
    nc = n // N_CHUNKS
    bc = batch // N_CHUNKS
    assert batch % N_CHUNKS == 0
    table = _pack_tables(u, v)
    utab = _pack_halves(u)
    vtab = _pack_halves(v)

    sc_rows = _sc_rows(n, nc)
    chunks = [None] * N_CHUNKS
    scores = [None] * N_CHUNKS
    ys = [None] * N_CHUNKS
    outs_sc = [None] * N_CHUNKS
    outs_tc = [None] * N_CHUNKS
    last = g_fin

    def front(c, last):
        proj = _in_proj(x2, g_mix, w_in, last, c * nc, nc)
        proj3 = proj.reshape(bc, seq, proj.shape[1])
        ret = _retention(proj3, intra, cross, kdec, cdec)
        dif = _diff_attention(proj3, slopes, lam_p, g_sub, 1.0 - li)
        half = ret.shape[2]
        x1, xn, xn_f32, q = _out_proj(x2, ret.reshape(nc, half), dif.reshape(nc, half), w_out, g_ffn, wq, c * nc)
        eids, gates, gates2 = _peer_topk(q, sk)
        chunks[c] = (x1, xn, eids, gates, gates2)
        if sc_rows[c]:
            scores[c] = _sc_scores(xn_f32, eids, utab, 0, sc_rows[c])
        return gates2

    def gate(c, last):
        _, _, eids, gates, _ = chunks[c]
        w = _gate_act(scores[c], gates, last, 0)
        ys[c] = _sc_combine(w, eids, vtab, 0)
        return w

    def tc_mix(c, last):
        x1, xn, eids, _, gates2 = chunks[c]
        outs_tc[c] = _peer_mix(eids, jnp.roll(eids, -1, axis=0), gates2, xn, x1, g_fin, table, last,
                               sc_rows[c], nc - sc_rows[c], final_norm)
        return outs_tc[c]

    def norm(c, last):
        outs_sc[c] = _resid_norm(chunks[c][0], ys[c], g_fin, last, final_norm)
        return outs_sc[c]

    run = {"front": front, "gate": gate, "mix": tc_mix, "norm": norm}
    for kind, c in _tc_schedule(nc, sc_rows):
        last = run[kind](c, last)
    pieces = [o for c in range(N_CHUNKS) for o in (outs_sc[c], outs_tc[c]) if o is not None]
    return jnp.concatenate(pieces, axis=0)


def _retention_tables(dk):
    c = RET_CHUNK
    h = RET_HEADS
    log_gamma = jnp.log1p(-jnp.exp2(-5.0 - jnp.arange(h, dtype=F32)))
    idx = jnp.arange(c, dtype=F32)
    rel = idx[:, None] - idx[None, :]
    intra = jnp.where(rel >= 0, jnp.exp(log_gamma[:, None, None] * jnp.maximum(rel, 0.0)), 0.0)
    cross = jnp.exp(log_gamma[:, None] * (idx + 1.0))
    kdec = jnp.exp(log_gamma[:, None] * (c - 1.0 - idx))
    cdec = jnp.exp(log_gamma * c)
    scale = dk ** -0.5
    return intra * scale, cross[:, :, None], (kdec * scale)[:, :, None], cdec[:, None, None]


def kernel(x, norm_mix_g, w_in, diff_lambda, diff_subln_g, w_out, norm_ffn_g, peer_wq, peer_subkeys,
           peer_u, peer_v, final_norm_g):
    b, t, d = x.shape
    depth = w_in.shape[0]
    n = b * t
    x2 = x.reshape(n, d)
    consts = _retention_tables(HEAD_W) + (
        jnp.exp2(-8.0 * (jnp.arange(DIFF_HEADS, dtype=F32) + 1.0) / DIFF_HEADS),)
    for l in range(depth):
        li = 0.8 - 0.6 * math.exp(-0.3 * l)
        params = (norm_mix_g[l][None, :], w_in[l].astype(BF16), diff_lambda[l], diff_subln_g[l][None, :],
                  w_out[l].astype(BF16), norm_ffn_g[l][None, :], peer_wq[l].astype(BF16),
                  peer_subkeys[l].astype(BF16), peer_u[l], peer_v[l], final_norm_g[None, :])
        x2 = _layer(x2, b, t, params, consts, li, final_norm=(l == depth - 1))
    return x2.reshape(b, t, d)
```

```python
import functools
import math

import jax
import jax.numpy as jnp
import numpy as np
from jax import lax
from jax.experimental import pallas as pl
from jax.experimental.pallas import tpu as pltpu
from jax.experimental.pallas import tpu_sc as plsc

F32 = jnp.float32
BF16 = jnp.bfloat16

RMS_EPS = 1e-6
NEG_INF = -1e30

RET_HEADS = 4
RET_CHUNK = 128
DIFF_HEADS = 4
HEAD_W = 128
PEER_HEADS = 8
PEER_TOPK = 16
N_KEYS = 128
N_SEL = PEER_HEADS * PEER_TOPK

SUBLANES = 8
VMEM_LIMIT = 56 * 1024 * 1024

_NT = (((1,), (1,)), ((), ()))


def _rms(x, eps=RMS_EPS):
    return x * lax.rsqrt(jnp.mean(x * x, axis=-1, keepdims=True) + eps)


def _gelu_exact(x):
    return 0.5 * x * (1.0 + lax.erf(x * (2.0 ** -0.5)))


def _inproj_kernel(x_ref, g_ref, w_ref, after_ref, o_ref, *, col_chunk):
    h = (_rms(x_ref[...]) * g_ref[...]).astype(BF16)
    for j in range(o_ref.shape[1] // col_chunk):
        sl = slice(j * col_chunk, (j + 1) * col_chunk)
        o_ref[:, sl] = jnp.dot(h, w_ref[:, sl], preferred_element_type=F32).astype(o_ref.dtype)


def _in_proj(x2, g, w_bf16, after, row0, rows, tm=512):
    d = x2.shape[1]
    cols = w_bf16.shape[1]
    assert row0 % tm == 0 and rows % tm == 0
    blk0 = row0 // tm
    return pl.pallas_call(
        functools.partial(_inproj_kernel, col_chunk=512),
        out_shape=jax.ShapeDtypeStruct((rows, cols), BF16),
        grid=(rows // tm,),
        in_specs=[pl.BlockSpec((tm, d), lambda i: (i + blk0, 0)),
                  pl.BlockSpec((1, d), lambda i: (0, 0)),
                  pl.BlockSpec((d, cols), lambda i: (0, 0)),
                  pl.BlockSpec(memory_space=pl.ANY)],
        out_specs=pl.BlockSpec((tm, cols), lambda i: (i, 0)),
        compiler_params=pltpu.CompilerParams(dimension_semantics=("parallel",),
                                             vmem_limit_bytes=VMEM_LIMIT),
        name="in_proj",
    )(x2, g, w_bf16, after)


def _ret_kernel(q_ref, k_ref, v_ref, g_ref, intra_ref, cross_ref, kdec_ref, cdec_ref, o_ref):
    c = RET_CHUNK
    n_chunks = q_ref.shape[1] // c
    intra = intra_ref[0]
    cross = cross_ref[0]
    kdec = kdec_ref[0]
    cdec = cdec_ref[0]

    def body(n, s):
        sl = pl.ds(pl.multiple_of(n * c, c), c)
        qi = q_ref[0, sl, :]
        ki = k_ref[0, sl, :]
        vi = v_ref[0, sl, :]
        a = lax.dot_general(qi, ki, _NT, preferred_element_type=F32) * intra
        o = jnp.dot(a.astype(BF16), vi, preferred_element_type=F32)
        o = o + jnp.dot(qi, s.astype(BF16), preferred_element_type=F32) * cross
        kd_t = (ki.astype(F32) * kdec).T.astype(BF16)
        s = s * cdec + jnp.dot(kd_t, vi, preferred_element_type=F32)
        gate = g_ref[0, sl, :].astype(F32)
        o_ref[0, sl, :] = (gate * jax.nn.sigmoid(gate) * _rms(o)).astype(o_ref.dtype)
        return s

    lax.fori_loop(0, n_chunks, body, jnp.zeros((HEAD_W, HEAD_W), F32), unroll=8)


def _retention(proj3, intra, cross, kdec, cdec):
    b, t, _ = proj3.shape
    h = RET_HEADS

    def col(off):
        return pl.BlockSpec((1, t, HEAD_W), lambda bi, hi: (bi, 0, off + hi))

    def per_head(shape):
        return pl.BlockSpec((1,) + shape, lambda bi, hi: (hi, 0, 0))

    return pl.pallas_call(
        _ret_kernel,
        out_shape=jax.ShapeDtypeStruct((b, t, h * HEAD_W), BF16),
        grid=(b, h),
        in_specs=[col(0), col(h), col(2 * h), col(3 * h),
                  per_head((RET_CHUNK, RET_CHUNK)), per_head((RET_CHUNK, 1)),
                  per_head((RET_CHUNK, 1)), per_head((1, 1))],
        out_specs=pl.BlockSpec((1, t, HEAD_W), lambda bi, hi: (bi, 0, hi)),
        compiler_params=pltpu.CompilerParams(dimension_semantics=("parallel", "parallel"),
                                             vmem_limit_bytes=VMEM_LIMIT),
        name="retention",
    )(proj3, proj3, proj3, proj3, intra, cross, kdec, cdec)


def _diff_kernel(slope_ref, lam_ref, rel_ref, q_ref, k_ref, v_ref, gsub_ref, o_ref, *, tq, tk, out_scale):
    hi = pl.program_id(1)
    qi = pl.program_id(2)
    half = HEAD_W // 2
    slope = slope_ref[hi]

    lp = lam_ref[...]
    lam = (jnp.exp(jnp.sum(lp[0:1] * lp[1:2], axis=1, keepdims=True))
           - jnp.exp(jnp.sum(lp[2:3] * lp[3:4], axis=1, keepdims=True))
           + (1.0 - out_scale))

    q = q_ref[0] * jnp.asarray(half ** -0.5, BF16)
    lane = lax.broadcasted_iota(jnp.int32, q.shape, 1)
    zero = jnp.zeros_like(q)
    qs = jnp.concatenate([jnp.where(lane < half, q, zero),
                          jnp.where(lane >= half, q, zero)], axis=0)

    rel = rel_ref[...]
    rel_bias = -slope * rel

    def body(j, carry):
        m, l, acc = carry
        ksl = pl.ds(pl.multiple_of(j * tk, tk), tk)
        k = k_ref[0, ksl, :]
        v = v_ref[0, ksl, :]
        off = (qi * tq - j * tk).astype(F32)
        s = lax.dot_general(qs, k, _NT, preferred_element_type=F32)
        s = s + jnp.where(rel >= -off, rel_bias - slope * off, NEG_INF)
        m_new = jnp.maximum(m, jnp.max(s, axis=1, keepdims=True))
        alpha = jnp.exp(m - m_new)
        p = jnp.exp(s - m_new)
        l = alpha * l + jnp.sum(p, axis=1, keepdims=True)
        acc = alpha * acc + jnp.dot(p.astype(BF16), v, preferred_element_type=F32)
        return m_new, l, acc

    n_kv = (qi * tq + tq - 1) // tk + 1
    m0 = jnp.full((2 * tq, 1), NEG_INF, F32)
    l0 = jnp.zeros((2 * tq, 1), F32)
    a0 = jnp.zeros((2 * tq, HEAD_W), F32)
    _, l, acc = lax.fori_loop(0, n_kv, body, (m0, l0, a0))
    o = acc / l
    dy = o[:tq] - lam * o[tq:]
    o_ref[0] = (_rms(dy) * gsub_ref[...] * out_scale).astype(o_ref.dtype)


def _diff_attention(proj3, slopes, diff_lambda_l, gsub, out_scale, tq=256, tk=512):
    b, t, _ = proj3.shape
    h = DIFF_HEADS
    base = 4 * RET_HEADS
    kern = functools.partial(_diff_kernel, tq=tq, tk=tk, out_scale=out_scale)
    rel = jnp.asarray((np.arange(2 * tq) % tq)[:, None] - np.arange(tk)[None, :], F32)
    return pl.pallas_call(
        kern,
        out_shape=jax.ShapeDtypeStruct((b, t, h * HEAD_W), BF16),
        grid=(b, h, t // tq),
        in_specs=[pl.BlockSpec(memory_space=pltpu.SMEM),
                  pl.BlockSpec(diff_lambda_l.shape, lambda bi, hi, qi: (0, 0)),
                  pl.BlockSpec((2 * tq, tk), lambda bi, hi, qi: (0, 0)),
                  pl.BlockSpec((1, tq, HEAD_W), lambda bi, hi, qi: (bi, qi, base + hi)),
                  pl.BlockSpec((1, t, HEAD_W), lambda bi, hi, qi: (bi, 0, base + h + hi)),
                  pl.BlockSpec((1, t, HEAD_W), lambda bi, hi, qi: (bi, 0, base + 2 * h + hi)),
                  pl.BlockSpec((1, HEAD_W), lambda bi, hi, qi: (0, 0))],
        out_specs=pl.BlockSpec((1, tq, HEAD_W), lambda bi, hi, qi: (bi, qi, hi)),
        compiler_params=pltpu.CompilerParams(
            dimension_semantics=("parallel", "parallel", "arbitrary"),
            vmem_limit_bytes=VMEM_LIMIT),
        name="diff_attention",
    )(slopes, diff_lambda_l, rel, proj3, proj3, proj3, gsub)


def _outproj_kernel(x_ref, ret_ref, dif_ref, wo_ref, g_ref, wq_ref, x1_ref, xn_ref, xnf_ref, q_ref, *, col_chunk):
    half = ret_ref.shape[1]
    mix = jnp.dot(ret_ref[...], wo_ref[:half, :], preferred_element_type=F32)
    mix = mix + jnp.dot(dif_ref[...], wo_ref[half:, :], preferred_element_type=F32)
    x1 = x_ref[...] + mix
    x1_ref[...] = x1
    xn = (_rms(x1) * g_ref[...]).astype(BF16)
    xn_ref[...] = xn
    xnf_ref[...] = xn.astype(F32)
    for j in range(q_ref.shape[1] // col_chunk):
        sl = slice(j * col_chunk, (j + 1) * col_chunk)
        q_ref[:, sl] = jnp.dot(xn, wq_ref[:, sl], preferred_element_type=F32).astype(q_ref.dtype)


def _out_proj(x2, ret2, dif2, wo_bf16, g, wq_bf16, row0, tm=512):
    n, half = ret2.shape
    d = x2.shape[1]
    qc = wq_bf16.shape[1]
    assert row0 % tm == 0 and n % tm == 0
    blk0 = row0 // tm
    row = lambda w: pl.BlockSpec((tm, w), lambda i: (i, 0))
    full = lambda a: pl.BlockSpec(a.shape, lambda i: (0, 0))
    return pl.pallas_call(
        functools.partial(_outproj_kernel, col_chunk=512),
        out_shape=(jax.ShapeDtypeStruct((n, d), F32),
                   jax.ShapeDtypeStruct((n, d), BF16),
                   jax.ShapeDtypeStruct((n, d), F32),
                   jax.ShapeDtypeStruct((n, qc), BF16)),
        grid=(n // tm,),
        in_specs=[pl.BlockSpec((tm, d), lambda i: (i + blk0, 0)), row(half), row(half),
                  full(wo_bf16), full(g), full(wq_bf16)],
        out_specs=(row(d), row(d), row(d), row(qc)),
        compiler_params=pltpu.CompilerParams(dimension_semantics=("parallel",),
                                             vmem_limit_bytes=VMEM_LIMIT),
        name="out_proj",
    )(x2, ret2, dif2, wo_bf16, g, wq_bf16)


def _split3(v):
    trunc = lambda x: pltpu.bitcast(pltpu.bitcast(x, jnp.uint32) & jnp.uint32(0xFFFF0000), F32)
    hi = trunc(v)
    r1 = v - hi
    mid = trunc(r1)
    return hi.astype(BF16), mid.astype(BF16), (r1 - mid).astype(BF16)


def _spread_exact(v, sel_bf16):
    hi, mid, lo = _split3(v)
    d = lambda a: jnp.dot(a, sel_bf16, preferred_element_type=F32)
    return (d(hi) + d(mid)) + d(lo)


def _spread_exact_left(sel_bf16, v):
    hi, mid, lo = _split3(v)
    d = lambda a: jnp.dot(sel_bf16, a, preferred_element_type=F32)
    return (d(hi) + d(mid)) + d(lo)


def _topk_kernel(q_ref, sk_ref, sel_ref, pad_ref, e_ref, g_ref, g2_ref):
    tb = q_ref.shape[0]
    k = PEER_TOPK
    n_sets = 2 * PEER_HEADS
    neg = -jnp.inf

    s = jnp.stack(
        [lax.dot_general(sk_ref[j // 2, j % 2], q_ref[:, j * N_KEYS:(j + 1) * N_KEYS], _NT,
                         preferred_element_type=F32) for j in range(n_sets)], axis=0)
    key_f = lax.broadcasted_iota(jnp.int32, s.shape, 1).astype(F32)
    rank = lax.broadcasted_iota(jnp.int32, (n_sets, k, tb), 1)
    top_v = jnp.zeros((n_sets, k, tb), F32)
    top_i = jnp.zeros((n_sets, k, tb), F32)
    for r in range(k):
        m = jnp.max(s, axis=1, keepdims=True)
        i = jnp.min(jnp.where(s == m, key_f, float(N_KEYS)), axis=1, keepdims=True)
        s = jnp.where(key_f == i, neg, s)
        top_v = jnp.where(rank == r, m, top_v)
        top_i = jnp.where(rank == r, i, top_i)

    pad = pad_ref[...]
    cands, eids = [], []
    for h in range(PEER_HEADS):
        cands.append(_spread_exact_left(sel_ref[0], top_v[2 * h]) + _spread_exact_left(sel_ref[1], top_v[2 * h + 1])
                     + pad)
        eids.append(jnp.dot(sel_ref[0], top_i[2 * h].astype(BF16), preferred_element_type=F32) * float(N_KEYS)
                    + jnp.dot(sel_ref[1], top_i[2 * h + 1].astype(BF16), preferred_element_type=F32))
    cand = jnp.stack(cands, axis=0)
    eid = jnp.stack(eids, axis=0)
    n_cand = cand.shape[1]

    c_f = lax.broadcasted_iota(jnp.int32, cand.shape, 1).astype(F32)
    rank2 = lax.broadcasted_iota(jnp.int32, (PEER_HEADS, k, tb), 1)
    top_c = []
    e_rank = jnp.zeros((PEER_HEADS, k, tb), F32)
    for r in range(k):
        m = jnp.max(cand, axis=1, keepdims=True)
        pos = jnp.min(jnp.where(cand == m, c_f, float(n_cand)), axis=1, keepdims=True)
        hit = c_f == pos
        e_rank = jnp.where(rank2 == r, jnp.sum(jnp.where(hit, eid, 0.0), axis=1, keepdims=True), e_rank)
        cand = jnp.where(hit, neg, cand)
        top_c.append(m)
    p_rank = jnp.zeros((PEER_HEADS, k, tb), F32)
    denom = jnp.zeros((PEER_HEADS, 1, tb), F32)
    for r in range(k):
        p = jnp.exp(top_c[r] - top_c[0])
        denom = denom + p
        p_rank = jnp.where(rank2 == r, p, p_rank)
    g_rank = p_rank / denom

    e_out = e_rank.reshape(PEER_HEADS * k, tb).T
    g1 = g_rank.reshape(PEER_HEADS * k, tb).T
    sr = lax.broadcasted_iota(jnp.int32, (N_SEL, 2 * N_SEL), 0)
    sc = lax.broadcasted_iota(jnp.int32, (N_SEL, 2 * N_SEL), 1)
    e_ref[...] = e_out.astype(jnp.int32)
    g_ref[...] = g1
    g2_ref[...] = _spread_exact(g1, jnp.where(sc == 2 * sr + 1, 1.0, 0.0).astype(BF16))


def _candidate_tables(tb):
    k = PEER_TOPK
    pairs = [(a, b) for a in range(k) for b in range(k) if (a + 1) * (b + 1) <= k]
    n_cand = -(-len(pairs) // SUBLANES) * SUBLANES
    sel = np.zeros((2, n_cand, k), np.float32)
    pad = np.full((n_cand, tb), -np.inf, np.float32)
    for c, (a, b) in enumerate(pairs):
        sel[0, c, a] = 1.0
        sel[1, c, b] = 1.0
        pad[c, :] = 0.0
    return jnp.asarray(sel, BF16), jnp.asarray(pad)


def _peer_topk(q2, sk_bf16, tb=128):
    n, qc = q2.shape
    sel, pad = _candidate_tables(tb)
    const = lambda a: pl.BlockSpec(a.shape, lambda i: (0,) * a.ndim)
    return pl.pallas_call(
        _topk_kernel,
        out_shape=(jax.ShapeDtypeStruct((n, N_SEL), jnp.int32),
                   jax.ShapeDtypeStruct((n, N_SEL), F32),
                   jax.ShapeDtypeStruct((n, 2 * N_SEL), F32)),
        grid=(n // tb,),
        in_specs=[pl.BlockSpec((tb, qc), lambda i: (i, 0)), const(sk_bf16), const(sel), const(pad)],
        out_specs=(pl.BlockSpec((tb, N_SEL), lambda i: (i, 0)),
                   pl.BlockSpec((tb, N_SEL), lambda i: (i, 0)),
                   pl.BlockSpec((tb, 2 * N_SEL), lambda i: (i, 0))),
        compiler_params=pltpu.CompilerParams(dimension_semantics=("parallel",),
                                             vmem_limit_bytes=VMEM_LIMIT),
        name="peer_topk",
    )(q2, sk_bf16, sel, pad)


def _bf16_bits(t):
    return lax.bitcast_convert_type(t.astype(BF16), jnp.uint16).astype(jnp.uint32)


def _pack_tables(u, v):
    e, d = u.shape
    return (_bf16_bits(u) | (_bf16_bits(v) << 16)).reshape(e, d // HEAD_W, HEAD_W)


def _pack_halves(t):
    b = _bf16_bits(t)
    half = t.shape[1] // 2
    return b[:, :half] | (b[:, half:] << 16)


def _mix_kernel(efirst_ref, enext_ref, g2_ref, xn_ref, x1_ref, gfin_ref, tab_hbm, after_ref, o_ref, buf, sem, *,
                final_norm):
    step = pl.program_id(0)
    tb = xn_ref.shape[0]
    n_chunks = tab_hbm.shape[1]

    def issue(e_ref, t, slot):
        for k in range(N_SEL):
            e = e_ref[t, k]
            pltpu.make_async_copy(tab_hbm.at[e], buf.at[slot, pl.ds(k * n_chunks, n_chunks), :],
                                  sem.at[slot]).start(priority=k % 2)

    def wait(slot):
        pltpu.make_async_copy(buf.at[slot], buf.at[slot], sem.at[slot]).wait()

    @pl.when(step == 0)
    def _():
        issue(efirst_ref, 0, 0)

    rid = lax.broadcasted_iota(jnp.int32, (SUBLANES, 2 * N_SEL), 0)

    def group(gi, c):
        base = pl.multiple_of(gi * SUBLANES, SUBLANES)
        xrows = xn_ref[pl.ds(base, SUBLANES), :]
        grows = g2_ref[pl.ds(base, SUBLANES), :]
        acc = x1_ref[pl.ds(base, SUBLANES), :]
        for j in range(SUBLANES):
            slot = j % 2
            issue(enext_ref, base + j, 1 - slot)
            wait(slot)
            uv = jnp.concatenate(
                [pltpu.bitcast(buf[slot, pl.ds(c, N_SEL, stride=n_chunks), :], BF16) for c in range(n_chunks)],
                axis=1)
            s2 = lax.dot_general(xrows, uv, _NT, preferred_element_type=F32)
            act = pltpu.roll(_gelu_exact(s2), 1, axis=1)
            w2 = jnp.where(rid == j, grows * act, 0.0)
            acc = acc + jnp.dot(w2.astype(BF16), uv, preferred_element_type=F32)
        o_ref[pl.ds(base, SUBLANES), :] = acc
        return c

    lax.fori_loop(0, tb // SUBLANES, group, 0)

    @pl.when(step == pl.num_programs(0) - 1)
    def _():
        wait(0)

    if final_norm:
        o_ref[...] = _rms(o_ref[...]) * gfin_ref[...]


def _peer_mix(eids, e_next, gates2, xn, x1, gfin, table, after, row0, rows, final_norm, tb=64):
    d = x1.shape[1]
    assert tb % (2 * SUBLANES) == 0 and row0 % tb == 0 and rows % tb == 0
    blk0 = row0 // tb
    row = lambda w: pl.BlockSpec((tb, w), lambda i: (i + blk0, 0))
    return pl.pallas_call(
        functools.partial(_mix_kernel, final_norm=final_norm),
        out_shape=jax.ShapeDtypeStruct((rows, d), F32),
        grid=(rows // tb,),
        in_specs=[pl.BlockSpec((SUBLANES, N_SEL), lambda i: (row0 // SUBLANES, 0), memory_space=pltpu.SMEM),
                  pl.BlockSpec((tb, N_SEL), lambda i: (i + blk0, 0), memory_space=pltpu.SMEM),
                  row(2 * N_SEL), row(d), row(d),
                  pl.BlockSpec((1, d), lambda i: (0, 0)),
                  pl.BlockSpec(memory_space=pl.ANY),
                  pl.BlockSpec(memory_space=pl.ANY)],
        out_specs=pl.BlockSpec((tb, d), lambda i: (i, 0)),
        scratch_shapes=[pltpu.VMEM((2, N_SEL * (d // HEAD_W), HEAD_W), jnp.uint32),
                        pltpu.SemaphoreType.DMA((2,))],
        compiler_params=pltpu.CompilerParams(dimension_semantics=("arbitrary",),
                                             vmem_limit_bytes=VMEM_LIMIT),
        name="peer_mix",
    )(eids, e_next, gates2, xn, x1, gfin, table, after)


SC_LANES = 16
SC_TILES = 32
SC_TBLK = 16
SC_SLOTS = 4
HI_MASK = 0xFFFF0000
_SC_PARAMS = pltpu.CompilerParams(needs_layout_passes=False)


def _sc_unpack(word):
    return plsc.bitcast(word << 16, F32), plsc.bitcast(word & jnp.uint32(HI_MASK), F32)


def _sc_pipelined_heads(tab_hbm, ev, rows, sem, compute):
    n_items = SC_TBLK * PEER_HEADS
    ahead = SC_SLOTS - 1
    assert n_items % SC_SLOTS == 0

    def gather(it, slot):
        ti = it // PEER_HEADS
        h = it % PEER_HEADS
        return pltpu.make_async_copy(tab_hbm.at[ev.at[ti, pl.ds(h * PEER_TOPK, PEER_TOPK)]], rows.at[slot],
                                     sem.at[slot])

    for a in range(ahead):
        gather(a, a).start()

    def group(jj, carry):
        for slot in range(SC_SLOTS):
            it = SC_SLOTS * jj + slot
            gather(jnp.minimum(it + ahead, n_items - 1), (slot + ahead) % SC_SLOTS).start()
            gather(it, slot).wait()
            compute(it, slot)
        return carry

    lax.fori_loop(0, n_items // SC_SLOTS, group, 0)
    for a in range(ahead):
        gather(n_items - 1, (n_items + a) % SC_SLOTS).wait()


def _sc_scores(x, eids, utab, row0, rows):
    d = x.shape[1]
    dw = d // 2
    nw = dw // SC_LANES
    per_tile = rows // SC_TILES
    assert rows % (SC_TILES * SC_TBLK) == 0 and row0 % SC_TBLK == 0

    def body(x_hbm, e_hbm, tab_hbm, s_hbm, xv, ev, rbuf, sv, tmp, sem):
        tile = lax.axis_index("c") * 16 + lax.axis_index("s")
        lane = lax.iota(jnp.int32, SC_LANES)

        def compute(it, slot):
            ti = it // PEER_HEADS
            h = it % PEER_HEADS

            def chunk(w, accs):
                off = pl.multiple_of(w * SC_LANES, SC_LANES)
                xlo = xv[ti, pl.ds(off, SC_LANES)]
                xhi = xv[ti, pl.ds(dw + off, SC_LANES)]
                out = []
                for j in range(PEER_TOPK):
                    lo, hi = _sc_unpack(rbuf[slot, j, pl.ds(off, SC_LANES)])
                    out.append(accs[j] + lo * xlo + hi * xhi)
                return tuple(out)

            accs = lax.fori_loop(0, nw, chunk, tuple(jnp.zeros((SC_LANES,), F32) for _ in range(PEER_TOPK)))
            for j in range(PEER_TOPK):
                tmp[j, :] = accs[j]
            svec = jnp.zeros((SC_LANES,), F32)
            for l in range(SC_LANES):
                svec = svec + plsc.load_gather(tmp, [lane, jnp.full((SC_LANES,), l, jnp.int32)])
            sv[ti, pl.ds(pl.multiple_of(h * PEER_TOPK, PEER_TOPK), PEER_TOPK)] = svec

        def block(b, carry):
            t_out = pl.multiple_of(tile * per_tile + b * SC_TBLK, SC_TBLK)
            pltpu.sync_copy(x_hbm.at[pl.ds(row0 + t_out, SC_TBLK)], xv)
            pltpu.sync_copy(e_hbm.at[pl.ds(row0 + t_out, SC_TBLK)], ev)
            _sc_pipelined_heads(tab_hbm, ev, rbuf, sem, compute)
            pltpu.sync_copy(sv, s_hbm.at[pl.ds(t_out, SC_TBLK)])
            return carry

        lax.fori_loop(0, per_tile // SC_TBLK, block, 0)

    return pl.kernel(
        body,
        out_type=jax.ShapeDtypeStruct((rows, N_SEL), F32),
        mesh=plsc.VectorSubcoreMesh(core_axis_name="c", subcore_axis_name="s"),
        scratch_types=[pltpu.VMEM((SC_TBLK, d), F32), pltpu.VMEM((SC_TBLK, N_SEL), jnp.int32),
                       pltpu.VMEM((SC_SLOTS, PEER_TOPK, dw), jnp.uint32), pltpu.VMEM((SC_TBLK, N_SEL), F32),
                       pltpu.VMEM((PEER_TOPK, SC_LANES), F32), pltpu.SemaphoreType.DMA((SC_SLOTS,))],
        compiler_params=_SC_PARAMS,
        name="sc_scores",
    )(x, eids, utab)


def _sc_combine(w, eids, vtab, row0):
    rows = w.shape[0]
    dw = vtab.shape[1]
    d = 2 * dw
    nw = dw // SC_LANES
    per_tile = rows // SC_TILES
    assert rows % (SC_TILES * SC_TBLK) == 0 and row0 % SC_TBLK == 0

    def body(w_hbm, e_hbm, tab_hbm, y_hbm, wv, ev, rbuf, yv, sem):
        tile = lax.axis_index("c") * 16 + lax.axis_index("s")
        zero_i = jnp.zeros((SC_LANES,), jnp.int32)

        def compute(it, slot):
            ti = it // PEER_HEADS
            h = it % PEER_HEADS
            wk = [plsc.load_gather(wv, [zero_i + ti, zero_i + (h * PEER_TOPK + j)]) for j in range(PEER_TOPK)]

            def chunk(c, carry):
                off = pl.multiple_of(c * SC_LANES, SC_LANES)
                alo = jnp.zeros((SC_LANES,), F32)
                ahi = jnp.zeros((SC_LANES,), F32)
                for j in range(PEER_TOPK):
                    lo, hi = _sc_unpack(rbuf[slot, j, pl.ds(off, SC_LANES)])
                    alo = alo + wk[j] * lo
                    ahi = ahi + wk[j] * hi
                plsc.addupdate(yv.at[ti, pl.ds(off, SC_LANES)], alo)
                plsc.addupdate(yv.at[ti, pl.ds(dw + off, SC_LANES)], ahi)
                return carry

            plsc.parallel_loop(0, nw, 1, carry=jnp.int32(0))(chunk)

        def block(b, carry):
            t_out = pl.multiple_of(tile * per_tile + b * SC_TBLK, SC_TBLK)
            pltpu.sync_copy(w_hbm.at[pl.ds(t_out, SC_TBLK)], wv)
            pltpu.sync_copy(e_hbm.at[pl.ds(row0 + t_out, SC_TBLK)], ev)

            def zero(i, carry):
                for ti in range(SC_TBLK):
                    yv[ti, pl.ds(pl.multiple_of(i * SC_LANES, SC_LANES), SC_LANES)] = jnp.zeros((SC_LANES,), F32)
                return carry

            lax.fori_loop(0, d // SC_LANES, zero, 0)
            _sc_pipelined_heads(tab_hbm, ev, rbuf, sem, compute)
            pltpu.sync_copy(yv, y_hbm.at[pl.ds(t_out, SC_TBLK)])
            return carry

        lax.fori_loop(0, per_tile // SC_TBLK, block, 0)

    return pl.kernel(
        body,
        out_type=jax.ShapeDtypeStruct((rows, d), F32),
        mesh=plsc.VectorSubcoreMesh(core_axis_name="c", subcore_axis_name="s"),
        scratch_types=[pltpu.VMEM((SC_TBLK, N_SEL), F32), pltpu.VMEM((SC_TBLK, N_SEL), jnp.int32),
                       pltpu.VMEM((SC_SLOTS, PEER_TOPK, dw), jnp.uint32), pltpu.VMEM((SC_TBLK, d), F32),
                       pltpu.SemaphoreType.DMA((SC_SLOTS,))],
        compiler_params=_SC_PARAMS,
        name="sc_combine",
    )(w, eids, vtab)


def _gate_kernel(s_ref, g_ref, after_ref, w_ref):
    w_ref[...] = g_ref[...] * _gelu_exact(s_ref[...])


def _gate_act(s, gates, after, row0, tb=512):
    rows = s.shape[0]
    assert rows % tb == 0 and row0 % tb == 0
    blk0 = row0 // tb
    return pl.pallas_call(
        _gate_kernel,
        out_shape=jax.ShapeDtypeStruct((rows, N_SEL), F32),
        grid=(rows // tb,),
        in_specs=[pl.BlockSpec((tb, N_SEL), lambda i: (i, 0)),
                  pl.BlockSpec((tb, N_SEL), lambda i: (i + blk0, 0)),
                  pl.BlockSpec(memory_space=pl.ANY)],
        out_specs=pl.BlockSpec((tb, N_SEL), lambda i: (i, 0)),
        compiler_params=pltpu.CompilerParams(dimension_semantics=("parallel",)),
        name="peer_gate",
    )(s, gates, after)


def _resid_norm_kernel(x1_ref, y_ref, gfin_ref, after_ref, o_ref, *, final_norm):
    z = x1_ref[...] + y_ref[...]
    o_ref[...] = _rms(z) * gfin_ref[...] if final_norm else z


def _resid_norm(x1, y, gfin, after, final_norm, tb=512):
    rows, d = y.shape
    assert rows % tb == 0
    return pl.pallas_call(
        functools.partial(_resid_norm_kernel, final_norm=final_norm),
        out_shape=jax.ShapeDtypeStruct((rows, d), F32),
        grid=(rows // tb,),
        in_specs=[pl.BlockSpec((tb, d), lambda i: (i, 0)),
                  pl.BlockSpec((tb, d), lambda i: (i, 0)),
                  pl.BlockSpec((1, d), lambda i: (0, 0)),
                  pl.BlockSpec(memory_space=pl.ANY)],
        out_specs=pl.BlockSpec((tb, d), lambda i: (i, 0)),
        compiler_params=pltpu.CompilerParams(dimension_semantics=("parallel",)),
        name="peer_resid_norm",
    )(x1, y, gfin, after)


N_CHUNKS = 16
SC_TOKEN_FRAC = 51 / 64
SPLIT_UNIT = 512
COST_FRONT, COST_SC_SCORES, COST_SC_COMBINE, COST_TC_MIX = 0.13, 0.21, 0.18, 0.85


def _sc_rows(n, tokens_per_chunk):
    n_sc = int(n * SC_TOKEN_FRAC) // SPLIT_UNIT * SPLIT_UNIT
    return [min(max(n_sc - c * tokens_per_chunk, 0), tokens_per_chunk) for c in range(N_CHUNKS)]


def _tc_schedule(tokens_per_chunk, sc_rows):
    sc_chunks = [c for c in range(N_CHUNKS) if sc_rows[c]]
    base = [("front", c) for c in range(N_CHUNKS)] + [("mix", c) for c in range(N_CHUNKS)
                                                       if sc_rows[c] < tokens_per_chunk]
    order = []
    tc = sc = 0.0
    scores_done, combine_done = {}, {}
    gates = list(sc_chunks)
    norms = list(sc_chunks)
    for kind, c in base + [("end", None)]:
        while gates and (kind == "end" or scores_done.get(gates[0], float("inf"))
                         + 0.25 * COST_SC_SCORES * sc_rows[gates[0]] <= tc):
            g = gates.pop(0)
            tc = max(tc, scores_done[g])
            sc = max(sc, tc) + COST_SC_COMBINE * sc_rows[g]
            combine_done[g] = sc
            order.append(("gate", g))
        while norms and (kind == "end" or combine_done.get(norms[0], float("inf")) <= tc):
            order.append(("norm", norms.pop(0)))
        if kind == "front":
            tc += COST_FRONT * tokens_per_chunk
            if sc_rows[c]:
                sc = max(sc, tc) + COST_SC_SCORES * sc_rows[c]
                scores_done[c] = sc
        elif kind == "mix":
            tc += COST_TC_MIX * (tokens_per_chunk - sc_rows[c])
        if kind != "end":
            order.append((kind, c))
    return order


def _layer(x2, batch, seq, params, consts, li, final_norm):
    (g_mix, w_in, lam_p, g_sub, w_out, g_ffn, wq, sk, u, v, g_fin) = params
    intra, cross, kdec, cdec, slopes = consts
    n, d = x2.shape
    nc = n // N_CHUNKS
    bc = batch // N_CHUNKS
    assert batch % N_CHUNKS == 0
    table = _pack_tables(u, v)
    utab = _pack_halves(u)
    vtab = _pack_halves(v)

    sc_rows = _sc_rows(n, nc)
    chunks = [None] * N_CHUNKS
    scores = [None] * N_CHUNKS
    ys = [None] * N_CHUNKS
    outs_sc = [None] * N_CHUNKS
    outs_tc = [None] * N_CHUNKS
    last = g_fin

    def front(c, last):
        proj = _in_proj(x2, g_mix, w_in, last, c * nc, nc)
        proj3 = proj.reshape(bc, seq, proj.shape[1])
        ret = _retention(proj3, intra, cross, kdec, cdec)
        dif = _diff_attention(proj3, slopes, lam_p, g_sub, 1.0 - li)
        half = ret.shape[2]
        x1, xn, xn_f32, q = _out_proj(x2, ret.reshape(nc, half), dif.reshape(nc, half), w_out, g_ffn, wq, c * nc)
        eids, gates, gates2 = _peer_topk(q, sk)
        chunks[c] = (x1, xn, eids, gates, gates2)
        if sc_rows[c]:
            scores[c] = _sc_scores(xn_f32, eids, utab, 0, sc_rows[c])
        return gates2

    def gate(c, last):
        _, _, eids, gates, _ = chunks[c]
        w = _gate_act(scores[c], gates, last, 0)
        ys[c] = _sc_combine(w, eids, vtab, 0)
        return w

    def tc_mix(c, last):
        x1, xn, eids, _, gates2 = chunks[c]
        outs_tc[c] = _peer_mix(eids, jnp.roll(eids, -1, axis=0), gates2, xn, x1, g_fin, table, last,
                               sc_rows[c], nc - sc_rows[c], final_norm)
        return outs_tc[c]

    def norm(c, last):
        outs_sc[c] = _resid_norm(chunks[c][0], ys[c], g_fin, last, final_norm)
        return outs_sc[c]

    run = {"front": front, "gate": gate, "mix": tc_mix, "norm": norm}
    for kind, c in _tc_schedule(nc, sc_rows):
        last = run[kind](c, last)
    pieces = [o for c in range(N_CHUNKS) for o in (outs_sc[c], outs_tc[c]) if o is not None]
    return jnp.concatenate(pieces, axis=0)


def _retention_tables(dk):
    c = RET_CHUNK
    h = RET_HEADS
    log_gamma = jnp.log1p(-jnp.exp2(-5.0 - jnp.arange(h, dtype=F32)))
    idx = jnp.arange(c, dtype=F32)
    rel = idx[:, None] - idx[None, :]
    intra = jnp.where(rel >= 0, jnp.exp(log_gamma[:, None, None] * jnp.maximum(rel, 0.0)), 0.0)
    cross = jnp.exp(log_gamma[:, None] * (idx + 1.0))
    kdec = jnp.exp(log_gamma[:, None] * (c - 1.0 - idx))
    cdec = jnp.exp(log_gamma * c)
    scale = dk ** -0.5
    return intra * scale, cross[:, :, None], (kdec * scale)[:, :, None], cdec[:, None, None]


def kernel(x, norm_mix_g, w_in, diff_lambda, diff_subln_g, w_out, norm_ffn_g, peer_wq, peer_subkeys,
           peer_u, peer_v, final_norm_g):
    b, t, d = x.shape
    depth = w_in.shape[0]
    n = b * t
    x2 = x.reshape(n, d)
    consts = _retention_tables(HEAD_W) + (
        jnp.exp2(-8.0 * (jnp.arange(DIFF_HEADS, dtype=F32) + 1.0) / DIFF_HEADS),)
    for l in range(depth):
        li = 0.8 - 0.6 * math.exp(-0.3 * l)
        params = (norm_mix_g[l][None, :], w_in[l].astype(BF16), diff_lambda[l], diff_subln_g[l][None, :],
                  w_out[l].astype(BF16), norm_ffn_g[l][None, :], peer_wq[l].astype(BF16),
                  peer_subkeys[l].astype(BF16), peer_u[l], peer_v[l], final_norm_g[None, :])
        x2 = _layer(x2, b, t, params, consts, li, final_norm=(l == depth - 1))
    return x2.reshape(b, t, d)
```

```python
import functools
import math

import jax
import jax.numpy as jnp
import numpy as np
from jax import lax
from jax.experimental import pallas as pl
from jax.experimental.pallas import tpu as pltpu
from jax.experimental.pallas import tpu_sc as plsc

F32 = jnp.float32
BF16 = jnp.bfloat16

RMS_EPS = 1e-6
NEG_INF = -1e30

RET_HEADS = 4
RET_CHUNK = 128
DIFF_HEADS = 4
HEAD_W = 128
PEER_HEADS = 8
PEER_TOPK = 16
N_KEYS = 128
N_SEL = PEER_HEADS * PEER_TOPK

SUBLANES = 8
VMEM_LIMIT = 56 * 1024 * 1024

_NT = (((1,), (1,)), ((), ()))


def _rms(x, eps=RMS_EPS):
    return x * lax.rsqrt(jnp.mean(x * x, axis=-1, keepdims=True) + eps)


def _gelu_exact(x):
    return 0.5 * x * (1.0 + lax.erf(x * (2.0 ** -0.5)))


def _inproj_kernel(x_ref, g_ref, w_ref, after_ref, o_ref, *, col_chunk):
    h = (_rms(x_ref[...]) * g_ref[...]).astype(BF16)
    for j in range(o_ref.shape[1] // col_chunk):
        sl = slice(j * col_chunk, (j + 1) * col_chunk)
        o_ref[:, sl] = jnp.dot(h, w_ref[:, sl], preferred_element_type=F32).astype(o_ref.dtype)


def _in_proj(x2, g, w_bf16, after, row0, rows, tm=512):
    d = x2.shape[1]
    cols = w_bf16.shape[1]
    assert row0 % tm == 0 and rows % tm == 0
    blk0 = row0 // tm
    return pl.pallas_call(
        functools.partial(_inproj_kernel, col_chunk=512),
        out_shape=jax.ShapeDtypeStruct((rows, cols), BF16),
        grid=(rows // tm,),
        in_specs=[pl.BlockSpec((tm, d), lambda i: (i + blk0, 0)),
                  pl.BlockSpec((1, d), lambda i: (0, 0)),
                  pl.BlockSpec((d, cols), lambda i: (0, 0)),
                  pl.BlockSpec(memory_space=pl.ANY)],
        out_specs=pl.BlockSpec((tm, cols), lambda i: (i, 0)),
        compiler_params=pltpu.CompilerParams(dimension_semantics=("parallel",),
                                             vmem_limit_bytes=VMEM_LIMIT),
        name="in_proj",
    )(x2, g, w_bf16, after)


def _ret_kernel(q_ref, k_ref, v_ref, g_ref, intra_ref, cross_ref, kdec_ref, cdec_ref, o_ref):
    c = RET_CHUNK
    n_chunks = q_ref.shape[1] // c
    intra = intra_ref[0]
    cross = cross_ref[0]
    kdec = kdec_ref[0]
    cdec = cdec_ref[0]

    def body(n, s):
        sl = pl.ds(pl.multiple_of(n * c, c), c)
        qi = q_ref[0, sl, :]
        ki = k_ref[0, sl, :]
        vi = v_ref[0, sl, :]
        a = lax.dot_general(qi, ki, _NT, preferred_element_type=F32) * intra
        o = jnp.dot(a.astype(BF16), vi, preferred_element_type=F32)
        o = o + jnp.dot(qi, s.astype(BF16), preferred_element_type=F32) * cross
        kd_t = (ki.astype(F32) * kdec).T.astype(BF16)
        s = s * cdec + jnp.dot(kd_t, vi, preferred_element_type=F32)
        gate = g_ref[0, sl, :].astype(F32)
        o_ref[0, sl, :] = (gate * jax.nn.sigmoid(gate) * _rms(o)).astype(o_ref.dtype)
        return s

    lax.fori_loop(0, n_chunks, body, jnp.zeros((HEAD_W, HEAD_W), F32), unroll=8)


def _retention(proj3, intra, cross, kdec, cdec):
    b, t, _ = proj3.shape
    h = RET_HEADS

    def col(off):
        return pl.BlockSpec((1, t, HEAD_W), lambda bi, hi: (bi, 0, off + hi))

    def per_head(shape):
        return pl.BlockSpec((1,) + shape, lambda bi, hi: (hi, 0, 0))

    return pl.pallas_call(
        _ret_kernel,
        out_shape=jax.ShapeDtypeStruct((b, t, h * HEAD_W), BF16),
        grid=(b, h),
        in_specs=[col(0), col(h), col(2 * h), col(3 * h),
                  per_head((RET_CHUNK, RET_CHUNK)), per_head((RET_CHUNK, 1)),
                  per_head((RET_CHUNK, 1)), per_head((1, 1))],
        out_specs=pl.BlockSpec((1, t, HEAD_W), lambda bi, hi: (bi, 0, hi)),
        compiler_params=pltpu.CompilerParams(dimension_semantics=("parallel", "parallel"),
                                             vmem_limit_bytes=VMEM_LIMIT),
        name="retention",
    )(proj3, proj3, proj3, proj3, intra, cross, kdec, cdec)


def _diff_kernel(slope_ref, lam_ref, rel_ref, q_ref, k_ref, v_ref, gsub_ref, o_ref, *, tq, tk, out_scale):
    hi = pl.program_id(1)
    qi = pl.program_id(2)
    half = HEAD_W // 2
    slope = slope_ref[hi]

    lp = lam_ref[...]
    lam = (jnp.exp(jnp.sum(lp[0:1] * lp[1:2], axis=1, keepdims=True))
           - jnp.exp(jnp.sum(lp[2:3] * lp[3:4], axis=1, keepdims=True))
           + (1.0 - out_scale))

    q = q_ref[0] * jnp.asarray(half ** -0.5, BF16)
    lane = lax.broadcasted_iota(jnp.int32, q.shape, 1)
    zero = jnp.zeros_like(q)
    qs = jnp.concatenate([jnp.where(lane < half, q, zero),
                          jnp.where(lane >= half, q, zero)], axis=0)

    rel = rel_ref[...]
    rel_bias = -slope * rel

    def body(j, carry):
        m, l, acc = carry
        ksl = pl.ds(pl.multiple_of(j * tk, tk), tk)
        k = k_ref[0, ksl, :]
        v = v_ref[0, ksl, :]
        off = (qi * tq - j * tk).astype(F32)
        s = lax.dot_general(qs, k, _NT, preferred_element_type=F32)
        s = s + jnp.where(rel >= -off, rel_bias - slope * off, NEG_INF)
        m_new = jnp.maximum(m, jnp.max(s, axis=1, keepdims=True))
        alpha = jnp.exp(m - m_new)
        p = jnp.exp(s - m_new)
        l = alpha * l + jnp.sum(p, axis=1, keepdims=True)
        acc = alpha * acc + jnp.dot(p.astype(BF16), v, preferred_element_type=F32)
        return m_new, l, acc

    n_kv = (qi * tq + tq - 1) // tk + 1
    m0 = jnp.full((2 * tq, 1), NEG_INF, F32)
    l0 = jnp.zeros((2 * tq, 1), F32)
    a0 = jnp.zeros((2 * tq, HEAD_W), F32)
    _, l, acc = lax.fori_loop(0, n_kv, body, (m0, l0, a0))
    o = acc / l
    dy = o[:tq] - lam * o[tq:]
    o_ref[0] = (_rms(dy) * gsub_ref[...] * out_scale).astype(o_ref.dtype)


def _diff_attention(proj3, slopes, diff_lambda_l, gsub, out_scale, tq=256, tk=512):
    b, t, _ = proj3.shape
    h = DIFF_HEADS
    base = 4 * RET_HEADS
    kern = functools.partial(_diff_kernel, tq=tq, tk=tk, out_scale=out_scale)
    rel = jnp.asarray((np.arange(2 * tq) % tq)[:, None] - np.arange(tk)[None, :], F32)
    return pl.pallas_call(
        kern,
        out_shape=jax.ShapeDtypeStruct((b, t, h * HEAD_W), BF16),
        grid=(b, h, t // tq),
        in_specs=[pl.BlockSpec(memory_space=pltpu.SMEM),
                  pl.BlockSpec(diff_lambda_l.shape, lambda bi, hi, qi: (0, 0)),
                  pl.BlockSpec((2 * tq, tk), lambda bi, hi, qi: (0, 0)),
                  pl.BlockSpec((1, tq, HEAD_W), lambda bi, hi, qi: (bi, qi, base + hi)),
                  pl.BlockSpec((1, t, HEAD_W), lambda bi, hi, qi: (bi, 0, base + h + hi)),
                  pl.BlockSpec((1, t, HEAD_W), lambda bi, hi, qi: (bi, 0, base + 2 * h + hi)),
                  pl.BlockSpec((1, HEAD_W), lambda bi, hi, qi: (0, 0))],
        out_specs=pl.BlockSpec((1, tq, HEAD_W), lambda bi, hi, qi: (bi, qi, hi)),
        compiler_params=pltpu.CompilerParams(
            dimension_semantics=("parallel", "parallel", "arbitrary"),
            vmem_limit_bytes=VMEM_LIMIT),
        name="diff_attention",
    )(slopes, diff_lambda_l, rel, proj3, proj3, proj3, gsub)


def _outproj_kernel(x_ref, ret_ref, dif_ref, wo_ref, g_ref, wq_ref, x1_ref, xn_ref, xnf_ref, q_ref, *, col_chunk):
    half = ret_ref.shape[1]
    mix = jnp.dot(ret_ref[...], wo_ref[:half, :], preferred_element_type=F32)
    mix = mix + jnp.dot(dif_ref[...], wo_ref[half:, :], preferred_element_type=F32)
    x1 = x_ref[...] + mix
    x1_ref[...] = x1
    xn = (_rms(x1) * g_ref[...]).astype(BF16)
    xn_ref[...] = xn
    xnf_ref[...] = xn.astype(F32)
    for j in range(q_ref.shape[1] // col_chunk):
        sl = slice(j * col_chunk, (j + 1) * col_chunk)
        q_ref[:, sl] = jnp.dot(xn, wq_ref[:, sl], preferred_element_type=F32).astype(q_ref.dtype)


def _out_proj(x2, ret2, dif2, wo_bf16, g, wq_bf16, row0, tm=512):
    n, half = ret2.shape
    d = x2.shape[1]
    qc = wq_bf16.shape[1]
    assert row0 % tm == 0 and n % tm == 0
    blk0 = row0 // tm
    row = lambda w: pl.BlockSpec((tm, w), lambda i: (i, 0))
    full = lambda a: pl.BlockSpec(a.shape, lambda i: (0, 0))
    return pl.pallas_call(
        functools.partial(_outproj_kernel, col_chunk=512),
        out_shape=(jax.ShapeDtypeStruct((n, d), F32),
                   jax.ShapeDtypeStruct((n, d), BF16),
                   jax.ShapeDtypeStruct((n, d), F32),
                   jax.ShapeDtypeStruct((n, qc), BF16)),
        grid=(n // tm,),
        in_specs=[pl.BlockSpec((tm, d), lambda i: (i + blk0, 0)), row(half), row(half),
                  full(wo_bf16), full(g), full(wq_bf16)],
        out_specs=(row(d), row(d), row(d), row(qc)),
        compiler_params=pltpu.CompilerParams(dimension_semantics=("parallel",),
                                             vmem_limit_bytes=VMEM_LIMIT),
        name="out_proj",
    )(x2, ret2, dif2, wo_bf16, g, wq_bf16)


def _split3(v):
    trunc = lambda x: pltpu.bitcast(pltpu.bitcast(x, jnp.uint32) & jnp.uint32(0xFFFF0000), F32)
    hi = trunc(v)
    r1 = v - hi
    mid = trunc(r1)
    return hi.astype(BF16), mid.astype(BF16), (r1 - mid).astype(BF16)


def _spread_exact(v, sel_bf16):
    hi, mid, lo = _split3(v)
    d = lambda a: jnp.dot(a, sel_bf16, preferred_element_type=F32)
    return (d(hi) + d(mid)) + d(lo)


def _spread_exact_left(sel_bf16, v):
    hi, mid, lo = _split3(v)
    d = lambda a: jnp.dot(sel_bf16, a, preferred_element_type=F32)
    return (d(hi) + d(mid)) + d(lo)


def _topk_kernel(q_ref, sk_ref, sel_ref, pad_ref, e_ref, g_ref, g2_ref):
    tb = q_ref.shape[0]
    k = PEER_TOPK
    n_sets = 2 * PEER_HEADS
    neg = -jnp.inf

    s = jnp.stack(
        [lax.dot_general(sk_ref[j // 2, j % 2], q_ref[:, j * N_KEYS:(j + 1) * N_KEYS], _NT,
                         preferred_element_type=F32) for j in range(n_sets)], axis=0)
    key_f = lax.broadcasted_iota(jnp.int32, s.shape, 1).astype(F32)
    rank = lax.broadcasted_iota(jnp.int32, (n_sets, k, tb), 1)
    top_v = jnp.zeros((n_sets, k, tb), F32)
    top_i = jnp.zeros((n_sets, k, tb), F32)
    for r in range(k):
        m = jnp.max(s, axis=1, keepdims=True)
        i = jnp.min(jnp.where(s == m, key_f, float(N_KEYS)), axis=1, keepdims=True)
        s = jnp.where(key_f == i, neg, s)
        top_v = jnp.where(rank == r, m, top_v)
        top_i = jnp.where(rank == r, i, top_i)

    pad = pad_ref[...]
    cands, eids = [], []
    for h in range(PEER_HEADS):
        cands.append(_spread_exact_left(sel_ref[0], top_v[2 * h]) + _spread_exact_left(sel_ref[1], top_v[2 * h + 1])
                     + pad)
        eids.append(jnp.dot(sel_ref[0], top_i[2 * h].astype(BF16), preferred_element_type=F32) * float(N_KEYS)
                    + jnp.dot(sel_ref[1], top_i[2 * h + 1].astype(BF16), preferred_element_type=F32))
    cand = jnp.stack(cands, axis=0)
    eid = jnp.stack(eids, axis=0)
    n_cand = cand.shape[1]

    c_f = lax.broadcasted_iota(jnp.int32, cand.shape, 1).astype(F32)
    rank2 = lax.broadcasted_iota(jnp.int32, (PEER_HEADS, k, tb), 1)
    top_c = []
    e_rank = jnp.zeros((PEER_HEADS, k, tb), F32)
    for r in range(k):
        m = jnp.max(cand, axis=1, keepdims=True)
        pos = jnp.min(jnp.where(cand == m, c_f, float(n_cand)), axis=1, keepdims=True)
        hit = c_f == pos
        e_rank = jnp.where(rank2 == r, jnp.sum(jnp.where(hit, eid, 0.0), axis=1, keepdims=True), e_rank)
        cand = jnp.where(hit, neg, cand)
        top_c.append(m)
    p_rank = jnp.zeros((PEER_HEADS, k, tb), F32)
    denom = jnp.zeros((PEER_HEADS, 1, tb), F32)
    for r in range(k):
        p = jnp.exp(top_c[r] - top_c[0])
        denom = denom + p
        p_rank = jnp.where(rank2 == r, p, p_rank)
    g_rank = p_rank / denom

    e_out = e_rank.reshape(PEER_HEADS * k, tb).T
    g1 = g_rank.reshape(PEER_HEADS * k, tb).T
    sr = lax.broadcasted_iota(jnp.int32, (N_SEL, 2 * N_SEL), 0)
    sc = lax.broadcasted_iota(jnp.int32, (N_SEL, 2 * N_SEL), 1)
    e_ref[...] = e_out.astype(jnp.int32)
    g_ref[...] = g1
    g2_ref[...] = _spread_exact(g1, jnp.where(sc == 2 * sr + 1, 1.0, 0.0).astype(BF16))


def _candidate_tables(tb):
    k = PEER_TOPK
    pairs = [(a, b) for a in range(k) for b in range(k) if (a + 1) * (b + 1) <= k]
    n_cand = -(-len(pairs) // SUBLANES) * SUBLANES
    sel = np.zeros((2, n_cand, k), np.float32)
    pad = np.full((n_cand, tb), -np.inf, np.float32)
    for c, (a, b) in enumerate(pairs):
        sel[0, c, a] = 1.0
        sel[1, c, b] = 1.0
        pad[c, :] = 0.0
    return jnp.asarray(sel, BF16), jnp.asarray(pad)


def _peer_topk(q2, sk_bf16, tb=128):
    n, qc = q2.shape
    sel, pad = _candidate_tables(tb)
    const = lambda a: pl.BlockSpec(a.shape, lambda i: (0,) * a.ndim)
    return pl.pallas_call(
        _topk_kernel,
        out_shape=(jax.ShapeDtypeStruct((n, N_SEL), jnp.int32),
                   jax.ShapeDtypeStruct((n, N_SEL), F32),
                   jax.ShapeDtypeStruct((n, 2 * N_SEL), F32)),
        grid=(n // tb,),
        in_specs=[pl.BlockSpec((tb, qc), lambda i: (i, 0)), const(sk_bf16), const(sel), const(pad)],
        out_specs=(pl.BlockSpec((tb, N_SEL), lambda i: (i, 0)),
                   pl.BlockSpec((tb, N_SEL), lambda i: (i, 0)),
                   pl.BlockSpec((tb, 2 * N_SEL), lambda i: (i, 0))),
        compiler_params=pltpu.CompilerParams(dimension_semantics=("parallel",),
                                             vmem_limit_bytes=VMEM_LIMIT),
        name="peer_topk",
    )(q2, sk_bf16, sel, pad)


def _bf16_bits(t):
    return lax.bitcast_convert_type(t.astype(BF16), jnp.uint16).astype(jnp.uint32)


def _pack_tables(u, v):
    e, d = u.shape
    return (_bf16_bits(u) | (_bf16_bits(v) << 16)).reshape(e, d // HEAD_W, HEAD_W)


def _pack_halves(t):
    b = _bf16_bits(t)
    half = t.shape[1] // 2
    return b[:, :half] | (b[:, half:] << 16)


def _mix_kernel(efirst_ref, enext_ref, g2_ref, xn_ref, x1_ref, gfin_ref, tab_hbm, after_ref, o_ref, buf, sem, *,
                final_norm):
    step = pl.program_id(0)
    tb = xn_ref.shape[0]
    n_chunks = tab_hbm.shape[1]

    def issue(e_ref, t, slot):
        for k in range(N_SEL):
            e = e_ref[t, k]
            pltpu.make_async_copy(tab_hbm.at[e], buf.at[slot, pl.ds(k * n_chunks, n_chunks), :],
                                  sem.at[slot]).start(priority=k % 2)

    def wait(slot):
        pltpu.make_async_copy(buf.at[slot], buf.at[slot], sem.at[slot]).wait()

    @pl.when(step == 0)
    def _():
        issue(efirst_ref, 0, 0)

    rid = lax.broadcasted_iota(jnp.int32, (SUBLANES, 2 * N_SEL), 0)

    def group(gi, c):
        base = pl.multiple_of(gi * SUBLANES, SUBLANES)
        xrows = xn_ref[pl.ds(base, SUBLANES), :]
        grows = g2_ref[pl.ds(base, SUBLANES), :]
        acc = x1_ref[pl.ds(base, SUBLANES), :]
        for j in range(SUBLANES):
            slot = j % 2
            issue(enext_ref, base + j, 1 - slot)
            wait(slot)
            uv = jnp.concatenate(
                [pltpu.bitcast(buf[slot, pl.ds(c, N_SEL, stride=n_chunks), :], BF16) for c in range(n_chunks)],
                axis=1)
            s2 = lax.dot_general(xrows, uv, _NT, preferred_element_type=F32)
            act = pltpu.roll(_gelu_exact(s2), 1, axis=1)
            w2 = jnp.where(rid == j, grows * act, 0.0)
            acc = acc + jnp.dot(w2.astype(BF16), uv, preferred_element_type=F32)
        o_ref[pl.ds(base, SUBLANES), :] = acc
        return c

    lax.fori_loop(0, tb // SUBLANES, group, 0)

    @pl.when(step == pl.num_programs(0) - 1)
    def _():
        wait(0)

    if final_norm:
        o_ref[...] = _rms(o_ref[...]) * gfin_ref[...]


def _peer_mix(eids, e_next, gates2, xn, x1, gfin, table, after, row0, rows, final_norm, tb=64):
    d = x1.shape[1]
    assert tb % (2 * SUBLANES) == 0 and row0 % tb == 0 and rows % tb == 0
    blk0 = row0 // tb
    row = lambda w: pl.BlockSpec((tb, w), lambda i: (i + blk0, 0))
    return pl.pallas_call(
        functools.partial(_mix_kernel, final_norm=final_norm),
        out_shape=jax.ShapeDtypeStruct((rows, d), F32),
        grid=(rows // tb,),
        in_specs=[pl.BlockSpec((SUBLANES, N_SEL), lambda i: (row0 // SUBLANES, 0), memory_space=pltpu.SMEM),
                  pl.BlockSpec((tb, N_SEL), lambda i: (i + blk0, 0), memory_space=pltpu.SMEM),
                  row(2 * N_SEL), row(d), row(d),
                  pl.BlockSpec((1, d), lambda i: (0, 0)),
                  pl.BlockSpec(memory_space=pl.ANY),
                  pl.BlockSpec(memory_space=pl.ANY)],
        out_specs=pl.BlockSpec((tb, d), lambda i: (i, 0)),
        scratch_shapes=[pltpu.VMEM((2, N_SEL * (d // HEAD_W), HEAD_W), jnp.uint32),
                        pltpu.SemaphoreType.DMA((2,))],
        compiler_params=pltpu.CompilerParams(dimension_semantics=("arbitrary",),
                                             vmem_limit_bytes=VMEM_LIMIT),
        name="peer_mix",
    )(eids, e_next, gates2, xn, x1, gfin, table, after)


SC_LANES = 16
SC_TILES = 32
SC_TBLK = 16
SC_SLOTS = 4
HI_MASK = 0xFFFF0000
_SC_PARAMS = pltpu.CompilerParams(needs_layout_passes=False)


def _sc_unpack(word):
    return plsc.bitcast(word << 16, F32), plsc.bitcast(word & jnp.uint32(HI_MASK), F32)


def _sc_pipelined_heads(tab_hbm, ev, rows, sem, compute):
    n_items = SC_TBLK * PEER_HEADS
    ahead = SC_SLOTS - 1
    assert n_items % SC_SLOTS == 0

    def gather(it, slot):
        ti = it // PEER_HEADS
        h = it % PEER_HEADS
        return pltpu.make_async_copy(tab_hbm.at[ev.at[ti, pl.ds(h * PEER_TOPK, PEER_TOPK)]], rows.at[slot],
                                     sem.at[slot])

    for a in range(ahead):
        gather(a, a).start()

    def group(jj, carry):
        for slot in range(SC_SLOTS):
            it = SC_SLOTS * jj + slot
            gather(jnp.minimum(it + ahead, n_items - 1), (slot + ahead) % SC_SLOTS).start()
            gather(it, slot).wait()
            compute(it, slot)
        return carry

    lax.fori_loop(0, n_items // SC_SLOTS, group, 0)
    for a in range(ahead):
        gather(n_items - 1, (n_items + a) % SC_SLOTS).wait()


def _sc_scores(x, eids, utab, row0, rows):
    d = x.shape[1]
    dw = d // 2
    nw = dw // SC_LANES
    per_tile = rows // SC_TILES
    assert rows % (SC_TILES * SC_TBLK) == 0 and row0 % SC_TBLK == 0

    def body(x_hbm, e_hbm, tab_hbm, s_hbm, xv, ev, rbuf, sv, tmp, sem):
        tile = lax.axis_index("c") * 16 + lax.axis_index("s")
        lane = lax.iota(jnp.int32, SC_LANES)

        def compute(it, slot):
            ti = it // PEER_HEADS
            h = it % PEER_HEADS

            def chunk(w, accs):
                off = pl.multiple_of(w * SC_LANES, SC_LANES)
                xlo = xv[ti, pl.ds(off, SC_LANES)]
                xhi = xv[ti, pl.ds(dw + off, SC_LANES)]
                out = []
                for j in range(PEER_TOPK):
                    lo, hi = _sc_unpack(rbuf[slot, j, pl.ds(off, SC_LANES)])
                    out.append(accs[j] + lo * xlo + hi * xhi)
                return tuple(out)

            accs = lax.fori_loop(0, nw, chunk, tuple(jnp.zeros((SC_LANES,), F32) for _ in range(PEER_TOPK)))
            for j in range(PEER_TOPK):
                tmp[j, :] = accs[j]
            svec = jnp.zeros((SC_LANES,), F32)
            for l in range(SC_LANES):
                svec = svec + plsc.load_gather(tmp, [lane, jnp.full((SC_LANES,), l, jnp.int32)])
            sv[ti, pl.ds(pl.multiple_of(h * PEER_TOPK, PEER_TOPK), PEER_TOPK)] = svec

        def block(b, carry):
            t_out = pl.multiple_of(tile * per_tile + b * SC_TBLK, SC_TBLK)
            pltpu.sync_copy(x_hbm.at[pl.ds(row0 + t_out, SC_TBLK)], xv)
            pltpu.sync_copy(e_hbm.at[pl.ds(row0 + t_out, SC_TBLK)], ev)
            _sc_pipelined_heads(tab_hbm, ev, rbuf, sem, compute)
            pltpu.sync_copy(sv, s_hbm.at[pl.ds(t_out, SC_TBLK)])
            return carry

        lax.fori_loop(0, per_tile // SC_TBLK, block, 0)

    return pl.kernel(
        body,
        out_type=jax.ShapeDtypeStruct((rows, N_SEL), F32),
        mesh=plsc.VectorSubcoreMesh(core_axis_name="c", subcore_axis_name="s"),
        scratch_types=[pltpu.VMEM((SC_TBLK, d), F32), pltpu.VMEM((SC_TBLK, N_SEL), jnp.int32),
                       pltpu.VMEM((SC_SLOTS, PEER_TOPK, dw), jnp.uint32), pltpu.VMEM((SC_TBLK, N_SEL), F32),
                       pltpu.VMEM((PEER_TOPK, SC_LANES), F32), pltpu.SemaphoreType.DMA((SC_SLOTS,))],
        compiler_params=_SC_PARAMS,
        name="sc_scores",
    )(x, eids, utab)


def _sc_combine(w, eids, vtab, row0):
    rows = w.shape[0]
    dw = vtab.shape[1]
    d = 2 * dw
    nw = dw // SC_LANES
    per_tile = rows // SC_TILES
    assert rows % (SC_TILES * SC_TBLK) == 0 and row0 % SC_TBLK == 0

    def body(w_hbm, e_hbm, tab_hbm, y_hbm, wv, ev, rbuf, yv, sem):
        tile = lax.axis_index("c") * 16 + lax.axis_index("s")
        zero_i = jnp.zeros((SC_LANES,), jnp.int32)

        def compute(it, slot):
            ti = it // PEER_HEADS
            h = it % PEER_HEADS
            wk = [plsc.load_gather(wv, [zero_i + ti, zero_i + (h * PEER_TOPK + j)]) for j in range(PEER_TOPK)]

            def chunk(c, carry):
                off = pl.multiple_of(c * SC_LANES, SC_LANES)
                alo = jnp.zeros((SC_LANES,), F32)
                ahi = jnp.zeros((SC_LANES,), F32)
                for j in range(PEER_TOPK):
                    lo, hi = _sc_unpack(rbuf[slot, j, pl.ds(off, SC_LANES)])
                    alo = alo + wk[j] * lo
                    ahi = ahi + wk[j] * hi
                plsc.addupdate(yv.at[ti, pl.ds(off, SC_LANES)], alo)
                plsc.addupdate(yv.at[ti, pl.ds(dw + off, SC_LANES)], ahi)
                return carry

            plsc.parallel_loop(0, nw, 1, carry=jnp.int32(0))(chunk)

        def block(b, carry):
            t_out = pl.multiple_of(tile * per_tile + b * SC_TBLK, SC_TBLK)
            pltpu.sync_copy(w_hbm.at[pl.ds(t_out, SC_TBLK)], wv)
            pltpu.sync_copy(e_hbm.at[pl.ds(row0 + t_out, SC_TBLK)], ev)

            def zero(i, carry):
                for ti in range(SC_TBLK):
                    yv[ti, pl.ds(pl.multiple_of(i * SC_LANES, SC_LANES), SC_LANES)] = jnp.zeros((SC_LANES,), F32)
                return carry

            lax.fori_loop(0, d // SC_LANES, zero, 0)
            _sc_pipelined_heads(tab_hbm, ev, rbuf, sem, compute)
            pltpu.sync_copy(yv, y_hbm.at[pl.ds(t_out, SC_TBLK)])
            return carry

        lax.fori_loop(0, per_tile // SC_TBLK, block, 0)

    return pl.kernel(
        body,
        out_type=jax.ShapeDtypeStruct((rows, d), F32),
        mesh=plsc.VectorSubcoreMesh(core_axis_name="c", subcore_axis_name="s"),
        scratch_types=[pltpu.VMEM((SC_TBLK, N_SEL), F32), pltpu.VMEM((SC_TBLK, N_SEL), jnp.int32),
                       pltpu.VMEM((SC_SLOTS, PEER_TOPK, dw), jnp.uint32), pltpu.VMEM((SC_TBLK, d), F32),
                       pltpu.SemaphoreType.DMA((SC_SLOTS,))],
        compiler_params=_SC_PARAMS,
        name="sc_combine",
    )(w, eids, vtab)


def _gate_kernel(s_ref, g_ref, after_ref, w_ref):
    w_ref[...] = g_ref[...] * _gelu_exact(s_ref[...])


def _gate_act(s, gates, after, row0, tb=512):
    rows = s.shape[0]
    assert rows % tb == 0 and row0 % tb == 0
    blk0 = row0 // tb
    return pl.pallas_call(
        _gate_kernel,
        out_shape=jax.ShapeDtypeStruct((rows, N_SEL), F32),
        grid=(rows // tb,),
        in_specs=[pl.BlockSpec((tb, N_SEL), lambda i: (i, 0)),
                  pl.BlockSpec((tb, N_SEL), lambda i: (i + blk0, 0)),
                  pl.BlockSpec(memory_space=pl.ANY)],
        out_specs=pl.BlockSpec((tb, N_SEL), lambda i: (i, 0)),
        compiler_params=pltpu.CompilerParams(dimension_semantics=("parallel",)),
        name="peer_gate",
    )(s, gates, after)


def _resid_norm_kernel(x1_ref, y_ref, gfin_ref, after_ref, o_ref, *, final_norm):
    z = x1_ref[...] + y_ref[...]
    o_ref[...] = _rms(z) * gfin_ref[...] if final_norm else z


def _resid_norm(x1, y, gfin, after, final_norm, tb=512):
    rows, d = y.shape
    assert rows % tb == 0
    return pl.pallas_call(
        functools.partial(_resid_norm_kernel, final_norm=final_norm),
        out_shape=jax.ShapeDtypeStruct((rows, d), F32),
        grid=(rows // tb,),
        in_specs=[pl.BlockSpec((tb, d), lambda i: (i, 0)),
                  pl.BlockSpec((tb, d), lambda i: (i, 0)),
                  pl.BlockSpec((1, d), lambda i: (0, 0)),
                  pl.BlockSpec(memory_space=pl.ANY)],
        out_specs=pl.BlockSpec((tb, d), lambda i: (i, 0)),
        compiler_params=pltpu.CompilerParams(dimension_semantics=("parallel",)),
        name="peer_resid_norm",
    )(x1, y, gfin, after)


N_CHUNKS = 16
SC_TOKEN_FRAC = 52 / 64
SPLIT_UNIT = 512
COST_FRONT, COST_SC_SCORES, COST_SC_COMBINE, COST_TC_MIX = 0.13, 0.21, 0.18, 0.85


def _sc_rows(n, tokens_per_chunk):
    n_sc = int(n * SC_TOKEN_FRAC) // SPLIT_UNIT * SPLIT_UNIT
    return [min(max(n_sc - c * tokens_per_chunk, 0), tokens_per_chunk) for c in range(N_CHUNKS)]


def _tc_schedule(tokens_per_chunk, sc_rows):
    sc_chunks = [c for c in range(N_CHUNKS) if sc_rows[c]]
    base = [("front", c) for c in range(N_CHUNKS)] + [("mix", c) for c in range(N_CHUNKS)
                                                       if sc_rows[c] < tokens_per_chunk]
    order = []
    tc = sc = 0.0
    scores_done, combine_done = {}, {}
    gates = list(sc_chunks)
    norms = list(sc_chunks)
    for kind, c in base + [("end", None)]:
        while gates and (kind == "end" or scores_done.get(gates[0], float("inf"))
                         + 0.25 * COST_SC_SCORES * sc_rows[gates[0]] <= tc):
            g = gates.pop(0)
            tc = max(tc, scores_done[g])
            sc = max(sc, tc) + COST_SC_COMBINE * sc_rows[g]
            combine_done[g] = sc
            order.append(("gate", g))
        while norms and (kind == "end" or combine_done.get(norms[0], float("inf")) <= tc):
            order.append(("norm", norms.pop(0)))
        if kind == "front":
            tc += COST_FRONT * tokens_per_chunk
            if sc_rows[c]:
                sc = max(sc, tc) + COST_SC_SCORES * sc_rows[c]
                scores_done[c] = sc
        elif kind == "mix":
            tc += COST_TC_MIX * (tokens_per_chunk - sc_rows[c])
        if kind != "end":
            order.append((kind, c))
    return order


def _layer(x2, batch, seq, params, consts, li, final_norm):
    (g_mix, w_in, lam_p, g_sub, w_out, g_ffn, wq, sk, u, v, g_fin) = params
    intra, cross, kdec, cdec, slopes = consts
    n, d = x2.shape
    nc = n // N_CHUNKS
    bc = batch // N_CHUNKS
    assert batch % N_CHUNKS == 0
    table = _pack_tables(u, v)
    utab = _pack_halves(u)
    vtab = _pack_halves(v)

    sc_rows = _sc_rows(n, nc)
    chunks = [None] * N_CHUNKS
    scores = [None] * N_CHUNKS
    ys = [None] * N_CHUNKS
    outs_sc = [None] * N_CHUNKS
    outs_tc = [None] * N_CHUNKS
    last = g_fin

    def front(c, last):
        proj = _in_proj(x2, g_mix, w_in, last, c * nc, nc)
        proj3 = proj.reshape(bc, seq, proj.shape[1])
        ret = _retention(proj3, intra, cross, kdec, cdec)
        dif = _diff_attention(proj3, slopes, lam_p, g_sub, 1.0 - li)
        half = ret.shape[2]
        x1, xn, xn_f32, q = _out_proj(x2, ret.reshape(nc, half), dif.reshape(nc, half), w_out, g_ffn, wq, c * nc)
        eids, gates, gates2 = _peer_topk(q, sk)
        chunks[c] = (x1, xn, eids, gates, gates2)
        if sc_rows[c]:
            scores[c] = _sc_scores(xn_f32, eids, utab, 0, sc_rows[c])
        return gates2

    def gate(c, last):
        _, _, eids, gates, _ = chunks[c]
        w = _gate_act(scores[c], gates, last, 0)
        ys[c] = _sc_combine(w, eids, vtab, 0)
        return w

    def tc_mix(c, last):
        x1, xn, eids, _, gates2 = chunks[c]
        outs_tc[c] = _peer_mix(eids, jnp.roll(eids, -1, axis=0), gates2, xn, x1, g_fin, table, last,
                               sc_rows[c], nc - sc_rows[c], final_norm)
        return outs_tc[c]

    def norm(c, last):
        outs_sc[c] = _resid_norm(chunks[c][0], ys[c], g_fin, last, final_norm)
        return outs_sc[c]

    run = {"front": front, "gate": gate, "mix": tc_mix, "norm": norm}
    for kind, c in _tc_schedule(nc, sc_rows):
        last = run[kind](c, last)
    pieces = [o for c in range(N_CHUNKS) for o in (outs_sc[c], outs_tc[c]) if o is not None]
    return jnp.concatenate(pieces, axis=0)


def _retention_tables(dk):
    c = RET_CHUNK
    h = RET_HEADS
    log_gamma = jnp.log1p(-jnp.exp2(-5.0 - jnp.arange(h, dtype=F32)))
    idx = jnp.arange(c, dtype=F32)
    rel = idx[:, None] - idx[None, :]
    intra = jnp.where(rel >= 0, jnp.exp(log_gamma[:, None, None] * jnp.maximum(rel, 0.0)), 0.0)
    cross = jnp.exp(log_gamma[:, None] * (idx + 1.0))
    kdec = jnp.exp(log_gamma[:, None] * (c - 1.0 - idx))
    cdec = jnp.exp(log_gamma * c)
    scale = dk ** -0.5
    return intra * scale, cross[:, :, None], (kdec * scale)[:, :, None], cdec[:, None, None]


def kernel(x, norm_mix_g, w_in, diff_lambda, diff_subln_g, w_out, norm_ffn_g, peer_wq, peer_subkeys,
           peer_u, peer_v, final_norm_g):
    b, t, d = x.shape
    depth = w_in.shape[0]
    n = b * t
    x2 = x.reshape(n, d)
    consts = _retention_tables(HEAD_W) + (
        jnp.exp2(-8.0 * (jnp.arange(DIFF_HEADS, dtype=F32) + 1.0) / DIFF_HEADS),)
    for l in range(depth):
        li = 0.8 - 0.6 * math.exp(-0.3 * l)
        params = (norm_mix_g[l][None, :], w_in[l].astype(BF16), diff_lambda[l], diff_subln_g[l][None, :],
                  w_out[l].astype(BF16), norm_ffn_g[l][None, :], peer_wq[l].astype(BF16),
                  peer_subkeys[l].astype(BF16), peer_u[l], peer_v[l], final_norm_g[None, :])
        x2 = _layer(x2, b, t, params, consts, li, final_norm=(l == depth - 1))
    return x2.reshape(b, t, d)
```

```python
import functools
import math

import jax
import jax.numpy as jnp
import numpy as np
from jax import lax
from jax.experimental import pallas as pl
from jax.experimental.pallas import tpu as pltpu
from jax.experimental.pallas import tpu_sc as plsc

F32 = jnp.float32
BF16 = jnp.bfloat16

RMS_EPS = 1e-6
NEG_INF = -1e30

RET_HEADS = 4
RET_CHUNK = 128
DIFF_HEADS = 4
HEAD_W = 128
PEER_HEADS = 8
PEER_TOPK = 16
N_KEYS = 128
N_SEL = PEER_HEADS * PEER_TOPK

SUBLANES = 8
VMEM_LIMIT = 56 * 1024 * 1024

_NT = (((1,), (1,)), ((), ()))


def _rms(x, eps=RMS_EPS):
    return x * lax.rsqrt(jnp.mean(x * x, axis=-1, keepdims=True) + eps)


def _gelu_exact(x):
    return 0.5 * x * (1.0 + lax.erf(x * (2.0 ** -0.5)))


def _inproj_kernel(x_ref, g_ref, w_ref, after_ref, o_ref, *, col_chunk):
    h = (_rms(x_ref[...]) * g_ref[...]).astype(BF16)
    for j in range(o_ref.shape[1] // col_chunk):
        sl = slice(j * col_chunk, (j + 1) * col_chunk)
        o_ref[:, sl] = jnp.dot(h, w_ref[:, sl], preferred_element_type=F32).astype(o_ref.dtype)


def _in_proj(x2, g, w_bf16, after, row0, rows, tm=512):
    d = x2.shape[1]
    cols = w_bf16.shape[1]
    assert row0 % tm == 0 and rows % tm == 0
    blk0 = row0 // tm
    return pl.pallas_call(
        functools.partial(_inproj_kernel, col_chunk=512),
        out_shape=jax.ShapeDtypeStruct((rows, cols), BF16),
        grid=(rows // tm,),
        in_specs=[pl.BlockSpec((tm, d), lambda i: (i + blk0, 0)),
                  pl.BlockSpec((1, d), lambda i: (0, 0)),
                  pl.BlockSpec((d, cols), lambda i: (0, 0)),
                  pl.BlockSpec(memory_space=pl.ANY)],
        out_specs=pl.BlockSpec((tm, cols), lambda i: (i, 0)),
        compiler_params=pltpu.CompilerParams(dimension_semantics=("parallel",),
                                             vmem_limit_bytes=VMEM_LIMIT),
        name="in_proj",
    )(x2, g, w_bf16, after)


def _ret_kernel(q_ref, k_ref, v_ref, g_ref, intra_ref, cross_ref, kdec_ref, cdec_ref, o_ref):
    c = RET_CHUNK
    n_chunks = q_ref.shape[1] // c
    intra = intra_ref[0]
    cross = cross_ref[0]
    kdec = kdec_ref[0]
    cdec = cdec_ref[0]

    def body(n, s):
        sl = pl.ds(pl.multiple_of(n * c, c), c)
        qi = q_ref[0, sl, :]
        ki = k_ref[0, sl, :]
        vi = v_ref[0, sl, :]
        a = lax.dot_general(qi, ki, _NT, preferred_element_type=F32) * intra
        o = jnp.dot(a.astype(BF16), vi, preferred_element_type=F32)
        o = o + jnp.dot(qi, s.astype(BF16), preferred_element_type=F32) * cross
        kd_t = (ki.astype(F32) * kdec).T.astype(BF16)
        s = s * cdec + jnp.dot(kd_t, vi, preferred_element_type=F32)
        gate = g_ref[0, sl, :].astype(F32)
        o_ref[0, sl, :] = (gate * jax.nn.sigmoid(gate) * _rms(o)).astype(o_ref.dtype)
        return s

    lax.fori_loop(0, n_chunks, body, jnp.zeros((HEAD_W, HEAD_W), F32), unroll=8)


def _retention(proj3, intra, cross, kdec, cdec):
    b, t, _ = proj3.shape
    h = RET_HEADS

    def col(off):
        return pl.BlockSpec((1, t, HEAD_W), lambda bi, hi: (bi, 0, off + hi))

    def per_head(shape):
        return pl.BlockSpec((1,) + shape, lambda bi, hi: (hi, 0, 0))

    return pl.pallas_call(
        _ret_kernel,
        out_shape=jax.ShapeDtypeStruct((b, t, h * HEAD_W), BF16),
        grid=(b, h),
        in_specs=[col(0), col(h), col(2 * h), col(3 * h),
                  per_head((RET_CHUNK, RET_CHUNK)), per_head((RET_CHUNK, 1)),
                  per_head((RET_CHUNK, 1)), per_head((1, 1))],
        out_specs=pl.BlockSpec((1, t, HEAD_W), lambda bi, hi: (bi, 0, hi)),
        compiler_params=pltpu.CompilerParams(dimension_semantics=("parallel", "parallel"),
                                             vmem_limit_bytes=VMEM_LIMIT),
        name="retention",
    )(proj3, proj3, proj3, proj3, intra, cross, kdec, cdec)


def _diff_kernel(slope_ref, lam_ref, rel_ref, q_ref, k_ref, v_ref, gsub_ref, o_ref, *, tq, tk, out_scale):
    hi = pl.program_id(1)
    qi = pl.program_id(2)
    half = HEAD_W // 2
    slope = slope_ref[hi]

    lp = lam_ref[...]
    lam = (jnp.exp(jnp.sum(lp[0:1] * lp[1:2], axis=1, keepdims=True))
           - jnp.exp(jnp.sum(lp[2:3] * lp[3:4], axis=1, keepdims=True))
           + (1.0 - out_scale))

    q = q_ref[0] * jnp.asarray(half ** -0.5, BF16)
    lane = lax.broadcasted_iota(jnp.int32, q.shape, 1)
    zero = jnp.zeros_like(q)
    qs = jnp.concatenate([jnp.where(lane < half, q, zero),
                          jnp.where(lane >= half, q, zero)], axis=0)

    rel = rel_ref[...]
    rel_bias = -slope * rel

    def body(j, carry):
        m, l, acc = carry
        ksl = pl.ds(pl.multiple_of(j * tk, tk), tk)
        k = k_ref[0, ksl, :]
        v = v_ref[0, ksl, :]
        off = (qi * tq - j * tk).astype(F32)
        s = lax.dot_general(qs, k, _NT, preferred_element_type=F32)
        s = s + jnp.where(rel >= -off, rel_bias - slope * off, NEG_INF)
        m_new = jnp.maximum(m, jnp.max(s, axis=1, keepdims=True))
        alpha = jnp.exp(m - m_new)
        p = jnp.exp(s - m_new)
        l = alpha * l + jnp.sum(p, axis=1, keepdims=True)
        acc = alpha * acc + jnp.dot(p.astype(BF16), v, preferred_element_type=F32)
        return m_new, l, acc

    n_kv = (qi * tq + tq - 1) // tk + 1
    m0 = jnp.full((2 * tq, 1), NEG_INF, F32)
    l0 = jnp.zeros((2 * tq, 1), F32)
    a0 = jnp.zeros((2 * tq, HEAD_W), F32)
    _, l, acc = lax.fori_loop(0, n_kv, body, (m0, l0, a0))
    o = acc / l
    dy = o[:tq] - lam * o[tq:]
    o_ref[0] = (_rms(dy) * gsub_ref[...] * out_scale).astype(o_ref.dtype)


def _diff_attention(proj3, slopes, diff_lambda_l, gsub, out_scale, tq=256, tk=512):
    b, t, _ = proj3.shape
    h = DIFF_HEADS
    base = 4 * RET_HEADS
    kern = functools.partial(_diff_kernel, tq=tq, tk=tk, out_scale=out_scale)
    rel = jnp.asarray((np.arange(2 * tq) % tq)[:, None] - np.arange(tk)[None, :], F32)
    return pl.pallas_call(
        kern,
        out_shape=jax.ShapeDtypeStruct((b, t, h * HEAD_W), BF16),
        grid=(b, h, t // tq),
        in_specs=[pl.BlockSpec(memory_space=pltpu.SMEM),
                  pl.BlockSpec(diff_lambda_l.shape, lambda bi, hi, qi: (0, 0)),
                  pl.BlockSpec((2 * tq, tk), lambda bi, hi, qi: (0, 0)),
                  pl.BlockSpec((1, tq, HEAD_W), lambda bi, hi, qi: (bi, qi, base + hi)),
                  pl.BlockSpec((1, t, HEAD_W), lambda bi, hi, qi: (bi, 0, base + h + hi)),
                  pl.BlockSpec((1, t, HEAD_W), lambda bi, hi, qi: (bi, 0, base + 2 * h + hi)),
                  pl.BlockSpec((1, HEAD_W), lambda bi, hi, qi: (0, 0))],
        out_specs=pl.BlockSpec((1, tq, HEAD_W), lambda bi, hi, qi: (bi, qi, hi)),
        compiler_params=pltpu.CompilerParams(
            dimension_semantics=("parallel", "parallel", "arbitrary"),
            vmem_limit_bytes=VMEM_LIMIT),
        name="diff_attention",
    )(slopes, diff_lambda_l, rel, proj3, proj3, proj3, gsub)


def _outproj_kernel(x_ref, ret_ref, dif_ref, wo_ref, g_ref, wq_ref, x1_ref, xn_ref, xnf_ref, q_ref, *, col_chunk):
    half = ret_ref.shape[1]
    mix = jnp.dot(ret_ref[...], wo_ref[:half, :], preferred_element_type=F32)
    mix = mix + jnp.dot(dif_ref[...], wo_ref[half:, :], preferred_element_type=F32)
    x1 = x_ref[...] + mix
    x1_ref[...] = x1
    xn = (_rms(x1) * g_ref[...]).astype(BF16)
    xn_ref[...] = xn
    xnf_ref[...] = xn.astype(F32)
    for j in range(q_ref.shape[1] // col_chunk):
        sl = slice(j * col_chunk, (j + 1) * col_chunk)
        q_ref[:, sl] = jnp.dot(xn, wq_ref[:, sl], preferred_element_type=F32).astype(q_ref.dtype)


def _out_proj(x2, ret2, dif2, wo_bf16, g, wq_bf16, row0, tm=512):
    n, half = ret2.shape
    d = x2.shape[1]
    qc = wq_bf16.shape[1]
    assert row0 % tm == 0 and n % tm == 0
    blk0 = row0 // tm
    row = lambda w: pl.BlockSpec((tm, w), lambda i: (i, 0))
    full = lambda a: pl.BlockSpec(a.shape, lambda i: (0, 0))
    return pl.pallas_call(
        functools.partial(_outproj_kernel, col_chunk=512),
        out_shape=(jax.ShapeDtypeStruct((n, d), F32),
                   jax.ShapeDtypeStruct((n, d), BF16),
                   jax.ShapeDtypeStruct((n, d), F32),
                   jax.ShapeDtypeStruct((n, qc), BF16)),
        grid=(n // tm,),
        in_specs=[pl.BlockSpec((tm, d), lambda i: (i + blk0, 0)), row(half), row(half),
                  full(wo_bf16), full(g), full(wq_bf16)],
        out_specs=(row(d), row(d), row(d), row(qc)),
        compiler_params=pltpu.CompilerParams(dimension_semantics=("parallel",),
                                             vmem_limit_bytes=VMEM_LIMIT),
        name="out_proj",
    )(x2, ret2, dif2, wo_bf16, g, wq_bf16)


def _split3(v):
    trunc = lambda x: pltpu.bitcast(pltpu.bitcast(x, jnp.uint32) & jnp.uint32(0xFFFF0000), F32)
    hi = trunc(v)
    r1 = v - hi
    mid = trunc(r1)
    return hi.astype(BF16), mid.astype(BF16), (r1 - mid).astype(BF16)


def _spread_exact(v, sel_bf16):
    hi, mid, lo = _split3(v)
    d = lambda a: jnp.dot(a, sel_bf16, preferred_element_type=F32)
    return (d(hi) + d(mid)) + d(lo)


def _spread_exact_left(sel_bf16, v):
    hi, mid, lo = _split3(v)
    d = lambda a: jnp.dot(sel_bf16, a, preferred_element_type=F32)
    return (d(hi) + d(mid)) + d(lo)


def _topk_kernel(q_ref, sk_ref, sel_ref, pad_ref, e_ref, g_ref, g2_ref):
    tb = q_ref.shape[0]
    k = PEER_TOPK
    n_sets = 2 * PEER_HEADS
    neg = -jnp.inf

    s = jnp.stack(
        [lax.dot_general(sk_ref[j // 2, j % 2], q_ref[:, j * N_KEYS:(j + 1) * N_KEYS], _NT,
                         preferred_element_type=F32) for j in range(n_sets)], axis=0)
    key_f = lax.broadcasted_iota(jnp.int32, s.shape, 1).astype(F32)
    rank = lax.broadcasted_iota(jnp.int32, (n_sets, k, tb), 1)
    top_v = jnp.zeros((n_sets, k, tb), F32)
    top_i = jnp.zeros((n_sets, k, tb), F32)
    for r in range(k):
        m = jnp.max(s, axis=1, keepdims=True)
        i = jnp.min(jnp.where(s == m, key_f, float(N_KEYS)), axis=1, keepdims=True)
        s = jnp.where(key_f == i, neg, s)
        top_v = jnp.where(rank == r, m, top_v)
        top_i = jnp.where(rank == r, i, top_i)

    pad = pad_ref[...]
    cands, eids = [], []
    for h in range(PEER_HEADS):
        cands.append(_spread_exact_left(sel_ref[0], top_v[2 * h]) + _spread_exact_left(sel_ref[1], top_v[2 * h + 1])
                     + pad)
        eids.append(jnp.dot(sel_ref[0], top_i[2 * h].astype(BF16), preferred_element_type=F32) * float(N_KEYS)
                    + jnp.dot(sel_ref[1], top_i[2 * h + 1].astype(BF16), preferred_element_type=F32))
    cand = jnp.stack(cands, axis=0)
    eid = jnp.stack(eids, axis=0)
    n_cand = cand.shape[1]

    c_f = lax.broadcasted_iota(jnp.int32, cand.shape, 1).astype(F32)
    rank2 = lax.broadcasted_iota(jnp.int32, (PEER_HEADS, k, tb), 1)
    top_c = []
    e_rank = jnp.zeros((PEER_HEADS, k, tb), F32)
    for r in range(k):
        m = jnp.max(cand, axis=1, keepdims=True)
        pos = jnp.min(jnp.where(cand == m, c_f, float(n_cand)), axis=1, keepdims=True)
        hit = c_f == pos
        e_rank = jnp.where(rank2 == r, jnp.sum(jnp.where(hit, eid, 0.0), axis=1, keepdims=True), e_rank)
        cand = jnp.where(hit, neg, cand)
        top_c.append(m)
    p_rank = jnp.zeros((PEER_HEADS, k, tb), F32)
    denom = jnp.zeros((PEER_HEADS, 1, tb), F32)
    for r in range(k):
        p = jnp.exp(top_c[r] - top_c[0])
        denom = denom + p
        p_rank = jnp.where(rank2 == r, p, p_rank)
    g_rank = p_rank / denom

    e_out = e_rank.reshape(PEER_HEADS * k, tb).T
    g1 = g_rank.reshape(PEER_HEADS * k, tb).T
    sr = lax.broadcasted_iota(jnp.int32, (N_SEL, 2 * N_SEL), 0)
    sc = lax.broadcasted_iota(jnp.int32, (N_SEL, 2 * N_SEL), 1)
    e_ref[...] = e_out.astype(jnp.int32)
    g_ref[...] = g1
    g2_ref[...] = _spread_exact(g1, jnp.where(sc == 2 * sr + 1, 1.0, 0.0).astype(BF16))


def _candidate_tables(tb):
    k = PEER_TOPK
    pairs = [(a, b) for a in range(k) for b in range(k) if (a + 1) * (b + 1) <= k]
    n_cand = -(-len(pairs) // SUBLANES) * SUBLANES
    sel = np.zeros((2, n_cand, k), np.float32)
    pad = np.full((n_cand, tb), -np.inf, np.float32)
    for c, (a, b) in enumerate(pairs):
        sel[0, c, a] = 1.0
        sel[1, c, b] = 1.0
        pad[c, :] = 0.0
    return jnp.asarray(sel, BF16), jnp.asarray(pad)


def _peer_topk(q2, sk_bf16, tb=128):
    n, qc = q2.shape
    sel, pad = _candidate_tables(tb)
    const = lambda a: pl.BlockSpec(a.shape, lambda i: (0,) * a.ndim)
    return pl.pallas_call(
        _topk_kernel,
        out_shape=(jax.ShapeDtypeStruct((n, N_SEL), jnp.int32),
                   jax.ShapeDtypeStruct((n, N_SEL), F32),
                   jax.ShapeDtypeStruct((n, 2 * N_SEL), F32)),
        grid=(n // tb,),
        in_specs=[pl.BlockSpec((tb, qc), lambda i: (i, 0)), const(sk_bf16), const(sel), const(pad)],
        out_specs=(pl.BlockSpec((tb, N_SEL), lambda i: (i, 0)),
                   pl.BlockSpec((tb, N_SEL), lambda i: (i, 0)),
                   pl.BlockSpec((tb, 2 * N_SEL), lambda i: (i, 0))),
        compiler_params=pltpu.CompilerParams(dimension_semantics=("parallel",),
                                             vmem_limit_bytes=VMEM_LIMIT),
        name="peer_topk",
    )(q2, sk_bf16, sel, pad)


def _bf16_bits(t):
    return lax.bitcast_convert_type(t.astype(BF16), jnp.uint16).astype(jnp.uint32)


def _pack_tables(u, v):
    e, d = u.shape
    return (_bf16_bits(u) | (_bf16_bits(v) << 16)).reshape(e, d // HEAD_W, HEAD_W)


def _pack_halves(t):
    b = _bf16_bits(t)
    half = t.shape[1] // 2
    return b[:, :half] | (b[:, half:] << 16)


def _mix_kernel(efirst_ref, enext_ref, g2_ref, xn_ref, x1_ref, gfin_ref, tab_hbm, after_ref, o_ref, buf, sem, *,
                final_norm):
    step = pl.program_id(0)
    tb = xn_ref.shape[0]
    n_chunks = tab_hbm.shape[1]

    def issue(e_ref, t, slot):
        for k in range(N_SEL):
            e = e_ref[t, k]
            pltpu.make_async_copy(tab_hbm.at[e], buf.at[slot, pl.ds(k * n_chunks, n_chunks), :],
                                  sem.at[slot]).start(priority=k % 2)

    def wait(slot):
        pltpu.make_async_copy(buf.at[slot], buf.at[slot], sem.at[slot]).wait()

    @pl.when(step == 0)
    def _():
        issue(efirst_ref, 0, 0)

    rid = lax.broadcasted_iota(jnp.int32, (SUBLANES, 2 * N_SEL), 0)

    def group(gi, c):
        base = pl.multiple_of(gi * SUBLANES, SUBLANES)
        xrows = xn_ref[pl.ds(base, SUBLANES), :]
        grows = g2_ref[pl.ds(base, SUBLANES), :]
        acc = x1_ref[pl.ds(base, SUBLANES), :]
        for j in range(SUBLANES):
            slot = j % 2
            issue(enext_ref, base + j, 1 - slot)
            wait(slot)
            uv = jnp.concatenate(
                [pltpu.bitcast(buf[slot, pl.ds(c, N_SEL, stride=n_chunks), :], BF16) for c in range(n_chunks)],
                axis=1)
            s2 = lax.dot_general(xrows, uv, _NT, preferred_element_type=F32)
            act = pltpu.roll(_gelu_exact(s2), 1, axis=1)
            w2 = jnp.where(rid == j, grows * act, 0.0)
            acc = acc + jnp.dot(w2.astype(BF16), uv, preferred_element_type=F32)
        o_ref[pl.ds(base, SUBLANES), :] = acc
        return c

    lax.fori_loop(0, tb // SUBLANES, group, 0)

    @pl.when(step == pl.num_programs(0) - 1)
    def _():
        wait(0)

    if final_norm:
        o_ref[...] = _rms(o_ref[...]) * gfin_ref[...]


def _peer_mix(eids, e_next, gates2, xn, x1, gfin, table, after, row0, rows, final_norm, tb=64):
    d = x1.shape[1]
    assert tb % (2 * SUBLANES) == 0 and row0 % tb == 0 and rows % tb == 0
    blk0 = row0 // tb
    row = lambda w: pl.BlockSpec((tb, w), lambda i: (i + blk0, 0))
    return pl.pallas_call(
        functools.partial(_mix_kernel, final_norm=final_norm),
        out_shape=jax.ShapeDtypeStruct((rows, d), F32),
        grid=(rows // tb,),
        in_specs=[pl.BlockSpec((SUBLANES, N_SEL), lambda i: (row0 // SUBLANES, 0), memory_space=pltpu.SMEM),
                  pl.BlockSpec((tb, N_SEL), lambda i: (i + blk0, 0), memory_space=pltpu.SMEM),
                  row(2 * N_SEL), row(d), row(d),
                  pl.BlockSpec((1, d), lambda i: (0, 0)),
                  pl.BlockSpec(memory_space=pl.ANY),
                  pl.BlockSpec(memory_space=pl.ANY)],
        out_specs=pl.BlockSpec((tb, d), lambda i: (i, 0)),
        scratch_shapes=[pltpu.VMEM((2, N_SEL * (d // HEAD_W), HEAD_W), jnp.uint32),
                        pltpu.SemaphoreType.DMA((2,))],
        compiler_params=pltpu.CompilerParams(dimension_semantics=("arbitrary",),
                                             vmem_limit_bytes=VMEM_LIMIT),
        name="peer_mix",
    )(eids, e_next, gates2, xn, x1, gfin, table, after)


SC_LANES = 16
SC_TILES = 32
SC_TBLK = 32
SC_SLOTS = 4
HI_MASK = 0xFFFF0000
_SC_PARAMS = pltpu.CompilerParams(needs_layout_passes=False)


def _sc_unpack(word):
    return plsc.bitcast(word << 16, F32), plsc.bitcast(word & jnp.uint32(HI_MASK), F32)


def _sc_pipelined_heads(tab_hbm, ev, rows, sem, compute):
    n_items = SC_TBLK * PEER_HEADS
    ahead = SC_SLOTS - 1
    assert n_items % SC_SLOTS == 0

    def gather(it, slot):
        ti = it // PEER_HEADS
        h = it % PEER_HEADS
        return pltpu.make_async_copy(tab_hbm.at[ev.at[ti, pl.ds(h * PEER_TOPK, PEER_TOPK)]], rows.at[slot],
                                     sem.at[slot])

    for a in range(ahead):
        gather(a, a).start()

    def group(jj, carry):
        for slot in range(SC_SLOTS):
            it = SC_SLOTS * jj + slot
            gather(jnp.minimum(it + ahead, n_items - 1), (slot + ahead) % SC_SLOTS).start()
            gather(it, slot).wait()
            compute(it, slot)
        return carry

    lax.fori_loop(0, n_items // SC_SLOTS, group, 0)
    for a in range(ahead):
        gather(n_items - 1, (n_items + a) % SC_SLOTS).wait()


def _sc_scores(x, eids, utab, row0, rows):
    d = x.shape[1]
    dw = d // 2
    nw = dw // SC_LANES
    per_tile = rows // SC_TILES
    assert rows % (SC_TILES * SC_TBLK) == 0 and row0 % SC_TBLK == 0

    def body(x_hbm, e_hbm, tab_hbm, s_hbm, xv, ev, rbuf, sv, tmp, sem):
        tile = lax.axis_index("c") * 16 + lax.axis_index("s")
        lane = lax.iota(jnp.int32, SC_LANES)

        def compute(it, slot):
            ti = it // PEER_HEADS
            h = it % PEER_HEADS

            def chunk(w, accs):
                off = pl.multiple_of(w * SC_LANES, SC_LANES)
                xlo = xv[ti, pl.ds(off, SC_LANES)]
                xhi = xv[ti, pl.ds(dw + off, SC_LANES)]
                out = []
                for j in range(PEER_TOPK):
                    lo, hi = _sc_unpack(rbuf[slot, j, pl.ds(off, SC_LANES)])
                    out.append(accs[j] + lo * xlo + hi * xhi)
                return tuple(out)

            accs = lax.fori_loop(0, nw, chunk, tuple(jnp.zeros((SC_LANES,), F32) for _ in range(PEER_TOPK)))
            for j in range(PEER_TOPK):
                tmp[j, :] = accs[j]
            svec = jnp.zeros((SC_LANES,), F32)
            for l in range(SC_LANES):
                svec = svec + plsc.load_gather(tmp, [lane, jnp.full((SC_LANES,), l, jnp.int32)])
            sv[ti, pl.ds(pl.multiple_of(h * PEER_TOPK, PEER_TOPK), PEER_TOPK)] = svec

        def block(b, carry):
            t_out = pl.multiple_of(tile * per_tile + b * SC_TBLK, SC_TBLK)
            pltpu.sync_copy(x_hbm.at[pl.ds(row0 + t_out, SC_TBLK)], xv)
            pltpu.sync_copy(e_hbm.at[pl.ds(row0 + t_out, SC_TBLK)], ev)
            _sc_pipelined_heads(tab_hbm, ev, rbuf, sem, compute)
            pltpu.sync_copy(sv, s_hbm.at[pl.ds(t_out, SC_TBLK)])
            return carry

        lax.fori_loop(0, per_tile // SC_TBLK, block, 0)

    return pl.kernel(
        body,
        out_type=jax.ShapeDtypeStruct((rows, N_SEL), F32),
        mesh=plsc.VectorSubcoreMesh(core_axis_name="c", subcore_axis_name="s"),
        scratch_types=[pltpu.VMEM((SC_TBLK, d), F32), pltpu.VMEM((SC_TBLK, N_SEL), jnp.int32),
                       pltpu.VMEM((SC_SLOTS, PEER_TOPK, dw), jnp.uint32), pltpu.VMEM((SC_TBLK, N_SEL), F32),
                       pltpu.VMEM((PEER_TOPK, SC_LANES), F32), pltpu.SemaphoreType.DMA((SC_SLOTS,))],
        compiler_params=_SC_PARAMS,
        name="sc_scores",
    )(x, eids, utab)


def _sc_combine(w, eids, vtab, row0):
    rows = w.shape[0]
    dw = vtab.shape[1]
    d = 2 * dw
    nw = dw // SC_LANES
    per_tile = rows // SC_TILES
    assert rows % (SC_TILES * SC_TBLK) == 0 and row0 % SC_TBLK == 0

    def body(w_hbm, e_hbm, tab_hbm, y_hbm, wv, ev, rbuf, yv, sem):
        tile = lax.axis_index("c") * 16 + lax.axis_index("s")
        zero_i = jnp.zeros((SC_LANES,), jnp.int32)

        def compute(it, slot):
            ti = it // PEER_HEADS
            h = it % PEER_HEADS
            wk = [plsc.load_gather(wv, [zero_i + ti, zero_i + (h * PEER_TOPK + j)]) for j in range(PEER_TOPK)]

            def chunk(c, carry):
                off = pl.multiple_of(c * SC_LANES, SC_LANES)
                alo = jnp.zeros((SC_LANES,), F32)
                ahi = jnp.zeros((SC_LANES,), F32)
                for j in range(PEER_TOPK):
                    lo, hi = _sc_unpack(rbuf[slot, j, pl.ds(off, SC_LANES)])
                    alo = alo + wk[j] * lo
                    ahi = ahi + wk[j] * hi
                plsc.addupdate(yv.at[ti, pl.ds(off, SC_LANES)], alo)
                plsc.addupdate(yv.at[ti, pl.ds(dw + off, SC_LANES)], ahi)
                return carry

            plsc.parallel_loop(0, nw, 1, carry=jnp.int32(0))(chunk)

        def block(b, carry):
            t_out = pl.multiple_of(tile * per_tile + b * SC_TBLK, SC_TBLK)
            pltpu.sync_copy(w_hbm.at[pl.ds(t_out, SC_TBLK)], wv)
            pltpu.sync_copy(e_hbm.at[pl.ds(row0 + t_out, SC_TBLK)], ev)

            def zero(i, carry):
                for ti in range(SC_TBLK):
                    yv[ti, pl.ds(pl.multiple_of(i * SC_LANES, SC_LANES), SC_LANES)] = jnp.zeros((SC_LANES,), F32)
                return carry

            lax.fori_loop(0, d // SC_LANES, zero, 0)
            _sc_pipelined_heads(tab_hbm, ev, rbuf, sem, compute)
            pltpu.sync_copy(yv, y_hbm.at[pl.ds(t_out, SC_TBLK)])
            return carry

        lax.fori_loop(0, per_tile // SC_TBLK, block, 0)

    return pl.kernel(
        body,
        out_type=jax.ShapeDtypeStruct((rows, d), F32),
        mesh=plsc.VectorSubcoreMesh(core_axis_name="c", subcore_axis_name="s"),
        scratch_types=[pltpu.VMEM((SC_TBLK, N_SEL), F32), pltpu.VMEM((SC_TBLK, N_SEL), jnp.int32),
                       pltpu.VMEM((SC_SLOTS, PEER_TOPK, dw), jnp.uint32), pltpu.VMEM((SC_TBLK, d), F32),
                       pltpu.SemaphoreType.DMA((SC_SLOTS,))],
        compiler_params=_SC_PARAMS,
        name="sc_combine",
    )(w, eids, vtab)


def _gate_kernel(s_ref, g_ref, after_ref, w_ref):
    w_ref[...] = g_ref[...] * _gelu_exact(s_ref[...])


def _gate_act(s, gates, after, row0, tb=512):
    rows = s.shape[0]
    assert rows % tb == 0 and row0 % tb == 0
    blk0 = row0 // tb
    return pl.pallas_call(
        _gate_kernel,
        out_shape=jax.ShapeDtypeStruct((rows, N_SEL), F32),
        grid=(rows // tb,),
        in_specs=[pl.BlockSpec((tb, N_SEL), lambda i: (i, 0)),
                  pl.BlockSpec((tb, N_SEL), lambda i: (i + blk0, 0)),
                  pl.BlockSpec(memory_space=pl.ANY)],
        out_specs=pl.BlockSpec((tb, N_SEL), lambda i: (i, 0)),
        compiler_params=pltpu.CompilerParams(dimension_semantics=("parallel",)),
        name="peer_gate",
    )(s, gates, after)


def _resid_norm_kernel(x1_ref, y_ref, gfin_ref, after_ref, o_ref, *, final_norm):
    z = x1_ref[...] + y_ref[...]
    o_ref[...] = _rms(z) * gfin_ref[...] if final_norm else z


def _resid_norm(x1, y, gfin, after, final_norm, tb=512):
    rows, d = y.shape
    assert rows % tb == 0
    return pl.pallas_call(
        functools.partial(_resid_norm_kernel, final_norm=final_norm),
        out_shape=jax.ShapeDtypeStruct((rows, d), F32),
        grid=(rows // tb,),
        in_specs=[pl.BlockSpec((tb, d), lambda i: (i, 0)),
                  pl.BlockSpec((tb, d), lambda i: (i, 0)),
                  pl.BlockSpec((1, d), lambda i: (0, 0)),
                  pl.BlockSpec(memory_space=pl.ANY)],
        out_specs=pl.BlockSpec((tb, d), lambda i: (i, 0)),
        compiler_params=pltpu.CompilerParams(dimension_semantics=("parallel",)),
        name="peer_resid_norm",
    )(x1, y, gfin, after)


N_CHUNKS = 16
SC_TOKEN_FRAC = 52 / 64
SPLIT_UNIT = SC_TILES * SC_TBLK
COST_FRONT, COST_SC_SCORES, COST_SC_COMBINE, COST_TC_MIX = 0.13, 0.21, 0.18, 0.85


def _sc_rows(n, tokens_per_chunk):
    n_sc = int(n * SC_TOKEN_FRAC) // SPLIT_UNIT * SPLIT_UNIT
    return [min(max(n_sc - c * tokens_per_chunk, 0), tokens_per_chunk) for c in range(N_CHUNKS)]


def _tc_schedule(tokens_per_chunk, sc_rows):
    sc_chunks = [c for c in range(N_CHUNKS) if sc_rows[c]]
    base = [("front", c) for c in range(N_CHUNKS)] + [("mix", c) for c in range(N_CHUNKS)
                                                       if sc_rows[c] < tokens_per_chunk]
    order = []
    tc = sc = 0.0
    scores_done, combine_done = {}, {}
    gates = list(sc_chunks)
    norms = list(sc_chunks)
    for kind, c in base + [("end", None)]:
        while gates and (kind == "end" or scores_done.get(gates[0], float("inf"))
                         + 0.25 * COST_SC_SCORES * sc_rows[gates[0]] <= tc):
            g = gates.pop(0)
            tc = max(tc, scores_done[g])
            sc = max(sc, tc) + COST_SC_COMBINE * sc_rows[g]
            combine_done[g] = sc
            order.append(("gate", g))
        while norms and (kind == "end" or combine_done.get(norms[0], float("inf")) <= tc):
            order.append(("norm", norms.pop(0)))
        if kind == "front":
            tc += COST_FRONT * tokens_per_chunk
            if sc_rows[c]:
                sc = max(sc, tc) + COST_SC_SCORES * sc_rows[c]
                scores_done[c] = sc
        elif kind == "mix":
            tc += COST_TC_MIX * (tokens_per_chunk - sc_rows[c])
        if kind != "end":
            order.append((kind, c))
    return order


def _layer(x2, batch, seq, params, consts, li, final_norm):
    (g_mix, w_in, lam_p, g_sub, w_out, g_ffn, wq, sk, u, v, g_fin) = params
    intra, cross, kdec, cdec, slopes = consts
    n, d = x2.shape
    nc = n // N_CHUNKS
    bc = batch // N_CHUNKS
    assert batch % N_CHUNKS == 0
    table = _pack_tables(u, v)
    utab = _pack_halves(u)
    vtab = _pack_halves(v)

    sc_rows = _sc_rows(n, nc)
    chunks = [None] * N_CHUNKS
    scores = [None] * N_CHUNKS
    ys = [None] * N_CHUNKS
    outs_sc = [None] * N_CHUNKS
    outs_tc = [None] * N_CHUNKS
    last = g_fin

    def front(c, last):
        proj = _in_proj(x2, g_mix, w_in, last, c * nc, nc)
        proj3 = proj.reshape(bc, seq, proj.shape[1])
        ret = _retention(proj3, intra, cross, kdec, cdec)
        dif = _diff_attention(proj3, slopes, lam_p, g_sub, 1.0 - li)
        half = ret.shape[2]
        x1, xn, xn_f32, q = _out_proj(x2, ret.reshape(nc, half), dif.reshape(nc, half), w_out, g_ffn, wq, c * nc)
        eids, gates, gates2 = _peer_topk(q, sk)
        chunks[c] = (x1, xn, eids, gates, gates2)
        if sc_rows[c]:
            scores[c] = _sc_scores(xn_f32, eids, utab, 0, sc_rows[c])
        return gates2

    def gate(c, last):
        _, _, eids, gates, _ = chunks[c]
        w = _gate_act(scores[c], gates, last, 0)
        ys[c] = _sc_combine(w, eids, vtab, 0)
        return w

    def tc_mix(c, last):
        x1, xn, eids, _, gates2 = chunks[c]
        outs_tc[c] = _peer_mix(eids, jnp.roll(eids, -1, axis=0), gates2, xn, x1, g_fin, table, last,
                               sc_rows[c], nc - sc_rows[c], final_norm)
        return outs_tc[c]

    def norm(c, last):
        outs_sc[c] = _resid_norm(chunks[c][0], ys[c], g_fin, last, final_norm)
        return outs_sc[c]

    run = {"front": front, "gate": gate, "mix": tc_mix, "norm": norm}
    for kind, c in _tc_schedule(nc, sc_rows):
        last = run[kind](c, last)
    pieces = [o for c in range(N_CHUNKS) for o in (outs_sc[c], outs_tc[c]) if o is not None]
    return jnp.concatenate(pieces, axis=0)


def _retention_tables(dk):
    c = RET_CHUNK
    h = RET_HEADS
    log_gamma = jnp.log1p(-jnp.exp2(-5.0 - jnp.arange(h, dtype=F32)))
    idx = jnp.arange(c, dtype=F32)
    rel = idx[:, None] - idx[None, :]
    intra = jnp.where(rel >= 0, jnp.exp(log_gamma[:, None, None] * jnp.maximum(rel, 0.0)), 0.0)
    cross = jnp.exp(log_gamma[:, None] * (idx + 1.0))
    kdec = jnp.exp(log_gamma[:, None] * (c - 1.0 - idx))
    cdec = jnp.exp(log_gamma * c)
    scale = dk ** -0.5
    return intra * scale, cross[:, :, None], (kdec * scale)[:, :, None], cdec[:, None, None]


def kernel(x, norm_mix_g, w_in, diff_lambda, diff_subln_g, w_out, norm_ffn_g, peer_wq, peer_subkeys,
           peer_u, peer_v, final_norm_g):
    b, t, d = x.shape
    depth = w_in.shape[0]
    n = b * t
    x2 = x.reshape(n, d)
    consts = _retention_tables(HEAD_W) + (
        jnp.exp2(-8.0 * (jnp.arange(DIFF_HEADS, dtype=F32) + 1.0) / DIFF_HEADS),)
    for l in range(depth):
        li = 0.8 - 0.6 * math.exp(-0.3 * l)
        params = (norm_mix_g[l][None, :], w_in[l].astype(BF16), diff_lambda[l], diff_subln_g[l][None, :],
                  w_out[l].astype(BF16), norm_ffn_g[l][None, :], peer_wq[l].astype(BF16),
                  peer_subkeys[l].astype(BF16), peer_u[l], peer_v[l], final_norm_g[None, :])
        x2 = _layer(x2, b, t, params, consts, li, final_norm=(l == depth - 1))
    return x2.reshape(b, t, d)
```

```python
import functools
import math

import jax
import jax.numpy as jnp
import numpy as np
from jax import lax
from jax.experimental import pallas as pl
from jax.experimental.pallas import tpu as pltpu
from jax.experimental.pallas import tpu_sc as plsc

F32 = jnp.float32
BF16 = jnp.bfloat16

RMS_EPS = 1e-6
NEG_INF = -1e30

RET_HEADS = 4
RET_CHUNK = 128
DIFF_HEADS = 4
HEAD_W = 128
PEER_HEADS = 8
PEER_TOPK = 16
N_KEYS = 128
N_SEL = PEER_HEADS * PEER_TOPK

SUBLANES = 8
VMEM_LIMIT = 56 * 1024 * 1024

_NT = (((1,), (1,)), ((), ()))


def _rms(x, eps=RMS_EPS):
    return x * lax.rsqrt(jnp.mean(x * x, axis=-1, keepdims=True) + eps)


def _gelu_exact(x):
    return 0.5 * x * (1.0 + lax.erf(x * (2.0 ** -0.5)))


def _inproj_kernel(x_ref, g_ref, w_ref, after_ref, o_ref, *, col_chunk):
    h = (_rms(x_ref[...]) * g_ref[...]).astype(BF16)
    for j in range(o_ref.shape[1] // col_chunk):
        sl = slice(j * col_chunk, (j + 1) * col_chunk)
        o_ref[:, sl] = jnp.dot(h, w_ref[:, sl], preferred_element_type=F32).astype(o_ref.dtype)


def _in_proj(x2, g, w_bf16, after, row0, rows, tm=512):
    d = x2.shape[1]
    cols = w_bf16.shape[1]
    assert row0 % tm == 0 and rows % tm == 0
    blk0 = row0 // tm
    return pl.pallas_call(
        functools.partial(_inproj_kernel, col_chunk=512),
        out_shape=jax.ShapeDtypeStruct((rows, cols), BF16),
        grid=(rows // tm,),
        in_specs=[pl.BlockSpec((tm, d), lambda i: (i + blk0, 0)),
                  pl.BlockSpec((1, d), lambda i: (0, 0)),
                  pl.BlockSpec((d, cols), lambda i: (0, 0)),
                  pl.BlockSpec(memory_space=pl.ANY)],
        out_specs=pl.BlockSpec((tm, cols), lambda i: (i, 0)),
        compiler_params=pltpu.CompilerParams(dimension_semantics=("parallel",),
                                             vmem_limit_bytes=VMEM_LIMIT),
        name="in_proj",
    )(x2, g, w_bf16, after)


def _ret_kernel(q_ref, k_ref, v_ref, g_ref, intra_ref, cross_ref, kdec_ref, cdec_ref, o_ref):
    c = RET_CHUNK
    n_chunks = q_ref.shape[1] // c
    intra = intra_ref[0]
    cross = cross_ref[0]
    kdec = kdec_ref[0]
    cdec = cdec_ref[0]

    def body(n, s):
        sl = pl.ds(pl.multiple_of(n * c, c), c)
        qi = q_ref[0, sl, :]
        ki = k_ref[0, sl, :]
        vi = v_ref[0, sl, :]
        a = lax.dot_general(qi, ki, _NT, preferred_element_type=F32) * intra
        o = jnp.dot(a.astype(BF16), vi, preferred_element_type=F32)
        o = o + jnp.dot(qi, s.astype(BF16), preferred_element_type=F32) * cross
        kd_t = (ki.astype(F32) * kdec).T.astype(BF16)
        s = s * cdec + jnp.dot(kd_t, vi, preferred_element_type=F32)
        gate = g_ref[0, sl, :].astype(F32)
        o_ref[0, sl, :] = (gate * jax.nn.sigmoid(gate) * _rms(o)).astype(o_ref.dtype)
        return s

    lax.fori_loop(0, n_chunks, body, jnp.zeros((HEAD_W, HEAD_W), F32), unroll=8)


def _retention(proj3, intra, cross, kdec, cdec):
    b, t, _ = proj3.shape
    h = RET_HEADS

    def col(off):
        return pl.BlockSpec((1, t, HEAD_W), lambda bi, hi: (bi, 0, off + hi))

    def per_head(shape):
        return pl.BlockSpec((1,) + shape, lambda bi, hi: (hi, 0, 0))

    return pl.pallas_call(
        _ret_kernel,
        out_shape=jax.ShapeDtypeStruct((b, t, h * HEAD_W), BF16),
        grid=(b, h),
        in_specs=[col(0), col(h), col(2 * h), col(3 * h),
                  per_head((RET_CHUNK, RET_CHUNK)), per_head((RET_CHUNK, 1)),
                  per_head((RET_CHUNK, 1)), per_head((1, 1))],
        out_specs=pl.BlockSpec((1, t, HEAD_W), lambda bi, hi: (bi, 0, hi)),
        compiler_params=pltpu.CompilerParams(dimension_semantics=("parallel", "parallel"),
                                             vmem_limit_bytes=VMEM_LIMIT),
        name="retention",
    )(proj3, proj3, proj3, proj3, intra, cross, kdec, cdec)


def _diff_kernel(slope_ref, lam_ref, rel_ref, q_ref, k_ref, v_ref, gsub_ref, o_ref, *, tq, tk, out_scale):
    hi = pl.program_id(1)
    qi = pl.program_id(2)
    half = HEAD_W // 2
    slope = slope_ref[hi]

    lp = lam_ref[...]
    lam = (jnp.exp(jnp.sum(lp[0:1] * lp[1:2], axis=1, keepdims=True))
           - jnp.exp(jnp.sum(lp[2:3] * lp[3:4], axis=1, keepdims=True))
           + (1.0 - out_scale))

    q = q_ref[0] * jnp.asarray(half ** -0.5, BF16)
    lane = lax.broadcasted_iota(jnp.int32, q.shape, 1)
    zero = jnp.zeros_like(q)
    qs = jnp.concatenate([jnp.where(lane < half, q, zero),
                          jnp.where(lane >= half, q, zero)], axis=0)

    rel = rel_ref[...]
    rel_bias = -slope * rel

    def body(j, carry):
        m, l, acc = carry
        ksl = pl.ds(pl.multiple_of(j * tk, tk), tk)
        k = k_ref[0, ksl, :]
        v = v_ref[0, ksl, :]
        off = (qi * tq - j * tk).astype(F32)
        s = lax.dot_general(qs, k, _NT, preferred_element_type=F32)
        s = s + jnp.where(rel >= -off, rel_bias - slope * off, NEG_INF)
        m_new = jnp.maximum(m, jnp.max(s, axis=1, keepdims=True))
        alpha = jnp.exp(m - m_new)
        p = jnp.exp(s - m_new)
        l = alpha * l + jnp.sum(p, axis=1, keepdims=True)
        acc = alpha * acc + jnp.dot(p.astype(BF16), v, preferred_element_type=F32)
        return m_new, l, acc

    n_kv = (qi * tq + tq - 1) // tk + 1
    m0 = jnp.full((2 * tq, 1), NEG_INF, F32)
    l0 = jnp.zeros((2 * tq, 1), F32)
    a0 = jnp.zeros((2 * tq, HEAD_W), F32)
    _, l, acc = lax.fori_loop(0, n_kv, body, (m0, l0, a0))
    o = acc / l
    dy = o[:tq] - lam * o[tq:]
    o_ref[0] = (_rms(dy) * gsub_ref[...] * out_scale).astype(o_ref.dtype)


def _diff_attention(proj3, slopes, diff_lambda_l, gsub, out_scale, tq=256, tk=512):
    b, t, _ = proj3.shape
    h = DIFF_HEADS
    base = 4 * RET_HEADS
    kern = functools.partial(_diff_kernel, tq=tq, tk=tk, out_scale=out_scale)
    rel = jnp.asarray((np.arange(2 * tq) % tq)[:, None] - np.arange(tk)[None, :], F32)
    return pl.pallas_call(
        kern,
        out_shape=jax.ShapeDtypeStruct((b, t, h * HEAD_W), BF16),
        grid=(b, h, t // tq),
        in_specs=[pl.BlockSpec(memory_space=pltpu.SMEM),
                  pl.BlockSpec(diff_lambda_l.shape, lambda bi, hi, qi: (0, 0)),
                  pl.BlockSpec((2 * tq, tk), lambda bi, hi, qi: (0, 0)),
                  pl.BlockSpec((1, tq, HEAD_W), lambda bi, hi, qi: (bi, qi, base + hi)),
                  pl.BlockSpec((1, t, HEAD_W), lambda bi, hi, qi: (bi, 0, base + h + hi)),
                  pl.BlockSpec((1, t, HEAD_W), lambda bi, hi, qi: (bi, 0, base + 2 * h + hi)),
                  pl.BlockSpec((1, HEAD_W), lambda bi, hi, qi: (0, 0))],
        out_specs=pl.BlockSpec((1, tq, HEAD_W), lambda bi, hi, qi: (bi, qi, hi)),
        compiler_params=pltpu.CompilerParams(
            dimension_semantics=("parallel", "parallel", "arbitrary"),
            vmem_limit_bytes=VMEM_LIMIT),
        name="diff_attention",
    )(slopes, diff_lambda_l, rel, proj3, proj3, proj3, gsub)


def _outproj_kernel(x_ref, ret_ref, dif_ref, wo_ref, g_ref, wq_ref, x1_ref, xn_ref, xnf_ref, q_ref, *, col_chunk):
    half = ret_ref.shape[1]
    mix = jnp.dot(ret_ref[...], wo_ref[:half, :], preferred_element_type=F32)
    mix = mix + jnp.dot(dif_ref[...], wo_ref[half:, :], preferred_element_type=F32)
    x1 = x_ref[...] + mix
    x1_ref[...] = x1
    xn = (_rms(x1) * g_ref[...]).astype(BF16)
    xn_ref[...] = xn
    xnf_ref[...] = xn.astype(F32)
    for j in range(q_ref.shape[1] // col_chunk):
        sl = slice(j * col_chunk, (j + 1) * col_chunk)
        q_ref[:, sl] = jnp.dot(xn, wq_ref[:, sl], preferred_element_type=F32).astype(q_ref.dtype)


def _out_proj(x2, ret2, dif2, wo_bf16, g, wq_bf16, row0, tm=512):
    n, half = ret2.shape
    d = x2.shape[1]
    qc = wq_bf16.shape[1]
    assert row0 % tm == 0 and n % tm == 0
    blk0 = row0 // tm
    row = lambda w: pl.BlockSpec((tm, w), lambda i: (i, 0))
    full = lambda a: pl.BlockSpec(a.shape, lambda i: (0, 0))
    return pl.pallas_call(
        functools.partial(_outproj_kernel, col_chunk=512),
        out_shape=(jax.ShapeDtypeStruct((n, d), F32),
                   jax.ShapeDtypeStruct((n, d), BF16),
                   jax.ShapeDtypeStruct((n, d), F32),
                   jax.ShapeDtypeStruct((n, qc), BF16)),
        grid=(n // tm,),
        in_specs=[pl.BlockSpec((tm, d), lambda i: (i + blk0, 0)), row(half), row(half),
                  full(wo_bf16), full(g), full(wq_bf16)],
        out_specs=(row(d), row(d), row(d), row(qc)),
        compiler_params=pltpu.CompilerParams(dimension_semantics=("parallel",),
                                             vmem_limit_bytes=VMEM_LIMIT),
        name="out_proj",
    )(x2, ret2, dif2, wo_bf16, g, wq_bf16)


def _split3(v):
    trunc = lambda x: pltpu.bitcast(pltpu.bitcast(x, jnp.uint32) & jnp.uint32(0xFFFF0000), F32)
    hi = trunc(v)
    r1 = v - hi
    mid = trunc(r1)
    return hi.astype(BF16), mid.astype(BF16), (r1 - mid).astype(BF16)


def _spread_exact(v, sel_bf16):
    hi, mid, lo = _split3(v)
    d = lambda a: jnp.dot(a, sel_bf16, preferred_element_type=F32)
    return (d(hi) + d(mid)) + d(lo)


def _spread_exact_left(sel_bf16, v):
    hi, mid, lo = _split3(v)
    d = lambda a: jnp.dot(sel_bf16, a, preferred_element_type=F32)
    return (d(hi) + d(mid)) + d(lo)


def _topk_kernel(q_ref, sk_ref, sel_ref, pad_ref, e_ref, g_ref, g2_ref):
    tb = q_ref.shape[0]
    k = PEER_TOPK
    n_sets = 2 * PEER_HEADS
    neg = -jnp.inf

    s = jnp.stack(
        [lax.dot_general(sk_ref[j // 2, j % 2], q_ref[:, j * N_KEYS:(j + 1) * N_KEYS], _NT,
                         preferred_element_type=F32) for j in range(n_sets)], axis=0)
    key_f = lax.broadcasted_iota(jnp.int32, s.shape, 1).astype(F32)
    rank = lax.broadcasted_iota(jnp.int32, (n_sets, k, tb), 1)
    top_v = jnp.zeros((n_sets, k, tb), F32)
    top_i = jnp.zeros((n_sets, k, tb), F32)
    for r in range(k):
        m = jnp.max(s, axis=1, keepdims=True)
        i = jnp.min(jnp.where(s == m, key_f, float(N_KEYS)), axis=1, keepdims=True)
        s = jnp.where(key_f == i, neg, s)
        top_v = jnp.where(rank == r, m, top_v)
        top_i = jnp.where(rank == r, i, top_i)

    pad = pad_ref[...]
    cands, eids = [], []
    for h in range(PEER_HEADS):
        cands.append(_spread_exact_left(sel_ref[0], top_v[2 * h]) + _spread_exact_left(sel_ref[1], top_v[2 * h + 1])
                     + pad)
        eids.append(jnp.dot(sel_ref[0], top_i[2 * h].astype(BF16), preferred_element_type=F32) * float(N_KEYS)
                    + jnp.dot(sel_ref[1], top_i[2 * h + 1].astype(BF16), preferred_element_type=F32))
    cand = jnp.stack(cands, axis=0)
    eid = jnp.stack(eids, axis=0)
    n_cand = cand.shape[1]

    c_f = lax.broadcasted_iota(jnp.int32, cand.shape, 1).astype(F32)
    rank2 = lax.broadcasted_iota(jnp.int32, (PEER_HEADS, k, tb), 1)
    top_c = []
    e_rank = jnp.zeros((PEER_HEADS, k, tb), F32)
    for r in range(k):
        m = jnp.max(cand, axis=1, keepdims=True)
        pos = jnp.min(jnp.where(cand == m, c_f, float(n_cand)), axis=1, keepdims=True)
        hit = c_f == pos
        e_rank = jnp.where(rank2 == r, jnp.sum(jnp.where(hit, eid, 0.0), axis=1, keepdims=True), e_rank)
        cand = jnp.where(hit, neg, cand)
        top_c.append(m)
    p_rank = jnp.zeros((PEER_HEADS, k, tb), F32)
    denom = jnp.zeros((PEER_HEADS, 1, tb), F32)
    for r in range(k):
        p = jnp.exp(top_c[r] - top_c[0])
        denom = denom + p
        p_rank = jnp.where(rank2 == r, p, p_rank)
    g_rank = p_rank / denom

    e_out = e_rank.reshape(PEER_HEADS * k, tb).T
    g1 = g_rank.reshape(PEER_HEADS * k, tb).T
    sr = lax.broadcasted_iota(jnp.int32, (N_SEL, 2 * N_SEL), 0)
    sc = lax.broadcasted_iota(jnp.int32, (N_SEL, 2 * N_SEL), 1)
    e_ref[...] = e_out.astype(jnp.int32)
    g_ref[...] = g1
    g2_ref[...] = _spread_exact(g1, jnp.where(sc == 2 * sr + 1, 1.0, 0.0).astype(BF16))


def _candidate_tables(tb):
    k = PEER_TOPK
    pairs = [(a, b) for a in range(k) for b in range(k) if (a + 1) * (b + 1) <= k]
    n_cand = -(-len(pairs) // SUBLANES) * SUBLANES
    sel = np.zeros((2, n_cand, k), np.float32)
    pad = np.full((n_cand, tb), -np.inf, np.float32)
    for c, (a, b) in enumerate(pairs):
        sel[0, c, a] = 1.0
        sel[1, c, b] = 1.0
        pad[c, :] = 0.0
    return jnp.asarray(sel, BF16), jnp.asarray(pad)


def _peer_topk(q2, sk_bf16, tb=128):
    n, qc = q2.shape
    sel, pad = _candidate_tables(tb)
    const = lambda a: pl.BlockSpec(a.shape, lambda i: (0,) * a.ndim)
    return pl.pallas_call(
        _topk_kernel,
        out_shape=(jax.ShapeDtypeStruct((n, N_SEL), jnp.int32),
                   jax.ShapeDtypeStruct((n, N_SEL), F32),
                   jax.ShapeDtypeStruct((n, 2 * N_SEL), F32)),
        grid=(n // tb,),
        in_specs=[pl.BlockSpec((tb, qc), lambda i: (i, 0)), const(sk_bf16), const(sel), const(pad)],
        out_specs=(pl.BlockSpec((tb, N_SEL), lambda i: (i, 0)),
                   pl.BlockSpec((tb, N_SEL), lambda i: (i, 0)),
                   pl.BlockSpec((tb, 2 * N_SEL), lambda i: (i, 0))),
        compiler_params=pltpu.CompilerParams(dimension_semantics=("parallel",),
                                             vmem_limit_bytes=VMEM_LIMIT),
        name="peer_topk",
    )(q2, sk_bf16, sel, pad)


def _bf16_bits(t):
    return lax.bitcast_convert_type(t.astype(BF16), jnp.uint16).astype(jnp.uint32)


def _pack_tables(u, v):
    e, d = u.shape
    return (_bf16_bits(u) | (_bf16_bits(v) << 16)).reshape(e, d // HEAD_W, HEAD_W)


def _pack_halves(t):
    b = _bf16_bits(t)
    half = t.shape[1] // 2
    return b[:, :half] | (b[:, half:] << 16)


def _mix_kernel(efirst_ref, enext_ref, g2_ref, xn_ref, x1_ref, gfin_ref, tab_hbm, after_ref, o_ref, buf, sem, *,
                final_norm):
    step = pl.program_id(0)
    tb = xn_ref.shape[0]
    n_chunks = tab_hbm.shape[1]

    def issue(e_ref, t, slot):
        for k in range(N_SEL):
            e = e_ref[t, k]
            pltpu.make_async_copy(tab_hbm.at[e], buf.at[slot, pl.ds(k * n_chunks, n_chunks), :],
                                  sem.at[slot]).start(priority=k % 2)

    def wait(slot):
        pltpu.make_async_copy(buf.at[slot], buf.at[slot], sem.at[slot]).wait()

    @pl.when(step == 0)
    def _():
        issue(efirst_ref, 0, 0)

    rid = lax.broadcasted_iota(jnp.int32, (SUBLANES, 2 * N_SEL), 0)

    def group(gi, c):
        base = pl.multiple_of(gi * SUBLANES, SUBLANES)
        xrows = xn_ref[pl.ds(base, SUBLANES), :]
        grows = g2_ref[pl.ds(base, SUBLANES), :]
        acc = x1_ref[pl.ds(base, SUBLANES), :]
        for j in range(SUBLANES):
            slot = j % 2
            issue(enext_ref, base + j, 1 - slot)
            wait(slot)
            uv = jnp.concatenate(
                [pltpu.bitcast(buf[slot, pl.ds(c, N_SEL, stride=n_chunks), :], BF16) for c in range(n_chunks)],
                axis=1)
            s2 = lax.dot_general(xrows, uv, _NT, preferred_element_type=F32)
            act = pltpu.roll(_gelu_exact(s2), 1, axis=1)
            w2 = jnp.where(rid == j, grows * act, 0.0)
            acc = acc + jnp.dot(w2.astype(BF16), uv, preferred_element_type=F32)
        o_ref[pl.ds(base, SUBLANES), :] = acc
        return c

    lax.fori_loop(0, tb // SUBLANES, group, 0)

    @pl.when(step == pl.num_programs(0) - 1)
    def _():
        wait(0)

    if final_norm:
        o_ref[...] = _rms(o_ref[...]) * gfin_ref[...]


def _peer_mix(eids, e_next, gates2, xn, x1, gfin, table, after, row0, rows, final_norm, tb=64):
    d = x1.shape[1]
    assert tb % (2 * SUBLANES) == 0 and row0 % tb == 0 and rows % tb == 0
    blk0 = row0 // tb
    row = lambda w: pl.BlockSpec((tb, w), lambda i: (i + blk0, 0))
    return pl.pallas_call(
        functools.partial(_mix_kernel, final_norm=final_norm),
        out_shape=jax.ShapeDtypeStruct((rows, d), F32),
        grid=(rows // tb,),
        in_specs=[pl.BlockSpec((SUBLANES, N_SEL), lambda i: (row0 // SUBLANES, 0), memory_space=pltpu.SMEM),
                  pl.BlockSpec((tb, N_SEL), lambda i: (i + blk0, 0), memory_space=pltpu.SMEM),
                  row(2 * N_SEL), row(d), row(d),
                  pl.BlockSpec((1, d), lambda i: (0, 0)),
                  pl.BlockSpec(memory_space=pl.ANY),
                  pl.BlockSpec(memory_space=pl.ANY)],
        out_specs=pl.BlockSpec((tb, d), lambda i: (i, 0)),
        scratch_shapes=[pltpu.VMEM((2, N_SEL * (d // HEAD_W), HEAD_W), jnp.uint32),
                        pltpu.SemaphoreType.DMA((2,))],
        compiler_params=pltpu.CompilerParams(dimension_semantics=("arbitrary",),
                                             vmem_limit_bytes=VMEM_LIMIT),
        name="peer_mix",
    )(eids, e_next, gates2, xn, x1, gfin, table, after)


SC_LANES = 16
SC_TILES = 32
SC_TBLK = 64
SC_SLOTS = 4
HI_MASK = 0xFFFF0000
_SC_PARAMS = pltpu.CompilerParams(needs_layout_passes=False)


def _sc_unpack(word):
    return plsc.bitcast(word << 16, F32), plsc.bitcast(word & jnp.uint32(HI_MASK), F32)


def _sc_pipelined_heads(tab_hbm, ev, rows, sem, compute):
    n_items = SC_TBLK * PEER_HEADS
    ahead = SC_SLOTS - 1
    assert n_items % SC_SLOTS == 0

    def gather(it, slot):
        ti = it // PEER_HEADS
        h = it % PEER_HEADS
        return pltpu.make_async_copy(tab_hbm.at[ev.at[ti, pl.ds(h * PEER_TOPK, PEER_TOPK)]], rows.at[slot],
                                     sem.at[slot])

    for a in range(ahead):
        gather(a, a).start()

    def group(jj, carry):
        for slot in range(SC_SLOTS):
            it = SC_SLOTS * jj + slot
            gather(jnp.minimum(it + ahead, n_items - 1), (slot + ahead) % SC_SLOTS).start()
            gather(it, slot).wait()
            compute(it, slot)
        return carry

    lax.fori_loop(0, n_items // SC_SLOTS, group, 0)
    for a in range(ahead):
        gather(n_items - 1, (n_items + a) % SC_SLOTS).wait()


def _sc_scores(x, eids, utab, row0, rows):
    d = x.shape[1]
    dw = d // 2
    nw = dw // SC_LANES
    per_tile = rows // SC_TILES
    assert rows % (SC_TILES * SC_TBLK) == 0 and row0 % SC_TBLK == 0

    def body(x_hbm, e_hbm, tab_hbm, s_hbm, xv, ev, rbuf, sv, tmp, sem):
        tile = lax.axis_index("c") * 16 + lax.axis_index("s")
        lane = lax.iota(jnp.int32, SC_LANES)

        def compute(it, slot):
            ti = it // PEER_HEADS
            h = it % PEER_HEADS

            def chunk(w, accs):
                off = pl.multiple_of(w * SC_LANES, SC_LANES)
                xlo = xv[ti, pl.ds(off, SC_LANES)]
                xhi = xv[ti, pl.ds(dw + off, SC_LANES)]
                out = []
                for j in range(PEER_TOPK):
                    lo, hi = _sc_unpack(rbuf[slot, j, pl.ds(off, SC_LANES)])
                    out.append(accs[j] + lo * xlo + hi * xhi)
                return tuple(out)

            accs = lax.fori_loop(0, nw, chunk, tuple(jnp.zeros((SC_LANES,), F32) for _ in range(PEER_TOPK)))
            for j in range(PEER_TOPK):
                tmp[j, :] = accs[j]
            svec = jnp.zeros((SC_LANES,), F32)
            for l in range(SC_LANES):
                svec = svec + plsc.load_gather(tmp, [lane, jnp.full((SC_LANES,), l, jnp.int32)])
            sv[ti, pl.ds(pl.multiple_of(h * PEER_TOPK, PEER_TOPK), PEER_TOPK)] = svec

        def block(b, carry):
            t_out = pl.multiple_of(tile * per_tile + b * SC_TBLK, SC_TBLK)
            pltpu.sync_copy(x_hbm.at[pl.ds(row0 + t_out, SC_TBLK)], xv)
            pltpu.sync_copy(e_hbm.at[pl.ds(row0 + t_out, SC_TBLK)], ev)
            _sc_pipelined_heads(tab_hbm, ev, rbuf, sem, compute)
            pltpu.sync_copy(sv, s_hbm.at[pl.ds(t_out, SC_TBLK)])
            return carry

        lax.fori_loop(0, per_tile // SC_TBLK, block, 0)

    return pl.kernel(
        body,
        out_type=jax.ShapeDtypeStruct((rows, N_SEL), F32),
        mesh=plsc.VectorSubcoreMesh(core_axis_name="c", subcore_axis_name="s"),
        scratch_types=[pltpu.VMEM((SC_TBLK, d), F32), pltpu.VMEM((SC_TBLK, N_SEL), jnp.int32),
                       pltpu.VMEM((SC_SLOTS, PEER_TOPK, dw), jnp.uint32), pltpu.VMEM((SC_TBLK, N_SEL), F32),
                       pltpu.VMEM((PEER_TOPK, SC_LANES), F32), pltpu.SemaphoreType.DMA((SC_SLOTS,))],
        compiler_params=_SC_PARAMS,
        name="sc_scores",
    )(x, eids, utab)


def _sc_combine(w, eids, vtab, row0):
    rows = w.shape[0]
    dw = vtab.shape[1]
    d = 2 * dw
    nw = dw // SC_LANES
    per_tile = rows // SC_TILES
    assert rows % (SC_TILES * SC_TBLK) == 0 and row0 % SC_TBLK == 0

    def body(w_hbm, e_hbm, tab_hbm, y_hbm, wv, ev, rbuf, yv, sem):
        tile = lax.axis_index("c") * 16 + lax.axis_index("s")
        zero_i = jnp.zeros((SC_LANES,), jnp.int32)

        def compute(it, slot):
            ti = it // PEER_HEADS
            h = it % PEER_HEADS
            wk = [plsc.load_gather(wv, [zero_i + ti, zero_i + (h * PEER_TOPK + j)]) for j in range(PEER_TOPK)]

            def chunk(c, carry):
                off = pl.multiple_of(c * SC_LANES, SC_LANES)
                alo = jnp.zeros((SC_LANES,), F32)
                ahi = jnp.zeros((SC_LANES,), F32)
                for j in range(PEER_TOPK):
                    lo, hi = _sc_unpack(rbuf[slot, j, pl.ds(off, SC_LANES)])
                    alo = alo + wk[j] * lo
                    ahi = ahi + wk[j] * hi
                plsc.addupdate(yv.at[ti, pl.ds(off, SC_LANES)], alo)
                plsc.addupdate(yv.at[ti, pl.ds(dw + off, SC_LANES)], ahi)
                return carry

            plsc.parallel_loop(0, nw, 1, carry=jnp.int32(0))(chunk)

        def block(b, carry):
            t_out = pl.multiple_of(tile * per_tile + b * SC_TBLK, SC_TBLK)
            pltpu.sync_copy(w_hbm.at[pl.ds(t_out, SC_TBLK)], wv)
            pltpu.sync_copy(e_hbm.at[pl.ds(row0 + t_out, SC_TBLK)], ev)

            def zero(i, carry):
                for ti in range(SC_TBLK):
                    yv[ti, pl.ds(pl.multiple_of(i * SC_LANES, SC_LANES), SC_LANES)] = jnp.zeros((SC_LANES,), F32)
                return carry

            lax.fori_loop(0, d // SC_LANES, zero, 0)
            _sc_pipelined_heads(tab_hbm, ev, rbuf, sem, compute)
            pltpu.sync_copy(yv, y_hbm.at[pl.ds(t_out, SC_TBLK)])
            return carry

        lax.fori_loop(0, per_tile // SC_TBLK, block, 0)

    return pl.kernel(
        body,
        out_type=jax.ShapeDtypeStruct((rows, d), F32),
        mesh=plsc.VectorSubcoreMesh(core_axis_name="c", subcore_axis_name="s"),
        scratch_types=[pltpu.VMEM((SC_TBLK, N_SEL), F32), pltpu.VMEM((SC_TBLK, N_SEL), jnp.int32),
                       pltpu.VMEM((SC_SLOTS, PEER_TOPK, dw), jnp.uint32), pltpu.VMEM((SC_TBLK, d), F32),
                       pltpu.SemaphoreType.DMA((SC_SLOTS,))],
        compiler_params=_SC_PARAMS,
        name="sc_combine",
    )(w, eids, vtab)


def _gate_kernel(s_ref, g_ref, after_ref, w_ref):
    w_ref[...] = g_ref[...] * _gelu_exact(s_ref[...])


def _gate_act(s, gates, after, row0, tb=512):
    rows = s.shape[0]
    assert rows % tb == 0 and row0 % tb == 0
    blk0 = row0 // tb
    return pl.pallas_call(
        _gate_kernel,
        out_shape=jax.ShapeDtypeStruct((rows, N_SEL), F32),
        grid=(rows // tb,),
        in_specs=[pl.BlockSpec((tb, N_SEL), lambda i: (i, 0)),
                  pl.BlockSpec((tb, N_SEL), lambda i: (i + blk0, 0)),
                  pl.BlockSpec(memory_space=pl.ANY)],
        out_specs=pl.BlockSpec((tb, N_SEL), lambda i: (i, 0)),
        compiler_params=pltpu.CompilerParams(dimension_semantics=("parallel",)),
        name="peer_gate",
    )(s, gates, after)


def _resid_norm_kernel(x1_ref, y_ref, gfin_ref, after_ref, o_ref, *, final_norm):
    z = x1_ref[...] + y_ref[...]
    o_ref[...] = _rms(z) * gfin_ref[...] if final_norm else z


def _resid_norm(x1, y, gfin, after, final_norm, tb=512):
    rows, d = y.shape
    assert rows % tb == 0
    return pl.pallas_call(
        functools.partial(_resid_norm_kernel, final_norm=final_norm),
        out_shape=jax.ShapeDtypeStruct((rows, d), F32),
        grid=(rows // tb,),
        in_specs=[pl.BlockSpec((tb, d), lambda i: (i, 0)),
                  pl.BlockSpec((tb, d), lambda i: (i, 0)),
                  pl.BlockSpec((1, d), lambda i: (0, 0)),
                  pl.BlockSpec(memory_space=pl.ANY)],
        out_specs=pl.BlockSpec((tb, d), lambda i: (i, 0)),
        compiler_params=pltpu.CompilerParams(dimension_semantics=("parallel",)),
        name="peer_resid_norm",
    )(x1, y, gfin, after)


N_CHUNKS = 16
SC_TOKEN_FRAC = 52 / 64
SPLIT_UNIT = SC_TILES * SC_TBLK
COST_FRONT, COST_SC_SCORES, COST_SC_COMBINE, COST_TC_MIX = 0.13, 0.21, 0.18, 0.85


def _sc_rows(n, tokens_per_chunk):
    n_sc = int(n * SC_TOKEN_FRAC) // SPLIT_UNIT * SPLIT_UNIT
    return [min(max(n_sc - c * tokens_per_chunk, 0), tokens_per_chunk) for c in range(N_CHUNKS)]


def _tc_schedule(tokens_per_chunk, sc_rows):
    sc_chunks = [c for c in range(N_CHUNKS) if sc_rows[c]]
    base = [("front", c) for c in range(N_CHUNKS)] + [("mix", c) for c in range(N_CHUNKS)
                                                       if sc_rows[c] < tokens_per_chunk]
    order = []
    tc = sc = 0.0
    scores_done, combine_done = {}, {}
    gates = list(sc_chunks)
    norms = list(sc_chunks)
    for kind, c in base + [("end", None)]:
        while gates and (kind == "end" or scores_done.get(gates[0], float("inf"))
                         + 0.25 * COST_SC_SCORES * sc_rows[gates[0]] <= tc):
            g = gates.pop(0)
            tc = max(tc, scores_done[g])
            sc = max(sc, tc) + COST_SC_COMBINE * sc_rows[g]
            combine_done[g] = sc
            order.append(("gate", g))
        while norms and (kind == "end" or combine_done.get(norms[0], float("inf")) <= tc):
            order.append(("norm", norms.pop(0)))
        if kind == "front":
            tc += COST_FRONT * tokens_per_chunk
            if sc_rows[c]:
                sc = max(sc, tc) + COST_SC_SCORES * sc_rows[c]
                scores_done[c] = sc
        elif kind == "mix":
            tc += COST_TC_MIX * (tokens_per_chunk - sc_rows[c])
        if kind != "end":
            order.append((kind, c))
    return order


def _layer(x2, batch, seq, params, consts, li, final_norm):
    (g_mix, w_in, lam_p, g_sub, w_out, g_ffn, wq, sk, u, v, g_fin) = params
    intra, cross, kdec, cdec, slopes = consts
    n, d = x2.shape
    nc = n // N_CHUNKS
    bc = batch // N_CHUNKS
    assert batch % N_CHUNKS == 0
    table = _pack_tables(u, v)
    utab = _pack_halves(u)
    vtab = _pack_halves(v)

    sc_rows = _sc_rows(n, nc)
    chunks = [None] * N_CHUNKS
    scores = [None] * N_CHUNKS
    ys = [None] * N_CHUNKS
    outs_sc = [None] * N_CHUNKS
    outs_tc = [None] * N_CHUNKS
    last = g_fin

    def front(c, last):
        proj = _in_proj(x2, g_mix, w_in, last, c * nc, nc)
        proj3 = proj.reshape(bc, seq, proj.shape[1])
        ret = _retention(proj3, intra, cross, kdec, cdec)
        dif = _diff_attention(proj3, slopes, lam_p, g_sub, 1.0 - li)
        half = ret.shape[2]
        x1, xn, xn_f32, q = _out_proj(x2, ret.reshape(nc, half), dif.reshape(nc, half), w_out, g_ffn, wq, c * nc)
        eids, gates, gates2 = _peer_topk(q, sk)
        chunks[c] = (x1, xn, eids, gates, gates2)
        if sc_rows[c]:
            scores[c] = _sc_scores(xn_f32, eids, utab, 0, sc_rows[c])
        return gates2

    def gate(c, last):
        _, _, eids, gates, _ = chunks[c]
        w = _gate_act(scores[c], gates, last, 0)
        ys[c] = _sc_combine(w, eids, vtab, 0)
        return w

    def tc_mix(c, last):
        x1, xn, eids, _, gates2 = chunks[c]
        outs_tc[c] = _peer_mix(eids, jnp.roll(eids, -1, axis=0), gates2, xn, x1, g_fin, table, last,
                               sc_rows[c], nc - sc_rows[c], final_norm)
        return outs_tc[c]

    def norm(c, last):
        outs_sc[c] = _resid_norm(chunks[c][0], ys[c], g_fin, last, final_norm)
        return outs_sc[c]

    run = {"front": front, "gate": gate, "mix": tc_mix, "norm": norm}
    for kind, c in _tc_schedule(nc, sc_rows):
        last = run[kind](c, last)
    pieces = [o for c in range(N_CHUNKS) for o in (outs_sc[c], outs_tc[c]) if o is not None]
    return jnp.concatenate(pieces, axis=0)


def _retention_tables(dk):
    c = RET_CHUNK
    h = RET_HEADS
    log_gamma = jnp.log1p(-jnp.exp2(-5.0 - jnp.arange(h, dtype=F32)))
    idx = jnp.arange(c, dtype=F32)
    rel = idx[:, None] - idx[None, :]
    intra = jnp.where(rel >= 0, jnp.exp(log_gamma[:, None, None] * jnp.maximum(rel, 0.0)), 0.0)
    cross = jnp.exp(log_gamma[:, None] * (idx + 1.0))
    kdec = jnp.exp(log_gamma[:, None] * (c - 1.0 - idx))
    cdec = jnp.exp(log_gamma * c)
    scale = dk ** -0.5
    return intra * scale, cross[:, :, None], (kdec * scale)[:, :, None], cdec[:, None, None]


def kernel(x, norm_mix_g, w_in, diff_lambda, diff_subln_g, w_out, norm_ffn_g, peer_wq, peer_subkeys,
           peer_u, peer_v, final_norm_g):
    b, t, d = x.shape
    depth = w_in.shape[0]
    n = b * t
    x2 = x.reshape(n, d)
    consts = _retention_tables(HEAD_W) + (
        jnp.exp2(-8.0 * (jnp.arange(DIFF_HEADS, dtype=F32) + 1.0) / DIFF_HEADS),)
    for l in range(depth):
        li = 0.8 - 0.6 * math.exp(-0.3 * l)
        params = (norm_mix_g[l][None, :], w_in[l].astype(BF16), diff_lambda[l], diff_subln_g[l][None, :],
                  w_out[l].astype(BF16), norm_ffn_g[l][None, :], peer_wq[l].astype(BF16),
                  peer_subkeys[l].astype(BF16), peer_u[l], peer_v[l], final_norm_g[None, :])
        x2 = _layer(x2, b, t, params, consts, li, final_norm=(l == depth - 1))
    return x2.reshape(b, t, d)
```

```python
import functools
import math

import jax
import jax.numpy as jnp
import numpy as np
from jax import lax
from jax.experimental import pallas as pl
from jax.experimental.pallas import tpu as pltpu
from jax.experimental.pallas import tpu_sc as plsc

F32 = jnp.float32
BF16 = jnp.bfloat16

RMS_EPS = 1e-6
NEG_INF = -1e30

RET_HEADS = 4
RET_CHUNK = 128
DIFF_HEADS = 4
HEAD_W = 128
PEER_HEADS = 8
PEER_TOPK = 16
N_KEYS = 128
N_SEL = PEER_HEADS * PEER_TOPK

SUBLANES = 8
VMEM_LIMIT = 56 * 1024 * 1024

_NT = (((1,), (1,)), ((), ()))


def _rms(x, eps=RMS_EPS):
    return x * lax.rsqrt(jnp.mean(x * x, axis=-1, keepdims=True) + eps)


def _gelu_exact(x):
    return 0.5 * x * (1.0 + lax.erf(x * (2.0 ** -0.5)))


def _inproj_kernel(x_ref, g_ref, w_ref, after_ref, o_ref, *, col_chunk):
    h = (_rms(x_ref[...]) * g_ref[...]).astype(BF16)
    for j in range(o_ref.shape[1] // col_chunk):
        sl = slice(j * col_chunk, (j + 1) * col_chunk)
        o_ref[:, sl] = jnp.dot(h, w_ref[:, sl], preferred_element_type=F32).astype(o_ref.dtype)


def _in_proj(x2, g, w_bf16, after, row0, rows, tm=512):
    d = x2.shape[1]
    cols = w_bf16.shape[1]
    assert row0 % tm == 0 and rows % tm == 0
    blk0 = row0 // tm
    return pl.pallas_call(
        functools.partial(_inproj_kernel, col_chunk=512),
        out_shape=jax.ShapeDtypeStruct((rows, cols), BF16),
        grid=(rows // tm,),
        in_specs=[pl.BlockSpec((tm, d), lambda i: (i + blk0, 0)),
                  pl.BlockSpec((1, d), lambda i: (0, 0)),
                  pl.BlockSpec((d, cols), lambda i: (0, 0)),
                  pl.BlockSpec(memory_space=pl.ANY)],
        out_specs=pl.BlockSpec((tm, cols), lambda i: (i, 0)),
        compiler_params=pltpu.CompilerParams(dimension_semantics=("parallel",),
                                             vmem_limit_bytes=VMEM_LIMIT),
        name="in_proj",
    )(x2, g, w_bf16, after)


def _ret_kernel(q_ref, k_ref, v_ref, g_ref, intra_ref, cross_ref, kdec_ref, cdec_ref, o_ref):
    c = RET_CHUNK
    n_chunks = q_ref.shape[1] // c
    intra = intra_ref[0]
    cross = cross_ref[0]
    kdec = kdec_ref[0]
    cdec = cdec_ref[0]

    def body(n, s):
        sl = pl.ds(pl.multiple_of(n * c, c), c)
        qi = q_ref[0, sl, :]
        ki = k_ref[0, sl, :]
        vi = v_ref[0, sl, :]
        a = lax.dot_general(qi, ki, _NT, preferred_element_type=F32) * intra
        o = jnp.dot(a.astype(BF16), vi, preferred_element_type=F32)
        o = o + jnp.dot(qi, s.astype(BF16), preferred_element_type=F32) * cross
        kd_t = (ki.astype(F32) * kdec).T.astype(BF16)
        s = s * cdec + jnp.dot(kd_t, vi, preferred_element_type=F32)
        gate = g_ref[0, sl, :].astype(F32)
        o_ref[0, sl, :] = (gate * jax.nn.sigmoid(gate) * _rms(o)).astype(o_ref.dtype)
        return s

    lax.fori_loop(0, n_chunks, body, jnp.zeros((HEAD_W, HEAD_W), F32), unroll=8)


def _retention(proj3, intra, cross, kdec, cdec):
    b, t, _ = proj3.shape
    h = RET_HEADS

    def col(off):
        return pl.BlockSpec((1, t, HEAD_W), lambda bi, hi: (bi, 0, off + hi))

    def per_head(shape):
        return pl.BlockSpec((1,) + shape, lambda bi, hi: (hi, 0, 0))

    return pl.pallas_call(
        _ret_kernel,
        out_shape=jax.ShapeDtypeStruct((b, t, h * HEAD_W), BF16),
        grid=(b, h),
        in_specs=[col(0), col(h), col(2 * h), col(3 * h),
                  per_head((RET_CHUNK, RET_CHUNK)), per_head((RET_CHUNK, 1)),
                  per_head((RET_CHUNK, 1)), per_head((1, 1))],
        out_specs=pl.BlockSpec((1, t, HEAD_W), lambda bi, hi: (bi, 0, hi)),
        compiler_params=pltpu.CompilerParams(dimension_semantics=("parallel", "parallel"),
                                             vmem_limit_bytes=VMEM_LIMIT),
        name="retention",
    )(proj3, proj3, proj3, proj3, intra, cross, kdec, cdec)


def _diff_kernel(slope_ref, lam_ref, rel_ref, q_ref, k_ref, v_ref, gsub_ref, o_ref, *, tq, tk, out_scale):
    hi = pl.program_id(1)
    qi = pl.program_id(2)
    half = HEAD_W // 2
    slope = slope_ref[hi]

    lp = lam_ref[...]
    lam = (jnp.exp(jnp.sum(lp[0:1] * lp[1:2], axis=1, keepdims=True))
           - jnp.exp(jnp.sum(lp[2:3] * lp[3:4], axis=1, keepdims=True))
           + (1.0 - out_scale))

    q = q_ref[0] * jnp.asarray(half ** -0.5, BF16)
    lane = lax.broadcasted_iota(jnp.int32, q.shape, 1)
    zero = jnp.zeros_like(q)
    qs = jnp.concatenate([jnp.where(lane < half, q, zero),
                          jnp.where(lane >= half, q, zero)], axis=0)

    rel = rel_ref[...]
    rel_bias = -slope * rel

    def body(j, carry):
        m, l, acc = carry
        ksl = pl.ds(pl.multiple_of(j * tk, tk), tk)
        k = k_ref[0, ksl, :]
        v = v_ref[0, ksl, :]
        off = (qi * tq - j * tk).astype(F32)
        s = lax.dot_general(qs, k, _NT, preferred_element_type=F32)
        s = s + jnp.where(rel >= -off, rel_bias - slope * off, NEG_INF)
        m_new = jnp.maximum(m, jnp.max(s, axis=1, keepdims=True))
        alpha = jnp.exp(m - m_new)
        p = jnp.exp(s - m_new)
        l = alpha * l + jnp.sum(p, axis=1, keepdims=True)
        acc = alpha * acc + jnp.dot(p.astype(BF16), v, preferred_element_type=F32)
        return m_new, l, acc

    n_kv = (qi * tq + tq - 1) // tk + 1
    m0 = jnp.full((2 * tq, 1), NEG_INF, F32)
    l0 = jnp.zeros((2 * tq, 1), F32)
    a0 = jnp.zeros((2 * tq, HEAD_W), F32)
    _, l, acc = lax.fori_loop(0, n_kv, body, (m0, l0, a0))
    o = acc / l
    dy = o[:tq] - lam * o[tq:]
    o_ref[0] = (_rms(dy) * gsub_ref[...] * out_scale).astype(o_ref.dtype)


def _diff_attention(proj3, slopes, diff_lambda_l, gsub, out_scale, tq=256, tk=512):
    b, t, _ = proj3.shape
    h = DIFF_HEADS
    base = 4 * RET_HEADS
    kern = functools.partial(_diff_kernel, tq=tq, tk=tk, out_scale=out_scale)
    rel = jnp.asarray((np.arange(2 * tq) % tq)[:, None] - np.arange(tk)[None, :], F32)
    return pl.pallas_call(
        kern,
        out_shape=jax.ShapeDtypeStruct((b, t, h * HEAD_W), BF16),
        grid=(b, h, t // tq),
        in_specs=[pl.BlockSpec(memory_space=pltpu.SMEM),
                  pl.BlockSpec(diff_lambda_l.shape, lambda bi, hi, qi: (0, 0)),
                  pl.BlockSpec((2 * tq, tk), lambda bi, hi, qi: (0, 0)),
                  pl.BlockSpec((1, tq, HEAD_W), lambda bi, hi, qi: (bi, qi, base + hi)),
                  pl.BlockSpec((1, t, HEAD_W), lambda bi, hi, qi: (bi, 0, base + h + hi)),
                  pl.BlockSpec((1, t, HEAD_W), lambda bi, hi, qi: (bi, 0, base + 2 * h + hi)),
                  pl.BlockSpec((1, HEAD_W), lambda bi, hi, qi: (0, 0))],
        out_specs=pl.BlockSpec((1, tq, HEAD_W), lambda bi, hi, qi: (bi, qi, hi)),
        compiler_params=pltpu.CompilerParams(
            dimension_semantics=("parallel", "parallel", "arbitrary"),
            vmem_limit_bytes=VMEM_LIMIT),
        name="diff_attention",
    )(slopes, diff_lambda_l, rel, proj3, proj3, proj3, gsub)


def _outproj_kernel(x_ref, ret_ref, dif_ref, wo_ref, g_ref, wq_ref, x1_ref, xn_ref, xnf_ref, q_ref, *, col_chunk):
    half = ret_ref.shape[1]
    mix = jnp.dot(ret_ref[...], wo_ref[:half, :], preferred_element_type=F32)
    mix = mix + jnp.dot(dif_ref[...], wo_ref[half:, :], preferred_element_type=F32)
    x1 = x_ref[...] + mix
    x1_ref[...] = x1
    xn = (_rms(x1) * g_ref[...]).astype(BF16)
    xn_ref[...] = xn
    xnf_ref[...] = xn.astype(F32)
    for j in range(q_ref.shape[1] // col_chunk):
        sl = slice(j * col_chunk, (j + 1) * col_chunk)
        q_ref[:, sl] = jnp.dot(xn, wq_ref[:, sl], preferred_element_type=F32).astype(q_ref.dtype)


def _out_proj(x2, ret2, dif2, wo_bf16, g, wq_bf16, row0, tm=512):
    n, half = ret2.shape
    d = x2.shape[1]
    qc = wq_bf16.shape[1]
    assert row0 % tm == 0 and n % tm == 0
    blk0 = row0 // tm
    row = lambda w: pl.BlockSpec((tm, w), lambda i: (i, 0))
    full = lambda a: pl.BlockSpec(a.shape, lambda i: (0, 0))
    return pl.pallas_call(
        functools.partial(_outproj_kernel, col_chunk=512),
        out_shape=(jax.ShapeDtypeStruct((n, d), F32),
                   jax.ShapeDtypeStruct((n, d), BF16),
                   jax.ShapeDtypeStruct((n, d), F32),
                   jax.ShapeDtypeStruct((n, qc), BF16)),
        grid=(n // tm,),
        in_specs=[pl.BlockSpec((tm, d), lambda i: (i + blk0, 0)), row(half), row(half),
                  full(wo_bf16), full(g), full(wq_bf16)],
        out_specs=(row(d), row(d), row(d), row(qc)),
        compiler_params=pltpu.CompilerParams(dimension_semantics=("parallel",),
                                             vmem_limit_bytes=VMEM_LIMIT),
        name="out_proj",
    )(x2, ret2, dif2, wo_bf16, g, wq_bf16)


def _split3(v):
    trunc = lambda x: pltpu.bitcast(pltpu.bitcast(x, jnp.uint32) & jnp.uint32(0xFFFF0000), F32)
    hi = trunc(v)
    r1 = v - hi
    mid = trunc(r1)
    return hi.astype(BF16), mid.astype(BF16), (r1 - mid).astype(BF16)


def _spread_exact(v, sel_bf16):
    hi, mid, lo = _split3(v)
    d = lambda a: jnp.dot(a, sel_bf16, preferred_element_type=F32)
    return (d(hi) + d(mid)) + d(lo)


def _spread_exact_left(sel_bf16, v):
    hi, mid, lo = _split3(v)
    d = lambda a: jnp.dot(sel_bf16, a, preferred_element_type=F32)
    return (d(hi) + d(mid)) + d(lo)


def _topk_kernel(q_ref, sk_ref, sel_ref, pad_ref, e_ref, g_ref, g2_ref):
    tb = q_ref.shape[0]
    k = PEER_TOPK
    n_sets = 2 * PEER_HEADS
    neg = -jnp.inf

    s = jnp.stack(
        [lax.dot_general(sk_ref[j // 2, j % 2], q_ref[:, j * N_KEYS:(j + 1) * N_KEYS], _NT,
                         preferred_element_type=F32) for j in range(n_sets)], axis=0)
    key_f = lax.broadcasted_iota(jnp.int32, s.shape, 1).astype(F32)
    rank = lax.broadcasted_iota(jnp.int32, (n_sets, k, tb), 1)
    top_v = jnp.zeros((n_sets, k, tb), F32)
    top_i = jnp.zeros((n_sets, k, tb), F32)
    for r in range(k):
        m = jnp.max(s, axis=1, keepdims=True)
        i = jnp.min(jnp.where(s == m, key_f, float(N_KEYS)), axis=1, keepdims=True)
        s = jnp.where(key_f == i, neg, s)
        top_v = jnp.where(rank == r, m, top_v)
        top_i = jnp.where(rank == r, i, top_i)

    pad = pad_ref[...]
    cands, eids = [], []
    for h in range(PEER_HEADS):
        cands.append(_spread_exact_left(sel_ref[0], top_v[2 * h]) + _spread_exact_left(sel_ref[1], top_v[2 * h + 1])
                     + pad)
        eids.append(jnp.dot(sel_ref[0], top_i[2 * h].astype(BF16), preferred_element_type=F32) * float(N_KEYS)
                    + jnp.dot(sel_ref[1], top_i[2 * h + 1].astype(BF16), preferred_element_type=F32))
    cand = jnp.stack(cands, axis=0)
    eid = jnp.stack(eids, axis=0)
    n_cand = cand.shape[1]

    c_f = lax.broadcasted_iota(jnp.int32, cand.shape, 1).astype(F32)
    rank2 = lax.broadcasted_iota(jnp.int32, (PEER_HEADS, k, tb), 1)
    top_c = []
    e_rank = jnp.zeros((PEER_HEADS, k, tb), F32)
    for r in range(k):
        m = jnp.max(cand, axis=1, keepdims=True)
        pos = jnp.min(jnp.where(cand == m, c_f, float(n_cand)), axis=1, keepdims=True)
        hit = c_f == pos
        e_rank = jnp.where(rank2 == r, jnp.sum(jnp.where(hit, eid, 0.0), axis=1, keepdims=True), e_rank)
        cand = jnp.where(hit, neg, cand)
        top_c.append(m)
    p_rank = jnp.zeros((PEER_HEADS, k, tb), F32)
    denom = jnp.zeros((PEER_HEADS, 1, tb), F32)
    for r in range(k):
        p = jnp.exp(top_c[r] - top_c[0])
        denom = denom + p
        p_rank = jnp.where(rank2 == r, p, p_rank)
    g_rank = p_rank / denom

    e_out = e_rank.reshape(PEER_HEADS * k, tb).T
    g1 = g_rank.reshape(PEER_HEADS * k, tb).T
    sr = lax.broadcasted_iota(jnp.int32, (N_SEL, 2 * N_SEL), 0)
    sc = lax.broadcasted_iota(jnp.int32, (N_SEL, 2 * N_SEL), 1)
    e_ref[...] = e_out.astype(jnp.int32)
    g_ref[...] = g1
    g2_ref[...] = _spread_exact(g1, jnp.where(sc == 2 * sr + 1, 1.0, 0.0).astype(BF16))


def _candidate_tables(tb):
    k = PEER_TOPK
    pairs = [(a, b) for a in range(k) for b in range(k) if (a + 1) * (b + 1) <= k]
    n_cand = -(-len(pairs) // SUBLANES) * SUBLANES
    sel = np.zeros((2, n_cand, k), np.float32)
    pad = np.full((n_cand, tb), -np.inf, np.float32)
    for c, (a, b) in enumerate(pairs):
        sel[0, c, a] = 1.0
        sel[1, c, b] = 1.0
        pad[c, :] = 0.0
    return jnp.asarray(sel, BF16), jnp.asarray(pad)


def _peer_topk(q2, sk_bf16, tb=128):
    n, qc = q2.shape
    sel, pad = _candidate_tables(tb)
    const = lambda a: pl.BlockSpec(a.shape, lambda i: (0,) * a.ndim)
    return pl.pallas_call(
        _topk_kernel,
        out_shape=(jax.ShapeDtypeStruct((n, N_SEL), jnp.int32),
                   jax.ShapeDtypeStruct((n, N_SEL), F32),
                   jax.ShapeDtypeStruct((n, 2 * N_SEL), F32)),
        grid=(n // tb,),
        in_specs=[pl.BlockSpec((tb, qc), lambda i: (i, 0)), const(sk_bf16), const(sel), const(pad)],
        out_specs=(pl.BlockSpec((tb, N_SEL), lambda i: (i, 0)),
                   pl.BlockSpec((tb, N_SEL), lambda i: (i, 0)),
                   pl.BlockSpec((tb, 2 * N_SEL), lambda i: (i, 0))),
        compiler_params=pltpu.CompilerParams(dimension_semantics=("parallel",),
                                             vmem_limit_bytes=VMEM_LIMIT),
        name="peer_topk",
    )(q2, sk_bf16, sel, pad)


def _bf16_bits(t):
    return lax.bitcast_convert_type(t.astype(BF16), jnp.uint16).astype(jnp.uint32)


def _pack_tables(u, v):
    e, d = u.shape
    return (_bf16_bits(u) | (_bf16_bits(v) << 16)).reshape(e, d // HEAD_W, HEAD_W)


def _pack_halves(t):
    b = _bf16_bits(t)
    half = t.shape[1] // 2
    return b[:, :half] | (b[:, half:] << 16)


def _mix_kernel(efirst_ref, enext_ref, g2_ref, xn_ref, x1_ref, gfin_ref, tab_hbm, after_ref, o_ref, buf, sem, *,
                final_norm):
    step = pl.program_id(0)
    tb = xn_ref.shape[0]
    n_chunks = tab_hbm.shape[1]

    def issue(e_ref, t, slot):
        for k in range(N_SEL):
            e = e_ref[t, k]
            pltpu.make_async_copy(tab_hbm.at[e], buf.at[slot, pl.ds(k * n_chunks, n_chunks), :],
                                  sem.at[slot]).start(priority=k % 2)

    def wait(slot):
        pltpu.make_async_copy(buf.at[slot], buf.at[slot], sem.at[slot]).wait()

    @pl.when(step == 0)
    def _():
        issue(efirst_ref, 0, 0)

    rid = lax.broadcasted_iota(jnp.int32, (SUBLANES, 2 * N_SEL), 0)

    def group(gi, c):
        base = pl.multiple_of(gi * SUBLANES, SUBLANES)
        xrows = xn_ref[pl.ds(base, SUBLANES), :]
        grows = g2_ref[pl.ds(base, SUBLANES), :]
        acc = x1_ref[pl.ds(base, SUBLANES), :]
        for j in range(SUBLANES):
            slot = j % 2
            issue(enext_ref, base + j, 1 - slot)
            wait(slot)
            uv = jnp.concatenate(
                [pltpu.bitcast(buf[slot, pl.ds(c, N_SEL, stride=n_chunks), :], BF16) for c in range(n_chunks)],
                axis=1)
            s2 = lax.dot_general(xrows, uv, _NT, preferred_element_type=F32)
            act = pltpu.roll(_gelu_exact(s2), 1, axis=1)
            w2 = jnp.where(rid == j, grows * act, 0.0)
            acc = acc + jnp.dot(w2.astype(BF16), uv, preferred_element_type=F32)
        o_ref[pl.ds(base, SUBLANES), :] = acc
        return c

    lax.fori_loop(0, tb // SUBLANES, group, 0)

    @pl.when(step == pl.num_programs(0) - 1)
    def _():
        wait(0)

    if final_norm:
        o_ref[...] = _rms(o_ref[...]) * gfin_ref[...]


def _peer_mix(eids, e_next, gates2, xn, x1, gfin, table, after, row0, rows, final_norm, tb=64):
    d = x1.shape[1]
    assert tb % (2 * SUBLANES) == 0 and row0 % tb == 0 and rows % tb == 0
    blk0 = row0 // tb
    row = lambda w: pl.BlockSpec((tb, w), lambda i: (i + blk0, 0))
    return pl.pallas_call(
        functools.partial(_mix_kernel, final_norm=final_norm),
        out_shape=jax.ShapeDtypeStruct((rows, d), F32),
        grid=(rows // tb,),
        in_specs=[pl.BlockSpec((SUBLANES, N_SEL), lambda i: (row0 // SUBLANES, 0), memory_space=pltpu.SMEM),
                  pl.BlockSpec((tb, N_SEL), lambda i: (i + blk0, 0), memory_space=pltpu.SMEM),
                  row(2 * N_SEL), row(d), row(d),
                  pl.BlockSpec((1, d), lambda i: (0, 0)),
                  pl.BlockSpec(memory_space=pl.ANY),
                  pl.BlockSpec(memory_space=pl.ANY)],
        out_specs=pl.BlockSpec((tb, d), lambda i: (i, 0)),
        scratch_shapes=[pltpu.VMEM((2, N_SEL * (d // HEAD_W), HEAD_W), jnp.uint32),
                        pltpu.SemaphoreType.DMA((2,))],
        compiler_params=pltpu.CompilerParams(dimension_semantics=("arbitrary",),
                                             vmem_limit_bytes=VMEM_LIMIT),
        name="peer_mix",
    )(eids, e_next, gates2, xn, x1, gfin, table, after)


SC_LANES = 16
SC_SUBCORES = 16
SC_TILES = 2 * SC_SUBCORES
SC_TBLK = 64
SC_SLOTS = 4
HI_MASK = 0xFFFF0000
_SC_PARAMS = pltpu.CompilerParams(needs_layout_passes=False)


def _sc_unpack(word):
    return plsc.bitcast(word << 16, F32), plsc.bitcast(word & jnp.uint32(HI_MASK), F32)


def _sc_pipelined_heads(tab_hbm, ev, rows, sem, compute):
    n_items = SC_TBLK * PEER_HEADS
    ahead = SC_SLOTS - 1
    assert n_items % SC_SLOTS == 0

    def gather(it, slot):
        ti = it // PEER_HEADS
        h = it % PEER_HEADS
        return pltpu.make_async_copy(tab_hbm.at[ev.at[ti, pl.ds(h * PEER_TOPK, PEER_TOPK)]], rows.at[slot],
                                     sem.at[slot])

    for a in range(ahead):
        gather(a, a).start()

    def group(jj, carry):
        for slot in range(SC_SLOTS):
            it = SC_SLOTS * jj + slot
            gather(jnp.minimum(it + ahead, n_items - 1), (slot + ahead) % SC_SLOTS).start()
            gather(it, slot).wait()
            compute(it, slot)
        return carry

    lax.fori_loop(0, n_items // SC_SLOTS, group, 0)
    for a in range(ahead):
        gather(n_items - 1, (n_items + a) % SC_SLOTS).wait()


def _sc_scores(x, eids, utab, row0, rows):
    d = x.shape[1]
    dw = d // 2
    nw = dw // SC_LANES
    per_tile = rows // SC_TILES
    assert rows % (SC_TILES * SC_TBLK) == 0 and row0 % SC_TBLK == 0

    def body(x_hbm, e_hbm, tab_hbm, s_hbm, xv, ev, rbuf, sv, tmp, sem):
        tile = lax.axis_index("c") * SC_SUBCORES + lax.axis_index("s")
        lane = lax.iota(jnp.int32, SC_LANES)

        def compute(it, slot):
            ti = it // PEER_HEADS
            h = it % PEER_HEADS

            def chunk(w, accs):
                off = pl.multiple_of(w * SC_LANES, SC_LANES)
                xlo = xv[ti, pl.ds(off, SC_LANES)]
                xhi = xv[ti, pl.ds(dw + off, SC_LANES)]
                out = []
                for j in range(PEER_TOPK):
                    lo, hi = _sc_unpack(rbuf[slot, j, pl.ds(off, SC_LANES)])
                    out.append(accs[j] + lo * xlo + hi * xhi)
                return tuple(out)

            accs = lax.fori_loop(0, nw, chunk, tuple(jnp.zeros((SC_LANES,), F32) for _ in range(PEER_TOPK)))
            for j in range(PEER_TOPK):
                tmp[j, :] = accs[j]
            svec = jnp.zeros((SC_LANES,), F32)
            for l in range(SC_LANES):
                svec = svec + plsc.load_gather(tmp, [lane, jnp.full((SC_LANES,), l, jnp.int32)])
            sv[ti, pl.ds(pl.multiple_of(h * PEER_TOPK, PEER_TOPK), PEER_TOPK)] = svec

        def block(b, carry):
            t_out = pl.multiple_of(tile * per_tile + b * SC_TBLK, SC_TBLK)
            pltpu.sync_copy(x_hbm.at[pl.ds(row0 + t_out, SC_TBLK)], xv)
            pltpu.sync_copy(e_hbm.at[pl.ds(row0 + t_out, SC_TBLK)], ev)
            _sc_pipelined_heads(tab_hbm, ev, rbuf, sem, compute)
            pltpu.sync_copy(sv, s_hbm.at[pl.ds(t_out, SC_TBLK)])
            return carry

        lax.fori_loop(0, per_tile // SC_TBLK, block, 0)

    return pl.kernel(
        body,
        out_type=jax.ShapeDtypeStruct((rows, N_SEL), F32),
        mesh=plsc.VectorSubcoreMesh(core_axis_name="c", subcore_axis_name="s"),
        scratch_types=[pltpu.VMEM((SC_TBLK, d), F32), pltpu.VMEM((SC_TBLK, N_SEL), jnp.int32),
                       pltpu.VMEM((SC_SLOTS, PEER_TOPK, dw), jnp.uint32), pltpu.VMEM((SC_TBLK, N_SEL), F32),
                       pltpu.VMEM((PEER_TOPK, SC_LANES), F32), pltpu.SemaphoreType.DMA((SC_SLOTS,))],
        compiler_params=_SC_PARAMS,
        name="sc_scores",
    )(x, eids, utab)


def _sc_combine(w, eids, vtab, row0):
    rows = w.shape[0]
    dw = vtab.shape[1]
    d = 2 * dw
    nw = dw // SC_LANES
    per_tile = rows // SC_TILES
    assert rows % (SC_TILES * SC_TBLK) == 0 and row0 % SC_TBLK == 0

    def body(w_hbm, e_hbm, tab_hbm, y_hbm, wv, ev, rbuf, yv, sem):
        tile = lax.axis_index("c") * SC_SUBCORES + lax.axis_index("s")
        zero_i = jnp.zeros((SC_LANES,), jnp.int32)

        def compute(it, slot):
            ti = it // PEER_HEADS
            h = it % PEER_HEADS
            wk = [plsc.load_gather(wv, [zero_i + ti, zero_i + (h * PEER_TOPK + j)]) for j in range(PEER_TOPK)]

            def chunk(c, carry):
                off = pl.multiple_of(c * SC_LANES, SC_LANES)
                alo = jnp.zeros((SC_LANES,), F32)
                ahi = jnp.zeros((SC_LANES,), F32)
                for j in range(PEER_TOPK):
                    lo, hi = _sc_unpack(rbuf[slot, j, pl.ds(off, SC_LANES)])
                    alo = alo + wk[j] * lo
                    ahi = ahi + wk[j] * hi
                plsc.addupdate(yv.at[ti, pl.ds(off, SC_LANES)], alo)
                plsc.addupdate(yv.at[ti, pl.ds(dw + off, SC_LANES)], ahi)
                return carry

            plsc.parallel_loop(0, nw, 1, carry=jnp.int32(0))(chunk)

        def block(b, carry):
            t_out = pl.multiple_of(tile * per_tile + b * SC_TBLK, SC_TBLK)
            pltpu.sync_copy(w_hbm.at[pl.ds(t_out, SC_TBLK)], wv)
            pltpu.sync_copy(e_hbm.at[pl.ds(row0 + t_out, SC_TBLK)], ev)

            def zero(i, carry):
                for ti in range(SC_TBLK):
                    yv[ti, pl.ds(pl.multiple_of(i * SC_LANES, SC_LANES), SC_LANES)] = jnp.zeros((SC_LANES,), F32)
                return carry

            lax.fori_loop(0, d // SC_LANES, zero, 0)
            _sc_pipelined_heads(tab_hbm, ev, rbuf, sem, compute)
            pltpu.sync_copy(yv, y_hbm.at[pl.ds(t_out, SC_TBLK)])
            return carry

        lax.fori_loop(0, per_tile // SC_TBLK, block, 0)

    return pl.kernel(
        body,
        out_type=jax.ShapeDtypeStruct((rows, d), F32),
        mesh=plsc.VectorSubcoreMesh(core_axis_name="c", subcore_axis_name="s"),
        scratch_types=[pltpu.VMEM((SC_TBLK, N_SEL), F32), pltpu.VMEM((SC_TBLK, N_SEL), jnp.int32),
                       pltpu.VMEM((SC_SLOTS, PEER_TOPK, dw), jnp.uint32), pltpu.VMEM((SC_TBLK, d), F32),
                       pltpu.SemaphoreType.DMA((SC_SLOTS,))],
        compiler_params=_SC_PARAMS,
        name="sc_combine",
    )(w, eids, vtab)


def _gate_kernel(s_ref, g_ref, after_ref, w_ref):
    w_ref[...] = g_ref[...] * _gelu_exact(s_ref[...])


def _gate_act(s, gates, after, row0, tb=512):
    rows = s.shape[0]
    assert rows % tb == 0 and row0 % tb == 0
    blk0 = row0 // tb
    return pl.pallas_call(
        _gate_kernel,
        out_shape=jax.ShapeDtypeStruct((rows, N_SEL), F32),
        grid=(rows // tb,),
        in_specs=[pl.BlockSpec((tb, N_SEL), lambda i: (i, 0)),
                  pl.BlockSpec((tb, N_SEL), lambda i: (i + blk0, 0)),
                  pl.BlockSpec(memory_space=pl.ANY)],
        out_specs=pl.BlockSpec((tb, N_SEL), lambda i: (i, 0)),
        compiler_params=pltpu.CompilerParams(dimension_semantics=("parallel",)),
        name="peer_gate",
    )(s, gates, after)


def _resid_norm_kernel(x1_ref, y_ref, gfin_ref, after_ref, o_ref, *, final_norm):
    z = x1_ref[...] + y_ref[...]
    o_ref[...] = _rms(z) * gfin_ref[...] if final_norm else z


def _resid_norm(x1, y, gfin, after, final_norm, tb=512):
    rows, d = y.shape
    assert rows % tb == 0
    return pl.pallas_call(
        functools.partial(_resid_norm_kernel, final_norm=final_norm),
        out_shape=jax.ShapeDtypeStruct((rows, d), F32),
        grid=(rows // tb,),
        in_specs=[pl.BlockSpec((tb, d), lambda i: (i, 0)),
                  pl.BlockSpec((tb, d), lambda i: (i, 0)),
                  pl.BlockSpec((1, d), lambda i: (0, 0)),
                  pl.BlockSpec(memory_space=pl.ANY)],
        out_specs=pl.BlockSpec((tb, d), lambda i: (i, 0)),
        compiler_params=pltpu.CompilerParams(dimension_semantics=("parallel",)),
        name="peer_resid_norm",
    )(x1, y, gfin, after)


N_CHUNKS = 16
SC_TOKEN_FRAC = 52 / 64
SPLIT_UNIT = SC_TILES * SC_TBLK
COST_FRONT, COST_SC_SCORES, COST_SC_COMBINE, COST_TC_MIX = 0.13, 0.21, 0.18, 0.85


def _sc_rows(n, tokens_per_chunk):
    n_sc = int(n * SC_TOKEN_FRAC) // SPLIT_UNIT * SPLIT_UNIT
    return [min(max(n_sc - c * tokens_per_chunk, 0), tokens_per_chunk) for c in range(N_CHUNKS)]


def _tc_schedule(tokens_per_chunk, sc_rows):
    sc_chunks = [c for c in range(N_CHUNKS) if sc_rows[c]]
    base = [("front", c) for c in range(N_CHUNKS)] + [("mix", c) for c in range(N_CHUNKS)
                                                       if sc_rows[c] < tokens_per_chunk]
    order = []
    tc = sc = 0.0
    scores_done, combine_done = {}, {}
    gates = list(sc_chunks)
    norms = list(sc_chunks)
    for kind, c in base + [("end", None)]:
        while gates and (kind == "end" or scores_done.get(gates[0], float("inf"))
                         + 0.25 * COST_SC_SCORES * sc_rows[gates[0]] <= tc):
            g = gates.pop(0)
            tc = max(tc, scores_done[g])
            sc = max(sc, tc) + COST_SC_COMBINE * sc_rows[g]
            combine_done[g] = sc
            order.append(("gate", g))
        while norms and (kind == "end" or combine_done.get(norms[0], float("inf")) <= tc):
            order.append(("norm", norms.pop(0)))
        if kind == "front":
            tc += COST_FRONT * tokens_per_chunk
            if sc_rows[c]:
                sc = max(sc, tc) + COST_SC_SCORES * sc_rows[c]
                scores_done[c] = sc
        elif kind == "mix":
            tc += COST_TC_MIX * (tokens_per_chunk - sc_rows[c])
        if kind != "end":
            order.append((kind, c))
    return order


def _layer(x2, batch, seq, params, consts, li, final_norm):
    (g_mix, w_in, lam_p, g_sub, w_out, g_ffn, wq, sk, u, v, g_fin) = params
    intra, cross, kdec, cdec, slopes = consts
    n, d = x2.shape
    nc = n // N_CHUNKS
    bc = batch // N_CHUNKS
    assert batch % N_CHUNKS == 0
    table = _pack_tables(u, v)
    utab = _pack_halves(u)
    vtab = _pack_halves(v)

    sc_rows = _sc_rows(n, nc)
    chunks = [None] * N_CHUNKS
    scores = [None] * N_CHUNKS
    ys = [None] * N_CHUNKS
    outs_sc = [None] * N_CHUNKS
    outs_tc = [None] * N_CHUNKS
    last = g_fin

    def front(c, last):
        proj = _in_proj(x2, g_mix, w_in, last, c * nc, nc)
        proj3 = proj.reshape(bc, seq, proj.shape[1])
        ret = _retention(proj3, intra, cross, kdec, cdec)
        dif = _diff_attention(proj3, slopes, lam_p, g_sub, 1.0 - li)
        half = ret.shape[2]
        x1, xn, xn_f32, q = _out_proj(x2, ret.reshape(nc, half), dif.reshape(nc, half), w_out, g_ffn, wq, c * nc)
        eids, gates, gates2 = _peer_topk(q, sk)
        chunks[c] = (x1, xn, eids, gates, gates2)
        if sc_rows[c]:
            scores[c] = _sc_scores(xn_f32, eids, utab, 0, sc_rows[c])
        return gates2

    def gate(c, last):
        _, _, eids, gates, _ = chunks[c]
        w = _gate_act(scores[c], gates, last, 0)
        ys[c] = _sc_combine(w, eids, vtab, 0)
        return w

    def tc_mix(c, last):
        x1, xn, eids, _, gates2 = chunks[c]
        outs_tc[c] = _peer_mix(eids, jnp.roll(eids, -1, axis=0), gates2, xn, x1, g_fin, table, last,
                               sc_rows[c], nc - sc_rows[c], final_norm)
        return outs_tc[c]

    def norm(c, last):
        outs_sc[c] = _resid_norm(chunks[c][0], ys[c], g_fin, last, final_norm)
        return outs_sc[c]

    run = {"front": front, "gate": gate, "mix": tc_mix, "norm": norm}
    for kind, c in _tc_schedule(nc, sc_rows):
        last = run[kind](c, last)
    pieces = [o for c in range(N_CHUNKS) for o in (outs_sc[c], outs_tc[c]) if o is not None]
    return jnp.concatenate(pieces, axis=0)


def _retention_tables(dk):
    c = RET_CHUNK
    h = RET_HEADS
    log_gamma = jnp.log1p(-jnp.exp2(-5.0 - jnp.arange(h, dtype=F32)))
    idx = jnp.arange(c, dtype=F32)
    rel = idx[:, None] - idx[None, :]
    intra = jnp.where(rel >= 0, jnp.exp(log_gamma[:, None, None] * jnp.maximum(rel, 0.0)), 0.0)
    cross = jnp.exp(log_gamma[:, None] * (idx + 1.0))
    kdec = jnp.exp(log_gamma[:, None] * (c - 1.0 - idx))
    cdec = jnp.exp(log_gamma * c)
    scale = dk ** -0.5
    return intra * scale, cross[:, :, None], (kdec * scale)[:, :, None], cdec[:, None, None]


def kernel(x, norm_mix_g, w_in, diff_lambda, diff_subln_g, w_out, norm_ffn_g, peer_wq, peer_subkeys,
           peer_u, peer_v, final_norm_g):
    b, t, d = x.shape
    depth = w_in.shape[0]
    n = b * t
    x2 = x.reshape(n, d)
    consts = _retention_tables(HEAD_W) + (
        jnp.exp2(-8.0 * (jnp.arange(DIFF_HEADS, dtype=F32) + 1.0) / DIFF_HEADS),)
    for l in range(depth):
        li = 0.8 - 0.6 * math.exp(-0.3 * l)
        params = (norm_mix_g[l][None, :], w_in[l].astype(BF16), diff_lambda[l], diff_subln_g[l][None, :],
                  w_out[l].astype(BF16), norm_ffn_g[l][None, :], peer_wq[l].astype(BF16),
                  peer_subkeys[l].astype(BF16), peer_u[l], peer_v[l], final_norm_g[None, :])
        x2 = _layer(x2, b, t, params, consts, li, final_norm=(l == depth - 1))
    return x2.reshape(b, t, d)
```

```python
import functools
import math

import jax
import jax.numpy as jnp
import numpy as np
from jax import lax
from jax.experimental import pallas as pl
from jax.experimental.pallas import tpu as pltpu
from jax.experimental.pallas import tpu_sc as plsc

F32 = jnp.float32
BF16 = jnp.bfloat16

RMS_EPS = 1e-6
NEG_INF = -1e30

RET_HEADS = 4
RET_CHUNK = 128
DIFF_HEADS = 4
HEAD_W = 128
PEER_HEADS = 8
PEER_TOPK = 16
N_KEYS = 128
N_SEL = PEER_HEADS * PEER_TOPK

SUBLANES = 8
VMEM_LIMIT = 56 * 1024 * 1024

_NT = (((1,), (1,)), ((), ()))


def _rms(x, eps=RMS_EPS):
    return x * lax.rsqrt(jnp.mean(x * x, axis=-1, keepdims=True) + eps)


def _gelu_exact(x):
    return 0.5 * x * (1.0 + lax.erf(x * (2.0 ** -0.5)))


def _inproj_kernel(x_ref, g_ref, w_ref, after_ref, o_ref, *, col_chunk):
    h = (_rms(x_ref[...]) * g_ref[...]).astype(BF16)
    for j in range(o_ref.shape[1] // col_chunk):
        sl = slice(j * col_chunk, (j + 1) * col_chunk)
        o_ref[:, sl] = jnp.dot(h, w_ref[:, sl], preferred_element_type=F32).astype(o_ref.dtype)


def _in_proj(x2, g, w_bf16, after, row0, rows, tm=512):
    d = x2.shape[1]
    cols = w_bf16.shape[1]
    assert row0 % tm == 0 and rows % tm == 0
    blk0 = row0 // tm
    return pl.pallas_call(
        functools.partial(_inproj_kernel, col_chunk=512),
        out_shape=jax.ShapeDtypeStruct((rows, cols), BF16),
        grid=(rows // tm,),
        in_specs=[pl.BlockSpec((tm, d), lambda i: (i + blk0, 0)),
                  pl.BlockSpec((1, d), lambda i: (0, 0)),
                  pl.BlockSpec((d, cols), lambda i: (0, 0)),
                  pl.BlockSpec(memory_space=pl.ANY)],
        out_specs=pl.BlockSpec((tm, cols), lambda i: (i, 0)),
        compiler_params=pltpu.CompilerParams(dimension_semantics=("parallel",),
                                             vmem_limit_bytes=VMEM_LIMIT),
        name="in_proj",
    )(x2, g, w_bf16, after)


def _ret_kernel(q_ref, k_ref, v_ref, g_ref, intra_ref, cross_ref, kdec_ref, cdec_ref, o_ref):
    c = RET_CHUNK
    n_chunks = q_ref.shape[1] // c
    intra = intra_ref[0]
    cross = cross_ref[0]
    kdec = kdec_ref[0]
    cdec = cdec_ref[0]

    def body(n, s):
        sl = pl.ds(pl.multiple_of(n * c, c), c)
        qi = q_ref[0, sl, :]
        ki = k_ref[0, sl, :]
        vi = v_ref[0, sl, :]
        a = lax.dot_general(qi, ki, _NT, preferred_element_type=F32) * intra
        o = jnp.dot(a.astype(BF16), vi, preferred_element_type=F32)
        o = o + jnp.dot(qi, s.astype(BF16), preferred_element_type=F32) * cross
        kd_t = (ki.astype(F32) * kdec).T.astype(BF16)
        s = s * cdec + jnp.dot(kd_t, vi, preferred_element_type=F32)
        gate = g_ref[0, sl, :].astype(F32)
        o_ref[0, sl, :] = (gate * jax.nn.sigmoid(gate) * _rms(o)).astype(o_ref.dtype)
        return s

    lax.fori_loop(0, n_chunks, body, jnp.zeros((HEAD_W, HEAD_W), F32), unroll=8)


def _retention(proj3, intra, cross, kdec, cdec):
    b, t, _ = proj3.shape
    h = RET_HEADS

    def col(off):
        return pl.BlockSpec((1, t, HEAD_W), lambda bi, hi: (bi, 0, off + hi))

    def per_head(shape):
        return pl.BlockSpec((1,) + shape, lambda bi, hi: (hi, 0, 0))

    return pl.pallas_call(
        _ret_kernel,
        out_shape=jax.ShapeDtypeStruct((b, t, h * HEAD_W), BF16),
        grid=(b, h),
        in_specs=[col(0), col(h), col(2 * h), col(3 * h),
                  per_head((RET_CHUNK, RET_CHUNK)), per_head((RET_CHUNK, 1)),
                  per_head((RET_CHUNK, 1)), per_head((1, 1))],
        out_specs=pl.BlockSpec((1, t, HEAD_W), lambda bi, hi: (bi, 0, hi)),
        compiler_params=pltpu.CompilerParams(dimension_semantics=("parallel", "parallel"),
                                             vmem_limit_bytes=VMEM_LIMIT),
        name="retention",
    )(proj3, proj3, proj3, proj3, intra, cross, kdec, cdec)


def _diff_kernel(slope_ref, lam_ref, rel_ref, q_ref, k_ref, v_ref, gsub_ref, o_ref, *, tq, tk, out_scale):
    hi = pl.program_id(1)
    qi = pl.program_id(2)
    half = HEAD_W // 2
    slope = slope_ref[hi]

    lp = lam_ref[...]
    lam = (jnp.exp(jnp.sum(lp[0:1] * lp[1:2], axis=1, keepdims=True))
           - jnp.exp(jnp.sum(lp[2:3] * lp[3:4], axis=1, keepdims=True))
           + (1.0 - out_scale))

    q = q_ref[0] * jnp.asarray(half ** -0.5, BF16)
    lane = lax.broadcasted_iota(jnp.int32, q.shape, 1)
    zero = jnp.zeros_like(q)
    qs = jnp.concatenate([jnp.where(lane < half, q, zero),
                          jnp.where(lane >= half, q, zero)], axis=0)

    rel = rel_ref[...]
    rel_bias = -slope * rel

    def body(j, carry):
        m, l, acc = carry
        ksl = pl.ds(pl.multiple_of(j * tk, tk), tk)
        k = k_ref[0, ksl, :]
        v = v_ref[0, ksl, :]
        off = (qi * tq - j * tk).astype(F32)
        s = lax.dot_general(qs, k, _NT, preferred_element_type=F32)
        s = s + jnp.where(rel >= -off, rel_bias - slope * off, NEG_INF)
        m_new = jnp.maximum(m, jnp.max(s, axis=1, keepdims=True))
        alpha = jnp.exp(m - m_new)
        p = jnp.exp(s - m_new)
        l = alpha * l + jnp.sum(p, axis=1, keepdims=True)
        acc = alpha * acc + jnp.dot(p.astype(BF16), v, preferred_element_type=F32)
        return m_new, l, acc

    n_kv = (qi * tq + tq - 1) // tk + 1
    m0 = jnp.full((2 * tq, 1), NEG_INF, F32)
    l0 = jnp.zeros((2 * tq, 1), F32)
    a0 = jnp.zeros((2 * tq, HEAD_W), F32)
    _, l, acc = lax.fori_loop(0, n_kv, body, (m0, l0, a0))
    o = acc / l
    dy = o[:tq] - lam * o[tq:]
    o_ref[0] = (_rms(dy) * gsub_ref[...] * out_scale).astype(o_ref.dtype)


def _diff_attention(proj3, slopes, diff_lambda_l, gsub, out_scale, tq=256, tk=512):
    b, t, _ = proj3.shape
    h = DIFF_HEADS
    base = 4 * RET_HEADS
    kern = functools.partial(_diff_kernel, tq=tq, tk=tk, out_scale=out_scale)
    rel = jnp.asarray((np.arange(2 * tq) % tq)[:, None] - np.arange(tk)[None, :], F32)
    return pl.pallas_call(
        kern,
        out_shape=jax.ShapeDtypeStruct((b, t, h * HEAD_W), BF16),
        grid=(b, h, t // tq),
        in_specs=[pl.BlockSpec(memory_space=pltpu.SMEM),
                  pl.BlockSpec(diff_lambda_l.shape, lambda bi, hi, qi: (0, 0)),
                  pl.BlockSpec((2 * tq, tk), lambda bi, hi, qi: (0, 0)),
                  pl.BlockSpec((1, tq, HEAD_W), lambda bi, hi, qi: (bi, qi, base + hi)),
                  pl.BlockSpec((1, t, HEAD_W), lambda bi, hi, qi: (bi, 0, base + h + hi)),
                  pl.BlockSpec((1, t, HEAD_W), lambda bi, hi, qi: (bi, 0, base + 2 * h + hi)),
                  pl.BlockSpec((1, HEAD_W), lambda bi, hi, qi: (0, 0))],
        out_specs=pl.BlockSpec((1, tq, HEAD_W), lambda bi, hi, qi: (bi, qi, hi)),
        compiler_params=pltpu.CompilerParams(
            dimension_semantics=("parallel", "parallel", "arbitrary"),
            vmem_limit_bytes=VMEM_LIMIT),
        name="diff_attention",
    )(slopes, diff_lambda_l, rel, proj3, proj3, proj3, gsub)


def _outproj_kernel(x_ref, ret_ref, dif_ref, wo_ref, g_ref, wq_ref, x1_ref, xn_ref, xnf_ref, q_ref, *, col_chunk):
    half = ret_ref.shape[1]
    mix = jnp.dot(ret_ref[...], wo_ref[:half, :], preferred_element_type=F32)
    mix = mix + jnp.dot(dif_ref[...], wo_ref[half:, :], preferred_element_type=F32)
    x1 = x_ref[...] + mix
    x1_ref[...] = x1
    xn = (_rms(x1) * g_ref[...]).astype(BF16)
    xn_ref[...] = xn
    xnf_ref[...] = xn.astype(F32)
    for j in range(q_ref.shape[1] // col_chunk):
        sl = slice(j * col_chunk, (j + 1) * col_chunk)
        q_ref[:, sl] = jnp.dot(xn, wq_ref[:, sl], preferred_element_type=F32).astype(q_ref.dtype)


def _out_proj(x2, ret2, dif2, wo_bf16, g, wq_bf16, row0, tm=512):
    n, half = ret2.shape
    d = x2.shape[1]
    qc = wq_bf16.shape[1]
    assert row0 % tm == 0 and n % tm == 0
    blk0 = row0 // tm
    row = lambda w: pl.BlockSpec((tm, w), lambda i: (i, 0))
    full = lambda a: pl.BlockSpec(a.shape, lambda i: (0, 0))
    return pl.pallas_call(
        functools.partial(_outproj_kernel, col_chunk=512),
        out_shape=(jax.ShapeDtypeStruct((n, d), F32),
                   jax.ShapeDtypeStruct((n, d), BF16),
                   jax.ShapeDtypeStruct((n, d), F32),
                   jax.ShapeDtypeStruct((n, qc), BF16)),
        grid=(n // tm,),
        in_specs=[pl.BlockSpec((tm, d), lambda i: (i + blk0, 0)), row(half), row(half),
                  full(wo_bf16), full(g), full(wq_bf16)],
        out_specs=(row(d), row(d), row(d), row(qc)),
        compiler_params=pltpu.CompilerParams(dimension_semantics=("parallel",),
                                             vmem_limit_bytes=VMEM_LIMIT),
        name="out_proj",
    )(x2, ret2, dif2, wo_bf16, g, wq_bf16)


def _split3(v):
    trunc = lambda x: pltpu.bitcast(pltpu.bitcast(x, jnp.uint32) & jnp.uint32(0xFFFF0000), F32)
    hi = trunc(v)
    r1 = v - hi
    mid = trunc(r1)
    return hi.astype(BF16), mid.astype(BF16), (r1 - mid).astype(BF16)


def _spread_exact(v, sel_bf16):
    hi, mid, lo = _split3(v)
    d = lambda a: jnp.dot(a, sel_bf16, preferred_element_type=F32)
    return (d(hi) + d(mid)) + d(lo)


def _spread_exact_left(sel_bf16, v):
    hi, mid, lo = _split3(v)
    d = lambda a: jnp.dot(sel_bf16, a, preferred_element_type=F32)
    return (d(hi) + d(mid)) + d(lo)


def _topk_kernel(q_ref, sk_ref, sel_ref, pad_ref, e_ref, g_ref, g2_ref):
    tb = q_ref.shape[0]
    k = PEER_TOPK
    n_sets = 2 * PEER_HEADS
    neg = -jnp.inf

    s = jnp.stack(
        [lax.dot_general(sk_ref[j // 2, j % 2], q_ref[:, j * N_KEYS:(j + 1) * N_KEYS], _NT,
                         preferred_element_type=F32) for j in range(n_sets)], axis=0)
    key_f = lax.broadcasted_iota(jnp.int32, s.shape, 1).astype(F32)
    rank = lax.broadcasted_iota(jnp.int32, (n_sets, k, tb), 1)
    top_v = jnp.zeros((n_sets, k, tb), F32)
    top_i = jnp.zeros((n_sets, k, tb), F32)
    for r in range(k):
        m = jnp.max(s, axis=1, keepdims=True)
        i = jnp.min(jnp.where(s == m, key_f, float(N_KEYS)), axis=1, keepdims=True)
        s = jnp.where(key_f == i, neg, s)
        top_v = jnp.where(rank == r, m, top_v)
        top_i = jnp.where(rank == r, i, top_i)

    pad = pad_ref[...]
    cands, eids = [], []
    for h in range(PEER_HEADS):
        cands.append(_spread_exact_left(sel_ref[0], top_v[2 * h]) + _spread_exact_left(sel_ref[1], top_v[2 * h + 1])
                     + pad)
        eids.append(jnp.dot(sel_ref[0], top_i[2 * h].astype(BF16), preferred_element_type=F32) * float(N_KEYS)
                    + jnp.dot(sel_ref[1], top_i[2 * h + 1].astype(BF16), preferred_element_type=F32))
    cand = jnp.stack(cands, axis=0)
    eid = jnp.stack(eids, axis=0)
    n_cand = cand.shape[1]

    c_f = lax.broadcasted_iota(jnp.int32, cand.shape, 1).astype(F32)
    rank2 = lax.broadcasted_iota(jnp.int32, (PEER_HEADS, k, tb), 1)
    top_c = []
    e_rank = jnp.zeros((PEER_HEADS, k, tb), F32)
    for r in range(k):
        m = jnp.max(cand, axis=1, keepdims=True)
        pos = jnp.min(jnp.where(cand == m, c_f, float(n_cand)), axis=1, keepdims=True)
        hit = c_f == pos
        e_rank = jnp.where(rank2 == r, jnp.sum(jnp.where(hit, eid, 0.0), axis=1, keepdims=True), e_rank)
        cand = jnp.where(hit, neg, cand)
        top_c.append(m)
    p_rank = jnp.zeros((PEER_HEADS, k, tb), F32)
    denom = jnp.zeros((PEER_HEADS, 1, tb), F32)
    for r in range(k):
        p = jnp.exp(top_c[r] - top_c[0])
        denom = denom + p
        p_rank = jnp.where(rank2 == r, p, p_rank)
    g_rank = p_rank / denom

    e_out = e_rank.reshape(PEER_HEADS * k, tb).T
    g1 = g_rank.reshape(PEER_HEADS * k, tb).T
    sr = lax.broadcasted_iota(jnp.int32, (N_SEL, 2 * N_SEL), 0)
    sc = lax.broadcasted_iota(jnp.int32, (N_SEL, 2 * N_SEL), 1)
    e_ref[...] = e_out.astype(jnp.int32)
    g_ref[...] = g1
    g2_ref[...] = _spread_exact(g1, jnp.where(sc == 2 * sr + 1, 1.0, 0.0).astype(BF16))


def _candidate_tables(tb):
    k = PEER_TOPK
    pairs = [(a, b) for a in range(k) for b in range(k) if (a + 1) * (b + 1) <= k]
    n_cand = -(-len(pairs) // SUBLANES) * SUBLANES
    sel = np.zeros((2, n_cand, k), np.float32)
    pad = np.full((n_cand, tb), -np.inf, np.float32)
    for c, (a, b) in enumerate(pairs):
        sel[0, c, a] = 1.0
        sel[1, c, b] = 1.0
        pad[c, :] = 0.0
    return jnp.asarray(sel, BF16), jnp.asarray(pad)


def _peer_topk(q2, sk_bf16, tb=128):
    n, qc = q2.shape
    sel, pad = _candidate_tables(tb)
    const = lambda a: pl.BlockSpec(a.shape, lambda i: (0,) * a.ndim)
    return pl.pallas_call(
        _topk_kernel,
        out_shape=(jax.ShapeDtypeStruct((n, N_SEL), jnp.int32),
                   jax.ShapeDtypeStruct((n, N_SEL), F32),
                   jax.ShapeDtypeStruct((n, 2 * N_SEL), F32)),
        grid=(n // tb,),
        in_specs=[pl.BlockSpec((tb, qc), lambda i: (i, 0)), const(sk_bf16), const(sel), const(pad)],
        out_specs=(pl.BlockSpec((tb, N_SEL), lambda i: (i, 0)),
                   pl.BlockSpec((tb, N_SEL), lambda i: (i, 0)),
                   pl.BlockSpec((tb, 2 * N_SEL), lambda i: (i, 0))),
        compiler_params=pltpu.CompilerParams(dimension_semantics=("parallel",),
                                             vmem_limit_bytes=VMEM_LIMIT),
        name="peer_topk",
    )(q2, sk_bf16, sel, pad)


def _bf16_bits(t):
    return lax.bitcast_convert_type(t.astype(BF16), jnp.uint16).astype(jnp.uint32)


def _pack_tables(u, v):
    e, d = u.shape
    return (_bf16_bits(u) | (_bf16_bits(v) << 16)).reshape(e, d // HEAD_W, HEAD_W)


def _pack_halves(t):
    b = _bf16_bits(t)
    half = t.shape[1] // 2
    return b[:, :half] | (b[:, half:] << 16)


def _mix_kernel(efirst_ref, enext_ref, g2_ref, xn_ref, x1_ref, gfin_ref, tab_hbm, after_ref, o_ref, buf, sem, *,
                final_norm):
    step = pl.program_id(0)
    tb = xn_ref.shape[0]
    n_chunks = tab_hbm.shape[1]

    def issue(e_ref, t, slot):
        for k in range(N_SEL):
            e = e_ref[t, k]
            pltpu.make_async_copy(tab_hbm.at[e], buf.at[slot, pl.ds(k * n_chunks, n_chunks), :],
                                  sem.at[slot]).start(priority=k % 2)

    def wait(slot):
        pltpu.make_async_copy(buf.at[slot], buf.at[slot], sem.at[slot]).wait()

    @pl.when(step == 0)
    def _():
        issue(efirst_ref, 0, 0)

    rid = lax.broadcasted_iota(jnp.int32, (SUBLANES, 2 * N_SEL), 0)

    def group(gi, c):
        base = pl.multiple_of(gi * SUBLANES, SUBLANES)
        xrows = xn_ref[pl.ds(base, SUBLANES), :]
        grows = g2_ref[pl.ds(base, SUBLANES), :]
        acc = x1_ref[pl.ds(base, SUBLANES), :]
        for j in range(SUBLANES):
            slot = j % 2
            issue(enext_ref, base + j, 1 - slot)
            wait(slot)
            uv = jnp.concatenate(
                [pltpu.bitcast(buf[slot, pl.ds(c, N_SEL, stride=n_chunks), :], BF16) for c in range(n_chunks)],
                axis=1)
            s2 = lax.dot_general(xrows, uv, _NT, preferred_element_type=F32)
            act = pltpu.roll(_gelu_exact(s2), 1, axis=1)
            w2 = jnp.where(rid == j, grows * act, 0.0)
            acc = acc + jnp.dot(w2.astype(BF16), uv, preferred_element_type=F32)
        o_ref[pl.ds(base, SUBLANES), :] = acc
        return c

    lax.fori_loop(0, tb // SUBLANES, group, 0)

    @pl.when(step == pl.num_programs(0) - 1)
    def _():
        wait(0)

    if final_norm:
        o_ref[...] = _rms(o_ref[...]) * gfin_ref[...]


def _peer_mix(eids, e_next, gates2, xn, x1, gfin, table, after, row0, rows, final_norm, tb=64):
    d = x1.shape[1]
    assert tb % (2 * SUBLANES) == 0 and row0 % tb == 0 and rows % tb == 0
    blk0 = row0 // tb
    row = lambda w: pl.BlockSpec((tb, w), lambda i: (i + blk0, 0))
    return pl.pallas_call(
        functools.partial(_mix_kernel, final_norm=final_norm),
        out_shape=jax.ShapeDtypeStruct((rows, d), F32),
        grid=(rows // tb,),
        in_specs=[pl.BlockSpec((SUBLANES, N_SEL), lambda i: (row0 // SUBLANES, 0), memory_space=pltpu.SMEM),
                  pl.BlockSpec((tb, N_SEL), lambda i: (i + blk0, 0), memory_space=pltpu.SMEM),
                  row(2 * N_SEL), row(d), row(d),
                  pl.BlockSpec((1, d), lambda i: (0, 0)),
                  pl.BlockSpec(memory_space=pl.ANY),
                  pl.BlockSpec(memory_space=pl.ANY)],
        out_specs=pl.BlockSpec((tb, d), lambda i: (i, 0)),
        scratch_shapes=[pltpu.VMEM((2, N_SEL * (d // HEAD_W), HEAD_W), jnp.uint32),
                        pltpu.SemaphoreType.DMA((2,))],
        compiler_params=pltpu.CompilerParams(dimension_semantics=("arbitrary",),
                                             vmem_limit_bytes=VMEM_LIMIT),
        name="peer_mix",
    )(eids, e_next, gates2, xn, x1, gfin, table, after)


SC_LANES = 16
SC_SUBCORES = 16
SC_TILES = 2 * SC_SUBCORES
SC_BLOCKS = (64, 32, 16, 8)
SC_SLOTS = 4
HI_MASK = 0xFFFF0000
_SC_PARAMS = pltpu.CompilerParams(needs_layout_passes=False)


def _sc_unpack(word):
    return plsc.bitcast(word << 16, F32), plsc.bitcast(word & jnp.uint32(HI_MASK), F32)


def _sc_block(rows):
    assert rows % (SC_TILES * SC_BLOCKS[-1]) == 0
    return next(b for b in SC_BLOCKS if rows % (SC_TILES * b) == 0)


def _sc_pipelined_heads(tab_hbm, ev, rows, sem, compute, tblk):
    n_items = tblk * PEER_HEADS
    ahead = SC_SLOTS - 1
    assert n_items % SC_SLOTS == 0

    def gather(it, slot):
        ti = it // PEER_HEADS
        h = it % PEER_HEADS
        return pltpu.make_async_copy(tab_hbm.at[ev.at[ti, pl.ds(h * PEER_TOPK, PEER_TOPK)]], rows.at[slot],
                                     sem.at[slot])

    for a in range(ahead):
        gather(a, a).start()

    def group(jj, carry):
        for slot in range(SC_SLOTS):
            it = SC_SLOTS * jj + slot
            gather(jnp.minimum(it + ahead, n_items - 1), (slot + ahead) % SC_SLOTS).start()
            gather(it, slot).wait()
            compute(it, slot)
        return carry

    lax.fori_loop(0, n_items // SC_SLOTS, group, 0)
    for a in range(ahead):
        gather(n_items - 1, (n_items + a) % SC_SLOTS).wait()


def _sc_scores(x, eids, utab, row0, rows):
    d = x.shape[1]
    dw = d // 2
    nw = dw // SC_LANES
    per_tile = rows // SC_TILES
    tblk = _sc_block(rows)
    assert row0 % SC_BLOCKS[0] == 0

    def body(x_hbm, e_hbm, tab_hbm, s_hbm, xv, ev, rbuf, sv, tmp, sem):
        tile = lax.axis_index("c") * SC_SUBCORES + lax.axis_index("s")
        lane = lax.iota(jnp.int32, SC_LANES)

        def compute(it, slot):
            ti = it // PEER_HEADS
            h = it % PEER_HEADS

            def chunk(w, accs):
                off = pl.multiple_of(w * SC_LANES, SC_LANES)
                xlo = xv[ti, pl.ds(off, SC_LANES)]
                xhi = xv[ti, pl.ds(dw + off, SC_LANES)]
                out = []
                for j in range(PEER_TOPK):
                    lo, hi = _sc_unpack(rbuf[slot, j, pl.ds(off, SC_LANES)])
                    out.append(accs[j] + lo * xlo + hi * xhi)
                return tuple(out)

            accs = lax.fori_loop(0, nw, chunk, tuple(jnp.zeros((SC_LANES,), F32) for _ in range(PEER_TOPK)))
            for j in range(PEER_TOPK):
                tmp[j, :] = accs[j]
            svec = jnp.zeros((SC_LANES,), F32)
            for l in range(SC_LANES):
                svec = svec + plsc.load_gather(tmp, [lane, jnp.full((SC_LANES,), l, jnp.int32)])
            sv[ti, pl.ds(pl.multiple_of(h * PEER_TOPK, PEER_TOPK), PEER_TOPK)] = svec

        def block(b, carry):
            t_out = pl.multiple_of(tile * per_tile + b * tblk, tblk)
            pltpu.sync_copy(x_hbm.at[pl.ds(row0 + t_out, tblk)], xv)
            pltpu.sync_copy(e_hbm.at[pl.ds(row0 + t_out, tblk)], ev)
            _sc_pipelined_heads(tab_hbm, ev, rbuf, sem, compute, tblk)
            pltpu.sync_copy(sv, s_hbm.at[pl.ds(t_out, tblk)])
            return carry

        lax.fori_loop(0, per_tile // tblk, block, 0)

    return pl.kernel(
        body,
        out_type=jax.ShapeDtypeStruct((rows, N_SEL), F32),
        mesh=plsc.VectorSubcoreMesh(core_axis_name="c", subcore_axis_name="s"),
        scratch_types=[pltpu.VMEM((tblk, d), F32), pltpu.VMEM((tblk, N_SEL), jnp.int32),
                       pltpu.VMEM((SC_SLOTS, PEER_TOPK, dw), jnp.uint32), pltpu.VMEM((tblk, N_SEL), F32),
                       pltpu.VMEM((PEER_TOPK, SC_LANES), F32), pltpu.SemaphoreType.DMA((SC_SLOTS,))],
        compiler_params=_SC_PARAMS,
        name="sc_scores",
    )(x, eids, utab)


def _sc_combine(w, eids, vtab, row0):
    rows = w.shape[0]
    dw = vtab.shape[1]
    d = 2 * dw
    nw = dw // SC_LANES
    per_tile = rows // SC_TILES
    tblk = _sc_block(rows)
    assert row0 % SC_BLOCKS[0] == 0

    def body(w_hbm, e_hbm, tab_hbm, y_hbm, wv, ev, rbuf, yv, sem):
        tile = lax.axis_index("c") * SC_SUBCORES + lax.axis_index("s")
        zero_i = jnp.zeros((SC_LANES,), jnp.int32)

        def compute(it, slot):
            ti = it // PEER_HEADS
            h = it % PEER_HEADS
            wk = [plsc.load_gather(wv, [zero_i + ti, zero_i + (h * PEER_TOPK + j)]) for j in range(PEER_TOPK)]

            def chunk(c, carry):
                off = pl.multiple_of(c * SC_LANES, SC_LANES)
                alo = jnp.zeros((SC_LANES,), F32)
                ahi = jnp.zeros((SC_LANES,), F32)
                for j in range(PEER_TOPK):
                    lo, hi = _sc_unpack(rbuf[slot, j, pl.ds(off, SC_LANES)])
                    alo = alo + wk[j] * lo
                    ahi = ahi + wk[j] * hi
                plsc.addupdate(yv.at[ti, pl.ds(off, SC_LANES)], alo)
                plsc.addupdate(yv.at[ti, pl.ds(dw + off, SC_LANES)], ahi)
                return carry

            plsc.parallel_loop(0, nw, 1, carry=jnp.int32(0))(chunk)

        def block(b, carry):
            t_out = pl.multiple_of(tile * per_tile + b * tblk, tblk)
            pltpu.sync_copy(w_hbm.at[pl.ds(t_out, tblk)], wv)
            pltpu.sync_copy(e_hbm.at[pl.ds(row0 + t_out, tblk)], ev)

            def zero(i, carry):
                for ti in range(tblk):
                    yv[ti, pl.ds(pl.multiple_of(i * SC_LANES, SC_LANES), SC_LANES)] = jnp.zeros((SC_LANES,), F32)
                return carry

            lax.fori_loop(0, d // SC_LANES, zero, 0)
            _sc_pipelined_heads(tab_hbm, ev, rbuf, sem, compute, tblk)
            pltpu.sync_copy(yv, y_hbm.at[pl.ds(t_out, tblk)])
            return carry

        lax.fori_loop(0, per_tile // tblk, block, 0)

    return pl.kernel(
        body,
        out_type=jax.ShapeDtypeStruct((rows, d), F32),
        mesh=plsc.VectorSubcoreMesh(core_axis_name="c", subcore_axis_name="s"),
        scratch_types=[pltpu.VMEM((tblk, N_SEL), F32), pltpu.VMEM((tblk, N_SEL), jnp.int32),
                       pltpu.VMEM((SC_SLOTS, PEER_TOPK, dw), jnp.uint32), pltpu.VMEM((tblk, d), F32),
                       pltpu.SemaphoreType.DMA((SC_SLOTS,))],
        compiler_params=_SC_PARAMS,
        name="sc_combine",
    )(w, eids, vtab)


def _gate_kernel(s_ref, g_ref, after_ref, w_ref):
    w_ref[...] = g_ref[...] * _gelu_exact(s_ref[...])


def _gate_act(s, gates, after, row0, tb=512):
    rows = s.shape[0]
    tb = math.gcd(rows, tb)
    assert row0 % tb == 0
    blk0 = row0 // tb
    return pl.pallas_call(
        _gate_kernel,
        out_shape=jax.ShapeDtypeStruct((rows, N_SEL), F32),
        grid=(rows // tb,),
        in_specs=[pl.BlockSpec((tb, N_SEL), lambda i: (i, 0)),
                  pl.BlockSpec((tb, N_SEL), lambda i: (i + blk0, 0)),
                  pl.BlockSpec(memory_space=pl.ANY)],
        out_specs=pl.BlockSpec((tb, N_SEL), lambda i: (i, 0)),
        compiler_params=pltpu.CompilerParams(dimension_semantics=("parallel",)),
        name="peer_gate",
    )(s, gates, after)


def _resid_norm_kernel(x1_ref, y_ref, gfin_ref, after_ref, o_ref, *, final_norm):
    z = x1_ref[...] + y_ref[...]
    o_ref[...] = _rms(z) * gfin_ref[...] if final_norm else z


def _resid_norm(x1, y, gfin, after, final_norm, tb=512):
    rows, d = y.shape
    tb = math.gcd(rows, tb)
    return pl.pallas_call(
        functools.partial(_resid_norm_kernel, final_norm=final_norm),
        out_shape=jax.ShapeDtypeStruct((rows, d), F32),
        grid=(rows // tb,),
        in_specs=[pl.BlockSpec((tb, d), lambda i: (i, 0)),
                  pl.BlockSpec((tb, d), lambda i: (i, 0)),
                  pl.BlockSpec((1, d), lambda i: (0, 0)),
                  pl.BlockSpec(memory_space=pl.ANY)],
        out_specs=pl.BlockSpec((tb, d), lambda i: (i, 0)),
        compiler_params=pltpu.CompilerParams(dimension_semantics=("parallel",)),
        name="peer_resid_norm",
    )(x1, y, gfin, after)


N_CHUNKS = 16
SC_TOKEN_FRAC = 103 / 128
SPLIT_UNIT = SC_TILES * SC_BLOCKS[-1]
COST_FRONT, COST_SC_SCORES, COST_SC_COMBINE, COST_TC_MIX = 0.13, 0.21, 0.18, 0.85


def _sc_rows(n, tokens_per_chunk):
    n_sc = int(n * SC_TOKEN_FRAC) // SPLIT_UNIT * SPLIT_UNIT
    return [min(max(n_sc - c * tokens_per_chunk, 0), tokens_per_chunk) for c in range(N_CHUNKS)]


def _tc_schedule(tokens_per_chunk, sc_rows):
    sc_chunks = [c for c in range(N_CHUNKS) if sc_rows[c]]
    base = [("front", c) for c in range(N_CHUNKS)] + [("mix", c) for c in range(N_CHUNKS)
                                                       if sc_rows[c] < tokens_per_chunk]
    order = []
    tc = sc = 0.0
    scores_done, combine_done = {}, {}
    gates = list(sc_chunks)
    norms = list(sc_chunks)
    for kind, c in base + [("end", None)]:
        while gates and (kind == "end" or scores_done.get(gates[0], float("inf"))
                         + 0.25 * COST_SC_SCORES * sc_rows[gates[0]] <= tc):
            g = gates.pop(0)
            tc = max(tc, scores_done[g])
            sc = max(sc, tc) + COST_SC_COMBINE * sc_rows[g]
            combine_done[g] = sc
            order.append(("gate", g))
        while norms and (kind == "end" or combine_done.get(norms[0], float("inf")) <= tc):
            order.append(("norm", norms.pop(0)))
        if kind == "front":
            tc += COST_FRONT * tokens_per_chunk
            if sc_rows[c]:
                sc = max(sc, tc) + COST_SC_SCORES * sc_rows[c]
                scores_done[c] = sc
        elif kind == "mix":
            tc += COST_TC_MIX * (tokens_per_chunk - sc_rows[c])
        if kind != "end":
            order.append((kind, c))
    return order


def _layer(x2, batch, seq, params, consts, li, final_norm):
    (g_mix, w_in, lam_p, g_sub, w_out, g_ffn, wq, sk, u, v, g_fin) = params
    intra, cross, kdec, cdec, slopes = consts
    n, d = x2.shape
    nc = n // N_CHUNKS
    bc = batch // N_CHUNKS
    assert batch % N_CHUNKS == 0
    table = _pack_tables(u, v)
    utab = _pack_halves(u)
    vtab = _pack_halves(v)

    sc_rows = _sc_rows(n, nc)
    chunks = [None] * N_CHUNKS
    scores = [None] * N_CHUNKS
    ys = [None] * N_CHUNKS
    outs_sc = [None] * N_CHUNKS
    outs_tc = [None] * N_CHUNKS
    last = g_fin

    def front(c, last):
        proj = _in_proj(x2, g_mix, w_in, last, c * nc, nc)
        proj3 = proj.reshape(bc, seq, proj.shape[1])
        ret = _retention(proj3, intra, cross, kdec, cdec)
        dif = _diff_attention(proj3, slopes, lam_p, g_sub, 1.0 - li)
        half = ret.shape[2]
        x1, xn, xn_f32, q = _out_proj(x2, ret.reshape(nc, half), dif.reshape(nc, half), w_out, g_ffn, wq, c * nc)
        eids, gates, gates2 = _peer_topk(q, sk)
        chunks[c] = (x1, xn, eids, gates, gates2)
        if sc_rows[c]:
            scores[c] = _sc_scores(xn_f32, eids, utab, 0, sc_rows[c])
        return gates2

    def gate(c, last):
        _, _, eids, gates, _ = chunks[c]
        w = _gate_act(scores[c], gates, last, 0)
        ys[c] = _sc_combine(w, eids, vtab, 0)
        return w

    def tc_mix(c, last):
        x1, xn, eids, _, gates2 = chunks[c]
        outs_tc[c] = _peer_mix(eids, jnp.roll(eids, -1, axis=0), gates2, xn, x1, g_fin, table, last,
                               sc_rows[c], nc - sc_rows[c], final_norm)
        return outs_tc[c]

    def norm(c, last):
        outs_sc[c] = _resid_norm(chunks[c][0], ys[c], g_fin, last, final_norm)
        return outs_sc[c]

    run = {"front": front, "gate": gate, "mix": tc_mix, "norm": norm}
    for kind, c in _tc_schedule(nc, sc_rows):
        last = run[kind](c, last)
    pieces = [o for c in range(N_CHUNKS) for o in (outs_sc[c], outs_tc[c]) if o is not None]
    return jnp.concatenate(pieces, axis=0)


def _retention_tables(dk):
    c = RET_CHUNK
    h = RET_HEADS
    log_gamma = jnp.log1p(-jnp.exp2(-5.0 - jnp.arange(h, dtype=F32)))
    idx = jnp.arange(c, dtype=F32)
    rel = idx[:, None] - idx[None, :]
    intra = jnp.where(rel >= 0, jnp.exp(log_gamma[:, None, None] * jnp.maximum(rel, 0.0)), 0.0)
    cross = jnp.exp(log_gamma[:, None] * (idx + 1.0))
    kdec = jnp.exp(log_gamma[:, None] * (c - 1.0 - idx))
    cdec = jnp.exp(log_gamma * c)
    scale = dk ** -0.5
    return intra * scale, cross[:, :, None], (kdec * scale)[:, :, None], cdec[:, None, None]


def kernel(x, norm_mix_g, w_in, diff_lambda, diff_subln_g, w_out, norm_ffn_g, peer_wq, peer_subkeys,
           peer_u, peer_v, final_norm_g):
    b, t, d = x.shape
    depth = w_in.shape[0]
    n = b * t
    x2 = x.reshape(n, d)
    consts = _retention_tables(HEAD_W) + (
        jnp.exp2(-8.0 * (jnp.arange(DIFF_HEADS, dtype=F32) + 1.0) / DIFF_HEADS),)
    for l in range(depth):
        li = 0.8 - 0.6 * math.exp(-0.3 * l)
        params = (norm_mix_g[l][None, :], w_in[l].astype(BF16), diff_lambda[l], diff_subln_g[l][None, :],
                  w_out[l].astype(BF16), norm_ffn_g[l][None, :], peer_wq[l].astype(BF16),
                  peer_subkeys[l].astype(BF16), peer_u[l], peer_v[l], final_norm_g[None, :])
        x2 = _layer(x2, b, t, params, consts, li, final_norm=(l == depth - 1))
    return x2.reshape(b, t, d)
```

```python
import functools
import math

import jax
import jax.numpy as jnp
import numpy as np
from jax import lax
from jax.experimental import pallas as pl
from jax.experimental.pallas import tpu as pltpu
from jax.experimental.pallas import tpu_sc as plsc

F32 = jnp.float32
BF16 = jnp.bfloat16

RMS_EPS = 1e-6
NEG_INF = -1e30

RET_HEADS = 4
RET_CHUNK = 128
DIFF_HEADS = 4
HEAD_W = 128
PEER_HEADS = 8
PEER_TOPK = 16
N_KEYS = 128
N_SEL = PEER_HEADS * PEER_TOPK

SUBLANES = 8
VMEM_LIMIT = 56 * 1024 * 1024

_NT = (((1,), (1,)), ((), ()))


def _rms(x, eps=RMS_EPS):
    return x * lax.rsqrt(jnp.mean(x * x, axis=-1, keepdims=True) + eps)


def _gelu_exact(x):
    return 0.5 * x * (1.0 + lax.erf(x * (2.0 ** -0.5)))


def _inproj_kernel(x_ref, g_ref, w_ref, after_ref, o_ref, *, col_chunk):
    h = (_rms(x_ref[...]) * g_ref[...]).astype(BF16)
    for j in range(o_ref.shape[1] // col_chunk):
        sl = slice(j * col_chunk, (j + 1) * col_chunk)
        o_ref[:, sl] = jnp.dot(h, w_ref[:, sl], preferred_element_type=F32).astype(o_ref.dtype)


def _in_proj(x2, g, w_bf16, after, row0, rows, tm=512):
    d = x2.shape[1]
    cols = w_bf16.shape[1]
    assert row0 % tm == 0 and rows % tm == 0
    blk0 = row0 // tm
    return pl.pallas_call(
        functools.partial(_inproj_kernel, col_chunk=512),
        out_shape=jax.ShapeDtypeStruct((rows, cols), BF16),
        grid=(rows // tm,),
        in_specs=[pl.BlockSpec((tm, d), lambda i: (i + blk0, 0)),
                  pl.BlockSpec((1, d), lambda i: (0, 0)),
                  pl.BlockSpec((d, cols), lambda i: (0, 0)),
                  pl.BlockSpec(memory_space=pl.ANY)],
        out_specs=pl.BlockSpec((tm, cols), lambda i: (i, 0)),
        compiler_params=pltpu.CompilerParams(dimension_semantics=("parallel",),
                                             vmem_limit_bytes=VMEM_LIMIT),
        name="in_proj",
    )(x2, g, w_bf16, after)


def _ret_kernel(q_ref, k_ref, v_ref, g_ref, intra_ref, cross_ref, kdec_ref, cdec_ref, o_ref):
    c = RET_CHUNK
    n_chunks = q_ref.shape[1] // c
    intra = intra_ref[0]
    cross = cross_ref[0]
    kdec = kdec_ref[0]
    cdec = cdec_ref[0]

    def body(n, s):
        sl = pl.ds(pl.multiple_of(n * c, c), c)
        qi = q_ref[0, sl, :]
        ki = k_ref[0, sl, :]
        vi = v_ref[0, sl, :]
        a = lax.dot_general(qi, ki, _NT, preferred_element_type=F32) * intra
        o = jnp.dot(a.astype(BF16), vi, preferred_element_type=F32)
        o = o + jnp.dot(qi, s.astype(BF16), preferred_element_type=F32) * cross
        kd_t = (ki.astype(F32) * kdec).T.astype(BF16)
        s = s * cdec + jnp.dot(kd_t, vi, preferred_element_type=F32)
        gate = g_ref[0, sl, :].astype(F32)
        o_ref[0, sl, :] = (gate * jax.nn.sigmoid(gate) * _rms(o)).astype(o_ref.dtype)
        return s

    lax.fori_loop(0, n_chunks, body, jnp.zeros((HEAD_W, HEAD_W), F32), unroll=8)


def _retention(proj3, intra, cross, kdec, cdec):
    b, t, _ = proj3.shape
    h = RET_HEADS

    def col(off):
        return pl.BlockSpec((1, t, HEAD_W), lambda bi, hi: (bi, 0, off + hi))

    def per_head(shape):
        return pl.BlockSpec((1,) + shape, lambda bi, hi: (hi, 0, 0))

    return pl.pallas_call(
        _ret_kernel,
        out_shape=jax.ShapeDtypeStruct((b, t, h * HEAD_W), BF16),
        grid=(b, h),
        in_specs=[col(0), col(h), col(2 * h), col(3 * h),
                  per_head((RET_CHUNK, RET_CHUNK)), per_head((RET_CHUNK, 1)),
                  per_head((RET_CHUNK, 1)), per_head((1, 1))],
        out_specs=pl.BlockSpec((1, t, HEAD_W), lambda bi, hi: (bi, 0, hi)),
        compiler_params=pltpu.CompilerParams(dimension_semantics=("parallel", "parallel"),
                                             vmem_limit_bytes=VMEM_LIMIT),
        name="retention",
    )(proj3, proj3, proj3, proj3, intra, cross, kdec, cdec)


def _diff_kernel(slope_ref, lam_ref, rel_ref, q_ref, k_ref, v_ref, gsub_ref, o_ref, *, tq, tk, out_scale):
    hi = pl.program_id(1)
    qi = pl.program_id(2)
    half = HEAD_W // 2
    slope = slope_ref[hi]

    lp = lam_ref[...]
    lam = (jnp.exp(jnp.sum(lp[0:1] * lp[1:2], axis=1, keepdims=True))
           - jnp.exp(jnp.sum(lp[2:3] * lp[3:4], axis=1, keepdims=True))
           + (1.0 - out_scale))

    q = q_ref[0] * jnp.asarray(half ** -0.5, BF16)
    lane = lax.broadcasted_iota(jnp.int32, q.shape, 1)
    zero = jnp.zeros_like(q)
    qs = jnp.concatenate([jnp.where(lane < half, q, zero),
                          jnp.where(lane >= half, q, zero)], axis=0)

    rel = rel_ref[...]
    rel_bias = -slope * rel

    def body(j, carry):
        m, l, acc = carry
        ksl = pl.ds(pl.multiple_of(j * tk, tk), tk)
        k = k_ref[0, ksl, :]
        v = v_ref[0, ksl, :]
        off = (qi * tq - j * tk).astype(F32)
        s = lax.dot_general(qs, k, _NT, preferred_element_type=F32)
        s = s + jnp.where(rel >= -off, rel_bias - slope * off, NEG_INF)
        m_new = jnp.maximum(m, jnp.max(s, axis=1, keepdims=True))
        alpha = jnp.exp(m - m_new)
        p = jnp.exp(s - m_new)
        l = alpha * l + jnp.sum(p, axis=1, keepdims=True)
        acc = alpha * acc + jnp.dot(p.astype(BF16), v, preferred_element_type=F32)
        return m_new, l, acc

    n_kv = (qi * tq + tq - 1) // tk + 1
    m0 = jnp.full((2 * tq, 1), NEG_INF, F32)
    l0 = jnp.zeros((2 * tq, 1), F32)
    a0 = jnp.zeros((2 * tq, HEAD_W), F32)
    _, l, acc = lax.fori_loop(0, n_kv, body, (m0, l0, a0))
    o = acc / l
    dy = o[:tq] - lam * o[tq:]
    o_ref[0] = (_rms(dy) * gsub_ref[...] * out_scale).astype(o_ref.dtype)


def _diff_attention(proj3, slopes, diff_lambda_l, gsub, out_scale, tq=256, tk=512):
    b, t, _ = proj3.shape
    h = DIFF_HEADS
    base = 4 * RET_HEADS
    kern = functools.partial(_diff_kernel, tq=tq, tk=tk, out_scale=out_scale)
    rel = jnp.asarray((np.arange(2 * tq) % tq)[:, None] - np.arange(tk)[None, :], F32)
    return pl.pallas_call(
        kern,
        out_shape=jax.ShapeDtypeStruct((b, t, h * HEAD_W), BF16),
        grid=(b, h, t // tq),
        in_specs=[pl.BlockSpec(memory_space=pltpu.SMEM),
                  pl.BlockSpec(diff_lambda_l.shape, lambda bi, hi, qi: (0, 0)),
                  pl.BlockSpec((2 * tq, tk), lambda bi, hi, qi: (0, 0)),
                  pl.BlockSpec((1, tq, HEAD_W), lambda bi, hi, qi: (bi, qi, base + hi)),
                  pl.BlockSpec((1, t, HEAD_W), lambda bi, hi, qi: (bi, 0, base + h + hi)),
                  pl.BlockSpec((1, t, HEAD_W), lambda bi, hi, qi: (bi, 0, base + 2 * h + hi)),
                  pl.BlockSpec((1, HEAD_W), lambda bi, hi, qi: (0, 0))],
        out_specs=pl.BlockSpec((1, tq, HEAD_W), lambda bi, hi, qi: (bi, qi, hi)),
        compiler_params=pltpu.CompilerParams(
            dimension_semantics=("parallel", "parallel", "arbitrary"),
            vmem_limit_bytes=VMEM_LIMIT),
        name="diff_attention",
    )(slopes, diff_lambda_l, rel, proj3, proj3, proj3, gsub)


def _outproj_kernel(x_ref, ret_ref, dif_ref, wo_ref, g_ref, wq_ref, x1_ref, xn_ref, xnf_ref, q_ref, *, col_chunk):
    half = ret_ref.shape[1]
    mix = jnp.dot(ret_ref[...], wo_ref[:half, :], preferred_element_type=F32)
    mix = mix + jnp.dot(dif_ref[...], wo_ref[half:, :], preferred_element_type=F32)
    x1 = x_ref[...] + mix
    x1_ref[...] = x1
    xn = (_rms(x1) * g_ref[...]).astype(BF16)
    xn_ref[...] = xn
    xnf_ref[...] = xn.astype(F32)
    for j in range(q_ref.shape[1] // col_chunk):
        sl = slice(j * col_chunk, (j + 1) * col_chunk)
        q_ref[:, sl] = jnp.dot(xn, wq_ref[:, sl], preferred_element_type=F32).astype(q_ref.dtype)


def _out_proj(x2, ret2, dif2, wo_bf16, g, wq_bf16, row0, tm=512):
    n, half = ret2.shape
    d = x2.shape[1]
    qc = wq_bf16.shape[1]
    assert row0 % tm == 0 and n % tm == 0
    blk0 = row0 // tm
    row = lambda w: pl.BlockSpec((tm, w), lambda i: (i, 0))
    full = lambda a: pl.BlockSpec(a.shape, lambda i: (0, 0))
    return pl.pallas_call(
        functools.partial(_outproj_kernel, col_chunk=512),
        out_shape=(jax.ShapeDtypeStruct((n, d), F32),
                   jax.ShapeDtypeStruct((n, d), BF16),
                   jax.ShapeDtypeStruct((n, d), F32),
                   jax.ShapeDtypeStruct((n, qc), BF16)),
        grid=(n // tm,),
        in_specs=[pl.BlockSpec((tm, d), lambda i: (i + blk0, 0)), row(half), row(half),
                  full(wo_bf16), full(g), full(wq_bf16)],
        out_specs=(row(d), row(d), row(d), row(qc)),
        compiler_params=pltpu.CompilerParams(dimension_semantics=("parallel",),
                                             vmem_limit_bytes=VMEM_LIMIT),
        name="out_proj",
    )(x2, ret2, dif2, wo_bf16, g, wq_bf16)


def _split3(v):
    trunc = lambda x: pltpu.bitcast(pltpu.bitcast(x, jnp.uint32) & jnp.uint32(0xFFFF0000), F32)
    hi = trunc(v)
    r1 = v - hi
    mid = trunc(r1)
    return hi.astype(BF16), mid.astype(BF16), (r1 - mid).astype(BF16)


def _spread_exact(v, sel_bf16):
    hi, mid, lo = _split3(v)
    d = lambda a: jnp.dot(a, sel_bf16, preferred_element_type=F32)
    return (d(hi) + d(mid)) + d(lo)


def _spread_exact_left(sel_bf16, v):
    hi, mid, lo = _split3(v)
    d = lambda a: jnp.dot(sel_bf16, a, preferred_element_type=F32)
    return (d(hi) + d(mid)) + d(lo)


def _topk_kernel(q_ref, sk_ref, sel_ref, pad_ref, e_ref, g_ref, g2_ref):
    tb = q_ref.shape[0]
    k = PEER_TOPK
    n_sets = 2 * PEER_HEADS
    neg = -jnp.inf

    s = jnp.stack(
        [lax.dot_general(sk_ref[j // 2, j % 2], q_ref[:, j * N_KEYS:(j + 1) * N_KEYS], _NT,
                         preferred_element_type=F32) for j in range(n_sets)], axis=0)
    key_f = lax.broadcasted_iota(jnp.int32, s.shape, 1).astype(F32)
    rank = lax.broadcasted_iota(jnp.int32, (n_sets, k, tb), 1)
    top_v = jnp.zeros((n_sets, k, tb), F32)
    top_i = jnp.zeros((n_sets, k, tb), F32)
    for r in range(k):
        m = jnp.max(s, axis=1, keepdims=True)
        i = jnp.min(jnp.where(s == m, key_f, float(N_KEYS)), axis=1, keepdims=True)
        s = jnp.where(key_f == i, neg, s)
        top_v = jnp.where(rank == r, m, top_v)
        top_i = jnp.where(rank == r, i, top_i)

    pad = pad_ref[...]
    cands, eids = [], []
    for h in range(PEER_HEADS):
        cands.append(_spread_exact_left(sel_ref[0], top_v[2 * h]) + _spread_exact_left(sel_ref[1], top_v[2 * h + 1])
                     + pad)
        eids.append(jnp.dot(sel_ref[0], top_i[2 * h].astype(BF16), preferred_element_type=F32) * float(N_KEYS)
                    + jnp.dot(sel_ref[1], top_i[2 * h + 1].astype(BF16), preferred_element_type=F32))
    cand = jnp.stack(cands, axis=0)
    eid = jnp.stack(eids, axis=0)
    n_cand = cand.shape[1]

    c_f = lax.broadcasted_iota(jnp.int32, cand.shape, 1).astype(F32)
    rank2 = lax.broadcasted_iota(jnp.int32, (PEER_HEADS, k, tb), 1)
    top_c = []
    e_rank = jnp.zeros((PEER_HEADS, k, tb), F32)
    for r in range(k):
        m = jnp.max(cand, axis=1, keepdims=True)
        pos = jnp.min(jnp.where(cand == m, c_f, float(n_cand)), axis=1, keepdims=True)
        hit = c_f == pos
        e_rank = jnp.where(rank2 == r, jnp.sum(jnp.where(hit, eid, 0.0), axis=1, keepdims=True), e_rank)
        cand = jnp.where(hit, neg, cand)
        top_c.append(m)
    p_rank = jnp.zeros((PEER_HEADS, k, tb), F32)
    denom = jnp.zeros((PEER_HEADS, 1, tb), F32)
    for r in range(k):
        p = jnp.exp(top_c[r] - top_c[0])
        denom = denom + p
        p_rank = jnp.where(rank2 == r, p, p_rank)
    g_rank = p_rank / denom

    e_out = e_rank.reshape(PEER_HEADS * k, tb).T
    g1 = g_rank.reshape(PEER_HEADS * k, tb).T
    sr = lax.broadcasted_iota(jnp.int32, (N_SEL, 2 * N_SEL), 0)
    sc = lax.broadcasted_iota(jnp.int32, (N_SEL, 2 * N_SEL), 1)
    e_ref[...] = e_out.astype(jnp.int32)
    g_ref[...] = g1
    g2_ref[...] = _spread_exact(g1, jnp.where(sc == 2 * sr + 1, 1.0, 0.0).astype(BF16))


def _candidate_tables(tb):
    k = PEER_TOPK
    pairs = [(a, b) for a in range(k) for b in range(k) if (a + 1) * (b + 1) <= k]
    n_cand = -(-len(pairs) // SUBLANES) * SUBLANES
    sel = np.zeros((2, n_cand, k), np.float32)
    pad = np.full((n_cand, tb), -np.inf, np.float32)
    for c, (a, b) in enumerate(pairs):
        sel[0, c, a] = 1.0
        sel[1, c, b] = 1.0
        pad[c, :] = 0.0
    return jnp.asarray(sel, BF16), jnp.asarray(pad)


def _peer_topk(q2, sk_bf16, tb=128):
    n, qc = q2.shape
    sel, pad = _candidate_tables(tb)
    const = lambda a: pl.BlockSpec(a.shape, lambda i: (0,) * a.ndim)
    return pl.pallas_call(
        _topk_kernel,
        out_shape=(jax.ShapeDtypeStruct((n, N_SEL), jnp.int32),
                   jax.ShapeDtypeStruct((n, N_SEL), F32),
                   jax.ShapeDtypeStruct((n, 2 * N_SEL), F32)),
        grid=(n // tb,),
        in_specs=[pl.BlockSpec((tb, qc), lambda i: (i, 0)), const(sk_bf16), const(sel), const(pad)],
        out_specs=(pl.BlockSpec((tb, N_SEL), lambda i: (i, 0)),
                   pl.BlockSpec((tb, N_SEL), lambda i: (i, 0)),
                   pl.BlockSpec((tb, 2 * N_SEL), lambda i: (i, 0))),
        compiler_params=pltpu.CompilerParams(dimension_semantics=("parallel",),
                                             vmem_limit_bytes=VMEM_LIMIT),
        name="peer_topk",
    )(q2, sk_bf16, sel, pad)


def _bf16_bits(t):
    return lax.bitcast_convert_type(t.astype(BF16), jnp.uint16).astype(jnp.uint32)


def _pack_tables(u, v):
    e, d = u.shape
    return (_bf16_bits(u) | (_bf16_bits(v) << 16)).reshape(e, d // HEAD_W, HEAD_W)


def _pack_halves(t):
    b = _bf16_bits(t)
    half = t.shape[1] // 2
    return b[:, :half] | (b[:, half:] << 16)


def _mix_kernel(efirst_ref, enext_ref, g2_ref, xn_ref, x1_ref, gfin_ref, tab_hbm, after_ref, o_ref, buf, sem, *,
                final_norm):
    step = pl.program_id(0)
    tb = xn_ref.shape[0]
    n_chunks = tab_hbm.shape[1]

    def issue(e_ref, t, slot):
        for k in range(N_SEL):
            e = e_ref[t, k]
            pltpu.make_async_copy(tab_hbm.at[e], buf.at[slot, pl.ds(k * n_chunks, n_chunks), :],
                                  sem.at[slot]).start(priority=k % 2)

    def wait(slot):
        pltpu.make_async_copy(buf.at[slot], buf.at[slot], sem.at[slot]).wait()

    @pl.when(step == 0)
    def _():
        issue(efirst_ref, 0, 0)

    rid = lax.broadcasted_iota(jnp.int32, (SUBLANES, 2 * N_SEL), 0)

    def group(gi, c):
        base = pl.multiple_of(gi * SUBLANES, SUBLANES)
        xrows = xn_ref[pl.ds(base, SUBLANES), :]
        grows = g2_ref[pl.ds(base, SUBLANES), :]
        acc = x1_ref[pl.ds(base, SUBLANES), :]
        for j in range(SUBLANES):
            slot = j % 2
            issue(enext_ref, base + j, 1 - slot)
            wait(slot)
            uv = jnp.concatenate(
                [pltpu.bitcast(buf[slot, pl.ds(c, N_SEL, stride=n_chunks), :], BF16) for c in range(n_chunks)],
                axis=1)
            s2 = lax.dot_general(xrows, uv, _NT, preferred_element_type=F32)
            act = pltpu.roll(_gelu_exact(s2), 1, axis=1)
            w2 = jnp.where(rid == j, grows * act, 0.0)
            acc = acc + jnp.dot(w2.astype(BF16), uv, preferred_element_type=F32)
        o_ref[pl.ds(base, SUBLANES), :] = acc
        return c

    lax.fori_loop(0, tb // SUBLANES, group, 0)

    @pl.when(step == pl.num_programs(0) - 1)
    def _():
        wait(0)

    if final_norm:
        o_ref[...] = _rms(o_ref[...]) * gfin_ref[...]


def _peer_mix(eids, e_next, gates2, xn, x1, gfin, table, after, row0, rows, final_norm, tb=64):
    d = x1.shape[1]
    assert tb % (2 * SUBLANES) == 0 and row0 % tb == 0 and rows % tb == 0
    blk0 = row0 // tb
    row = lambda w: pl.BlockSpec((tb, w), lambda i: (i + blk0, 0))
    return pl.pallas_call(
        functools.partial(_mix_kernel, final_norm=final_norm),
        out_shape=jax.ShapeDtypeStruct((rows, d), F32),
        grid=(rows // tb,),
        in_specs=[pl.BlockSpec((SUBLANES, N_SEL), lambda i: (row0 // SUBLANES, 0), memory_space=pltpu.SMEM),
                  pl.BlockSpec((tb, N_SEL), lambda i: (i + blk0, 0), memory_space=pltpu.SMEM),
                  row(2 * N_SEL), row(d), row(d),
                  pl.BlockSpec((1, d), lambda i: (0, 0)),
                  pl.BlockSpec(memory_space=pl.ANY),
                  pl.BlockSpec(memory_space=pl.ANY)],
        out_specs=pl.BlockSpec((tb, d), lambda i: (i, 0)),
        scratch_shapes=[pltpu.VMEM((2, N_SEL * (d // HEAD_W), HEAD_W), jnp.uint32),
                        pltpu.SemaphoreType.DMA((2,))],
        compiler_params=pltpu.CompilerParams(dimension_semantics=("arbitrary",),
                                             vmem_limit_bytes=VMEM_LIMIT),
        name="peer_mix",
    )(eids, e_next, gates2, xn, x1, gfin, table, after)


SC_LANES = 16
SC_SUBCORES = 16
SC_TILES = 2 * SC_SUBCORES
SC_TBLK = 64
SC_SLOTS = 4
SC_EXPERT_GROUP = 4
SC_CHUNK_UNROLL = 4
HI_MASK = 0xFFFF0000
_SC_PARAMS = pltpu.CompilerParams(needs_layout_passes=False)


def _sc_unpack(word):
    return plsc.bitcast(word << 16, F32), plsc.bitcast(word & jnp.uint32(HI_MASK), F32)


def _sc_pipelined_heads(tab_hbm, ev, rows, sem, compute):
    n_items = SC_TBLK * PEER_HEADS
    ahead = SC_SLOTS - 1
    assert n_items % SC_SLOTS == 0

    def gather(it, slot):
        ti = it // PEER_HEADS
        h = it % PEER_HEADS
        return pltpu.make_async_copy(tab_hbm.at[ev.at[ti, pl.ds(h * PEER_TOPK, PEER_TOPK)]], rows.at[slot],
                                     sem.at[slot])

    for a in range(ahead):
        gather(a, a).start()

    def group(jj, carry):
        for slot in range(SC_SLOTS):
            it = SC_SLOTS * jj + slot
            gather(jnp.minimum(it + ahead, n_items - 1), (slot + ahead) % SC_SLOTS).start()
            gather(it, slot).wait()
            compute(it, slot)
        return carry

    lax.fori_loop(0, n_items // SC_SLOTS, group, 0)
    for a in range(ahead):
        gather(n_items - 1, (n_items + a) % SC_SLOTS).wait()


def _sc_scores(x, eids, utab, row0, rows):
    d = x.shape[1]
    dw = d // 2
    nw = dw // SC_LANES
    per_tile = rows // SC_TILES
    assert rows % (SC_TILES * SC_TBLK) == 0 and row0 % SC_TBLK == 0

    def body(x_hbm, e_hbm, tab_hbm, s_hbm, xv, ev, rbuf, sv, tmp, sem):
        tile = lax.axis_index("c") * SC_SUBCORES + lax.axis_index("s")
        lane = lax.iota(jnp.int32, SC_LANES)

        def compute(it, slot):
            ti = it // PEER_HEADS
            h = it % PEER_HEADS

            accs = []
            for g0 in range(0, PEER_TOPK, SC_EXPERT_GROUP):
                def chunks(w, acc, g0=g0):
                    acc = list(acc)
                    for sub in range(SC_CHUNK_UNROLL):
                        off = pl.multiple_of((w * SC_CHUNK_UNROLL + sub) * SC_LANES, SC_LANES)
                        xlo = xv[ti, pl.ds(off, SC_LANES)]
                        xhi = xv[ti, pl.ds(dw + off, SC_LANES)]
                        for j in range(SC_EXPERT_GROUP):
                            lo, hi = _sc_unpack(rbuf[slot, g0 + j, pl.ds(off, SC_LANES)])
                            acc[j] = acc[j] + lo * xlo + hi * xhi
                    return tuple(acc)

                accs.extend(lax.fori_loop(0, nw // SC_CHUNK_UNROLL, chunks,
                                          tuple(jnp.zeros((SC_LANES,), F32) for _ in range(SC_EXPERT_GROUP))))
            for j in range(PEER_TOPK):
                tmp[j, :] = accs[j]
            svec = jnp.zeros((SC_LANES,), F32)
            for l in range(SC_LANES):
                svec = svec + plsc.load_gather(tmp, [lane, jnp.full((SC_LANES,), l, jnp.int32)])
            sv[ti, pl.ds(pl.multiple_of(h * PEER_TOPK, PEER_TOPK), PEER_TOPK)] = svec

        def block(b, carry):
            t_out = pl.multiple_of(tile * per_tile + b * SC_TBLK, SC_TBLK)
            pltpu.sync_copy(x_hbm.at[pl.ds(row0 + t_out, SC_TBLK)], xv)
            pltpu.sync_copy(e_hbm.at[pl.ds(row0 + t_out, SC_TBLK)], ev)
            _sc_pipelined_heads(tab_hbm, ev, rbuf, sem, compute)
            pltpu.sync_copy(sv, s_hbm.at[pl.ds(t_out, SC_TBLK)])
            return carry

        lax.fori_loop(0, per_tile // SC_TBLK, block, 0)

    return pl.kernel(
        body,
        out_type=jax.ShapeDtypeStruct((rows, N_SEL), F32),
        mesh=plsc.VectorSubcoreMesh(core_axis_name="c", subcore_axis_name="s"),
        scratch_types=[pltpu.VMEM((SC_TBLK, d), F32), pltpu.VMEM((SC_TBLK, N_SEL), jnp.int32),
                       pltpu.VMEM((SC_SLOTS, PEER_TOPK, dw), jnp.uint32), pltpu.VMEM((SC_TBLK, N_SEL), F32),
                       pltpu.VMEM((PEER_TOPK, SC_LANES), F32), pltpu.SemaphoreType.DMA((SC_SLOTS,))],
        compiler_params=_SC_PARAMS,
        name="sc_scores",
    )(x, eids, utab)


def _sc_combine(w, eids, vtab, row0):
    rows = w.shape[0]
    dw = vtab.shape[1]
    d = 2 * dw
    nw = dw // SC_LANES
    per_tile = rows // SC_TILES
    assert rows % (SC_TILES * SC_TBLK) == 0 and row0 % SC_TBLK == 0

    def body(w_hbm, e_hbm, tab_hbm, y_hbm, wv, ev, rbuf, yv, sem):
        tile = lax.axis_index("c") * SC_SUBCORES + lax.axis_index("s")
        zero_i = jnp.zeros((SC_LANES,), jnp.int32)

        def compute(it, slot):
            ti = it // PEER_HEADS
            h = it % PEER_HEADS
            wk = [plsc.load_gather(wv, [zero_i + ti, zero_i + (h * PEER_TOPK + j)]) for j in range(PEER_TOPK)]

            def chunk(c, carry):
                off = pl.multiple_of(c * SC_LANES, SC_LANES)
                alo = jnp.zeros((SC_LANES,), F32)
                ahi = jnp.zeros((SC_LANES,), F32)
                for j in range(PEER_TOPK):
                    lo, hi = _sc_unpack(rbuf[slot, j, pl.ds(off, SC_LANES)])
                    alo = alo + wk[j] * lo
                    ahi = ahi + wk[j] * hi
                plsc.addupdate(yv.at[ti, pl.ds(off, SC_LANES)], alo)
                plsc.addupdate(yv.at[ti, pl.ds(dw + off, SC_LANES)], ahi)
                return carry

            plsc.parallel_loop(0, nw, 1, carry=jnp.int32(0))(chunk)

        def block(b, carry):
            t_out = pl.multiple_of(tile * per_tile + b * SC_TBLK, SC_TBLK)
            pltpu.sync_copy(w_hbm.at[pl.ds(t_out, SC_TBLK)], wv)
            pltpu.sync_copy(e_hbm.at[pl.ds(row0 + t_out, SC_TBLK)], ev)

            def zero(i, carry):
                for ti in range(SC_TBLK):
                    yv[ti, pl.ds(pl.multiple_of(i * SC_LANES, SC_LANES), SC_LANES)] = jnp.zeros((SC_LANES,), F32)
                return carry

            lax.fori_loop(0, d // SC_LANES, zero, 0)
            _sc_pipelined_heads(tab_hbm, ev, rbuf, sem, compute)
            pltpu.sync_copy(yv, y_hbm.at[pl.ds(t_out, SC_TBLK)])
            return carry

        lax.fori_loop(0, per_tile // SC_TBLK, block, 0)

    return pl.kernel(
        body,
        out_type=jax.ShapeDtypeStruct((rows, d), F32),
        mesh=plsc.VectorSubcoreMesh(core_axis_name="c", subcore_axis_name="s"),
        scratch_types=[pltpu.VMEM((SC_TBLK, N_SEL), F32), pltpu.VMEM((SC_TBLK, N_SEL), jnp.int32),
                       pltpu.VMEM((SC_SLOTS, PEER_TOPK, dw), jnp.uint32), pltpu.VMEM((SC_TBLK, d), F32),
                       pltpu.SemaphoreType.DMA((SC_SLOTS,))],
        compiler_params=_SC_PARAMS,
        name="sc_combine",
    )(w, eids, vtab)


def _gate_kernel(s_ref, g_ref, after_ref, w_ref):
    w_ref[...] = g_ref[...] * _gelu_exact(s_ref[...])


def _gate_act(s, gates, after, row0, tb=512):
    rows = s.shape[0]
    assert rows % tb == 0 and row0 % tb == 0
    blk0 = row0 // tb
    return pl.pallas_call(
        _gate_kernel,
        out_shape=jax.ShapeDtypeStruct((rows, N_SEL), F32),
        grid=(rows // tb,),
        in_specs=[pl.BlockSpec((tb, N_SEL), lambda i: (i, 0)),
                  pl.BlockSpec((tb, N_SEL), lambda i: (i + blk0, 0)),
                  pl.BlockSpec(memory_space=pl.ANY)],
        out_specs=pl.BlockSpec((tb, N_SEL), lambda i: (i, 0)),
        compiler_params=pltpu.CompilerParams(dimension_semantics=("parallel",)),
        name="peer_gate",
    )(s, gates, after)


def _resid_norm_kernel(x1_ref, y_ref, gfin_ref, after_ref, o_ref, *, final_norm):
    z = x1_ref[...] + y_ref[...]
    o_ref[...] = _rms(z) * gfin_ref[...] if final_norm else z


def _resid_norm(x1, y, gfin, after, final_norm, tb=512):
    rows, d = y.shape
    assert rows % tb == 0
    return pl.pallas_call(
        functools.partial(_resid_norm_kernel, final_norm=final_norm),
        out_shape=jax.ShapeDtypeStruct((rows, d), F32),
        grid=(rows // tb,),
        in_specs=[pl.BlockSpec((tb, d), lambda i: (i, 0)),
                  pl.BlockSpec((tb, d), lambda i: (i, 0)),
                  pl.BlockSpec((1, d), lambda i: (0, 0)),
                  pl.BlockSpec(memory_space=pl.ANY)],
        out_specs=pl.BlockSpec((tb, d), lambda i: (i, 0)),
        compiler_params=pltpu.CompilerParams(dimension_semantics=("parallel",)),
        name="peer_resid_norm",
    )(x1, y, gfin, after)


N_CHUNKS = 16
SC_TOKEN_FRAC = 52 / 64
SPLIT_UNIT = SC_TILES * SC_TBLK
COST_FRONT, COST_SC_SCORES, COST_SC_COMBINE, COST_TC_MIX = 0.13, 0.21, 0.18, 0.85


def _sc_rows(n, tokens_per_chunk):
    n_sc = int(n * SC_TOKEN_FRAC) // SPLIT_UNIT * SPLIT_UNIT
    return [min(max(n_sc - c * tokens_per_chunk, 0), tokens_per_chunk) for c in range(N_CHUNKS)]


def _tc_schedule(tokens_per_chunk, sc_rows):
    sc_chunks = [c for c in range(N_CHUNKS) if sc_rows[c]]
    base = [("front", c) for c in range(N_CHUNKS)] + [("mix", c) for c in range(N_CHUNKS)
                                                       if sc_rows[c] < tokens_per_chunk]
    order = []
    tc = sc = 0.0
    scores_done, combine_done = {}, {}
    gates = list(sc_chunks)
    norms = list(sc_chunks)
    for kind, c in base + [("end", None)]:
        while gates and (kind == "end" or scores_done.get(gates[0], float("inf"))
                         + 0.25 * COST_SC_SCORES * sc_rows[gates[0]] <= tc):
            g = gates.pop(0)
            tc = max(tc, scores_done[g])
            sc = max(sc, tc) + COST_SC_COMBINE * sc_rows[g]
            combine_done[g] = sc
            order.append(("gate", g))
        while norms and (kind == "end" or combine_done.get(norms[0], float("inf")) <= tc):
            order.append(("norm", norms.pop(0)))
        if kind == "front":
            tc += COST_FRONT * tokens_per_chunk
            if sc_rows[c]:
                sc = max(sc, tc) + COST_SC_SCORES * sc_rows[c]
                scores_done[c] = sc
        elif kind == "mix":
            tc += COST_TC_MIX * (tokens_per_chunk - sc_rows[c])
        if kind != "end":
            order.append((kind, c))
    return order


def _layer(x2, batch, seq, params, consts, li, final_norm):
    (g_mix, w_in, lam_p, g_sub, w_out, g_ffn, wq, sk, u, v, g_fin) = params
    intra, cross, kdec, cdec, slopes = consts
    n, d = x2.shape
    nc = n // N_CHUNKS
    bc = batch // N_CHUNKS
    assert batch % N_CHUNKS == 0
    table = _pack_tables(u, v)
    utab = _pack_halves(u)
    vtab = _pack_halves(v)

    sc_rows = _sc_rows(n, nc)
    chunks = [None] * N_CHUNKS
    scores = [None] * N_CHUNKS
    ys = [None] * N_CHUNKS
    outs_sc = [None] * N_CHUNKS
    outs_tc = [None] * N_CHUNKS
    last = g_fin

    def front(c, last):
        proj = _in_proj(x2, g_mix, w_in, last, c * nc, nc)
        proj3 = proj.reshape(bc, seq, proj.shape[1])
        ret = _retention(proj3, intra, cross, kdec, cdec)
        dif = _diff_attention(proj3, slopes, lam_p, g_sub, 1.0 - li)
        half = ret.shape[2]
        x1, xn, xn_f32, q = _out_proj(x2, ret.reshape(nc, half), dif.reshape(nc, half), w_out, g_ffn, wq, c * nc)
        eids, gates, gates2 = _peer_topk(q, sk)
        chunks[c] = (x1, xn, eids, gates, gates2)
        if sc_rows[c]:
            scores[c] = _sc_scores(xn_f32, eids, utab, 0, sc_rows[c])
        return gates2

    def gate(c, last):
        _, _, eids, gates, _ = chunks[c]
        w = _gate_act(scores[c], gates, last, 0)
        ys[c] = _sc_combine(w, eids, vtab, 0)
        return w

    def tc_mix(c, last):
        x1, xn, eids, _, gates2 = chunks[c]
        outs_tc[c] = _peer_mix(eids, jnp.roll(eids, -1, axis=0), gates2, xn, x1, g_fin, table, last,
                               sc_rows[c], nc - sc_rows[c], final_norm)
        return outs_tc[c]

    def norm(c, last):
        outs_sc[c] = _resid_norm(chunks[c][0], ys[c], g_fin, last, final_norm)
        return outs_sc[c]

    run = {"front": front, "gate": gate, "mix": tc_mix, "norm": norm}
    for kind, c in _tc_schedule(nc, sc_rows):
        last = run[kind](c, last)
    pieces = [o for c in range(N_CHUNKS) for o in (outs_sc[c], outs_tc[c]) if o is not None]
    return jnp.concatenate(pieces, axis=0)


def _retention_tables(dk):
    c = RET_CHUNK
    h = RET_HEADS
    log_gamma = jnp.log1p(-jnp.exp2(-5.0 - jnp.arange(h, dtype=F32)))
    idx = jnp.arange(c, dtype=F32)
    rel = idx[:, None] - idx[None, :]
    intra = jnp.where(rel >= 0, jnp.exp(log_gamma[:, None, None] * jnp.maximum(rel, 0.0)), 0.0)
    cross = jnp.exp(log_gamma[:, None] * (idx + 1.0))
    kdec = jnp.exp(log_gamma[:, None] * (c - 1.0 - idx))
    cdec = jnp.exp(log_gamma * c)
    scale = dk ** -0.5
    return intra * scale, cross[:, :, None], (kdec * scale)[:, :, None], cdec[:, None, None]


def kernel(x, norm_mix_g, w_in, diff_lambda, diff_subln_g, w_out, norm_ffn_g, peer_wq, peer_subkeys,
           peer_u, peer_v, final_norm_g):
    b, t, d = x.shape
    depth = w_in.shape[0]
    n = b * t
    x2 = x.reshape(n, d)
    consts = _retention_tables(HEAD_W) + (
        jnp.exp2(-8.0 * (jnp.arange(DIFF_HEADS, dtype=F32) + 1.0) / DIFF_HEADS),)
    for l in range(depth):
        li = 0.8 - 0.6 * math.exp(-0.3 * l)
        params = (norm_mix_g[l][None, :], w_in[l].astype(BF16), diff_lambda[l], diff_subln_g[l][None, :],
                  w_out[l].astype(BF16), norm_ffn_g[l][None, :], peer_wq[l].astype(BF16),
                  peer_subkeys[l].astype(BF16), peer_u[l], peer_v[l], final_norm_g[None, :])
        x2 = _layer(x2, b, t, params, consts, li, final_norm=(l == depth - 1))
    return x2.reshape(b, t, d)
```
